```python
import math
import jax, jax.numpy as jnp
from jax import lax
import numpy as np

D_MODEL = 1024
BATCH = 8
SEQ = 4096
DEPTH = 2
DEC_BATCH = 8
DEC_SEQ = 64
PAST_LEN = 1024

CHUNK = 64
N_META = 16
Q_BLOCK = 128
EPS = 1e-6
D_FF = 2816
GDN_HEADS = 4
GDN_DK = 128
GDN_DV = 128
GDN_CONV = 4
MLA_HEADS = 8
MLA_NOPE = 64
MLA_ROPE = 32
MLA_V = 64
MLA_Q_RANK = 384
MLA_KV_RANK = 256
ROPE_THETA = 10000.0
SC_WIDTH = 3

N_A_LAYERS = (DEPTH + 1) // 2
N_C_LAYERS = DEPTH // 2
GDN_QKV = GDN_HEADS * (2 * GDN_DK + GDN_DV)
GDN_COLS = GDN_QKV + GDN_HEADS * GDN_DV + 2 * GDN_HEADS
MLA_COLS = MLA_Q_RANK + MLA_KV_RANK + MLA_ROPE
IN_COLS = GDN_COLS + MLA_COLS
MIX_WIDTH = GDN_HEADS * GDN_DV + MLA_HEADS * MLA_V
BIG_CHUNK_ID = 1 << 30

kernel_name = 'hybrid_streaming_gdn_mla_shortconv_step'


def rms_norm(x, g):
    xf = x.astype(jnp.float32)
    y = xf * lax.rsqrt(jnp.mean(xf * xf, -1, keepdims=True) + EPS)
    return (y * g.astype(jnp.float32)).astype(x.dtype)


def l2norm(x):
    xf = x.astype(jnp.float32)
    return xf * lax.rsqrt(jnp.sum(xf * xf, -1, keepdims=True) + EPS)


def half_ffn(x, g, w_gate, w_up, w_down):
    h = rms_norm(x, g)
    return x + 0.5 * ((jax.nn.silu(h @ w_gate) * (h @ w_up)) @ w_down)


def causal_dwconv(x, w, prev):
    width = w.shape[0]
    t = x.shape[1]
    xp = jnp.concatenate([prev, x], 1)
    y = sum(xp[:, i:i + t] * w[i] for i in range(width))
    return y, xp[:, xp.shape[1] - (width - 1):]


def rope(x, pos):
    half = x.shape[-1] // 2
    inv = ROPE_THETA ** (-jnp.arange(half, dtype=jnp.float32) / half)
    ang = pos.astype(jnp.float32)[:, None] * inv
    shape = (pos.shape[0],) + (1,) * (x.ndim - 3) + (half,)
    cos, sin = jnp.cos(ang).reshape(shape), jnp.sin(ang).reshape(shape)
    xf = x.astype(jnp.float32)
    x1, x2 = xf[..., :half], xf[..., half:]
    return jnp.concatenate([x1 * cos - x2 * sin, x2 * cos + x1 * sin], -1).astype(x.dtype)


def gated_delta_chunked(q, k, v, g, beta, S0):
    b, T, H, dk = q.shape
    dv = v.shape[-1]
    n = T // CHUNK
    blk = lambda a: a.reshape((b, n, CHUNK) + a.shape[2:])
    to_h = lambda a: jnp.moveaxis(a, 3, 2)
    qh, kh, vh = to_h(blk(q)), to_h(blk(k)), to_h(blk(v))
    bh = to_h(blk(beta))
    Gh = jnp.cumsum(to_h(blk(g)), axis=-1)
    dG = Gh[..., :, None] - Gh[..., None, :]
    idx = jnp.arange(CHUNK)
    dec_strict = jnp.exp(jnp.where(idx[:, None] > idx[None, :], dG, -jnp.inf))
    dec_causal = jnp.exp(jnp.where(idx[:, None] >= idx[None, :], dG, -jnp.inf))
    kk = jnp.einsum('bnhid,bnhjd->bnhij', kh, kh)
    A = jnp.eye(CHUNK, dtype=jnp.float32) + bh[..., :, None] * kk * dec_strict
    rhs = jnp.concatenate([bh[..., None] * vh, (bh * jnp.exp(Gh))[..., None] * kh], -1)
    sol = lax.linalg.triangular_solve(A, rhs, left_side=True, lower=True)
    U, Wk = sol[..., :dv], sol[..., dv:]
    qk = jnp.einsum('bnhid,bnhjd->bnhij', qh, kh) * dec_causal
    qd = qh * jnp.exp(Gh)[..., None]
    g_last = Gh[..., -1]
    kt = kh * jnp.exp(g_last[..., None] - Gh)[..., None]

    def step(S, xs):
        qk_c, qd_c, U_c, Wk_c, kt_c, gl_c = xs
        W = U_c - jnp.einsum('bhcd,bhde->bhce', Wk_c, S)
        o = jnp.einsum('bhcd,bhde->bhce', qd_c, S) + jnp.einsum('bhij,bhje->bhie', qk_c, W)
        S = S * jnp.exp(gl_c)[..., None, None] + jnp.einsum('bhcd,bhce->bhde', kt_c, W)
        return S, o

    front = lambda a: jnp.moveaxis(a, 1, 0)
    S, o = lax.scan(step, S0, (front(qk), front(qd), front(U), front(Wk), front(kt), front(g_last)))
    o = jnp.transpose(o, (1, 0, 3, 2, 4)).reshape(b, T, H, dv)
    return o, S


def gdn_mixer(h, S0, conv_prev, conv_w, A_log, dt_bias, o_norm):
    b, t, _ = h.shape
    s1 = GDN_QKV + GDN_HEADS * GDN_DV
    qkv, z, a, bt = jnp.split(h, [GDN_QKV, s1, s1 + GDN_HEADS], axis=-1)
    qkv, conv_new = causal_dwconv(qkv, conv_w, conv_prev)
    qkv = jax.nn.silu(qkv)
    q, k, v = jnp.split(qkv, [GDN_HEADS * GDN_DK, 2 * GDN_HEADS * GDN_DK], axis=-1)
    q = l2norm(q.reshape(b, t, GDN_HEADS, GDN_DK)) * (GDN_DK ** -0.5)
    k = l2norm(k.reshape(b, t, GDN_HEADS, GDN_DK))
    v = v.reshape(b, t, GDN_HEADS, GDN_DV).astype(jnp.float32)
    g = -jnp.exp(A_log.astype(jnp.float32)) * jax.nn.softplus(a.astype(jnp.float32) + dt_bias.astype(jnp.float32))
    beta = jax.nn.sigmoid(bt.astype(jnp.float32))
    pad = (-t) % CHUNK
    padt = lambda u: jnp.pad(u, [(0, 0), (0, pad)] + [(0, 0)] * (u.ndim - 2))
    o, S = gated_delta_chunked(padt(q), padt(k), padt(v), padt(g), padt(beta), S0.astype(jnp.float32))
    o = rms_norm(o[:, :t], o_norm) * jax.nn.silu(z.reshape(b, t, GDN_HEADS, GDN_DV).astype(jnp.float32))
    return o.reshape(b, t, GDN_HEADS * GDN_DV).astype(h.dtype), S.astype(S0.dtype), conv_new


def attn_probs_v(qn, qr, kn, kr, v, mask):
    s = jnp.einsum('bqhd,bkhd->bhqk', qn, kn) + jnp.einsum('bqhd,bkd->bhqk', qr, kr)
    s = s.astype(jnp.float32) * ((MLA_NOPE + MLA_ROPE) ** -0.5)
    if mask is not None:
        s = jnp.where(mask, s, -jnp.inf)
    p = jax.nn.softmax(s, axis=-1).astype(v.dtype)
    return jnp.einsum('bhqk,bkhd->bqhd', p, v)


def mla_mixer(h, pos, cache_ckv, cache_kr, q_norm, w_uq, kv_norm, w_ukv, qn_norm, qr_norm, kn_norm, kr_norm):
    b, t, _ = h.shape
    cq, ckv, kr = jnp.split(h, [MLA_Q_RANK, MLA_Q_RANK + MLA_KV_RANK], axis=-1)
    q = (rms_norm(cq, q_norm) @ w_uq).reshape(b, t, MLA_HEADS, MLA_NOPE + MLA_ROPE)
    q_nope = rms_norm(q[..., :MLA_NOPE], qn_norm)
    q_rope = rope(rms_norm(q[..., MLA_NOPE:], qr_norm), pos)
    ckv_new = rms_norm(ckv, kv_norm)
    kr_new = rope(rms_norm(kr, kr_norm), pos)
    if cache_ckv is None:
        ckv_all, kr_all = ckv_new, kr_new
    else:
        ckv_all = jnp.concatenate([cache_ckv, ckv_new], 1)
        kr_all = jnp.concatenate([cache_kr, kr_new], 1)
    kv = (ckv_all @ w_ukv).reshape(b, ckv_all.shape[1], MLA_HEADS, MLA_NOPE + MLA_V)
    k_nope = rms_norm(kv[..., :MLA_NOPE], kn_norm)
    v = kv[..., MLA_NOPE:]
    if cache_ckv is None:
        L = t
        cid = (jnp.arange(L) - N_META) // CHUNK
        nb = -(-L // Q_BLOCK)
        pad = nb * Q_BLOCK - L
        qblk = lambda a: jnp.moveaxis(jnp.pad(a, [(0, 0), (0, pad), (0, 0), (0, 0)]).reshape((b, nb, Q_BLOCK) + a.shape[2:]), 1, 0)
        qcid = jnp.pad(cid, (0, pad), constant_values=BIG_CHUNK_ID).reshape(nb, Q_BLOCK)

        def block(args):
            qn_b, qr_b, qc_b = args
            return attn_probs_v(qn_b, qr_b, k_nope, kr_all, v, cid[None, :] <= qc_b[:, None])

        o = lax.map(block, (qblk(q_nope), qblk(q_rope), qcid))
        o = jnp.moveaxis(o, 0, 1).reshape(b, nb * Q_BLOCK, MLA_HEADS, MLA_V)[:, :L]
    else:
        o = attn_probs_v(q_nope, q_rope, k_nope, kr_all, v, None)
    return o.reshape(b, t, MLA_HEADS * MLA_V), ckv_new, kr_new


def mixer_ab(hn, pos, S0, conv_prev, cache_ckv, cache_kr, w_in, w_out, conv_w, A_log, dt_bias, o_norm,
             q_norm, w_uq, kv_norm, w_ukv, qn_norm, qr_norm, kn_norm, kr_norm):
    h = hn @ w_in
    o_g, S_new, conv_new = gdn_mixer(h[..., :GDN_COLS], S0, conv_prev, conv_w, A_log, dt_bias, o_norm)
    o_m, ckv_new, kr_new = mla_mixer(h[..., GDN_COLS:], pos, cache_ckv, cache_kr, q_norm, w_uq, kv_norm,
                                     w_ukv, qn_norm, qr_norm, kn_norm, kr_norm)
    y = jnp.concatenate([o_g, o_m], -1) @ w_out
    return y, ckv_new, kr_new, S_new, conv_new


def short_conv_mixer(hn, prev, w_in, conv_w, w_out):
    bg, cg, xin = jnp.split(hn @ w_in, 3, axis=-1)
    y, new_prev = causal_dwconv(cg * xin, conv_w, prev)
    return (bg * y) @ w_out, new_prev


def setup_inputs(seed: int = 0) -> dict:
    key = jax.random.key(seed)
    ks = iter(jax.random.split(key, 64))
    f32 = jnp.float32
    nrm = lambda shape, scale: jax.random.normal(next(ks), shape, f32) * scale
    gain = lambda shape: 1.0 + 0.02 * jax.random.normal(next(ks), shape, f32)
    NA, NC = N_A_LAYERS, N_C_LAYERS
    dt = jnp.exp(jax.random.uniform(next(ks), (NA, GDN_HEADS), f32, math.log(1e-3), math.log(1e-1)))
    return {
        'x_prompt': nrm((BATCH, SEQ, D_MODEL), 1.0),
        'x_sample': nrm((DEC_BATCH, DEC_SEQ, D_MODEL), 1.0),
        'cache_mla_ckv': nrm((NA, DEC_BATCH, PAST_LEN, MLA_KV_RANK), 1.0),
        'cache_mla_krope': nrm((NA, DEC_BATCH, PAST_LEN, MLA_ROPE), 1.0),
        'state_gdn_S': nrm((NA, DEC_BATCH, GDN_HEADS, GDN_DK, GDN_DV), 0.1),
        'state_gdn_conv': nrm((NA, DEC_BATCH, GDN_CONV - 1, GDN_QKV), 1.0),
        'state_sconv': nrm((NC, DEC_BATCH, SC_WIDTH - 1, D_MODEL), 1.0),
        'meta_tokens': nrm((N_META, D_MODEL), 1.0),
        'ffn1_norm': gain((DEPTH, D_MODEL)),
        'ffn1_w_gate': nrm((DEPTH, D_MODEL, D_FF), D_MODEL ** -0.5),
        'ffn1_w_up': nrm((DEPTH, D_MODEL, D_FF), D_MODEL ** -0.5),
        'ffn1_w_down': nrm((DEPTH, D_FF, D_MODEL), D_FF ** -0.5),
        'ffn2_norm': gain((DEPTH, D_MODEL)),
        'ffn2_w_gate': nrm((DEPTH, D_MODEL, D_FF), D_MODEL ** -0.5),
        'ffn2_w_up': nrm((DEPTH, D_MODEL, D_FF), D_MODEL ** -0.5),
        'ffn2_w_down': nrm((DEPTH, D_FF, D_MODEL), D_FF ** -0.5),
        'mix_norm': gain((DEPTH, D_MODEL)),
        'ab_w_in': nrm((NA, D_MODEL, IN_COLS), D_MODEL ** -0.5),
        'ab_w_out': nrm((NA, MIX_WIDTH, D_MODEL), MIX_WIDTH ** -0.5),
        'gdn_conv_w': nrm((NA, GDN_CONV, GDN_QKV), GDN_CONV ** -0.5),
        'gdn_A_log': jnp.log(jax.random.uniform(next(ks), (NA, GDN_HEADS), f32, 1.0, 16.0)),
        'gdn_dt_bias': dt + jnp.log(-jnp.expm1(-dt)),
        'gdn_o_norm': gain((NA, GDN_DV)),
        'mla_q_norm': gain((NA, MLA_Q_RANK)),
        'mla_w_uq': nrm((NA, MLA_Q_RANK, MLA_HEADS * (MLA_NOPE + MLA_ROPE)), MLA_Q_RANK ** -0.5),
        'mla_kv_norm': gain((NA, MLA_KV_RANK)),
        'mla_w_ukv': nrm((NA, MLA_KV_RANK, MLA_HEADS * (MLA_NOPE + MLA_V)), MLA_KV_RANK ** -0.5),
        'mla_qn_norm': gain((NA, MLA_NOPE)),
        'mla_qr_norm': gain((NA, MLA_ROPE)),
        'mla_kn_norm': gain((NA, MLA_NOPE)),
        'mla_kr_norm': gain((NA, MLA_ROPE)),
        'sc_w_in': nrm((NC, D_MODEL, 3 * D_MODEL), D_MODEL ** -0.5),
        'sc_conv_w': nrm((NC, SC_WIDTH, D_MODEL), SC_WIDTH ** -0.5),
        'sc_w_out': nrm((NC, D_MODEL, D_MODEL), D_MODEL ** -0.5),
    }


def reference(x_prompt, x_sample, cache_mla_ckv, cache_mla_krope, state_gdn_S, state_gdn_conv, state_sconv,
              meta_tokens, ffn1_norm, ffn1_w_gate, ffn1_w_up, ffn1_w_down, ffn2_norm, ffn2_w_gate, ffn2_w_up,
              ffn2_w_down, mix_norm, ab_w_in, ab_w_out, gdn_conv_w, gdn_A_log, gdn_dt_bias, gdn_o_norm,
              mla_q_norm, mla_w_uq, mla_kv_norm, mla_w_ukv, mla_qn_norm, mla_qr_norm, mla_kn_norm, mla_kr_norm,
              sc_w_in, sc_conv_w, sc_w_out):
    bp, tp = x_prompt.shape[0], x_prompt.shape[1]
    bs, ts = x_sample.shape[0], x_sample.shape[1]
    dtype = x_prompt.dtype
    xp = jnp.concatenate([jnp.broadcast_to(meta_tokens.astype(dtype)[None], (bp, N_META, D_MODEL)), x_prompt], 1)
    xs = x_sample
    pos_p = jnp.arange(N_META + tp)
    pos_s = PAST_LEN + jnp.arange(ts)
    p_ckv, p_kr, p_S, p_conv, p_sc = [], [], [], [], []
    s_ckv, s_kr, s_S, s_conv, s_sc = [], [], [], [], []
    for l in range(DEPTH):
        xp = half_ffn(xp, ffn1_norm[l], ffn1_w_gate[l], ffn1_w_up[l], ffn1_w_down[l])
        xs = half_ffn(xs, ffn1_norm[l], ffn1_w_gate[l], ffn1_w_up[l], ffn1_w_down[l])
        hp, hs = rms_norm(xp, mix_norm[l]), rms_norm(xs, mix_norm[l])
        if l % 2 == 0:
            i = l // 2
            wa = (ab_w_in[i], ab_w_out[i], gdn_conv_w[i], gdn_A_log[i], gdn_dt_bias[i], gdn_o_norm[i],
                  mla_q_norm[i], mla_w_uq[i], mla_kv_norm[i], mla_w_ukv[i], mla_qn_norm[i], mla_qr_norm[i],
                  mla_kn_norm[i], mla_kr_norm[i])
            S0p = jnp.zeros((bp, GDN_HEADS, GDN_DK, GDN_DV), dtype)
            c0p = jnp.zeros((bp, GDN_CONV - 1, GDN_QKV), dtype)
            yp, ckv, kr, S, cv = mixer_ab(hp, pos_p, S0p, c0p, None, None, *wa)
            p_ckv.append(ckv); p_kr.append(kr); p_S.append(S); p_conv.append(cv)
            ys, ckv, kr, S, cv = mixer_ab(hs, pos_s, state_gdn_S[i], state_gdn_conv[i], cache_mla_ckv[i],
                                          cache_mla_krope[i], *wa)
            s_ckv.append(ckv); s_kr.append(kr); s_S.append(S); s_conv.append(cv)
        else:
            i = l // 2
            yp, cp = short_conv_mixer(hp, jnp.zeros((bp, SC_WIDTH - 1, D_MODEL), dtype), sc_w_in[i], sc_conv_w[i], sc_w_out[i])
            ys, cs = short_conv_mixer(hs, state_sconv[i], sc_w_in[i], sc_conv_w[i], sc_w_out[i])
            p_sc.append(cp); s_sc.append(cs)
        xp, xs = xp + yp, xs + ys
        xp = half_ffn(xp, ffn2_norm[l], ffn2_w_gate[l], ffn2_w_up[l], ffn2_w_down[l])
        xs = half_ffn(xs, ffn2_norm[l], ffn2_w_gate[l], ffn2_w_up[l], ffn2_w_down[l])
    y_prompt = xp[:, N_META:]
    y_sample = xs
    return (y_prompt, y_sample,
            jnp.stack(p_ckv), jnp.stack(p_kr), jnp.stack(p_S), jnp.stack(p_conv), jnp.stack(p_sc),
            jnp.stack(s_ckv), jnp.stack(s_kr), jnp.stack(s_S), jnp.stack(s_conv), jnp.stack(s_sc))
```

```python
import functools

import jax
import jax.numpy as jnp
from jax import lax
from jax.experimental import pallas as pl
from jax.experimental.pallas import tpu as pltpu

F32 = jnp.float32
BF16 = jnp.bfloat16
HIGHEST = lax.Precision.HIGHEST

D_MODEL = 1024
D_FF = 2816
CHUNK = 64
CHUNK_SHIFT = 6
N_META = 16
EPS = 1e-6
GDN_HEADS = 4
GDN_DK = 128
GDN_DV = 128
GDN_QKV = GDN_HEADS * (2 * GDN_DK + GDN_DV)
GDN_Z = GDN_HEADS * GDN_DV
MLA_HEADS = 8
MLA_NOPE = 64
MLA_ROPE = 32
MLA_V = 64
MLA_Q_RANK = 384
MLA_KV_RANK = 256
ROPE_THETA = 10000.0
PAST_LEN = 1024
LANES = 128
HALO = 8
MLA_PAD = MLA_HEADS * LANES
FF_BLOCK = 256
FF_CHUNKS = D_FF // FF_BLOCK
VMEM_LIMIT = 60000 * 1024
NEG_INF = float("-inf")

IN_QKV = 0
IN_Z = IN_QKV + GDN_QKV
IN_AB = IN_Z + GDN_Z
IN_CQ = IN_AB + LANES
IN_CKV = IN_CQ + MLA_Q_RANK
IN_KR = IN_CKV + MLA_KV_RANK
IN_TOTAL = IN_KR + LANES


def _params(n_axes):
    return pltpu.CompilerParams(dimension_semantics=("arbitrary",) * n_axes, vmem_limit_bytes=VMEM_LIMIT)


def _rms(x, g):
    return x * lax.rsqrt(jnp.mean(x * x, -1, keepdims=True) + EPS) * g


def _sigmoid(x):
    return 1.0 / (1.0 + jnp.exp(-x))


def _silu(x):
    return x * _sigmoid(x)


def _softplus(x):
    return jnp.maximum(x, 0.0) + jnp.log1p(jnp.exp(-jnp.abs(x)))


def _dot(a, b):
    return jnp.dot(a, b, preferred_element_type=F32)


def _dot_nt(a, b):
    return lax.dot_general(a, b, (((1,), (1,)), ((), ())), preferred_element_type=F32)


def _dot_tn(a, b):
    return lax.dot_general(a, b, (((0,), (0,)), ((), ())), preferred_element_type=F32)


def _dot_hi(a, b):
    return jnp.dot(a, b, precision=HIGHEST, preferred_element_type=F32)


def _full_spec(shape):
    zeros = (0,) * len(shape)
    return pl.BlockSpec(shape, lambda *_: zeros)


def _row_tile(rows, want):
    return want if rows % want == 0 else rows


def _ffn_body(x_ref, g_ref, wg_ref, wu_ref, wd_ref, o_ref, acc_ref):
    x = x_ref[...]
    h = _rms(x, g_ref[...]).astype(BF16)
    for c in range(FF_CHUNKS):
        gate = _dot(h, wg_ref[c])
        up = _dot(h, wu_ref[c])
        act = (_silu(gate) * up).astype(BF16)
        down = _dot(act, wd_ref[c])
        if c == 0:
            acc_ref[...] = down
        else:
            acc_ref[...] += down
    o_ref[...] = x + 0.5 * acc_ref[...]


def _ffn(x, w):
    rows = x.shape[0]
    tm = _row_tile(rows, 512)
    g, wg, wu, wd = w
    return pl.pallas_call(
        _ffn_body,
        grid=(rows // tm,),
        in_specs=[
            pl.BlockSpec((tm, D_MODEL), lambda i: (i, 0)),
            _full_spec(g.shape),
            pl.BlockSpec(wg.shape, lambda i: (0, 0, 0), pipeline_mode=pl.Buffered(1)),
            pl.BlockSpec(wu.shape, lambda i: (0, 0, 0), pipeline_mode=pl.Buffered(1)),
            pl.BlockSpec(wd.shape, lambda i: (0, 0, 0), pipeline_mode=pl.Buffered(1)),
        ],
        out_specs=pl.BlockSpec((tm, D_MODEL), lambda i: (i, 0)),
        out_shape=jax.ShapeDtypeStruct((rows, D_MODEL), F32),
        scratch_shapes=[pltpu.VMEM((tm, D_MODEL), F32)],
        compiler_params=_params(1),
        name="half_ffn",
    )(x, g, wg, wu, wd)


def _ffn_weights(norm, w_gate, w_up, w_down):
    wg = w_gate.astype(BF16).reshape(D_MODEL, FF_CHUNKS, FF_BLOCK).transpose(1, 0, 2)
    wu = w_up.astype(BF16).reshape(D_MODEL, FF_CHUNKS, FF_BLOCK).transpose(1, 0, 2)
    wd = w_down.astype(BF16).reshape(FF_CHUNKS, FF_BLOCK, D_MODEL)
    return norm.reshape(1, D_MODEL), wg, wu, wd


def _inproj_body(x_ref, g_ref, w_ref, qkv_ref, z_ref, ab_ref, cq_ref, ckv_ref, kr_ref):
    hn = _rms(x_ref[...], g_ref[...]).astype(BF16)
    qkv_ref[...] = _dot(hn, w_ref[:, IN_QKV:IN_Z])
    z_ref[...] = _dot(hn, w_ref[:, IN_Z:IN_AB])
    ab_ref[...] = _dot(hn, w_ref[:, IN_AB:IN_CQ])
    cq_ref[...] = _dot(hn, w_ref[:, IN_CQ:IN_CKV])
    ckv_ref[...] = _dot(hn, w_ref[:, IN_CKV:IN_KR])
    kr_ref[...] = _dot(hn, w_ref[:, IN_KR:IN_TOTAL])


def _inproj(x, g, w):
    rows = x.shape[0]
    tm = _row_tile(rows, 512)
    widths = (GDN_QKV, GDN_Z, LANES, MLA_Q_RANK, MLA_KV_RANK, LANES)
    return pl.pallas_call(
        _inproj_body,
        grid=(rows // tm,),
        in_specs=[pl.BlockSpec((tm, D_MODEL), lambda i: (i, 0)), _full_spec(g.shape), _full_spec(w.shape)],
        out_specs=[pl.BlockSpec((tm, n), lambda i: (i, 0)) for n in widths],
        out_shape=[jax.ShapeDtypeStruct((rows, n), F32) for n in widths],
        compiler_params=_params(1),
        name="mixer_inproj",
    )(x, g, w)


def _unit_lower_inverse(n, eye_f, m16, m32):
    nd = jnp.where(m16, n, 0.0)
    n2 = _dot_hi(nd, nd)
    n4 = _dot_hi(n2, n2)
    n8 = _dot_hi(n4, n4)
    t = _dot_hi(eye_f - nd, eye_f + n2)
    t = _dot_hi(t, eye_f + n4)
    t = _dot_hi(t, eye_f + n8)
    off = jnp.where(jnp.logical_and(m32, jnp.logical_not(m16)), n, 0.0)
    t = t - _dot_hi(t, _dot_hi(off, t))
    off = jnp.where(m32, 0.0, n)
    t = t - _dot_hi(t, _dot_hi(off, t))
    return t


def _gdn_body(qkv_ref, z_ref, ab_ref, s0_ref, cprev_ref, cw_ref, alog_ref, dtb_ref, on_ref,
              og_ref, sout_ref, cout_ref, s_scr, halo_scr, *, rows, n_valid):
    t = pl.program_id(1)
    C = CHUNK

    @pl.when(t == 0)
    def _():
        s_scr[...] = s0_ref[...]
        halo_scr[...] = cprev_ref[...]

    x = qkv_ref[...]
    halo = halo_scr[...]
    row = lax.broadcasted_iota(jnp.int32, (rows, 1), 0)

    def shifted(k):
        r = pltpu.roll(x, k, 0)
        for j in range(k):
            r = jnp.where(row == j, halo[HALO - k + j:HALO - k + j + 1, :], r)
        return r

    cw = cw_ref[...]
    y = shifted(3) * cw[0:1] + shifted(2) * cw[1:2] + shifted(1) * cw[2:3] + x * cw[3:4]
    last_rows = x[n_valid - HALO:n_valid, :]
    halo_scr[...] = last_rows
    y = _silu(y)

    ab = ab_ref[...]
    g_all = -jnp.exp(alog_ref[...]) * _softplus(ab + dtb_ref[...])
    b_all = _sigmoid(ab)
    if n_valid < rows:
        valid = row < n_valid
        y = jnp.where(valid, y, 0.0)
        g_all = jnp.where(valid, g_all, 0.0)
        b_all = jnp.where(valid, b_all, 0.0)

    ii = lax.broadcasted_iota(jnp.int32, (C, C), 0)
    jj = lax.broadcasted_iota(jnp.int32, (C, C), 1)
    causal = ii >= jj
    strict = ii > jj
    eye = ii == jj
    eye_f = eye.astype(F32)
    tril_f = causal.astype(F32)
    ones_f = jnp.ones((C, C), F32)
    m16 = (ii >> 4) == (jj >> 4)
    m32 = (ii >> 5) == (jj >> 5)
    on = on_ref[...]

    for h in range(GDN_HEADS):
        s = s_scr[h]
        for c in range(rows // C):
            sl = slice(c * C, (c + 1) * C)
            qh = y[sl, h * GDN_DK:(h + 1) * GDN_DK]
            kh = y[sl, GDN_HEADS * GDN_DK + h * GDN_DK:GDN_HEADS * GDN_DK + (h + 1) * GDN_DK]
            vh = y[sl, 2 * GDN_HEADS * GDN_DK + h * GDN_DV:2 * GDN_HEADS * GDN_DK + (h + 1) * GDN_DV]
            qh = qh * lax.rsqrt(jnp.sum(qh * qh, -1, keepdims=True) + EPS) * (GDN_DK ** -0.5)
            kh = kh * lax.rsqrt(jnp.sum(kh * kh, -1, keepdims=True) + EPS)
            g = g_all[sl, h:h + 1]
            beta = b_all[sl, GDN_HEADS + h:GDN_HEADS + h + 1]
            gcum = _dot_hi(tril_f, jnp.broadcast_to(g, (C, LANES)))
            gcol = gcum[:, :C]
            grow = _dot_hi(ones_f, jnp.where(eye, gcol, 0.0))
            dec_causal = jnp.exp(jnp.where(causal, gcol - grow, NEG_INF))
            dec_strict = jnp.where(strict, dec_causal, 0.0)
            kb = kh.astype(BF16)
            kk = _dot_nt(kb, kb)
            qk = (_dot_nt(qh.astype(BF16), kb) * dec_causal).astype(BF16)
            tinv = _unit_lower_inverse(beta * kk * dec_strict, eye_f, m16, m32).astype(BF16)
            eg = jnp.exp(gcum)
            u = _dot(tinv, (beta * vh).astype(BF16))
            wk = _dot(tinv, ((beta * eg) * kh).astype(BF16)).astype(BF16)
            qd = (qh * eg).astype(BF16)
            g_last = gcum[C - 1:C, :]
            kt = (kh * jnp.exp(g_last - gcum)).astype(BF16)

            sb = s.astype(BF16)
            w = (u - _dot(wk, sb)).astype(BF16)
            o = _dot(qd, sb) + _dot(qk, w)
            s = s * jnp.exp(g_last) + _dot_tn(kt, w)

            o = _rms(o, on) * _silu(z_ref[sl, h * GDN_DV:(h + 1) * GDN_DV])
            og_ref[sl, h * GDN_DV:(h + 1) * GDN_DV] = o.astype(BF16)
        s_scr[h] = s

    @pl.when(t == pl.num_programs(1) - 1)
    def _():
        sout_ref[...] = s_scr[...]
        cout_ref[...] = last_rows


def _gdn(qkv, z, ab, s0, cprev, w, *, rows, n_valid=None):
    b, t_len, _ = qkv.shape
    n_valid = rows if n_valid is None else n_valid
    conv_w, a_log, dt_bias, o_norm = w
    shared = s0.shape[0] == 1 and b > 1
    bsel = (lambda bi: 0) if shared else (lambda bi: bi)
    return pl.pallas_call(
        functools.partial(_gdn_body, rows=rows, n_valid=n_valid),
        grid=(b, t_len // rows),
        in_specs=[
            pl.BlockSpec((None, rows, GDN_QKV), lambda bi, ti: (bi, ti, 0)),
            pl.BlockSpec((None, rows, GDN_Z), lambda bi, ti: (bi, ti, 0)),
            pl.BlockSpec((None, rows, LANES), lambda bi, ti: (bi, ti, 0)),
            pl.BlockSpec((None, GDN_HEADS, GDN_DK, GDN_DV), lambda bi, ti: (bsel(bi), 0, 0, 0)),
            pl.BlockSpec((None, HALO, GDN_QKV), lambda bi, ti: (bsel(bi), 0, 0)),
            _full_spec(conv_w.shape), _full_spec(a_log.shape), _full_spec(dt_bias.shape), _full_spec(o_norm.shape),
        ],
        out_specs=[
            pl.BlockSpec((None, rows, GDN_Z), lambda bi, ti: (bi, ti, 0)),
            pl.BlockSpec((None, GDN_HEADS, GDN_DK, GDN_DV), lambda bi, ti: (bi, 0, 0, 0)),
            pl.BlockSpec((None, HALO, GDN_QKV), lambda bi, ti: (bi, 0, 0)),
        ],
        out_shape=[
            jax.ShapeDtypeStruct((b, t_len, GDN_Z), BF16),
            jax.ShapeDtypeStruct((b, GDN_HEADS, GDN_DK, GDN_DV), F32),
            jax.ShapeDtypeStruct((b, HALO, GDN_QKV), F32),
        ],
        scratch_shapes=[pltpu.VMEM((GDN_HEADS, GDN_DK, GDN_DV), F32), pltpu.VMEM((HALO, GDN_QKV), F32)],
        compiler_params=_params(2),
        name="gdn_mixer",
    )(qkv, z, ab, s0, cprev, conv_w, a_log, dt_bias, o_norm)


def _lane_masks():
    lane = lax.broadcasted_iota(jnp.int32, (1, LANES), 1)
    nope = lane < MLA_NOPE
    rope_a = jnp.logical_and(lane >= MLA_NOPE, lane < MLA_NOPE + MLA_ROPE // 2)
    rope_b = jnp.logical_and(lane >= MLA_NOPE + MLA_ROPE // 2, lane < MLA_NOPE + MLA_ROPE)
    return nope, rope_a, rope_b


def _rope_block(x, cos, sin, rope_a, rope_b):
    half = MLA_ROPE // 2
    rot = jnp.where(rope_a, -pltpu.roll(x, LANES - half, 1), jnp.where(rope_b, pltpu.roll(x, half, 1), 0.0))
    return x * cos + rot * sin


def _k_heads(k, kr_new, gk, k_out):
    for h in range(MLA_HEADS):
        x = k[:, h * LANES:(h + 1) * LANES]
        inv = lax.rsqrt(jnp.sum(x * x, -1, keepdims=True) / MLA_NOPE + EPS)
        k_out[:, h * LANES:(h + 1) * LANES] = (x * inv * gk + kr_new).astype(BF16)


def _mla_prep_body(cq_ref, ckv_ref, kr_ref, cos_ref, sin_ref, qn_ref, wuq_ref, kvn_ref, wk_ref, wv_ref,
                   gq_ref, gk_ref, q_out, k_out, v_out, ckv_out, kr_out):
    nope, rope_a, rope_b = _lane_masks()
    rope_l = jnp.logical_or(rope_a, rope_b)
    cos = cos_ref[...]
    sin = sin_ref[...]
    gq = gq_ref[...]
    gk = gk_ref[...]
    scale = (MLA_NOPE + MLA_ROPE) ** -0.5

    cqn = _rms(cq_ref[...], qn_ref[...]).astype(BF16)
    q = _dot(cqn, wuq_ref[...])
    for h in range(MLA_HEADS):
        x = q[:, h * LANES:(h + 1) * LANES]
        x2 = x * x
        sn = jnp.sum(jnp.where(nope, x2, 0.0), -1, keepdims=True)
        sr = jnp.sum(jnp.where(rope_l, x2, 0.0), -1, keepdims=True)
        inv = jnp.where(nope, lax.rsqrt(sn / MLA_NOPE + EPS), lax.rsqrt(sr / MLA_ROPE + EPS))
        x = _rope_block(x * inv * gq, cos, sin, rope_a, rope_b) * scale
        q_out[:, h * LANES:(h + 1) * LANES] = x.astype(BF16)

    ckvn = _rms(ckv_ref[...], kvn_ref[...])
    ckv_out[...] = ckvn
    cb = ckvn.astype(BF16)
    krp = kr_ref[...]
    krn = krp * lax.rsqrt(jnp.sum(krp * krp, -1, keepdims=True) / MLA_ROPE + EPS) * jnp.where(nope, 0.0, gk)
    kr_new = _rope_block(krn, cos, sin, rope_a, rope_b)
    kr_out[...] = kr_new
    v_out[...] = _dot(cb, wv_ref[...]).astype(BF16)
    _k_heads(_dot(cb, wk_ref[...]), kr_new, jnp.where(nope, gk, 0.0), k_out)


def _mla_prep(cq, ckv, krp, cos, sin, w, *, table_tiles):
    rows = cq.shape[0]
    tm = _row_tile(rows, 512)
    q_norm, w_uq, kv_norm, w_k, w_v, gq, gk = w
    row_spec = lambda n: pl.BlockSpec((tm, n), lambda i: (i, 0))
    tab_spec = pl.BlockSpec((tm, LANES), lambda i: (i % table_tiles, 0))
    return pl.pallas_call(
        _mla_prep_body,
        grid=(rows // tm,),
        in_specs=[row_spec(MLA_Q_RANK), row_spec(MLA_KV_RANK), row_spec(LANES), tab_spec, tab_spec,
                  _full_spec(q_norm.shape), _full_spec(w_uq.shape), _full_spec(kv_norm.shape),
                  _full_spec(w_k.shape), _full_spec(w_v.shape), _full_spec(gq.shape), _full_spec(gk.shape)],
        out_specs=[row_spec(MLA_PAD), row_spec(MLA_PAD), row_spec(MLA_PAD), row_spec(MLA_KV_RANK), row_spec(LANES)],
        out_shape=[jax.ShapeDtypeStruct((rows, MLA_PAD), BF16), jax.ShapeDtypeStruct((rows, MLA_PAD), BF16),
                   jax.ShapeDtypeStruct((rows, MLA_PAD), BF16), jax.ShapeDtypeStruct((rows, MLA_KV_RANK), F32),
                   jax.ShapeDtypeStruct((rows, LANES), F32)],
        compiler_params=_params(1),
        name="mla_prep",
    )(cq, ckv, krp, cos, sin, q_norm, w_uq, kv_norm, w_k, w_v, gq, gk)


def _kv_up_body(ckv_ref, kr_ref, wk_ref, wv_ref, gk_ref, k_out, v_out):
    nope, _, _ = _lane_masks()
    cb = ckv_ref[...].astype(BF16)
    v_out[...] = _dot(cb, wv_ref[...]).astype(BF16)
    _k_heads(_dot(cb, wk_ref[...]), kr_ref[...], jnp.where(nope, gk_ref[...], 0.0), k_out)


def _kv_up(ckv, krp, w_k, w_v, gk):
    rows = ckv.shape[0]
    tm = _row_tile(rows, 512)
    row_spec = lambda n: pl.BlockSpec((tm, n), lambda i: (i, 0))
    return pl.pallas_call(
        _kv_up_body,
        grid=(rows // tm,),
        in_specs=[row_spec(MLA_KV_RANK), row_spec(LANES), _full_spec(w_k.shape), _full_spec(w_v.shape),
                  _full_spec(gk.shape)],
        out_specs=[row_spec(MLA_PAD), row_spec(MLA_PAD)],
        out_shape=[jax.ShapeDtypeStruct((rows, MLA_PAD), BF16), jax.ShapeDtypeStruct((rows, MLA_PAD), BF16)],
        compiler_params=_params(1),
        name="mla_kv_up",
    )(ckv, krp, w_k, w_v, gk)


def _attn_body(*refs, tq, tk, causal, nk_valid, has_ext):
    if has_ext:
        q_ref, k_ref, v_ref, ke_ref, ve_ref, o_ref, m_scr, l_scr, acc_scr = refs
    else:
        q_ref, k_ref, v_ref, o_ref, m_scr, l_scr, acc_scr = refs
    i = pl.program_id(1)
    j = pl.program_id(2)
    nj = pl.num_programs(2)
    low = lax.broadcasted_iota(jnp.int32, (1, LANES), 1) < MLA_V

    def update(k_of, v_of, bias):
        for h in range(MLA_HEADS):
            hs = slice(h * LANES, (h + 1) * LANES)
            s = _dot_nt(q_ref[:, hs], k_of(hs))
            if bias is not None:
                s = s + bias
            m_old = m_scr[h][:, :1]
            m_new = jnp.maximum(m_old, jnp.max(s, -1, keepdims=True))
            alpha = jnp.exp(m_old - m_new)
            p = jnp.exp(s - m_new)
            l_new = alpha * l_scr[h][:, :1] + jnp.sum(p, -1, keepdims=True)
            m_scr[h] = jnp.broadcast_to(m_new, (tq, LANES))
            l_scr[h] = jnp.broadcast_to(l_new, (tq, LANES))
            mine = low if h % 2 == 0 else jnp.logical_not(low)
            acc_scr[h // 2] = acc_scr[h // 2] * jnp.where(mine, alpha, 1.0) + _dot(p.astype(BF16), v_of(hs))

    @pl.when(j == 0)
    def _():
        m_scr[...] = jnp.full(m_scr.shape, NEG_INF, F32)
        l_scr[...] = jnp.zeros(l_scr.shape, F32)
        acc_scr[...] = jnp.zeros(acc_scr.shape, F32)
        if has_ext:
            update(lambda hs: ke_ref[:, hs], lambda hs: ve_ref[:, hs], None)

    if causal:
        n_needed = ((i + 1) * tq + tk - 1) // tk
    else:
        n_needed = nj

    @pl.when(j < n_needed)
    def _():
        bias = None
        kpos = j * tk + lax.broadcasted_iota(jnp.int32, (1, tk), 1)
        if causal:
            qpos = i * tq + lax.broadcasted_iota(jnp.int32, (tq, 1), 0)
            bias = jnp.where((kpos >> CHUNK_SHIFT) <= (qpos >> CHUNK_SHIFT), 0.0, NEG_INF)
        elif nk_valid is not None:
            bias = jnp.where(kpos < nk_valid, 0.0, NEG_INF)
        update(lambda hs: k_ref[:, hs], lambda hs: v_ref[:, hs], bias)

    @pl.when(j == nj - 1)
    def _():
        for p in range(MLA_HEADS // 2):
            l_pair = jnp.where(low, l_scr[2 * p], l_scr[2 * p + 1])
            o_ref[:, p * LANES:(p + 1) * LANES] = (acc_scr[p] / l_pair).astype(BF16)


def _attention(q, k, v, ext, *, tq, tk, causal=False, nk_valid=None):
    b, nq, _ = q.shape
    nk = k.shape[1]
    nj = nk // tk
    if causal:
        kv_map = lambda bi, i, j: (bi, jnp.minimum(j, ((i + 1) * tq + tk - 1) // tk - 1), 0)
    else:
        kv_map = lambda bi, i, j: (bi, j, 0)
    in_specs = [
        pl.BlockSpec((None, tq, MLA_PAD), lambda bi, i, j: (bi, i, 0)),
        pl.BlockSpec((None, tk, MLA_PAD), kv_map),
        pl.BlockSpec((None, tk, MLA_PAD), kv_map),
    ]
    args = [q, k, v]
    if ext is not None:
        ne = ext[0].shape[1]
        shared = ext[0].shape[0] == 1 and b > 1
        emap = (lambda bi, i, j: (0, 0, 0)) if shared else (lambda bi, i, j: (bi, 0, 0))
        in_specs += [pl.BlockSpec((None, ne, MLA_PAD), emap), pl.BlockSpec((None, ne, MLA_PAD), emap)]
        args += list(ext)
    return pl.pallas_call(
        functools.partial(_attn_body, tq=tq, tk=tk, causal=causal, nk_valid=nk_valid, has_ext=ext is not None),
        grid=(b, nq // tq, nj),
        in_specs=in_specs,
        out_specs=pl.BlockSpec((None, tq, MLA_HEADS * MLA_V), lambda bi, i, j: (bi, i, 0)),
        out_shape=jax.ShapeDtypeStruct((b, nq, MLA_HEADS * MLA_V), BF16),
        scratch_shapes=[pltpu.VMEM((MLA_HEADS, tq, LANES), F32), pltpu.VMEM((MLA_HEADS, tq, LANES), F32),
                        pltpu.VMEM((MLA_HEADS // 2, tq, LANES), F32)],
        compiler_params=_params(3),
        name="mla_attention",
    )(*args)


def _outproj_body(x_ref, og_ref, om_ref, w_ref, o_ref):
    o_ref[...] = x_ref[...] + _dot(og_ref[...], w_ref[:GDN_Z, :]) + _dot(om_ref[...], w_ref[GDN_Z:, :])


def _outproj(x, og, om, w):
    rows = x.shape[0]
    tm = _row_tile(rows, 512)
    half = GDN_Z
    return pl.pallas_call(
        _outproj_body,
        grid=(rows // tm,),
        in_specs=[pl.BlockSpec((tm, D_MODEL), lambda i: (i, 0)), pl.BlockSpec((tm, half), lambda i: (i, 0)),
                  pl.BlockSpec((tm, half), lambda i: (i, 0)), _full_spec(w.shape)],
        out_specs=pl.BlockSpec((tm, D_MODEL), lambda i: (i, 0)),
        out_shape=jax.ShapeDtypeStruct((rows, D_MODEL), F32),
        compiler_params=_params(1),
        name="mixer_outproj",
    )(x, og, om, w)


def _sconv_body(x_ref, g_ref, win_ref, cw_ref, wout_ref, prev_ref, o_ref, pout_ref, halo_scr, *, rows):
    t = pl.program_id(1)

    @pl.when(t == 0)
    def _():
        halo_scr[...] = prev_ref[...]

    x = x_ref[...]
    hn = _rms(x, g_ref[...]).astype(BF16)
    bg = _dot(hn, win_ref[:, :D_MODEL])
    u = _dot(hn, win_ref[:, D_MODEL:2 * D_MODEL]) * _dot(hn, win_ref[:, 2 * D_MODEL:])
    halo = halo_scr[...]
    row = lax.broadcasted_iota(jnp.int32, (rows, 1), 0)
    u1 = jnp.where(row == 0, halo[HALO - 1:HALO, :], pltpu.roll(u, 1, 0))
    u2 = jnp.where(row == 0, halo[HALO - 2:HALO - 1, :],
                   jnp.where(row == 1, halo[HALO - 1:HALO, :], pltpu.roll(u, 2, 0)))
    cw = cw_ref[...]
    y = u2 * cw[0:1] + u1 * cw[1:2] + u * cw[2:3]
    last_rows = u[rows - HALO:rows, :]
    halo_scr[...] = last_rows
    o_ref[...] = x + _dot((bg * y).astype(BF16), wout_ref[...])

    @pl.when(t == pl.num_programs(1) - 1)
    def _():
        pout_ref[...] = last_rows


def _sconv(x, prev, w, *, rows):
    b, t_len, _ = x.shape
    g, w_in, conv_w, w_out = w
    shared = prev.shape[0] == 1 and b > 1
    bsel = (lambda bi: 0) if shared else (lambda bi: bi)
    return pl.pallas_call(
        functools.partial(_sconv_body, rows=rows),
        grid=(b, t_len // rows),
        in_specs=[pl.BlockSpec((None, rows, D_MODEL), lambda bi, ti: (bi, ti, 0)), _full_spec(g.shape),
                  _full_spec(w_in.shape), _full_spec(conv_w.shape), _full_spec(w_out.shape),
                  pl.BlockSpec((None, HALO, D_MODEL), lambda bi, ti: (bsel(bi), 0, 0))],
        out_specs=[pl.BlockSpec((None, rows, D_MODEL), lambda bi, ti: (bi, ti, 0)),
                   pl.BlockSpec((None, HALO, D_MODEL), lambda bi, ti: (bi, 0, 0))],
        out_shape=[jax.ShapeDtypeStruct((b, t_len, D_MODEL), F32), jax.ShapeDtypeStruct((b, HALO, D_MODEL), F32)],
        scratch_shapes=[pltpu.VMEM((HALO, D_MODEL), F32)],
        compiler_params=_params(2),
        name="sconv_mixer",
    )(x, g, w_in, conv_w, w_out, prev)


def _pad_lanes(a, before, total):
    return jnp.pad(a, [(0, 0)] * (a.ndim - 1) + [(before, total - before - a.shape[-1])])


def _rope_tables(pos):
    half = MLA_ROPE // 2
    inv = ROPE_THETA ** (-jnp.arange(half, dtype=F32) / half)
    ang = pos.astype(F32)[:, None] * inv
    cos, sin = jnp.cos(ang), jnp.sin(ang)
    n = pos.shape[0]
    tail = LANES - MLA_NOPE - MLA_ROPE
    cos_l = jnp.concatenate([jnp.ones((n, MLA_NOPE), F32), cos, cos, jnp.ones((n, tail), F32)], -1)
    sin_l = jnp.concatenate([jnp.zeros((n, MLA_NOPE), F32), sin, sin, jnp.zeros((n, tail), F32)], -1)
    return cos_l, sin_l


def _tail_rows(a, n):
    return jnp.pad(a, ((0, 0), (HALO - n, 0), (0, 0)))


def kernel(x_prompt, x_sample, cache_mla_ckv, cache_mla_krope, state_gdn_S, state_gdn_conv, state_sconv, meta_tokens, ffn1_norm, ffn1_w_gate, ffn1_w_up, ffn1_w_down, ffn2_norm, ffn2_w_gate, ffn2_w_up, ffn2_w_down, mix_norm, ab_w_in, ab_w_out, gdn_conv_w, gdn_A_log, gdn_dt_bias, gdn_o_norm, mla_q_norm, mla_w_uq, mla_kv_norm, mla_w_ukv, mla_qn_norm, mla_qr_norm, mla_kn_norm, mla_kr_norm, sc_w_in, sc_conv_w, sc_w_out):
    bp, tp, _ = x_prompt.shape
    bs, ts, _ = x_sample.shape
    n_frames = bp * tp
    n_sample = bs * ts

    ffn1 = [_ffn_weights(ffn1_norm[l], ffn1_w_gate[l], ffn1_w_up[l], ffn1_w_down[l]) for l in range(2)]
    ffn2 = [_ffn_weights(ffn2_norm[l], ffn2_w_gate[l], ffn2_w_up[l], ffn2_w_down[l]) for l in range(2)]
    w_in = ab_w_in[0]
    s1 = GDN_QKV + GDN_Z
    c0 = s1 + 2 * GDN_HEADS
    w_in_packed = jnp.concatenate([
        w_in[:, :s1],
        _pad_lanes(w_in[:, s1:c0], 0, LANES),
        w_in[:, c0:c0 + MLA_Q_RANK + MLA_KV_RANK],
        _pad_lanes(w_in[:, c0 + MLA_Q_RANK + MLA_KV_RANK:], MLA_NOPE, LANES),
    ], -1).astype(BF16)
    mix_g = mix_norm.reshape(2, 1, D_MODEL)
    gdn_w = (gdn_conv_w[0], _pad_lanes(gdn_A_log[0][None], 0, LANES), _pad_lanes(gdn_dt_bias[0][None], 0, LANES),
             gdn_o_norm[0][None])
    qdim = MLA_NOPE + MLA_ROPE
    w_uq = _pad_lanes(mla_w_uq[0].reshape(MLA_Q_RANK, MLA_HEADS, qdim), 0, LANES).reshape(MLA_Q_RANK, MLA_PAD)
    w_ukv = mla_w_ukv[0].reshape(MLA_KV_RANK, MLA_HEADS, MLA_NOPE + MLA_V)
    w_k = _pad_lanes(w_ukv[..., :MLA_NOPE], 0, LANES).reshape(MLA_KV_RANK, MLA_PAD)
    w_v_even = _pad_lanes(w_ukv[..., MLA_NOPE:], 0, LANES)
    w_v_odd = _pad_lanes(w_ukv[..., MLA_NOPE:], MLA_V, LANES)
    odd_head = (jnp.arange(MLA_HEADS) % 2 == 1)[None, :, None]
    w_v = jnp.where(odd_head, w_v_odd, w_v_even).reshape(MLA_KV_RANK, MLA_PAD)
    gq = _pad_lanes(jnp.concatenate([mla_qn_norm[0], mla_qr_norm[0]])[None], 0, LANES)
    gk = _pad_lanes(jnp.concatenate([mla_kn_norm[0], mla_kr_norm[0]])[None], 0, LANES)
    mla_w = (mla_q_norm[0][None], w_uq.astype(BF16), mla_kv_norm[0][None], w_k.astype(BF16), w_v.astype(BF16), gq, gk)
    w_out = ab_w_out[0].astype(BF16)
    sc_w = (mix_g[1], sc_w_in[0].astype(BF16), sc_conv_w[0], sc_w_out[0].astype(BF16))

    xf = x_prompt.reshape(n_frames, D_MODEL)
    xs = jnp.concatenate([x_sample.reshape(n_sample, D_MODEL), meta_tokens.astype(F32)], 0)
    cos_f, sin_f = _rope_tables(N_META + jnp.arange(tp))
    pos_small = jnp.concatenate([jnp.tile(PAST_LEN + jnp.arange(ts), bs), jnp.arange(N_META)])
    cos_s, sin_s = _rope_tables(pos_small)

    xf = _ffn(xf, ffn1[0])
    xs = _ffn(xs, ffn1[0])
    qkv_f, z_f, ab_f, cq_f, ckv_f, kr_f = _inproj(xf, mix_g[0], w_in_packed)
    qkv_s, z_s, ab_s, cq_s, ckv_s, kr_s = _inproj(xs, mix_g[0], w_in_packed)

    pad_meta = lambda a: jnp.pad(a[n_sample:], ((0, CHUNK - N_META), (0, 0)))[None]
    og_m, s_m, conv_m = _gdn(pad_meta(qkv_s), pad_meta(z_s), pad_meta(ab_s),
                             jnp.zeros((1, GDN_HEADS, GDN_DK, GDN_DV), F32), jnp.zeros((1, HALO, GDN_QKV), F32),
                             gdn_w, rows=CHUNK, n_valid=N_META)
    og_f, p_s, p_conv = _gdn(qkv_f.reshape(bp, tp, GDN_QKV), z_f.reshape(bp, tp, GDN_Z), ab_f.reshape(bp, tp, LANES),
                             s_m, conv_m, gdn_w, rows=2 * CHUNK)
    og_s, s_s, s_conv = _gdn(qkv_s[:n_sample].reshape(bs, ts, GDN_QKV), z_s[:n_sample].reshape(bs, ts, GDN_Z),
                             ab_s[:n_sample].reshape(bs, ts, LANES), state_gdn_S[0],
                             _tail_rows(state_gdn_conv[0], state_gdn_conv.shape[2]), gdn_w, rows=ts)

    q_f, k_f, v_f, ckvn_f, krn_f = _mla_prep(cq_f, ckv_f, kr_f, cos_f, sin_f, mla_w, table_tiles=tp // 512)
    q_s, k_s, v_s, ckvn_s, krn_s = _mla_prep(cq_s, ckv_s, kr_s, cos_s, sin_s, mla_w, table_tiles=1)
    k_c, v_c = _kv_up(cache_mla_ckv[0].reshape(bs * PAST_LEN, MLA_KV_RANK),
                      _pad_lanes(cache_mla_krope[0].reshape(bs * PAST_LEN, MLA_ROPE), MLA_NOPE, LANES),
                      mla_w[3], mla_w[4], gk)
    pad_rows = lambda a: jnp.pad(a[n_sample:], ((0, CHUNK - N_META), (0, 0)))[None]
    meta_kv = (k_s[n_sample:][None], v_s[n_sample:][None])
    om_f = _attention(q_f.reshape(bp, tp, MLA_PAD), k_f.reshape(bp, tp, MLA_PAD), v_f.reshape(bp, tp, MLA_PAD),
                      meta_kv, tq=256, tk=512, causal=True)
    om_s = _attention(q_s[:n_sample].reshape(bs, ts, MLA_PAD), k_c.reshape(bs, PAST_LEN, MLA_PAD),
                      v_c.reshape(bs, PAST_LEN, MLA_PAD),
                      (k_s[:n_sample].reshape(bs, ts, MLA_PAD), v_s[:n_sample].reshape(bs, ts, MLA_PAD)),
                      tq=ts, tk=512)
    om_m = _attention(pad_rows(q_s), pad_rows(k_s), pad_rows(v_s), None, tq=CHUNK, tk=CHUNK, nk_valid=N_META)

    og_small = jnp.concatenate([og_s.reshape(n_sample, GDN_Z), og_m[0, :N_META]], 0)
    om_small = jnp.concatenate([om_s.reshape(n_sample, GDN_Z), om_m[0, :N_META]], 0)
    xf = _outproj(xf, og_f.reshape(n_frames, GDN_Z), om_f.reshape(n_frames, GDN_Z), w_out)
    xs = _outproj(xs, og_small, om_small, w_out)
    xf = _ffn(xf, ffn2[0])
    xs = _ffn(xs, ffn2[0])

    xf = _ffn(xf, ffn1[1])
    xs = _ffn(xs, ffn1[1])
    n_sc = state_sconv.shape[2]
    xm, sc_m = _sconv(xs[n_sample:][None], jnp.zeros((1, HALO, D_MODEL), F32), sc_w, rows=N_META)
    xf3, p_sc = _sconv(xf.reshape(bp, tp, D_MODEL), sc_m, sc_w, rows=512)
    xs3, s_sc = _sconv(xs[:n_sample].reshape(bs, ts, D_MODEL), _tail_rows(state_sconv[0], n_sc), sc_w, rows=ts)
    xf = _ffn(xf3.reshape(n_frames, D_MODEL), ffn2[1])
    xs = _ffn(jnp.concatenate([xs3.reshape(n_sample, D_MODEL), xm[0]], 0), ffn2[1])

    y_prompt = xf.reshape(bp, tp, D_MODEL)
    y_sample = xs[:n_sample].reshape(bs, ts, D_MODEL)
    rope_lo, rope_hi = MLA_NOPE, MLA_NOPE + MLA_ROPE
    meta_ckv = jnp.broadcast_to(ckvn_s[n_sample:][None], (bp, N_META, MLA_KV_RANK))
    meta_kr = jnp.broadcast_to(krn_s[n_sample:, rope_lo:rope_hi][None], (bp, N_META, MLA_ROPE))
    p_ckv = jnp.concatenate([meta_ckv, ckvn_f.reshape(bp, tp, MLA_KV_RANK)], 1)
    p_kr = jnp.concatenate([meta_kr, krn_f[:, rope_lo:rope_hi].reshape(bp, tp, MLA_ROPE)], 1)
    n_gc = state_gdn_conv.shape[2]
    return (y_prompt, y_sample,
            p_ckv[None], p_kr[None], p_s[None], p_conv[:, HALO - n_gc:][None], p_sc[:, HALO - n_sc:][None],
            ckvn_s[:n_sample].reshape(bs, ts, MLA_KV_RANK)[None],
            krn_s[:n_sample, rope_lo:rope_hi].reshape(bs, ts, MLA_ROPE)[None],
            s_s[None], s_conv[:, HALO - n_gc:][None], s_sc[:, HALO - n_sc:][None])
```

```python
import functools

import jax
import jax.numpy as jnp
from jax import lax
from jax.experimental import pallas as pl
from jax.experimental.pallas import tpu as pltpu

F32 = jnp.float32
BF16 = jnp.bfloat16

D_MODEL = 1024
D_FF = 2816
CHUNK = 64
CHUNK_SHIFT = 6
N_META = 16
EPS = 1e-6
GDN_HEADS = 4
GDN_DK = 128
GDN_DV = 128
GDN_QKV = GDN_HEADS * (2 * GDN_DK + GDN_DV)
GDN_Z = GDN_HEADS * GDN_DV
MLA_HEADS = 8
MLA_NOPE = 64
MLA_ROPE = 32
MLA_V = 64
MLA_Q_RANK = 384
MLA_KV_RANK = 256
ROPE_THETA = 10000.0
PAST_LEN = 1024
LANES = 128
HALO = 8
MLA_PAD = MLA_HEADS * LANES
FF_BLOCK = 256
FF_CHUNKS = D_FF // FF_BLOCK
VMEM_LIMIT = 60000 * 1024
NEG_INF = float("-inf")
LOG2_E = 1.4426950408889634

IN_QKV = 0
IN_Z = IN_QKV + GDN_QKV
IN_AB = IN_Z + GDN_Z
IN_CQ = IN_AB + LANES
IN_CKV = IN_CQ + MLA_Q_RANK
IN_KR = IN_CKV + MLA_KV_RANK
IN_TOTAL = IN_KR + LANES


def _params(n_axes):
    return pltpu.CompilerParams(dimension_semantics=("arbitrary",) * n_axes, vmem_limit_bytes=VMEM_LIMIT)


def _rms(x, g):
    return x * lax.rsqrt(jnp.mean(x * x, -1, keepdims=True) + EPS) * g


def _sigmoid(x):
    return 1.0 / (1.0 + jnp.exp(-x))


def _silu(x):
    return x * _sigmoid(x)


def _softplus(x):
    return jnp.maximum(x, 0.0) + jnp.log1p(jnp.exp(-jnp.abs(x)))


def _dot(a, b):
    return jnp.dot(a, b, preferred_element_type=F32)


def _dot_nt(a, b):
    return lax.dot_general(a, b, (((1,), (1,)), ((), ())), preferred_element_type=F32)


def _dot_tn(a, b):
    return lax.dot_general(a, b, (((0,), (0,)), ((), ())), preferred_element_type=F32)


def _full_spec(shape):
    zeros = (0,) * len(shape)
    return pl.BlockSpec(shape, lambda *_: zeros)


def _row_tile(rows, want):
    return want if rows % want == 0 else rows


def _ffn_body(x_ref, g_ref, wg_ref, wu_ref, wd_ref, o_ref, acc_ref):
    x = x_ref[...]
    h = _rms(x, g_ref[...]).astype(BF16)
    for c in range(FF_CHUNKS):
        gate = _dot(h, wg_ref[c])
        up = _dot(h, wu_ref[c])
        act = (_silu(gate) * up).astype(BF16)
        down = _dot(act, wd_ref[c])
        if c == 0:
            acc_ref[...] = down
        else:
            acc_ref[...] += down
    o_ref[...] = x + 0.5 * acc_ref[...]


def _ffn(x, w):
    rows = x.shape[0]
    tm = _row_tile(rows, 512)
    g, wg, wu, wd = w
    return pl.pallas_call(
        _ffn_body,
        grid=(rows // tm,),
        in_specs=[
            pl.BlockSpec((tm, D_MODEL), lambda i: (i, 0)),
            _full_spec(g.shape),
            pl.BlockSpec(wg.shape, lambda i: (0, 0, 0), pipeline_mode=pl.Buffered(1)),
            pl.BlockSpec(wu.shape, lambda i: (0, 0, 0), pipeline_mode=pl.Buffered(1)),
            pl.BlockSpec(wd.shape, lambda i: (0, 0, 0), pipeline_mode=pl.Buffered(1)),
        ],
        out_specs=pl.BlockSpec((tm, D_MODEL), lambda i: (i, 0)),
        out_shape=jax.ShapeDtypeStruct((rows, D_MODEL), F32),
        scratch_shapes=[pltpu.VMEM((tm, D_MODEL), F32)],
        compiler_params=_params(1),
        name="half_ffn",
    )(x, g, wg, wu, wd)


def _ffn_weights(norm, w_gate, w_up, w_down):
    wg = w_gate.astype(BF16).reshape(D_MODEL, FF_CHUNKS, FF_BLOCK).transpose(1, 0, 2)
    wu = w_up.astype(BF16).reshape(D_MODEL, FF_CHUNKS, FF_BLOCK).transpose(1, 0, 2)
    wd = w_down.astype(BF16).reshape(FF_CHUNKS, FF_BLOCK, D_MODEL)
    return norm.reshape(1, D_MODEL), wg, wu, wd


def _inproj_body(x_ref, g_ref, w_ref, qkv_ref, z_ref, ab_ref, cq_ref, ckv_ref, kr_ref):
    hn = _rms(x_ref[...], g_ref[...]).astype(BF16)
    qkv_ref[...] = _dot(hn, w_ref[:, IN_QKV:IN_Z])
    z_ref[...] = _dot(hn, w_ref[:, IN_Z:IN_AB])
    ab_ref[...] = _dot(hn, w_ref[:, IN_AB:IN_CQ])
    cq_ref[...] = _dot(hn, w_ref[:, IN_CQ:IN_CKV])
    ckv_ref[...] = _dot(hn, w_ref[:, IN_CKV:IN_KR])
    kr_ref[...] = _dot(hn, w_ref[:, IN_KR:IN_TOTAL])


def _inproj(x, g, w):
    rows = x.shape[0]
    tm = _row_tile(rows, 512)
    widths = (GDN_QKV, GDN_Z, LANES, MLA_Q_RANK, MLA_KV_RANK, LANES)
    return pl.pallas_call(
        _inproj_body,
        grid=(rows // tm,),
        in_specs=[pl.BlockSpec((tm, D_MODEL), lambda i: (i, 0)), _full_spec(g.shape), _full_spec(w.shape)],
        out_specs=[pl.BlockSpec((tm, n), lambda i: (i, 0)) for n in widths],
        out_shape=[jax.ShapeDtypeStruct((rows, n), F32) for n in widths],
        compiler_params=_params(1),
        name="mixer_inproj",
    )(x, g, w)


class _Split:
    def __init__(self, x):
        self.x = x
        self._parts = None

    def parts(self):
        if self._parts is None:
            hi = self.x.astype(BF16)
            self._parts = (hi, (self.x - hi.astype(F32)).astype(BF16))
        return self._parts


def _split_products(pairs):
    parts = [(a.parts(), b.parts()) for a, b in pairs]
    outs = []
    for (a_hi, a_lo), (b_hi, b_lo) in parts:
        a2 = jnp.concatenate([a_hi, a_lo], 0)
        r = _dot(a2, b_hi) + _dot(a2, b_lo)
        half = a_hi.shape[0]
        outs.append(_Split(r[:half] + r[half:]))
    return outs


def _unit_lower_inverses(ns, eye_f, m16, m32):
    k = len(ns)
    nd = [_Split(jnp.where(m16, n, 0.0)) for n in ns]
    n2 = _split_products([(x, x) for x in nd])
    lvl = _split_products([(x, x) for x in n2] +
                          [(_Split(eye_f - a.x), _Split(eye_f + b.x)) for a, b in zip(nd, n2)])
    n4, t = lvl[:k], lvl[k:]
    lvl = _split_products([(x, x) for x in n4] + [(a, _Split(eye_f + b.x)) for a, b in zip(t, n4)])
    n8, t = lvl[:k], lvl[k:]
    t = _split_products([(a, _Split(eye_f + b.x)) for a, b in zip(t, n8)])
    for keep in (jnp.logical_and(m32, jnp.logical_not(m16)), jnp.logical_not(m32)):
        off = [_Split(jnp.where(keep, n, 0.0)) for n in ns]
        x = _split_products(list(zip(off, t)))
        y = _split_products(list(zip(t, x)))
        t = [_Split(a.x - b.x) for a, b in zip(t, y)]
    return [a.x for a in t]


def _gdn_body(qkv_ref, z_ref, ab_ref, s0_ref, cprev_ref, cw_ref, alog_ref, dtb_ref, on_ref,
              og_ref, sout_ref, cout_ref, s_scr, halo_scr, *, rows, n_valid):
    t = pl.program_id(1)
    C = CHUNK

    @pl.when(t == 0)
    def _():
        s_scr[...] = s0_ref[...]
        halo_scr[...] = cprev_ref[...]

    x = qkv_ref[...]
    halo = halo_scr[...]
    row = lax.broadcasted_iota(jnp.int32, (rows, 1), 0)

    def shifted(k):
        r = pltpu.roll(x, k, 0)
        for j in range(k):
            r = jnp.where(row == j, halo[HALO - k + j:HALO - k + j + 1, :], r)
        return r

    cw = cw_ref[...]
    y = shifted(3) * cw[0:1] + shifted(2) * cw[1:2] + shifted(1) * cw[2:3] + x * cw[3:4]
    last_rows = x[n_valid - HALO:n_valid, :]
    halo_scr[...] = last_rows
    y = _silu(y)

    ab = ab_ref[...]
    g_all = -jnp.exp(alog_ref[...]) * _softplus(ab + dtb_ref[...])
    b_all = _sigmoid(ab)
    if n_valid < rows:
        valid = row < n_valid
        y = jnp.where(valid, y, 0.0)
        g_all = jnp.where(valid, g_all, 0.0)
        b_all = jnp.where(valid, b_all, 0.0)

    ii = lax.broadcasted_iota(jnp.int32, (C, C), 0)
    jj = lax.broadcasted_iota(jnp.int32, (C, C), 1)
    causal = ii >= jj
    strict = ii > jj
    eye_f = (ii == jj).astype(F32)
    m16 = (ii >> 4) == (jj >> 4)
    m32 = (ii >> 5) == (jj >> 5)
    on = on_ref[...]
    n_chunks = rows // C
    heads = range(GDN_HEADS)

    ri = lax.broadcasted_iota(jnp.int32, (rows, rows), 0)
    rj = lax.broadcasted_iota(jnp.int32, (rows, rows), 1)
    tril_blk = jnp.logical_and(ri >= rj, (ri >> CHUNK_SHIFT) == (rj >> CHUNK_SHIFT)).astype(BF16)
    g_hi = g_all.astype(BF16)
    g_r1 = g_all - g_hi.astype(F32)
    g_mid = g_r1.astype(BF16)
    g_lo = (g_r1 - g_mid.astype(F32)).astype(BF16)
    gcum = _dot(tril_blk, g_hi) + _dot(tril_blk, g_mid) + _dot(tril_blk, g_lo)
    egcum = jnp.exp(gcum)

    qn, kn, vn = [], [], []
    for h in heads:
        qh = y[:, h * GDN_DK:(h + 1) * GDN_DK]
        kh = y[:, GDN_HEADS * GDN_DK + h * GDN_DK:GDN_HEADS * GDN_DK + (h + 1) * GDN_DK]
        qn.append(qh * lax.rsqrt(jnp.sum(qh * qh, -1, keepdims=True) + EPS) * (GDN_DK ** -0.5))
        kn.append(kh * lax.rsqrt(jnp.sum(kh * kh, -1, keepdims=True) + EPS))
        vn.append(y[:, 2 * GDN_HEADS * GDN_DK + h * GDN_DV:2 * GDN_HEADS * GDN_DK + (h + 1) * GDN_DV])

    chains = [(c, h) for c in range(n_chunks) for h in heads]
    pre = {}
    n_mats = []
    for c in range(n_chunks):
        sl = slice(c * C, (c + 1) * C)
        gcum_t = gcum[sl].T
        for h in heads:
            gcol = gcum[sl, h:h + 1]
            beta = b_all[sl, GDN_HEADS + h:GDN_HEADS + h + 1]
            eg = egcum[sl, h:h + 1]
            g_last = gcum[(c + 1) * C - 1:(c + 1) * C, h:h + 1]
            dec_causal = jnp.exp(jnp.where(causal, gcol - gcum_t[h:h + 1, :], NEG_INF))
            qh, kh, vh = qn[h][sl], kn[h][sl], vn[h][sl]
            kb = kh.astype(BF16)
            n_mats.append(beta * _dot_nt(kb, kb) * jnp.where(strict, dec_causal, 0.0))
            pre[c, h] = dict(
                qk=(_dot_nt(qh.astype(BF16), kb) * dec_causal).astype(BF16),
                rhs=jnp.concatenate([(beta * vh).astype(BF16), ((beta * eg) * kh).astype(BF16)], -1),
                qd=(qh * eg).astype(BF16),
                kt=(kh * jnp.exp(g_last - gcol)).astype(BF16),
                decay=jnp.exp(g_last))
    for key, tinv in zip(chains, _unit_lower_inverses(n_mats, eye_f, m16, m32)):
        uw = _dot(tinv.astype(BF16), pre[key]["rhs"])
        pre[key]["u"] = uw[:, :GDN_DV]
        pre[key]["wk"] = uw[:, GDN_DV:].astype(BF16)

    state = [s_scr[h] for h in heads]
    for c in range(n_chunks):
        sl = slice(c * C, (c + 1) * C)
        sb = [s.astype(BF16) for s in state]
        w = [(pre[c, h]["u"] - _dot(pre[c, h]["wk"], sb[h])).astype(BF16) for h in heads]
        o = [_dot(pre[c, h]["qd"], sb[h]) + _dot(pre[c, h]["qk"], w[h]) for h in heads]
        state = [state[h] * pre[c, h]["decay"] + _dot_tn(pre[c, h]["kt"], w[h]) for h in heads]
        for h in heads:
            gated = _rms(o[h], on) * _silu(z_ref[sl, h * GDN_DV:(h + 1) * GDN_DV])
            og_ref[sl, h * GDN_DV:(h + 1) * GDN_DV] = gated.astype(BF16)
    for h in heads:
        s_scr[h] = state[h]

    @pl.when(t == pl.num_programs(1) - 1)
    def _():
        sout_ref[...] = s_scr[...]
        cout_ref[...] = last_rows


def _gdn(qkv, z, ab, s0, cprev, w, *, rows, n_valid=None):
    b, t_len, _ = qkv.shape
    n_valid = rows if n_valid is None else n_valid
    conv_w, a_log, dt_bias, o_norm = w
    shared = s0.shape[0] == 1 and b > 1
    bsel = (lambda bi: 0) if shared else (lambda bi: bi)
    return pl.pallas_call(
        functools.partial(_gdn_body, rows=rows, n_valid=n_valid),
        grid=(b, t_len // rows),
        in_specs=[
            pl.BlockSpec((None, rows, GDN_QKV), lambda bi, ti: (bi, ti, 0)),
            pl.BlockSpec((None, rows, GDN_Z), lambda bi, ti: (bi, ti, 0)),
            pl.BlockSpec((None, rows, LANES), lambda bi, ti: (bi, ti, 0)),
            pl.BlockSpec((None, GDN_HEADS, GDN_DK, GDN_DV), lambda bi, ti: (bsel(bi), 0, 0, 0)),
            pl.BlockSpec((None, HALO, GDN_QKV), lambda bi, ti: (bsel(bi), 0, 0)),
            _full_spec(conv_w.shape), _full_spec(a_log.shape), _full_spec(dt_bias.shape), _full_spec(o_norm.shape),
        ],
        out_specs=[
            pl.BlockSpec((None, rows, GDN_Z), lambda bi, ti: (bi, ti, 0)),
            pl.BlockSpec((None, GDN_HEADS, GDN_DK, GDN_DV), lambda bi, ti: (bi, 0, 0, 0)),
            pl.BlockSpec((None, HALO, GDN_QKV), lambda bi, ti: (bi, 0, 0)),
        ],
        out_shape=[
            jax.ShapeDtypeStruct((b, t_len, GDN_Z), BF16),
            jax.ShapeDtypeStruct((b, GDN_HEADS, GDN_DK, GDN_DV), F32),
            jax.ShapeDtypeStruct((b, HALO, GDN_QKV), F32),
        ],
        scratch_shapes=[pltpu.VMEM((GDN_HEADS, GDN_DK, GDN_DV), F32), pltpu.VMEM((HALO, GDN_QKV), F32)],
        compiler_params=_params(2),
        name="gdn_mixer",
    )(qkv, z, ab, s0, cprev, conv_w, a_log, dt_bias, o_norm)


def _lane_masks():
    lane = lax.broadcasted_iota(jnp.int32, (1, LANES), 1)
    nope = lane < MLA_NOPE
    rope_a = jnp.logical_and(lane >= MLA_NOPE, lane < MLA_NOPE + MLA_ROPE // 2)
    rope_b = jnp.logical_and(lane >= MLA_NOPE + MLA_ROPE // 2, lane < MLA_NOPE + MLA_ROPE)
    return nope, rope_a, rope_b


def _rope_block(x, cos, sin, rope_a, rope_b):
    half = MLA_ROPE // 2
    rot = jnp.where(rope_a, -pltpu.roll(x, LANES - half, 1), jnp.where(rope_b, pltpu.roll(x, half, 1), 0.0))
    return x * cos + rot * sin


def _k_heads(k, kr_new, gk, k_out):
    for h in range(MLA_HEADS):
        x = k[:, h * LANES:(h + 1) * LANES]
        inv = lax.rsqrt(jnp.sum(x * x, -1, keepdims=True) / MLA_NOPE + EPS)
        k_out[:, h * LANES:(h + 1) * LANES] = (x * inv * gk + kr_new).astype(BF16)


def _mla_prep_body(cq_ref, ckv_ref, kr_ref, cos_ref, sin_ref, qn_ref, wuq_ref, kvn_ref, wk_ref, wvt_ref,
                   gq_ref, gk_ref, q_out, k_out, vt_out, ckv_out, kr_out):
    nope, rope_a, rope_b = _lane_masks()
    rope_l = jnp.logical_or(rope_a, rope_b)
    cos = cos_ref[...]
    sin = sin_ref[...]
    gq = gq_ref[...]
    gk = gk_ref[...]
    scale = (MLA_NOPE + MLA_ROPE) ** -0.5 * LOG2_E

    cqn = _rms(cq_ref[...], qn_ref[...]).astype(BF16)
    q = _dot(cqn, wuq_ref[...])
    for h in range(MLA_HEADS):
        x = q[:, h * LANES:(h + 1) * LANES]
        x2 = x * x
        sn = jnp.sum(jnp.where(nope, x2, 0.0), -1, keepdims=True)
        sr = jnp.sum(jnp.where(rope_l, x2, 0.0), -1, keepdims=True)
        inv = jnp.where(nope, lax.rsqrt(sn / MLA_NOPE + EPS), lax.rsqrt(sr / MLA_ROPE + EPS))
        x = _rope_block(x * inv * gq, cos, sin, rope_a, rope_b) * scale
        q_out[:, h * LANES:(h + 1) * LANES] = x.astype(BF16)

    ckvn = _rms(ckv_ref[...], kvn_ref[...])
    ckv_out[...] = ckvn
    cb = ckvn.astype(BF16)
    krp = kr_ref[...]
    krn = krp * lax.rsqrt(jnp.sum(krp * krp, -1, keepdims=True) / MLA_ROPE + EPS) * jnp.where(nope, 0.0, gk)
    kr_new = _rope_block(krn, cos, sin, rope_a, rope_b)
    kr_out[...] = kr_new
    vt_out[...] = _dot_nt(wvt_ref[...], cb).astype(BF16)
    _k_heads(_dot(cb, wk_ref[...]), kr_new, jnp.where(nope, gk, 0.0), k_out)


def _vt_spec(tm, tiles_per_batch):
    return pl.BlockSpec((None, MLA_HEADS * MLA_V, tm), lambda i: (i // tiles_per_batch, 0, i % tiles_per_batch))


def _mla_prep(cq, ckv, krp, cos, sin, w, *, batch):
    rows = cq.shape[0]
    tm = _row_tile(rows, 512)
    tiles_per_batch = rows // batch // tm
    q_norm, w_uq, kv_norm, w_k, w_vt, gq, gk = w
    row_spec = lambda n: pl.BlockSpec((tm, n), lambda i: (i, 0))
    tab_spec = pl.BlockSpec((tm, LANES), lambda i: (i % tiles_per_batch, 0))
    return pl.pallas_call(
        _mla_prep_body,
        grid=(rows // tm,),
        in_specs=[row_spec(MLA_Q_RANK), row_spec(MLA_KV_RANK), row_spec(LANES), tab_spec, tab_spec,
                  _full_spec(q_norm.shape), _full_spec(w_uq.shape), _full_spec(kv_norm.shape),
                  _full_spec(w_k.shape), _full_spec(w_vt.shape), _full_spec(gq.shape), _full_spec(gk.shape)],
        out_specs=[row_spec(MLA_PAD), row_spec(MLA_PAD), _vt_spec(tm, tiles_per_batch), row_spec(MLA_KV_RANK),
                   row_spec(LANES)],
        out_shape=[jax.ShapeDtypeStruct((rows, MLA_PAD), BF16), jax.ShapeDtypeStruct((rows, MLA_PAD), BF16),
                   jax.ShapeDtypeStruct((batch, MLA_HEADS * MLA_V, rows // batch), BF16),
                   jax.ShapeDtypeStruct((rows, MLA_KV_RANK), F32), jax.ShapeDtypeStruct((rows, LANES), F32)],
        compiler_params=_params(1),
        name="mla_prep",
    )(cq, ckv, krp, cos, sin, q_norm, w_uq, kv_norm, w_k, w_vt, gq, gk)


def _kv_up_body(ckv_ref, kr_ref, wk_ref, wvt_ref, gk_ref, k_out, vt_out):
    nope, _, _ = _lane_masks()
    cb = ckv_ref[...].astype(BF16)
    vt_out[...] = _dot_nt(wvt_ref[...], cb).astype(BF16)
    _k_heads(_dot(cb, wk_ref[...]), kr_ref[...], jnp.where(nope, gk_ref[...], 0.0), k_out)


def _kv_up(ckv, krp, w_k, w_vt, gk, *, batch):
    rows = ckv.shape[0]
    tm = _row_tile(rows, 512)
    tiles_per_batch = rows // batch // tm
    row_spec = lambda n: pl.BlockSpec((tm, n), lambda i: (i, 0))
    return pl.pallas_call(
        _kv_up_body,
        grid=(rows // tm,),
        in_specs=[row_spec(MLA_KV_RANK), row_spec(LANES), _full_spec(w_k.shape), _full_spec(w_vt.shape),
                  _full_spec(gk.shape)],
        out_specs=[row_spec(MLA_PAD), _vt_spec(tm, tiles_per_batch)],
        out_shape=[jax.ShapeDtypeStruct((rows, MLA_PAD), BF16),
                   jax.ShapeDtypeStruct((batch, MLA_HEADS * MLA_V, rows // batch), BF16)],
        compiler_params=_params(1),
        name="mla_kv_up",
    )(ckv, krp, w_k, w_vt, gk)


def _attn_body(*refs, tq, tk, causal, nk_valid, has_ext):
    if has_ext:
        q_ref, k_ref, vt_ref, ke_ref, vte_ref, o_ref, m_scr, l_scr, acc_scr = refs
    else:
        q_ref, k_ref, vt_ref, o_ref, m_scr, l_scr, acc_scr = refs
    i = pl.program_id(1)
    j = pl.program_id(2)
    nj = pl.num_programs(2)

    def update(k_blk, vt_blk, bias):
        head_lanes = [slice(h * LANES, (h + 1) * LANES) for h in range(MLA_HEADS)]
        scores = [_dot_nt(k_blk[:, hs], q_ref[:, hs]) for hs in head_lanes]
        for h in range(MLA_HEADS):
            s = scores[h]
            if bias is not None:
                s = s + bias
            m_old = m_scr[h][0:1]
            m_new = jnp.maximum(m_old, jnp.max(s, 0, keepdims=True))
            alpha = jnp.exp2(m_old - m_new)
            p = jnp.exp2(s - m_new)
            l_new = alpha * l_scr[h][0:1] + jnp.sum(p, 0, keepdims=True)
            m_scr[h] = jnp.broadcast_to(m_new, (HALO, tq))
            l_scr[h] = jnp.broadcast_to(l_new, (HALO, tq))
            acc_scr[h] = acc_scr[h] * alpha + _dot(vt_blk[h * MLA_V:(h + 1) * MLA_V, :], p.astype(BF16))

    @pl.when(j == 0)
    def _():
        m_scr[...] = jnp.full(m_scr.shape, NEG_INF, F32)
        l_scr[...] = jnp.zeros(l_scr.shape, F32)
        acc_scr[...] = jnp.zeros(acc_scr.shape, F32)
        if has_ext:
            update(ke_ref, vte_ref, None)

    if causal:
        n_needed = ((i + 1) * tq + tk - 1) // tk
        n_unmasked = n_needed - 1
    else:
        n_needed = nj
        n_unmasked = nj if nk_valid is None else 0

    @pl.when(j < n_unmasked)
    def _():
        update(k_ref, vt_ref, None)

    if causal or nk_valid is not None:
        @pl.when(jnp.logical_and(j >= n_unmasked, j < n_needed))
        def _():
            kpos = j * tk + lax.broadcasted_iota(jnp.int32, (tk, 1), 0)
            if causal:
                qpos = i * tq + lax.broadcasted_iota(jnp.int32, (1, tq), 1)
                bias = jnp.where((kpos >> CHUNK_SHIFT) <= (qpos >> CHUNK_SHIFT), 0.0, NEG_INF)
            else:
                bias = jnp.where(kpos < nk_valid, 0.0, NEG_INF)
            update(k_ref, vt_ref, bias)

    @pl.when(j == nj - 1)
    def _():
        for h in range(MLA_HEADS):
            o_ref[h * MLA_V:(h + 1) * MLA_V, :] = (acc_scr[h] / l_scr[h][0:1]).astype(BF16)


def _attention(q, k, vt, ext, *, tq, tk, causal=False, nk_valid=None):
    b, nq, _ = q.shape
    nk = k.shape[1]
    nj = nk // tk
    if causal:
        assert tk % tq == 0
        last = lambda i: ((i + 1) * tq + tk - 1) // tk - 1
        k_map = lambda bi, i, j: (bi, jnp.minimum(j, last(i)), 0)
        vt_map = lambda bi, i, j: (bi, 0, jnp.minimum(j, last(i)))
    else:
        k_map = lambda bi, i, j: (bi, j, 0)
        vt_map = lambda bi, i, j: (bi, 0, j)
    hv = MLA_HEADS * MLA_V
    in_specs = [
        pl.BlockSpec((None, tq, MLA_PAD), lambda bi, i, j: (bi, i, 0)),
        pl.BlockSpec((None, tk, MLA_PAD), k_map),
        pl.BlockSpec((None, hv, tk), vt_map),
    ]
    args = [q, k, vt]
    if ext is not None:
        ne = ext[0].shape[1]
        shared = ext[0].shape[0] == 1 and b > 1
        emap = (lambda bi, i, j: (0, 0, 0)) if shared else (lambda bi, i, j: (bi, 0, 0))
        in_specs += [pl.BlockSpec((None, ne, MLA_PAD), emap), pl.BlockSpec((None, hv, ne), emap)]
        args += list(ext)
    return pl.pallas_call(
        functools.partial(_attn_body, tq=tq, tk=tk, causal=causal, nk_valid=nk_valid, has_ext=ext is not None),
        grid=(b, nq // tq, nj),
        in_specs=in_specs,
        out_specs=pl.BlockSpec((None, hv, tq), lambda bi, i, j: (bi, 0, i)),
        out_shape=jax.ShapeDtypeStruct((b, hv, nq), BF16),
        scratch_shapes=[pltpu.VMEM((MLA_HEADS, HALO, tq), F32), pltpu.VMEM((MLA_HEADS, HALO, tq), F32),
                        pltpu.VMEM((MLA_HEADS, MLA_V, tq), F32)],
        compiler_params=_params(3),
        name="mla_attention",
    )(*args)


def _outproj_body(x_ref, og_ref, omt_ref, w_ref, o_ref):
    o_ref[...] = x_ref[...] + _dot(og_ref[...], w_ref[:GDN_Z, :]) + _dot_tn(omt_ref[...], w_ref[GDN_Z:, :])


def _outproj(x, og, omt, w, *, batch):
    rows = x.shape[0]
    tm = _row_tile(rows, 512)
    tiles_per_batch = rows // batch // tm
    half = GDN_Z
    return pl.pallas_call(
        _outproj_body,
        grid=(rows // tm,),
        in_specs=[pl.BlockSpec((tm, D_MODEL), lambda i: (i, 0)), pl.BlockSpec((tm, half), lambda i: (i, 0)),
                  pl.BlockSpec((None, half, tm), lambda i: (i // tiles_per_batch, 0, i % tiles_per_batch)),
                  _full_spec(w.shape)],
        out_specs=pl.BlockSpec((tm, D_MODEL), lambda i: (i, 0)),
        out_shape=jax.ShapeDtypeStruct((rows, D_MODEL), F32),
        compiler_params=_params(1),
        name="mixer_outproj",
    )(x, og, omt, w)


def _sconv_body(x_ref, g_ref, win_ref, cw_ref, wout_ref, prev_ref, o_ref, pout_ref, halo_scr, *, rows):
    t = pl.program_id(1)

    @pl.when(t == 0)
    def _():
        halo_scr[...] = prev_ref[...]

    x = x_ref[...]
    hn = _rms(x, g_ref[...]).astype(BF16)
    bg = _dot(hn, win_ref[:, :D_MODEL])
    u = _dot(hn, win_ref[:, D_MODEL:2 * D_MODEL]) * _dot(hn, win_ref[:, 2 * D_MODEL:])
    halo = halo_scr[...]
    row = lax.broadcasted_iota(jnp.int32, (rows, 1), 0)
    u1 = jnp.where(row == 0, halo[HALO - 1:HALO, :], pltpu.roll(u, 1, 0))
    u2 = jnp.where(row == 0, halo[HALO - 2:HALO - 1, :],
                   jnp.where(row == 1, halo[HALO - 1:HALO, :], pltpu.roll(u, 2, 0)))
    cw = cw_ref[...]
    y = u2 * cw[0:1] + u1 * cw[1:2] + u * cw[2:3]
    last_rows = u[rows - HALO:rows, :]
    halo_scr[...] = last_rows
    o_ref[...] = x + _dot((bg * y).astype(BF16), wout_ref[...])

    @pl.when(t == pl.num_programs(1) - 1)
    def _():
        pout_ref[...] = last_rows


def _sconv(x, prev, w, *, rows):
    b, t_len, _ = x.shape
    g, w_in, conv_w, w_out = w
    shared = prev.shape[0] == 1 and b > 1
    bsel = (lambda bi: 0) if shared else (lambda bi: bi)
    return pl.pallas_call(
        functools.partial(_sconv_body, rows=rows),
        grid=(b, t_len // rows),
        in_specs=[pl.BlockSpec((None, rows, D_MODEL), lambda bi, ti: (bi, ti, 0)), _full_spec(g.shape),
                  _full_spec(w_in.shape), _full_spec(conv_w.shape), _full_spec(w_out.shape),
                  pl.BlockSpec((None, HALO, D_MODEL), lambda bi, ti: (bsel(bi), 0, 0))],
        out_specs=[pl.BlockSpec((None, rows, D_MODEL), lambda bi, ti: (bi, ti, 0)),
                   pl.BlockSpec((None, HALO, D_MODEL), lambda bi, ti: (bi, 0, 0))],
        out_shape=[jax.ShapeDtypeStruct((b, t_len, D_MODEL), F32), jax.ShapeDtypeStruct((b, HALO, D_MODEL), F32)],
        scratch_shapes=[pltpu.VMEM((HALO, D_MODEL), F32)],
        compiler_params=_params(2),
        name="sconv_mixer",
    )(x, g, w_in, conv_w, w_out, prev)


def _pad_lanes(a, before, total):
    return jnp.pad(a, [(0, 0)] * (a.ndim - 1) + [(before, total - before - a.shape[-1])])


def _rope_tables(pos):
    half = MLA_ROPE // 2
    inv = ROPE_THETA ** (-jnp.arange(half, dtype=F32) / half)
    ang = pos.astype(F32)[:, None] * inv
    cos, sin = jnp.cos(ang), jnp.sin(ang)
    n = pos.shape[0]
    tail = LANES - MLA_NOPE - MLA_ROPE
    cos_l = jnp.concatenate([jnp.ones((n, MLA_NOPE), F32), cos, cos, jnp.ones((n, tail), F32)], -1)
    sin_l = jnp.concatenate([jnp.zeros((n, MLA_NOPE), F32), sin, sin, jnp.zeros((n, tail), F32)], -1)
    return cos_l, sin_l


def _tail_rows(a, n):
    return jnp.pad(a, ((0, 0), (HALO - n, 0), (0, 0)))


def kernel(x_prompt, x_sample, cache_mla_ckv, cache_mla_krope, state_gdn_S, state_gdn_conv, state_sconv, meta_tokens, ffn1_norm, ffn1_w_gate, ffn1_w_up, ffn1_w_down, ffn2_norm, ffn2_w_gate, ffn2_w_up, ffn2_w_down, mix_norm, ab_w_in, ab_w_out, gdn_conv_w, gdn_A_log, gdn_dt_bias, gdn_o_norm, mla_q_norm, mla_w_uq, mla_kv_norm, mla_w_ukv, mla_qn_norm, mla_qr_norm, mla_kn_norm, mla_kr_norm, sc_w_in, sc_conv_w, sc_w_out):
    bp, tp, _ = x_prompt.shape
    bs, ts, _ = x_sample.shape
    n_frames = bp * tp
    n_sample = bs * ts

    ffn1 = [_ffn_weights(ffn1_norm[l], ffn1_w_gate[l], ffn1_w_up[l], ffn1_w_down[l]) for l in range(2)]
    ffn2 = [_ffn_weights(ffn2_norm[l], ffn2_w_gate[l], ffn2_w_up[l], ffn2_w_down[l]) for l in range(2)]
    w_in = ab_w_in[0]
    s1 = GDN_QKV + GDN_Z
    c0 = s1 + 2 * GDN_HEADS
    w_in_packed = jnp.concatenate([
        w_in[:, :s1],
        _pad_lanes(w_in[:, s1:c0], 0, LANES),
        w_in[:, c0:c0 + MLA_Q_RANK + MLA_KV_RANK],
        _pad_lanes(w_in[:, c0 + MLA_Q_RANK + MLA_KV_RANK:], MLA_NOPE, LANES),
    ], -1).astype(BF16)
    mix_g = mix_norm.reshape(2, 1, D_MODEL)
    gdn_w = (gdn_conv_w[0], _pad_lanes(gdn_A_log[0][None], 0, LANES), _pad_lanes(gdn_dt_bias[0][None], 0, LANES),
             gdn_o_norm[0][None])
    qdim = MLA_NOPE + MLA_ROPE
    w_uq = _pad_lanes(mla_w_uq[0].reshape(MLA_Q_RANK, MLA_HEADS, qdim), 0, LANES).reshape(MLA_Q_RANK, MLA_PAD)
    w_ukv = mla_w_ukv[0].reshape(MLA_KV_RANK, MLA_HEADS, MLA_NOPE + MLA_V)
    w_k = _pad_lanes(w_ukv[..., :MLA_NOPE], 0, LANES).reshape(MLA_KV_RANK, MLA_PAD)
    hv = MLA_HEADS * MLA_V
    w_vt = w_ukv[..., MLA_NOPE:].reshape(MLA_KV_RANK, hv).T
    gq = _pad_lanes(jnp.concatenate([mla_qn_norm[0], mla_qr_norm[0]])[None], 0, LANES)
    gk = _pad_lanes(jnp.concatenate([mla_kn_norm[0], mla_kr_norm[0]])[None], 0, LANES)
    mla_w = (mla_q_norm[0][None], w_uq.astype(BF16), mla_kv_norm[0][None], w_k.astype(BF16), w_vt.astype(BF16), gq, gk)
    w_out = ab_w_out[0].astype(BF16)
    sc_w = (mix_g[1], sc_w_in[0].astype(BF16), sc_conv_w[0], sc_w_out[0].astype(BF16))

    xf = x_prompt.reshape(n_frames, D_MODEL)
    xs = jnp.concatenate([x_sample.reshape(n_sample, D_MODEL), meta_tokens.astype(F32)], 0)
    cos_f, sin_f = _rope_tables(N_META + jnp.arange(tp))
    pos_small = jnp.concatenate([jnp.tile(PAST_LEN + jnp.arange(ts), bs), jnp.arange(N_META)])
    cos_s, sin_s = _rope_tables(pos_small)

    xf = _ffn(xf, ffn1[0])
    xs = _ffn(xs, ffn1[0])
    qkv_f, z_f, ab_f, cq_f, ckv_f, kr_f = _inproj(xf, mix_g[0], w_in_packed)
    qkv_s, z_s, ab_s, cq_s, ckv_s, kr_s = _inproj(xs, mix_g[0], w_in_packed)

    pad_meta = lambda a: jnp.pad(a[n_sample:], ((0, CHUNK - N_META), (0, 0)))[None]
    og_m, s_m, conv_m = _gdn(pad_meta(qkv_s), pad_meta(z_s), pad_meta(ab_s),
                             jnp.zeros((1, GDN_HEADS, GDN_DK, GDN_DV), F32), jnp.zeros((1, HALO, GDN_QKV), F32),
                             gdn_w, rows=CHUNK, n_valid=N_META)
    og_f, p_s, p_conv = _gdn(qkv_f.reshape(bp, tp, GDN_QKV), z_f.reshape(bp, tp, GDN_Z), ab_f.reshape(bp, tp, LANES),
                             s_m, conv_m, gdn_w, rows=4 * CHUNK)
    og_s, s_s, s_conv = _gdn(qkv_s[:n_sample].reshape(bs, ts, GDN_QKV), z_s[:n_sample].reshape(bs, ts, GDN_Z),
                             ab_s[:n_sample].reshape(bs, ts, LANES), state_gdn_S[0],
                             _tail_rows(state_gdn_conv[0], state_gdn_conv.shape[2]), gdn_w, rows=ts)

    q_f, k_f, vt_f, ckvn_f, krn_f = _mla_prep(cq_f, ckv_f, kr_f, cos_f, sin_f, mla_w, batch=bp)
    q_s, k_s, vt_s, ckvn_s, krn_s = _mla_prep(cq_s, ckv_s, kr_s, cos_s, sin_s, mla_w, batch=1)
    k_c, vt_c = _kv_up(cache_mla_ckv[0].reshape(bs * PAST_LEN, MLA_KV_RANK),
                       _pad_lanes(cache_mla_krope[0].reshape(bs * PAST_LEN, MLA_ROPE), MLA_NOPE, LANES),
                       mla_w[3], mla_w[4], gk, batch=bs)
    k_meta, vt_meta = k_s[n_sample:][None], vt_s[:, :, n_sample:]
    vt_new = vt_s[0, :, :n_sample].reshape(hv, bs, ts).transpose(1, 0, 2)
    omt_f = _attention(q_f.reshape(bp, tp, MLA_PAD), k_f.reshape(bp, tp, MLA_PAD), vt_f, (k_meta, vt_meta),
                       tq=256, tk=512, causal=True)
    omt_s = _attention(q_s[:n_sample].reshape(bs, ts, MLA_PAD), k_c.reshape(bs, PAST_LEN, MLA_PAD), vt_c,
                       (k_s[:n_sample].reshape(bs, ts, MLA_PAD), vt_new), tq=ts, tk=512)
    omt_m = _attention(jnp.pad(q_s[n_sample:], ((0, LANES - N_META), (0, 0)))[None],
                       jnp.pad(k_s[n_sample:], ((0, CHUNK - N_META), (0, 0)))[None],
                       jnp.pad(vt_meta, ((0, 0), (0, 0), (0, CHUNK - N_META))),
                       None, tq=LANES, tk=CHUNK, nk_valid=N_META)

    og_small = jnp.concatenate([og_s.reshape(n_sample, GDN_Z), og_m[0, :N_META]], 0)
    omt_small = jnp.concatenate([omt_s.transpose(1, 0, 2).reshape(hv, n_sample), omt_m[0, :, :N_META]], -1)[None]
    xf = _outproj(xf, og_f.reshape(n_frames, GDN_Z), omt_f, w_out, batch=bp)
    xs = _outproj(xs, og_small, omt_small, w_out, batch=1)
    xf = _ffn(xf, ffn2[0])
    xs = _ffn(xs, ffn2[0])

    xf = _ffn(xf, ffn1[1])
    xs = _ffn(xs, ffn1[1])
    n_sc = state_sconv.shape[2]
    xm, sc_m = _sconv(xs[n_sample:][None], jnp.zeros((1, HALO, D_MODEL), F32), sc_w, rows=N_META)
    xf3, p_sc = _sconv(xf.reshape(bp, tp, D_MODEL), sc_m, sc_w, rows=512)
    xs3, s_sc = _sconv(xs[:n_sample].reshape(bs, ts, D_MODEL), _tail_rows(state_sconv[0], n_sc), sc_w, rows=ts)
    xf = _ffn(xf3.reshape(n_frames, D_MODEL), ffn2[1])
    xs = _ffn(jnp.concatenate([xs3.reshape(n_sample, D_MODEL), xm[0]], 0), ffn2[1])

    y_prompt = xf.reshape(bp, tp, D_MODEL)
    y_sample = xs[:n_sample].reshape(bs, ts, D_MODEL)
    rope_lo, rope_hi = MLA_NOPE, MLA_NOPE + MLA_ROPE
    meta_ckv = jnp.broadcast_to(ckvn_s[n_sample:][None], (bp, N_META, MLA_KV_RANK))
    meta_kr = jnp.broadcast_to(krn_s[n_sample:, rope_lo:rope_hi][None], (bp, N_META, MLA_ROPE))
    p_ckv = jnp.concatenate([meta_ckv, ckvn_f.reshape(bp, tp, MLA_KV_RANK)], 1)
    p_kr = jnp.concatenate([meta_kr, krn_f[:, rope_lo:rope_hi].reshape(bp, tp, MLA_ROPE)], 1)
    n_gc = state_gdn_conv.shape[2]
    return (y_prompt, y_sample,
            p_ckv[None], p_kr[None], p_s[None], p_conv[:, HALO - n_gc:][None], p_sc[:, HALO - n_sc:][None],
            ckvn_s[:n_sample].reshape(bs, ts, MLA_KV_RANK)[None],
            krn_s[:n_sample, rope_lo:rope_hi].reshape(bs, ts, MLA_ROPE)[None],
            s_s[None], s_conv[:, HALO - n_gc:][None], s_sc[:, HALO - n_sc:][None])
```

```python
import functools

import jax
import jax.numpy as jnp
from jax import lax
from jax.experimental import pallas as pl
from jax.experimental.pallas import tpu as pltpu

F32 = jnp.float32
BF16 = jnp.bfloat16

D_MODEL = 1024
D_FF = 2816
CHUNK = 64
CHUNK_SHIFT = 6
N_META = 16
EPS = 1e-6
GDN_HEADS = 4
GDN_DK = 128
GDN_DV = 128
GDN_QKV = GDN_HEADS * (2 * GDN_DK + GDN_DV)
GDN_Z = GDN_HEADS * GDN_DV
MLA_HEADS = 8
MLA_NOPE = 64
MLA_ROPE = 32
MLA_V = 64
MLA_Q_RANK = 384
MLA_KV_RANK = 256
ROPE_THETA = 10000.0
PAST_LEN = 1024
LANES = 128
HALO = 8
MLA_PAD = MLA_HEADS * LANES
FF_BLOCK = 256
FF_CHUNKS = D_FF // FF_BLOCK
VMEM_LIMIT = 60000 * 1024
NEG_INF = float("-inf")
LOG2_E = 1.4426950408889634
MASK_LANE0 = MLA_NOPE + MLA_ROPE
MASK_BIG = 2.0 ** 100
ATTN_TQ = 512
ATTN_TK = 512

IN_QKV = 0
IN_Z = IN_QKV + GDN_QKV
IN_AB = IN_Z + GDN_Z
IN_CQ = IN_AB + LANES
IN_CKV = IN_CQ + MLA_Q_RANK
IN_KR = IN_CKV + MLA_KV_RANK
IN_TOTAL = IN_KR + LANES


def _params(n_axes):
    return pltpu.CompilerParams(dimension_semantics=("arbitrary",) * n_axes, vmem_limit_bytes=VMEM_LIMIT)


def _rms(x, g):
    return x * lax.rsqrt(jnp.mean(x * x, -1, keepdims=True) + EPS) * g


def _sigmoid(x):
    return 1.0 / (1.0 + jnp.exp(-x))


def _silu(x):
    return x * _sigmoid(x)


def _softplus(x):
    return jnp.maximum(x, 0.0) + jnp.log1p(jnp.exp(-jnp.abs(x)))


def _dot(a, b):
    return jnp.dot(a, b, preferred_element_type=F32)


def _dot_nt(a, b):
    return lax.dot_general(a, b, (((1,), (1,)), ((), ())), preferred_element_type=F32)


def _dot_tn(a, b):
    return lax.dot_general(a, b, (((0,), (0,)), ((), ())), preferred_element_type=F32)


def _full_spec(shape):
    zeros = (0,) * len(shape)
    return pl.BlockSpec(shape, lambda *_: zeros)


def _row_tile(rows, want):
    return want if rows % want == 0 else rows


def _ffn_body(x_ref, g_ref, wg_ref, wu_ref, wd_ref, o_ref, acc_ref):
    x = x_ref[...]
    h = _rms(x, g_ref[...]).astype(BF16)
    for c in range(FF_CHUNKS):
        cols = slice(c * FF_BLOCK, (c + 1) * FF_BLOCK)
        gate = _dot(h, wg_ref[:, cols])
        up = _dot(h, wu_ref[:, cols])
        act = (_silu(gate) * up).astype(BF16)
        down = _dot(act, wd_ref[cols, :])
        if c == 0:
            acc_ref[...] = down
        else:
            acc_ref[...] += down
    o_ref[...] = x + 0.5 * acc_ref[...]


def _ffn(x, w):
    rows = x.shape[0]
    tm = _row_tile(rows, 512)
    g, wg, wu, wd = w
    return pl.pallas_call(
        _ffn_body,
        grid=(rows // tm,),
        in_specs=[
            pl.BlockSpec((tm, D_MODEL), lambda i: (i, 0)),
            _full_spec(g.shape),
            pl.BlockSpec(wg.shape, lambda i: (0, 0), pipeline_mode=pl.Buffered(1)),
            pl.BlockSpec(wu.shape, lambda i: (0, 0), pipeline_mode=pl.Buffered(1)),
            pl.BlockSpec(wd.shape, lambda i: (0, 0), pipeline_mode=pl.Buffered(1)),
        ],
        out_specs=pl.BlockSpec((tm, D_MODEL), lambda i: (i, 0)),
        out_shape=jax.ShapeDtypeStruct((rows, D_MODEL), F32),
        scratch_shapes=[pltpu.VMEM((tm, D_MODEL), F32)],
        compiler_params=_params(1),
        name="half_ffn",
    )(x, g, wg, wu, wd)


def _ffn_weights(norm, w_gate, w_up, w_down):
    return norm.reshape(1, D_MODEL), w_gate.astype(BF16), w_up.astype(BF16), w_down.astype(BF16)


def _inproj_body(x_ref, g_ref, w_ref, qkv_ref, z_ref, ab_ref, cq_ref, ckv_ref, kr_ref):
    hn = _rms(x_ref[...], g_ref[...]).astype(BF16)
    qkv_ref[...] = _dot(hn, w_ref[:, IN_QKV:IN_Z])
    z_ref[...] = _dot(hn, w_ref[:, IN_Z:IN_AB])
    ab_ref[...] = _dot(hn, w_ref[:, IN_AB:IN_CQ])
    cq_ref[...] = _dot(hn, w_ref[:, IN_CQ:IN_CKV])
    ckv_ref[...] = _dot(hn, w_ref[:, IN_CKV:IN_KR])
    kr_ref[...] = _dot(hn, w_ref[:, IN_KR:IN_TOTAL])


def _inproj(x, g, w):
    rows = x.shape[0]
    tm = _row_tile(rows, 512)
    widths = (GDN_QKV, GDN_Z, LANES, MLA_Q_RANK, MLA_KV_RANK, LANES)
    return pl.pallas_call(
        _inproj_body,
        grid=(rows // tm,),
        in_specs=[pl.BlockSpec((tm, D_MODEL), lambda i: (i, 0)), _full_spec(g.shape), _full_spec(w.shape)],
        out_specs=[pl.BlockSpec((tm, n), lambda i: (i, 0)) for n in widths],
        out_shape=[jax.ShapeDtypeStruct((rows, n), F32) for n in widths],
        compiler_params=_params(1),
        name="mixer_inproj",
    )(x, g, w)


class _Split:
    def __init__(self, x):
        self.x = x
        self._parts = None

    def parts(self):
        if self._parts is None:
            hi = self.x.astype(BF16)
            self._parts = (hi, (self.x - hi.astype(F32)).astype(BF16))
        return self._parts


def _split_products(pairs):
    parts = [(a.parts(), b.parts()) for a, b in pairs]
    outs = []
    for (a_hi, a_lo), (b_hi, b_lo) in parts:
        a2 = jnp.concatenate([a_hi, a_lo], 0)
        r = _dot(a2, b_hi) + _dot(a2, b_lo)
        half = a_hi.shape[0]
        outs.append(_Split(r[:half] + r[half:]))
    return outs


def _unit_lower_inverses(ns, eye_f, m16, m32):
    k = len(ns)
    nd = [_Split(jnp.where(m16, n, 0.0)) for n in ns]
    n2 = _split_products([(x, x) for x in nd])
    lvl = _split_products([(x, x) for x in n2] +
                          [(_Split(eye_f - a.x), _Split(eye_f + b.x)) for a, b in zip(nd, n2)])
    n4, t = lvl[:k], lvl[k:]
    lvl = _split_products([(x, x) for x in n4] + [(a, _Split(eye_f + b.x)) for a, b in zip(t, n4)])
    n8, t = lvl[:k], lvl[k:]
    t = _split_products([(a, _Split(eye_f + b.x)) for a, b in zip(t, n8)])
    for keep in (jnp.logical_and(m32, jnp.logical_not(m16)), jnp.logical_not(m32)):
        off = [_Split(jnp.where(keep, n, 0.0)) for n in ns]
        x = _split_products(list(zip(off, t)))
        y = _split_products(list(zip(t, x)))
        t = [_Split(a.x - b.x) for a, b in zip(t, y)]
    return [a.x for a in t]


def _gdn_body(qkv_ref, z_ref, ab_ref, s0_ref, cprev_ref, cw_ref, alog_ref, dtb_ref, on_ref,
              og_ref, sout_ref, cout_ref, s_scr, halo_scr, *, rows, n_valid):
    t = pl.program_id(1)
    C = CHUNK

    @pl.when(t == 0)
    def _():
        s_scr[...] = s0_ref[...]
        halo_scr[...] = cprev_ref[...]

    x = qkv_ref[...]
    halo = halo_scr[...]
    row = lax.broadcasted_iota(jnp.int32, (rows, 1), 0)

    def shifted(k):
        r = pltpu.roll(x, k, 0)
        for j in range(k):
            r = jnp.where(row == j, halo[HALO - k + j:HALO - k + j + 1, :], r)
        return r

    cw = cw_ref[...]
    y = shifted(3) * cw[0:1] + shifted(2) * cw[1:2] + shifted(1) * cw[2:3] + x * cw[3:4]
    last_rows = x[n_valid - HALO:n_valid, :]
    halo_scr[...] = last_rows
    y = _silu(y)

    ab = ab_ref[...]
    g_all = -jnp.exp(alog_ref[...]) * _softplus(ab + dtb_ref[...])
    b_all = _sigmoid(ab)
    if n_valid < rows:
        valid = row < n_valid
        y = jnp.where(valid, y, 0.0)
        g_all = jnp.where(valid, g_all, 0.0)
        b_all = jnp.where(valid, b_all, 0.0)

    ii = lax.broadcasted_iota(jnp.int32, (C, C), 0)
    jj = lax.broadcasted_iota(jnp.int32, (C, C), 1)
    causal = ii >= jj
    strict = ii > jj
    eye_f = (ii == jj).astype(F32)
    m16 = (ii >> 4) == (jj >> 4)
    m32 = (ii >> 5) == (jj >> 5)
    on = on_ref[...]
    n_chunks = rows // C
    heads = range(GDN_HEADS)

    ri = lax.broadcasted_iota(jnp.int32, (rows, rows), 0)
    rj = lax.broadcasted_iota(jnp.int32, (rows, rows), 1)
    tril_blk = jnp.logical_and(ri >= rj, (ri >> CHUNK_SHIFT) == (rj >> CHUNK_SHIFT)).astype(BF16)
    g_hi = g_all.astype(BF16)
    g_r1 = g_all - g_hi.astype(F32)
    g_mid = g_r1.astype(BF16)
    g_lo = (g_r1 - g_mid.astype(F32)).astype(BF16)
    gcum = _dot(tril_blk, g_hi) + _dot(tril_blk, g_mid) + _dot(tril_blk, g_lo)
    egcum = jnp.exp(gcum)

    qn, kn, vn = [], [], []
    for h in heads:
        qh = y[:, h * GDN_DK:(h + 1) * GDN_DK]
        kh = y[:, GDN_HEADS * GDN_DK + h * GDN_DK:GDN_HEADS * GDN_DK + (h + 1) * GDN_DK]
        qn.append(qh * lax.rsqrt(jnp.sum(qh * qh, -1, keepdims=True) + EPS) * (GDN_DK ** -0.5))
        kn.append(kh * lax.rsqrt(jnp.sum(kh * kh, -1, keepdims=True) + EPS))
        vn.append(y[:, 2 * GDN_HEADS * GDN_DK + h * GDN_DV:2 * GDN_HEADS * GDN_DK + (h + 1) * GDN_DV])

    chains = [(c, h) for c in range(n_chunks) for h in heads]
    pre = {}
    n_mats = []
    for c in range(n_chunks):
        sl = slice(c * C, (c + 1) * C)
        gcum_t = gcum[sl].T
        for h in heads:
            gcol = gcum[sl, h:h + 1]
            beta = b_all[sl, GDN_HEADS + h:GDN_HEADS + h + 1]
            eg = egcum[sl, h:h + 1]
            g_last = gcum[(c + 1) * C - 1:(c + 1) * C, h:h + 1]
            dec_causal = jnp.exp(jnp.where(causal, gcol - gcum_t[h:h + 1, :], NEG_INF))
            qh, kh, vh = qn[h][sl], kn[h][sl], vn[h][sl]
            kb = kh.astype(BF16)
            n_mats.append(beta * _dot_nt(kb, kb) * jnp.where(strict, dec_causal, 0.0))
            pre[c, h] = dict(
                qk=(_dot_nt(qh.astype(BF16), kb) * dec_causal).astype(BF16),
                rhs=jnp.concatenate([(beta * vh).astype(BF16), ((beta * eg) * kh).astype(BF16)], -1),
                qd=(qh * eg).astype(BF16),
                kt=(kh * jnp.exp(g_last - gcol)).astype(BF16),
                decay=jnp.exp(g_last))
    for key, tinv in zip(chains, _unit_lower_inverses(n_mats, eye_f, m16, m32)):
        uw = _dot(tinv.astype(BF16), pre[key]["rhs"])
        pre[key]["u"] = uw[:, :GDN_DV]
        pre[key]["wk"] = uw[:, GDN_DV:].astype(BF16)

    state = [s_scr[h] for h in heads]
    for c in range(n_chunks):
        sl = slice(c * C, (c + 1) * C)
        sb = [s.astype(BF16) for s in state]
        w = [(pre[c, h]["u"] - _dot(pre[c, h]["wk"], sb[h])).astype(BF16) for h in heads]
        o = [_dot(pre[c, h]["qd"], sb[h]) + _dot(pre[c, h]["qk"], w[h]) for h in heads]
        state = [state[h] * pre[c, h]["decay"] + _dot_tn(pre[c, h]["kt"], w[h]) for h in heads]
        for h in heads:
            gated = _rms(o[h], on) * _silu(z_ref[sl, h * GDN_DV:(h + 1) * GDN_DV])
            og_ref[sl, h * GDN_DV:(h + 1) * GDN_DV] = gated.astype(BF16)
    for h in heads:
        s_scr[h] = state[h]

    @pl.when(t == pl.num_programs(1) - 1)
    def _():
        sout_ref[...] = s_scr[...]
        cout_ref[...] = last_rows


def _gdn(qkv, z, ab, s0, cprev, w, *, rows, n_valid=None):
    b, t_len, _ = qkv.shape
    n_valid = rows if n_valid is None else n_valid
    conv_w, a_log, dt_bias, o_norm = w
    shared = s0.shape[0] == 1 and b > 1
    bsel = (lambda bi: 0) if shared else (lambda bi: bi)
    return pl.pallas_call(
        functools.partial(_gdn_body, rows=rows, n_valid=n_valid),
        grid=(b, t_len // rows),
        in_specs=[
            pl.BlockSpec((None, rows, GDN_QKV), lambda bi, ti: (bi, ti, 0)),
            pl.BlockSpec((None, rows, GDN_Z), lambda bi, ti: (bi, ti, 0)),
            pl.BlockSpec((None, rows, LANES), lambda bi, ti: (bi, ti, 0)),
            pl.BlockSpec((None, GDN_HEADS, GDN_DK, GDN_DV), lambda bi, ti: (bsel(bi), 0, 0, 0)),
            pl.BlockSpec((None, HALO, GDN_QKV), lambda bi, ti: (bsel(bi), 0, 0)),
            _full_spec(conv_w.shape), _full_spec(a_log.shape), _full_spec(dt_bias.shape), _full_spec(o_norm.shape),
        ],
        out_specs=[
            pl.BlockSpec((None, rows, GDN_Z), lambda bi, ti: (bi, ti, 0)),
            pl.BlockSpec((None, GDN_HEADS, GDN_DK, GDN_DV), lambda bi, ti: (bi, 0, 0, 0)),
            pl.BlockSpec((None, HALO, GDN_QKV), lambda bi, ti: (bi, 0, 0)),
        ],
        out_shape=[
            jax.ShapeDtypeStruct((b, t_len, GDN_Z), BF16),
            jax.ShapeDtypeStruct((b, GDN_HEADS, GDN_DK, GDN_DV), F32),
            jax.ShapeDtypeStruct((b, HALO, GDN_QKV), F32),
        ],
        scratch_shapes=[pltpu.VMEM((GDN_HEADS, GDN_DK, GDN_DV), F32), pltpu.VMEM((HALO, GDN_QKV), F32)],
        compiler_params=_params(2),
        name="gdn_mixer",
    )(qkv, z, ab, s0, cprev, conv_w, a_log, dt_bias, o_norm)


def _lane_masks():
    lane = lax.broadcasted_iota(jnp.int32, (1, LANES), 1)
    nope = lane < MLA_NOPE
    rope_a = jnp.logical_and(lane >= MLA_NOPE, lane < MLA_NOPE + MLA_ROPE // 2)
    rope_b = jnp.logical_and(lane >= MLA_NOPE + MLA_ROPE // 2, lane < MLA_NOPE + MLA_ROPE)
    return nope, rope_a, rope_b


def _rope_block(x, cos, sin, rope_a, rope_b):
    half = MLA_ROPE // 2
    rot = jnp.where(rope_a, -pltpu.roll(x, LANES - half, 1), jnp.where(rope_b, pltpu.roll(x, half, 1), 0.0))
    return x * cos + rot * sin


def _k_heads(k, kr_new, gk, k_out):
    for h in range(MLA_HEADS):
        x = k[:, h * LANES:(h + 1) * LANES]
        inv = lax.rsqrt(jnp.sum(x * x, -1, keepdims=True) / MLA_NOPE + EPS)
        k_out[:, h * LANES:(h + 1) * LANES] = (x * inv * gk + kr_new).astype(BF16)


def _mla_prep_body(cq_ref, ckv_ref, kr_ref, cos_ref, sin_ref, qn_ref, wuq_ref, kvn_ref, wk_ref, wvt_ref,
                   gq_ref, gk_ref, q_out, k_out, vt_out, ckv_out, kr_out, *, tiles_per_batch, key_tile_chunks):
    nope, rope_a, rope_b = _lane_masks()
    rope_l = jnp.logical_or(rope_a, rope_b)
    cos = cos_ref[...]
    sin = sin_ref[...]
    gq = gq_ref[...]
    gk = gk_ref[...]
    scale = (MLA_NOPE + MLA_ROPE) ** -0.5 * LOG2_E

    cqn = _rms(cq_ref[...], qn_ref[...]).astype(BF16)
    q = _dot(cqn, wuq_ref[...])
    for h in range(MLA_HEADS):
        x = q[:, h * LANES:(h + 1) * LANES]
        x2 = x * x
        sn = jnp.sum(jnp.where(nope, x2, 0.0), -1, keepdims=True)
        sr = jnp.sum(jnp.where(rope_l, x2, 0.0), -1, keepdims=True)
        inv = jnp.where(nope, lax.rsqrt(sn / MLA_NOPE + EPS), lax.rsqrt(sr / MLA_ROPE + EPS))
        x = _rope_block(x * inv * gq, cos, sin, rope_a, rope_b) * scale
        q_out[:, h * LANES:(h + 1) * LANES] = x.astype(BF16)

    ckvn = _rms(ckv_ref[...], kvn_ref[...])
    ckv_out[...] = ckvn
    cb = ckvn.astype(BF16)
    krp = kr_ref[...]
    krn = krp * lax.rsqrt(jnp.sum(krp * krp, -1, keepdims=True) / MLA_ROPE + EPS) * jnp.where(nope, 0.0, gk)
    kr_new = _rope_block(krn, cos, sin, rope_a, rope_b)
    kr_out[...] = kr_new
    vt_out[...] = _dot_nt(wvt_ref[...], cb).astype(BF16)
    shared_lanes = kr_new
    if key_tile_chunks is not None:
        tm = kr_new.shape[0]
        pos = (pl.program_id(0) % tiles_per_batch) * tm + lax.broadcasted_iota(jnp.int32, (tm, 1), 0)
        lane = lax.broadcasted_iota(jnp.int32, (1, LANES), 1)
        chunk = (pos >> CHUNK_SHIFT) & (key_tile_chunks - 1)
        shared_lanes = kr_new + (lane - MASK_LANE0 == chunk).astype(F32)
    _k_heads(_dot(cb, wk_ref[...]), shared_lanes, jnp.where(nope, gk, 0.0), k_out)


def _vt_spec(tm, tiles_per_batch):
    return pl.BlockSpec((None, MLA_HEADS * MLA_V, tm), lambda i: (i // tiles_per_batch, 0, i % tiles_per_batch))


def _mla_prep(cq, ckv, krp, cos, sin, w, *, batch, key_tile=None):
    rows = cq.shape[0]
    tm = _row_tile(rows, 512)
    tiles_per_batch = rows // batch // tm
    key_tile_chunks = None if key_tile is None else key_tile // CHUNK
    assert key_tile_chunks is None or (key_tile_chunks & (key_tile_chunks - 1) == 0
                                       and key_tile_chunks <= LANES - MASK_LANE0)
    q_norm, w_uq, kv_norm, w_k, w_vt, gq, gk = w
    row_spec = lambda n: pl.BlockSpec((tm, n), lambda i: (i, 0))
    tab_spec = pl.BlockSpec((tm, LANES), lambda i: (i % tiles_per_batch, 0))
    return pl.pallas_call(
        functools.partial(_mla_prep_body, tiles_per_batch=tiles_per_batch, key_tile_chunks=key_tile_chunks),
        grid=(rows // tm,),
        in_specs=[row_spec(MLA_Q_RANK), row_spec(MLA_KV_RANK), row_spec(LANES), tab_spec, tab_spec,
                  _full_spec(q_norm.shape), _full_spec(w_uq.shape), _full_spec(kv_norm.shape),
                  _full_spec(w_k.shape), _full_spec(w_vt.shape), _full_spec(gq.shape), _full_spec(gk.shape)],
        out_specs=[row_spec(MLA_PAD), row_spec(MLA_PAD), _vt_spec(tm, tiles_per_batch), row_spec(MLA_KV_RANK),
                   row_spec(LANES)],
        out_shape=[jax.ShapeDtypeStruct((rows, MLA_PAD), BF16), jax.ShapeDtypeStruct((rows, MLA_PAD), BF16),
                   jax.ShapeDtypeStruct((batch, MLA_HEADS * MLA_V, rows // batch), BF16),
                   jax.ShapeDtypeStruct((rows, MLA_KV_RANK), F32), jax.ShapeDtypeStruct((rows, LANES), F32)],
        compiler_params=_params(1),
        name="mla_prep",
    )(cq, ckv, krp, cos, sin, q_norm, w_uq, kv_norm, w_k, w_vt, gq, gk)


def _kv_up_body(ckv_ref, kr_ref, wk_ref, wvt_ref, gk_ref, k_out, vt_out):
    nope, _, _ = _lane_masks()
    cb = ckv_ref[...].astype(BF16)
    vt_out[...] = _dot_nt(wvt_ref[...], cb).astype(BF16)
    _k_heads(_dot(cb, wk_ref[...]), kr_ref[...], jnp.where(nope, gk_ref[...], 0.0), k_out)


def _kv_up(ckv, krp, w_k, w_vt, gk, *, batch):
    rows = ckv.shape[0]
    tm = _row_tile(rows, 512)
    tiles_per_batch = rows // batch // tm
    row_spec = lambda n: pl.BlockSpec((tm, n), lambda i: (i, 0))
    return pl.pallas_call(
        _kv_up_body,
        grid=(rows // tm,),
        in_specs=[row_spec(MLA_KV_RANK), row_spec(LANES), _full_spec(w_k.shape), _full_spec(w_vt.shape),
                  _full_spec(gk.shape)],
        out_specs=[row_spec(MLA_PAD), _vt_spec(tm, tiles_per_batch)],
        out_shape=[jax.ShapeDtypeStruct((rows, MLA_PAD), BF16),
                   jax.ShapeDtypeStruct((batch, MLA_HEADS * MLA_V, rows // batch), BF16)],
        compiler_params=_params(1),
        name="mla_kv_up",
    )(ckv, krp, w_k, w_vt, gk)


def _attn_body(qi_ref, kj_ref, *refs, tq, tk, n_key_tiles, causal, nk_valid, has_ext):
    if has_ext:
        q_ref, k_ref, vt_ref, ke_ref, vte_ref, o_ref, m_scr, l_scr, acc_scr = refs
    else:
        q_ref, k_ref, vt_ref, o_ref, m_scr, l_scr, acc_scr = refs
    step = pl.program_id(1)
    i = qi_ref[step]
    j = kj_ref[step]
    last_j = ((i + 1) * tq + tk - 1) // tk - 1 if causal else n_key_tiles - 1
    head_lanes = [slice(h * LANES, (h + 1) * LANES) for h in range(MLA_HEADS)]

    def update(k_blk, vt_blk, bias=None, q_mask=None):
        if q_mask is None:
            q_heads = [q_ref[:, hs] for hs in head_lanes]
        else:
            q_heads = [q_ref[:, hs] + q_mask for hs in head_lanes]
        scores = [_dot_nt(k_blk[:, hs], qh) for hs, qh in zip(head_lanes, q_heads)]
        for h in range(MLA_HEADS):
            s = scores[h]
            if bias is not None:
                s = s + bias
            m_old = m_scr[h][0:1]
            m_new = jnp.maximum(m_old, jnp.max(s, 0, keepdims=True))
            alpha = jnp.exp2(m_old - m_new)
            p = jnp.exp2(s - m_new)
            l_new = alpha * l_scr[h][0:1] + jnp.sum(p, 0, keepdims=True)
            m_scr[h] = jnp.broadcast_to(m_new, (HALO, tq))
            l_scr[h] = jnp.broadcast_to(l_new, (HALO, tq))
            acc_scr[h] = acc_scr[h] * alpha + _dot(vt_blk[h * MLA_V:(h + 1) * MLA_V, :], p.astype(BF16))

    @pl.when(j == 0)
    def _():
        m_scr[...] = jnp.full(m_scr.shape, NEG_INF, F32)
        l_scr[...] = jnp.zeros(l_scr.shape, F32)
        acc_scr[...] = jnp.zeros(acc_scr.shape, F32)
        if has_ext:
            update(ke_ref, vte_ref)

    if causal:
        @pl.when(j < last_j)
        def _():
            update(k_ref, vt_ref)

        @pl.when(j == last_j)
        def _():
            lane = lax.broadcasted_iota(jnp.int32, (1, LANES), 1)
            q_chunk = (i * tq + lax.broadcasted_iota(jnp.int32, (tq, 1), 0)) >> CHUNK_SHIFT
            key_chunk = j * (tk // CHUNK) + lane - MASK_LANE0
            hidden = jnp.logical_and(jnp.logical_and(lane >= MASK_LANE0, lane < MASK_LANE0 + tk // CHUNK),
                                     key_chunk > q_chunk)
            update(k_ref, vt_ref, q_mask=jnp.where(hidden, -MASK_BIG, 0.0).astype(BF16))
    elif nk_valid is not None:
        kpos = j * tk + lax.broadcasted_iota(jnp.int32, (tk, 1), 0)
        update(k_ref, vt_ref, bias=jnp.where(kpos < nk_valid, 0.0, NEG_INF))
    else:
        update(k_ref, vt_ref)

    @pl.when(j == last_j)
    def _():
        for h in range(MLA_HEADS):
            o_ref[h * MLA_V:(h + 1) * MLA_V, :] = (acc_scr[h] / l_scr[h][0:1]).astype(BF16)


def _attention(q, k, vt, ext, *, tq, tk, causal=False, nk_valid=None):
    b, nq, _ = q.shape
    nk = k.shape[1]
    n_key_tiles = nk // tk
    assert not causal or tk % tq == 0
    pairs = [(i, j) for i in range(nq // tq) for j in range(n_key_tiles) if not causal or j * tk < (i + 1) * tq]
    qi = jnp.asarray([p[0] for p in pairs], jnp.int32)
    kj = jnp.asarray([p[1] for p in pairs], jnp.int32)
    hv = MLA_HEADS * MLA_V
    in_specs = [
        pl.BlockSpec((None, tq, MLA_PAD), lambda bi, s, qi, kj: (bi, qi[s], 0)),
        pl.BlockSpec((None, tk, MLA_PAD), lambda bi, s, qi, kj: (bi, kj[s], 0)),
        pl.BlockSpec((None, hv, tk), lambda bi, s, qi, kj: (bi, 0, kj[s])),
    ]
    args = [q, k, vt]
    if ext is not None:
        ne = ext[0].shape[1]
        shared = ext[0].shape[0] == 1 and b > 1
        emap = (lambda bi, s, qi, kj: (0, 0, 0)) if shared else (lambda bi, s, qi, kj: (bi, 0, 0))
        in_specs += [pl.BlockSpec((None, ne, MLA_PAD), emap), pl.BlockSpec((None, hv, ne), emap)]
        args += list(ext)
    return pl.pallas_call(
        functools.partial(_attn_body, tq=tq, tk=tk, n_key_tiles=n_key_tiles, causal=causal, nk_valid=nk_valid,
                          has_ext=ext is not None),
        grid_spec=pltpu.PrefetchScalarGridSpec(
            num_scalar_prefetch=2,
            grid=(b, len(pairs)),
            in_specs=in_specs,
            out_specs=pl.BlockSpec((None, hv, tq), lambda bi, s, qi, kj: (bi, 0, qi[s])),
            scratch_shapes=[pltpu.VMEM((MLA_HEADS, HALO, tq), F32), pltpu.VMEM((MLA_HEADS, HALO, tq), F32),
                            pltpu.VMEM((MLA_HEADS, MLA_V, tq), F32)]),
        out_shape=jax.ShapeDtypeStruct((b, hv, nq), BF16),
        compiler_params=_params(2),
        name="mla_attention",
    )(qi, kj, *args)


def _outproj_body(x_ref, og_ref, omt_ref, w_ref, o_ref):
    o_ref[...] = x_ref[...] + _dot(og_ref[...], w_ref[:GDN_Z, :]) + _dot_tn(omt_ref[...], w_ref[GDN_Z:, :])


def _outproj(x, og, omt, w, *, batch):
    rows = x.shape[0]
    tm = _row_tile(rows, 512)
    tiles_per_batch = rows // batch // tm
    half = GDN_Z
    return pl.pallas_call(
        _outproj_body,
        grid=(rows // tm,),
        in_specs=[pl.BlockSpec((tm, D_MODEL), lambda i: (i, 0)), pl.BlockSpec((tm, half), lambda i: (i, 0)),
                  pl.BlockSpec((None, half, tm), lambda i: (i // tiles_per_batch, 0, i % tiles_per_batch)),
                  _full_spec(w.shape)],
        out_specs=pl.BlockSpec((tm, D_MODEL), lambda i: (i, 0)),
        out_shape=jax.ShapeDtypeStruct((rows, D_MODEL), F32),
        compiler_params=_params(1),
        name="mixer_outproj",
    )(x, og, omt, w)


def _sconv_body(x_ref, g_ref, win_ref, cw_ref, wout_ref, prev_ref, o_ref, pout_ref, halo_scr, *, rows):
    t = pl.program_id(1)

    @pl.when(t == 0)
    def _():
        halo_scr[...] = prev_ref[...]

    x = x_ref[...]
    hn = _rms(x, g_ref[...]).astype(BF16)
    bg = _dot(hn, win_ref[:, :D_MODEL])
    u = _dot(hn, win_ref[:, D_MODEL:2 * D_MODEL]) * _dot(hn, win_ref[:, 2 * D_MODEL:])
    halo = halo_scr[...]
    row = lax.broadcasted_iota(jnp.int32, (rows, 1), 0)
    u1 = jnp.where(row == 0, halo[HALO - 1:HALO, :], pltpu.roll(u, 1, 0))
    u2 = jnp.where(row == 0, halo[HALO - 2:HALO - 1, :],
                   jnp.where(row == 1, halo[HALO - 1:HALO, :], pltpu.roll(u, 2, 0)))
    cw = cw_ref[...]
    y = u2 * cw[0:1] + u1 * cw[1:2] + u * cw[2:3]
    last_rows = u[rows - HALO:rows, :]
    halo_scr[...] = last_rows
    o_ref[...] = x + _dot((bg * y).astype(BF16), wout_ref[...])

    @pl.when(t == pl.num_programs(1) - 1)
    def _():
        pout_ref[...] = last_rows


def _sconv(x, prev, w, *, rows):
    b, t_len, _ = x.shape
    g, w_in, conv_w, w_out = w
    shared = prev.shape[0] == 1 and b > 1
    bsel = (lambda bi: 0) if shared else (lambda bi: bi)
    return pl.pallas_call(
        functools.partial(_sconv_body, rows=rows),
        grid=(b, t_len // rows),
        in_specs=[pl.BlockSpec((None, rows, D_MODEL), lambda bi, ti: (bi, ti, 0)), _full_spec(g.shape),
                  _full_spec(w_in.shape), _full_spec(conv_w.shape), _full_spec(w_out.shape),
                  pl.BlockSpec((None, HALO, D_MODEL), lambda bi, ti: (bsel(bi), 0, 0))],
        out_specs=[pl.BlockSpec((None, rows, D_MODEL), lambda bi, ti: (bi, ti, 0)),
                   pl.BlockSpec((None, HALO, D_MODEL), lambda bi, ti: (bi, 0, 0))],
        out_shape=[jax.ShapeDtypeStruct((b, t_len, D_MODEL), F32), jax.ShapeDtypeStruct((b, HALO, D_MODEL), F32)],
        scratch_shapes=[pltpu.VMEM((HALO, D_MODEL), F32)],
        compiler_params=_params(2),
        name="sconv_mixer",
    )(x, g, w_in, conv_w, w_out, prev)


def _pad_lanes(a, before, total):
    return jnp.pad(a, [(0, 0)] * (a.ndim - 1) + [(before, total - before - a.shape[-1])])


def _rope_tables(pos):
    half = MLA_ROPE // 2
    inv = ROPE_THETA ** (-jnp.arange(half, dtype=F32) / half)
    ang = pos.astype(F32)[:, None] * inv
    cos, sin = jnp.cos(ang), jnp.sin(ang)
    n = pos.shape[0]
    tail = LANES - MLA_NOPE - MLA_ROPE
    cos_l = jnp.concatenate([jnp.ones((n, MLA_NOPE), F32), cos, cos, jnp.ones((n, tail), F32)], -1)
    sin_l = jnp.concatenate([jnp.zeros((n, MLA_NOPE), F32), sin, sin, jnp.zeros((n, tail), F32)], -1)
    return cos_l, sin_l


def _tail_rows(a, n):
    return jnp.pad(a, ((0, 0), (HALO - n, 0), (0, 0)))


def kernel(x_prompt, x_sample, cache_mla_ckv, cache_mla_krope, state_gdn_S, state_gdn_conv, state_sconv, meta_tokens, ffn1_norm, ffn1_w_gate, ffn1_w_up, ffn1_w_down, ffn2_norm, ffn2_w_gate, ffn2_w_up, ffn2_w_down, mix_norm, ab_w_in, ab_w_out, gdn_conv_w, gdn_A_log, gdn_dt_bias, gdn_o_norm, mla_q_norm, mla_w_uq, mla_kv_norm, mla_w_ukv, mla_qn_norm, mla_qr_norm, mla_kn_norm, mla_kr_norm, sc_w_in, sc_conv_w, sc_w_out):
    bp, tp, _ = x_prompt.shape
    bs, ts, _ = x_sample.shape
    n_frames = bp * tp
    n_sample = bs * ts

    ffn1 = [_ffn_weights(ffn1_norm[l], ffn1_w_gate[l], ffn1_w_up[l], ffn1_w_down[l]) for l in range(2)]
    ffn2 = [_ffn_weights(ffn2_norm[l], ffn2_w_gate[l], ffn2_w_up[l], ffn2_w_down[l]) for l in range(2)]
    w_in = ab_w_in[0]
    s1 = GDN_QKV + GDN_Z
    c0 = s1 + 2 * GDN_HEADS
    w_in_packed = jnp.concatenate([
        w_in[:, :s1],
        _pad_lanes(w_in[:, s1:c0], 0, LANES),
        w_in[:, c0:c0 + MLA_Q_RANK + MLA_KV_RANK],
        _pad_lanes(w_in[:, c0 + MLA_Q_RANK + MLA_KV_RANK:], MLA_NOPE, LANES),
    ], -1).astype(BF16)
    mix_g = mix_norm.reshape(2, 1, D_MODEL)
    gdn_w = (gdn_conv_w[0], _pad_lanes(gdn_A_log[0][None], 0, LANES), _pad_lanes(gdn_dt_bias[0][None], 0, LANES),
             gdn_o_norm[0][None])
    qdim = MLA_NOPE + MLA_ROPE
    w_uq = _pad_lanes(mla_w_uq[0].reshape(MLA_Q_RANK, MLA_HEADS, qdim), 0, LANES).reshape(MLA_Q_RANK, MLA_PAD)
    w_ukv = mla_w_ukv[0].reshape(MLA_KV_RANK, MLA_HEADS, MLA_NOPE + MLA_V)
    w_k = _pad_lanes(w_ukv[..., :MLA_NOPE], 0, LANES).reshape(MLA_KV_RANK, MLA_PAD)
    hv = MLA_HEADS * MLA_V
    w_vt = w_ukv[..., MLA_NOPE:].reshape(MLA_KV_RANK, hv).T
    gq = _pad_lanes(jnp.concatenate([mla_qn_norm[0], mla_qr_norm[0]])[None], 0, LANES)
    gk = _pad_lanes(jnp.concatenate([mla_kn_norm[0], mla_kr_norm[0]])[None], 0, LANES)
    mla_w = (mla_q_norm[0][None], w_uq.astype(BF16), mla_kv_norm[0][None], w_k.astype(BF16), w_vt.astype(BF16), gq, gk)
    w_out = ab_w_out[0].astype(BF16)
    sc_w = (mix_g[1], sc_w_in[0].astype(BF16), sc_conv_w[0], sc_w_out[0].astype(BF16))

    xf = x_prompt.reshape(n_frames, D_MODEL)
    xs = jnp.concatenate([x_sample.reshape(n_sample, D_MODEL), meta_tokens.astype(F32)], 0)
    cos_f, sin_f = _rope_tables(N_META + jnp.arange(tp))
    pos_small = jnp.concatenate([jnp.tile(PAST_LEN + jnp.arange(ts), bs), jnp.arange(N_META)])
    cos_s, sin_s = _rope_tables(pos_small)

    xf = _ffn(xf, ffn1[0])
    xs = _ffn(xs, ffn1[0])
    qkv_f, z_f, ab_f, cq_f, ckv_f, kr_f = _inproj(xf, mix_g[0], w_in_packed)
    qkv_s, z_s, ab_s, cq_s, ckv_s, kr_s = _inproj(xs, mix_g[0], w_in_packed)

    pad_meta = lambda a: jnp.pad(a[n_sample:], ((0, CHUNK - N_META), (0, 0)))[None]
    og_m, s_m, conv_m = _gdn(pad_meta(qkv_s), pad_meta(z_s), pad_meta(ab_s),
                             jnp.zeros((1, GDN_HEADS, GDN_DK, GDN_DV), F32), jnp.zeros((1, HALO, GDN_QKV), F32),
                             gdn_w, rows=CHUNK, n_valid=N_META)
    og_f, p_s, p_conv = _gdn(qkv_f.reshape(bp, tp, GDN_QKV), z_f.reshape(bp, tp, GDN_Z), ab_f.reshape(bp, tp, LANES),
                             s_m, conv_m, gdn_w, rows=4 * CHUNK)
    og_s, s_s, s_conv = _gdn(qkv_s[:n_sample].reshape(bs, ts, GDN_QKV), z_s[:n_sample].reshape(bs, ts, GDN_Z),
                             ab_s[:n_sample].reshape(bs, ts, LANES), state_gdn_S[0],
                             _tail_rows(state_gdn_conv[0], state_gdn_conv.shape[2]), gdn_w, rows=ts)

    q_f, k_f, vt_f, ckvn_f, krn_f = _mla_prep(cq_f, ckv_f, kr_f, cos_f, sin_f, mla_w, batch=bp, key_tile=ATTN_TK)
    q_s, k_s, vt_s, ckvn_s, krn_s = _mla_prep(cq_s, ckv_s, kr_s, cos_s, sin_s, mla_w, batch=1)
    k_c, vt_c = _kv_up(cache_mla_ckv[0].reshape(bs * PAST_LEN, MLA_KV_RANK),
                       _pad_lanes(cache_mla_krope[0].reshape(bs * PAST_LEN, MLA_ROPE), MLA_NOPE, LANES),
                       mla_w[3], mla_w[4], gk, batch=bs)
    k_meta, vt_meta = k_s[n_sample:][None], vt_s[:, :, n_sample:]
    vt_new = vt_s[0, :, :n_sample].reshape(hv, bs, ts).transpose(1, 0, 2)
    omt_f = _attention(q_f.reshape(bp, tp, MLA_PAD), k_f.reshape(bp, tp, MLA_PAD), vt_f, (k_meta, vt_meta),
                       tq=ATTN_TQ, tk=ATTN_TK, causal=True)
    omt_s = _attention(q_s[:n_sample].reshape(bs, ts, MLA_PAD), k_c.reshape(bs, PAST_LEN, MLA_PAD), vt_c,
                       (k_s[:n_sample].reshape(bs, ts, MLA_PAD), vt_new), tq=ts, tk=512)
    omt_m = _attention(jnp.pad(q_s[n_sample:], ((0, LANES - N_META), (0, 0)))[None],
                       jnp.pad(k_s[n_sample:], ((0, CHUNK - N_META), (0, 0)))[None],
                       jnp.pad(vt_meta, ((0, 0), (0, 0), (0, CHUNK - N_META))),
                       None, tq=LANES, tk=CHUNK, nk_valid=N_META)

    og_small = jnp.concatenate([og_s.reshape(n_sample, GDN_Z), og_m[0, :N_META]], 0)
    omt_small = jnp.concatenate([omt_s.transpose(1, 0, 2).reshape(hv, n_sample), omt_m[0, :, :N_META]], -1)[None]
    xf = _outproj(xf, og_f.reshape(n_frames, GDN_Z), omt_f, w_out, batch=bp)
    xs = _outproj(xs, og_small, omt_small, w_out, batch=1)
    xf = _ffn(xf, ffn2[0])
    xs = _ffn(xs, ffn2[0])

    xf = _ffn(xf, ffn1[1])
    xs = _ffn(xs, ffn1[1])
    n_sc = state_sconv.shape[2]
    xm, sc_m = _sconv(xs[n_sample:][None], jnp.zeros((1, HALO, D_MODEL), F32), sc_w, rows=N_META)
    xf3, p_sc = _sconv(xf.reshape(bp, tp, D_MODEL), sc_m, sc_w, rows=512)
    xs3, s_sc = _sconv(xs[:n_sample].reshape(bs, ts, D_MODEL), _tail_rows(state_sconv[0], n_sc), sc_w, rows=ts)
    xf = _ffn(xf3.reshape(n_frames, D_MODEL), ffn2[1])
    xs = _ffn(jnp.concatenate([xs3.reshape(n_sample, D_MODEL), xm[0]], 0), ffn2[1])

    y_prompt = xf.reshape(bp, tp, D_MODEL)
    y_sample = xs[:n_sample].reshape(bs, ts, D_MODEL)
    rope_lo, rope_hi = MLA_NOPE, MLA_NOPE + MLA_ROPE
    meta_ckv = jnp.broadcast_to(ckvn_s[n_sample:][None], (bp, N_META, MLA_KV_RANK))
    meta_kr = jnp.broadcast_to(krn_s[n_sample:, rope_lo:rope_hi][None], (bp, N_META, MLA_ROPE))
    p_ckv = jnp.concatenate([meta_ckv, ckvn_f.reshape(bp, tp, MLA_KV_RANK)], 1)
    p_kr = jnp.concatenate([meta_kr, krn_f[:, rope_lo:rope_hi].reshape(bp, tp, MLA_ROPE)], 1)
    n_gc = state_gdn_conv.shape[2]
    return (y_prompt, y_sample,
            p_ckv[None], p_kr[None], p_s[None], p_conv[:, HALO - n_gc:][None], p_sc[:, HALO - n_sc:][None],
            ckvn_s[:n_sample].reshape(bs, ts, MLA_KV_RANK)[None],
            krn_s[:n_sample, rope_lo:rope_hi].reshape(bs, ts, MLA_ROPE)[None],
            s_s[None], s_conv[:, HALO - n_gc:][None], s_sc[:, HALO - n_sc:][None])
```

```python
import functools

import jax
import jax.numpy as jnp
from jax import lax
from jax.experimental import pallas as pl
from jax.experimental.pallas import tpu as pltpu

F32 = jnp.float32
BF16 = jnp.bfloat16

D_MODEL = 1024
D_FF = 2816
CHUNK = 64
CHUNK_SHIFT = 6
N_META = 16
EPS = 1e-6
GDN_HEADS = 4
GDN_DK = 128
GDN_DV = 128
GDN_QKV = GDN_HEADS * (2 * GDN_DK + GDN_DV)
GDN_Z = GDN_HEADS * GDN_DV
MLA_HEADS = 8
MLA_NOPE = 64
MLA_ROPE = 32
MLA_V = 64
MLA_Q_RANK = 384
MLA_KV_RANK = 256
ROPE_THETA = 10000.0
PAST_LEN = 1024
LANES = 128
HALO = 8
MLA_PAD = MLA_HEADS * LANES
FF_BLOCK = 256
FF_CHUNKS = D_FF // FF_BLOCK
VMEM_LIMIT = 60000 * 1024
NEG_INF = float("-inf")
LOG2_E = 1.4426950408889634
MASK_LANE0 = MLA_NOPE + MLA_ROPE
MASK_BIG = 2.0 ** 100
ATTN_TQ = 512
ATTN_TK = 512
FFN_ROWS = 1024

IN_QKV = 0
IN_Z = IN_QKV + GDN_QKV
IN_AB = IN_Z + GDN_Z
IN_CQ = IN_AB + LANES
IN_CKV = IN_CQ + MLA_Q_RANK
IN_KR = IN_CKV + MLA_KV_RANK
IN_TOTAL = IN_KR + LANES


def _params(n_axes):
    return pltpu.CompilerParams(dimension_semantics=("arbitrary",) * n_axes, vmem_limit_bytes=VMEM_LIMIT)


def _rms(x, g):
    return x * lax.rsqrt(jnp.mean(x * x, -1, keepdims=True) + EPS) * g


def _sigmoid(x):
    return 1.0 / (1.0 + jnp.exp(-x))


def _silu(x):
    return x * _sigmoid(x)


def _softplus(x):
    return jnp.maximum(x, 0.0) + jnp.log1p(jnp.exp(-jnp.abs(x)))


def _dot(a, b):
    return jnp.dot(a, b, preferred_element_type=F32)


def _dot_nt(a, b):
    return lax.dot_general(a, b, (((1,), (1,)), ((), ())), preferred_element_type=F32)


def _dot_tn(a, b):
    return lax.dot_general(a, b, (((0,), (0,)), ((), ())), preferred_element_type=F32)


def _full_spec(shape):
    zeros = (0,) * len(shape)
    return pl.BlockSpec(shape, lambda *_: zeros)


def _row_tile(rows, want):
    return want if rows % want == 0 else rows


def _ffn_body(x_ref, g_ref, wg_ref, wu_ref, wd_ref, *rest, mixed):
    if mixed:
        og_ref, omt_ref, wmix_ref, o_ref, acc_ref = rest
        x = x_ref[...] + _dot(og_ref[...], wmix_ref[:GDN_Z, :]) + _dot_tn(omt_ref[...], wmix_ref[GDN_Z:, :])
    else:
        o_ref, acc_ref = rest
        x = x_ref[...]
    h = _rms(x, g_ref[...]).astype(BF16)
    for c in range(FF_CHUNKS):
        cols = slice(c * FF_BLOCK, (c + 1) * FF_BLOCK)
        gate = _dot(h, wg_ref[:, cols])
        up = _dot(h, wu_ref[:, cols])
        act = (_silu(gate) * up).astype(BF16)
        down = _dot(act, wd_ref[cols, :])
        if c == 0:
            acc_ref[...] = down
        else:
            acc_ref[...] += down
    o_ref[...] = x + 0.5 * acc_ref[...]


def _ffn(x, w, mix=None):
    rows = x.shape[0]
    tm = _row_tile(rows, FFN_ROWS)
    g, wg, wu, wd = w
    resident = lambda a: pl.BlockSpec(a.shape, lambda i: (0,) * a.ndim, pipeline_mode=pl.Buffered(1))
    in_specs = [pl.BlockSpec((tm, D_MODEL), lambda i: (i, 0)), _full_spec(g.shape), resident(wg), resident(wu),
                resident(wd)]
    args = [x, g, wg, wu, wd]
    if mix is not None:
        og, omt, w_mix = mix
        tiles_per_batch = rows // omt.shape[0] // tm
        in_specs += [pl.BlockSpec((tm, GDN_Z), lambda i: (i, 0)),
                     pl.BlockSpec((None, GDN_Z, tm), lambda i: (i // tiles_per_batch, 0, i % tiles_per_batch)),
                     resident(w_mix)]
        args += [og, omt, w_mix]
    return pl.pallas_call(
        functools.partial(_ffn_body, mixed=mix is not None),
        grid=(rows // tm,),
        in_specs=in_specs,
        out_specs=pl.BlockSpec((tm, D_MODEL), lambda i: (i, 0)),
        out_shape=jax.ShapeDtypeStruct((rows, D_MODEL), F32),
        scratch_shapes=[pltpu.VMEM((tm, D_MODEL), F32)],
        compiler_params=_params(1),
        name="half_ffn",
    )(*args)


def _ffn_weights(norm, w_gate, w_up, w_down):
    return norm.reshape(1, D_MODEL), w_gate.astype(BF16), w_up.astype(BF16), w_down.astype(BF16)


def _inproj_body(x_ref, g_ref, w_ref, qkv_ref, z_ref, ab_ref, cq_ref, ckv_ref, kr_ref):
    hn = _rms(x_ref[...], g_ref[...]).astype(BF16)
    qkv_ref[...] = _dot(hn, w_ref[:, IN_QKV:IN_Z])
    z_ref[...] = _dot(hn, w_ref[:, IN_Z:IN_AB])
    ab_ref[...] = _dot(hn, w_ref[:, IN_AB:IN_CQ])
    cq_ref[...] = _dot(hn, w_ref[:, IN_CQ:IN_CKV])
    ckv_ref[...] = _dot(hn, w_ref[:, IN_CKV:IN_KR])
    kr_ref[...] = _dot(hn, w_ref[:, IN_KR:IN_TOTAL])


def _inproj(x, g, w):
    rows = x.shape[0]
    tm = _row_tile(rows, 512)
    widths = (GDN_QKV, GDN_Z, LANES, MLA_Q_RANK, MLA_KV_RANK, LANES)
    return pl.pallas_call(
        _inproj_body,
        grid=(rows // tm,),
        in_specs=[pl.BlockSpec((tm, D_MODEL), lambda i: (i, 0)), _full_spec(g.shape), _full_spec(w.shape)],
        out_specs=[pl.BlockSpec((tm, n), lambda i: (i, 0)) for n in widths],
        out_shape=[jax.ShapeDtypeStruct((rows, n), F32) for n in widths],
        compiler_params=_params(1),
        name="mixer_inproj",
    )(x, g, w)


class _Split:
    def __init__(self, x):
        self.x = x
        self._parts = None
        self._diag = None

    def parts(self):
        if self._parts is None:
            hi = self.x.astype(BF16)
            self._parts = (hi, (self.x - hi.astype(F32)).astype(BF16))
        return self._parts

    def diag_parts(self):
        if self._diag is None:
            first = lax.broadcasted_iota(jnp.int32, self.x.shape, 1) < CHUNK
            zero = jnp.zeros(self.x.shape, BF16)
            self._diag = tuple(jnp.concatenate([jnp.where(first, p, zero), jnp.where(first, zero, p)], 0)
                               for p in self.parts())
        return self._diag


def _split_products(pairs):
    parts = [(a.parts(), b.diag_parts()) for a, b in pairs]
    outs = []
    for (a_hi, a_lo), (b_hi, b_lo) in parts:
        a2 = jnp.concatenate([a_hi, a_lo], 0)
        r = _dot(a2, b_hi) + _dot(a2, b_lo)
        half = a_hi.shape[0]
        outs.append(_Split(r[:half] + r[half:]))
    return outs


def _unit_lower_inverses(ns, eye_f, m16, m32):
    k = len(ns)
    nd = [_Split(jnp.where(m16, n, 0.0)) for n in ns]
    n2 = _split_products([(x, x) for x in nd])
    lvl = _split_products([(x, x) for x in n2] +
                          [(_Split(eye_f - a.x), _Split(eye_f + b.x)) for a, b in zip(nd, n2)])
    n4, t = lvl[:k], lvl[k:]
    lvl = _split_products([(x, x) for x in n4] + [(a, _Split(eye_f + b.x)) for a, b in zip(t, n4)])
    n8, t = lvl[:k], lvl[k:]
    t = _split_products([(a, _Split(eye_f + b.x)) for a, b in zip(t, n8)])
    for keep in (jnp.logical_and(m32, jnp.logical_not(m16)), jnp.logical_not(m32)):
        off = [_Split(jnp.where(keep, n, 0.0)) for n in ns]
        x = _split_products(list(zip(off, t)))
        y = _split_products(list(zip(t, x)))
        t = [_Split(a.x - b.x) for a, b in zip(t, y)]
    return [a.x for a in t]


def _gdn_body(qkv_ref, z_ref, ab_ref, s0_ref, cprev_ref, cw_ref, alog_ref, dtb_ref, on_ref,
              og_ref, sout_ref, cout_ref, s_scr, halo_scr, *, rows, n_valid):
    t = pl.program_id(1)
    C = CHUNK

    @pl.when(t == 0)
    def _():
        s_scr[...] = s0_ref[...]
        halo_scr[...] = cprev_ref[...]

    x = qkv_ref[...]
    halo = halo_scr[...]
    row = lax.broadcasted_iota(jnp.int32, (rows, 1), 0)

    def shifted(k):
        r = pltpu.roll(x, k, 0)
        for j in range(k):
            r = jnp.where(row == j, halo[HALO - k + j:HALO - k + j + 1, :], r)
        return r

    cw = cw_ref[...]
    y = shifted(3) * cw[0:1] + shifted(2) * cw[1:2] + shifted(1) * cw[2:3] + x * cw[3:4]
    last_rows = x[n_valid - HALO:n_valid, :]
    halo_scr[...] = last_rows
    y = _silu(y)

    ab = ab_ref[...]
    g_all = -jnp.exp(alog_ref[...]) * _softplus(ab + dtb_ref[...])
    b_all = _sigmoid(ab)
    if n_valid < rows:
        valid = row < n_valid
        y = jnp.where(valid, y, 0.0)
        g_all = jnp.where(valid, g_all, 0.0)
        b_all = jnp.where(valid, b_all, 0.0)

    ii = lax.broadcasted_iota(jnp.int32, (C, LANES), 0)
    pair_lane = lax.broadcasted_iota(jnp.int32, (C, LANES), 1)
    jj = pair_lane & (C - 1)
    first_head = pair_lane < C
    causal = ii >= jj
    strict = ii > jj
    eye_f = (ii == jj).astype(F32)
    m16 = (ii >> 4) == (jj >> 4)
    m32 = (ii >> 5) == (jj >> 5)
    on = on_ref[...]
    n_chunks = rows // C
    heads = range(GDN_HEADS)

    ri = lax.broadcasted_iota(jnp.int32, (rows, rows), 0)
    rj = lax.broadcasted_iota(jnp.int32, (rows, rows), 1)
    tril_blk = jnp.logical_and(ri >= rj, (ri >> CHUNK_SHIFT) == (rj >> CHUNK_SHIFT)).astype(BF16)
    g_hi = g_all.astype(BF16)
    g_r1 = g_all - g_hi.astype(F32)
    g_mid = g_r1.astype(BF16)
    g_lo = (g_r1 - g_mid.astype(F32)).astype(BF16)
    gcum = _dot(tril_blk, g_hi) + _dot(tril_blk, g_mid) + _dot(tril_blk, g_lo)
    egcum = jnp.exp(gcum)

    qn, kn, vn = [], [], []
    for h in heads:
        qh = y[:, h * GDN_DK:(h + 1) * GDN_DK]
        kh = y[:, GDN_HEADS * GDN_DK + h * GDN_DK:GDN_HEADS * GDN_DK + (h + 1) * GDN_DK]
        qn.append(qh * lax.rsqrt(jnp.sum(qh * qh, -1, keepdims=True) + EPS) * (GDN_DK ** -0.5))
        kn.append(kh * lax.rsqrt(jnp.sum(kh * kh, -1, keepdims=True) + EPS))
        vn.append(y[:, 2 * GDN_HEADS * GDN_DK + h * GDN_DV:2 * GDN_HEADS * GDN_DK + (h + 1) * GDN_DV])

    head_pairs = [(a, a + 1) for a in range(0, GDN_HEADS, 2)]
    chains = [(c, pair) for c in range(n_chunks) for pair in head_pairs]
    pre = {}
    n_mats = []
    for c in range(n_chunks):
        sl = slice(c * C, (c + 1) * C)
        gcum_t = gcum[sl].T
        for a, b in head_pairs:
            col = lambda arr, h: arr[sl, h:h + 1]
            gcol = jnp.where(first_head, col(gcum, a), col(gcum, b))
            beta2 = jnp.where(first_head, col(b_all, GDN_HEADS + a), col(b_all, GDN_HEADS + b))
            grow = jnp.concatenate([gcum_t[a:a + 1, :], gcum_t[b:b + 1, :]], -1)
            dec_causal = jnp.exp(jnp.where(causal, gcol - grow, NEG_INF))
            kb = {h: kn[h][sl].astype(BF16) for h in (a, b)}
            kk = jnp.concatenate([_dot_nt(kb[a], kb[a]), _dot_nt(kb[b], kb[b])], -1)
            n_mats.append(beta2 * kk * jnp.where(strict, dec_causal, 0.0))
            qk = jnp.concatenate([_dot_nt(qn[h][sl].astype(BF16), kb[h]) for h in (a, b)], -1)
            qk = (qk * dec_causal).astype(BF16)
            for h, lanes in ((a, slice(0, C)), (b, slice(C, 2 * C))):
                beta = col(b_all, GDN_HEADS + h)
                eg = col(egcum, h)
                g_last = gcum[(c + 1) * C - 1:(c + 1) * C, h:h + 1]
                qh, kh, vh = qn[h][sl], kn[h][sl], vn[h][sl]
                pre[c, h] = dict(
                    qk=qk[:, lanes],
                    rhs=jnp.concatenate([(beta * vh).astype(BF16), ((beta * eg) * kh).astype(BF16)], -1),
                    qd=(qh * eg).astype(BF16),
                    kt=(kh * jnp.exp(g_last - col(gcum, h))).astype(BF16),
                    decay=jnp.exp(g_last))
    for (c, (a, b)), tinv in zip(chains, _unit_lower_inverses(n_mats, eye_f, m16, m32)):
        tinv = tinv.astype(BF16)
        for h, lanes in ((a, slice(0, C)), (b, slice(C, 2 * C))):
            uw = _dot(tinv[:, lanes], pre[c, h]["rhs"])
            pre[c, h]["u"] = uw[:, :GDN_DV]
            pre[c, h]["wk"] = uw[:, GDN_DV:].astype(BF16)

    state = [s_scr[h] for h in heads]
    for c in range(n_chunks):
        sl = slice(c * C, (c + 1) * C)
        sb = [s.astype(BF16) for s in state]
        w = [(pre[c, h]["u"] - _dot(pre[c, h]["wk"], sb[h])).astype(BF16) for h in heads]
        o = [_dot(pre[c, h]["qd"], sb[h]) + _dot(pre[c, h]["qk"], w[h]) for h in heads]
        state = [state[h] * pre[c, h]["decay"] + _dot_tn(pre[c, h]["kt"], w[h]) for h in heads]
        for h in heads:
            gated = _rms(o[h], on) * _silu(z_ref[sl, h * GDN_DV:(h + 1) * GDN_DV])
            og_ref[sl, h * GDN_DV:(h + 1) * GDN_DV] = gated.astype(BF16)
    for h in heads:
        s_scr[h] = state[h]

    @pl.when(t == pl.num_programs(1) - 1)
    def _():
        sout_ref[...] = s_scr[...]
        cout_ref[...] = last_rows


def _gdn(qkv, z, ab, s0, cprev, w, *, rows, n_valid=None):
    b, t_len, _ = qkv.shape
    n_valid = rows if n_valid is None else n_valid
    conv_w, a_log, dt_bias, o_norm = w
    shared = s0.shape[0] == 1 and b > 1
    bsel = (lambda bi: 0) if shared else (lambda bi: bi)
    return pl.pallas_call(
        functools.partial(_gdn_body, rows=rows, n_valid=n_valid),
        grid=(b, t_len // rows),
        in_specs=[
            pl.BlockSpec((None, rows, GDN_QKV), lambda bi, ti: (bi, ti, 0)),
            pl.BlockSpec((None, rows, GDN_Z), lambda bi, ti: (bi, ti, 0)),
            pl.BlockSpec((None, rows, LANES), lambda bi, ti: (bi, ti, 0)),
            pl.BlockSpec((None, GDN_HEADS, GDN_DK, GDN_DV), lambda bi, ti: (bsel(bi), 0, 0, 0)),
            pl.BlockSpec((None, HALO, GDN_QKV), lambda bi, ti: (bsel(bi), 0, 0)),
            _full_spec(conv_w.shape), _full_spec(a_log.shape), _full_spec(dt_bias.shape), _full_spec(o_norm.shape),
        ],
        out_specs=[
            pl.BlockSpec((None, rows, GDN_Z), lambda bi, ti: (bi, ti, 0)),
            pl.BlockSpec((None, GDN_HEADS, GDN_DK, GDN_DV), lambda bi, ti: (bi, 0, 0, 0)),
            pl.BlockSpec((None, HALO, GDN_QKV), lambda bi, ti: (bi, 0, 0)),
        ],
        out_shape=[
            jax.ShapeDtypeStruct((b, t_len, GDN_Z), BF16),
            jax.ShapeDtypeStruct((b, GDN_HEADS, GDN_DK, GDN_DV), F32),
            jax.ShapeDtypeStruct((b, HALO, GDN_QKV), F32),
        ],
        scratch_shapes=[pltpu.VMEM((GDN_HEADS, GDN_DK, GDN_DV), F32), pltpu.VMEM((HALO, GDN_QKV), F32)],
        compiler_params=_params(2),
        name="gdn_mixer",
    )(qkv, z, ab, s0, cprev, conv_w, a_log, dt_bias, o_norm)


def _lane_masks():
    lane = lax.broadcasted_iota(jnp.int32, (1, LANES), 1)
    nope = lane < MLA_NOPE
    rope_a = jnp.logical_and(lane >= MLA_NOPE, lane < MLA_NOPE + MLA_ROPE // 2)
    rope_b = jnp.logical_and(lane >= MLA_NOPE + MLA_ROPE // 2, lane < MLA_NOPE + MLA_ROPE)
    return nope, rope_a, rope_b


def _rope_block(x, cos, sin, rope_a, rope_b):
    half = MLA_ROPE // 2
    rot = jnp.where(rope_a, -pltpu.roll(x, LANES - half, 1), jnp.where(rope_b, pltpu.roll(x, half, 1), 0.0))
    return x * cos + rot * sin


PAIR = 2 * LANES


def _twice(a):
    return jnp.concatenate([a, a], -1)


def _k_heads(k, shared, gk, gmean, k_out):
    shared2, gk2 = _twice(shared), _twice(gk)
    for p in range(MLA_HEADS // 2):
        x = k[:, p * PAIR:(p + 1) * PAIR]
        ms = _dot((x * x).astype(BF16), gmean)
        k_out[:, p * PAIR:(p + 1) * PAIR] = (x * lax.rsqrt(ms + EPS) * gk2 + shared2).astype(BF16)


def _mla_prep_body(cq_ref, ckv_ref, kr_ref, cos_ref, sin_ref, qn_ref, wuq_ref, kvn_ref, wk_ref, wvt_ref,
                   gq_ref, gk_ref, gmean_ref, rot_ref, q_out, k_out, vt_out, ckv_out, kr_out,
                   *, tiles_per_batch, key_tile_chunks):
    nope, rope_a, rope_b = _lane_masks()
    cos = cos_ref[...]
    sin = sin_ref[...]
    gk = gk_ref[...]
    gmean = gmean_ref[...]
    rot_mat = rot_ref[...]
    gq2 = _twice(gq_ref[...] * ((MLA_NOPE + MLA_ROPE) ** -0.5 * LOG2_E))
    cos2, sin2 = _twice(cos), _twice(sin)

    cqn = _rms(cq_ref[...], qn_ref[...]).astype(BF16)
    q = _dot(cqn, wuq_ref[...])
    for p in range(MLA_HEADS // 2):
        x = q[:, p * PAIR:(p + 1) * PAIR]
        ms = _dot((x * x).astype(BF16), gmean)
        x = x * lax.rsqrt(ms + EPS) * gq2
        q_out[:, p * PAIR:(p + 1) * PAIR] = (x * cos2 + _dot(x.astype(BF16), rot_mat) * sin2).astype(BF16)

    ckvn = _rms(ckv_ref[...], kvn_ref[...])
    ckv_out[...] = ckvn
    cb = ckvn.astype(BF16)
    krp = kr_ref[...]
    krn = krp * lax.rsqrt(jnp.sum(krp * krp, -1, keepdims=True) / MLA_ROPE + EPS) * jnp.where(nope, 0.0, gk)
    kr_new = _rope_block(krn, cos, sin, rope_a, rope_b)
    kr_out[...] = kr_new
    vt_out[...] = _dot_nt(wvt_ref[...], cb).astype(BF16)
    shared_lanes = kr_new
    if key_tile_chunks is not None:
        tm = kr_new.shape[0]
        pos = (pl.program_id(0) % tiles_per_batch) * tm + lax.broadcasted_iota(jnp.int32, (tm, 1), 0)
        lane = lax.broadcasted_iota(jnp.int32, (1, LANES), 1)
        chunk = (pos >> CHUNK_SHIFT) & (key_tile_chunks - 1)
        shared_lanes = kr_new + (lane - MASK_LANE0 == chunk).astype(F32)
    _k_heads(_dot(cb, wk_ref[...]), shared_lanes, jnp.where(nope, gk, 0.0), gmean, k_out)


def _vt_spec(tm, tiles_per_batch):
    return pl.BlockSpec((None, MLA_HEADS * MLA_V, tm), lambda i: (i // tiles_per_batch, 0, i % tiles_per_batch))


def _mla_prep(cq, ckv, krp, cos, sin, w, *, batch, key_tile=None):
    rows = cq.shape[0]
    tm = _row_tile(rows, 512)
    tiles_per_batch = rows // batch // tm
    key_tile_chunks = None if key_tile is None else key_tile // CHUNK
    assert key_tile_chunks is None or (key_tile_chunks & (key_tile_chunks - 1) == 0
                                       and key_tile_chunks <= LANES - MASK_LANE0)
    q_norm, w_uq, kv_norm, w_k, w_vt, gq, gk, gmean, rot_mat = w
    row_spec = lambda n: pl.BlockSpec((tm, n), lambda i: (i, 0))
    tab_spec = pl.BlockSpec((tm, LANES), lambda i: (i % tiles_per_batch, 0))
    return pl.pallas_call(
        functools.partial(_mla_prep_body, tiles_per_batch=tiles_per_batch, key_tile_chunks=key_tile_chunks),
        grid=(rows // tm,),
        in_specs=[row_spec(MLA_Q_RANK), row_spec(MLA_KV_RANK), row_spec(LANES), tab_spec, tab_spec,
                  _full_spec(q_norm.shape), _full_spec(w_uq.shape), _full_spec(kv_norm.shape),
                  _full_spec(w_k.shape), _full_spec(w_vt.shape), _full_spec(gq.shape), _full_spec(gk.shape),
                  _full_spec(gmean.shape), _full_spec(rot_mat.shape)],
        out_specs=[row_spec(MLA_PAD), row_spec(MLA_PAD), _vt_spec(tm, tiles_per_batch), row_spec(MLA_KV_RANK),
                   row_spec(LANES)],
        out_shape=[jax.ShapeDtypeStruct((rows, MLA_PAD), BF16), jax.ShapeDtypeStruct((rows, MLA_PAD), BF16),
                   jax.ShapeDtypeStruct((batch, MLA_HEADS * MLA_V, rows // batch), BF16),
                   jax.ShapeDtypeStruct((rows, MLA_KV_RANK), F32), jax.ShapeDtypeStruct((rows, LANES), F32)],
        compiler_params=_params(1),
        name="mla_prep",
    )(cq, ckv, krp, cos, sin, q_norm, w_uq, kv_norm, w_k, w_vt, gq, gk, gmean, rot_mat)


def _kv_up_body(ckv_ref, kr_ref, wk_ref, wvt_ref, gk_ref, gmean_ref, k_out, vt_out):
    nope, _, _ = _lane_masks()
    cb = ckv_ref[...].astype(BF16)
    vt_out[...] = _dot_nt(wvt_ref[...], cb).astype(BF16)
    _k_heads(_dot(cb, wk_ref[...]), kr_ref[...], jnp.where(nope, gk_ref[...], 0.0), gmean_ref[...], k_out)


def _kv_up(ckv, krp, w_k, w_vt, gk, gmean, *, batch):
    rows = ckv.shape[0]
    tm = _row_tile(rows, 512)
    tiles_per_batch = rows // batch // tm
    row_spec = lambda n: pl.BlockSpec((tm, n), lambda i: (i, 0))
    return pl.pallas_call(
        _kv_up_body,
        grid=(rows // tm,),
        in_specs=[row_spec(MLA_KV_RANK), row_spec(LANES), _full_spec(w_k.shape), _full_spec(w_vt.shape),
                  _full_spec(gk.shape), _full_spec(gmean.shape)],
        out_specs=[row_spec(MLA_PAD), _vt_spec(tm, tiles_per_batch)],
        out_shape=[jax.ShapeDtypeStruct((rows, MLA_PAD), BF16),
                   jax.ShapeDtypeStruct((batch, MLA_HEADS * MLA_V, rows // batch), BF16)],
        compiler_params=_params(1),
        name="mla_kv_up",
    )(ckv, krp, w_k, w_vt, gk, gmean)


def _attn_body(qi_ref, kj_ref, *refs, tq, tk, n_key_tiles, causal, nk_valid, has_ext):
    if has_ext:
        q_ref, k_ref, vt_ref, ke_ref, vte_ref, o_ref, m_scr, l_scr, acc_scr = refs
    else:
        q_ref, k_ref, vt_ref, o_ref, m_scr, l_scr, acc_scr = refs
    step = pl.program_id(1)
    i = qi_ref[step]
    j = kj_ref[step]
    last_j = ((i + 1) * tq + tk - 1) // tk - 1 if causal else n_key_tiles - 1
    head_lanes = [slice(h * LANES, (h + 1) * LANES) for h in range(MLA_HEADS)]

    def update(k_blk, vt_blk, bias=None, q_mask=None):
        if q_mask is None:
            q_heads = [q_ref[:, hs] for hs in head_lanes]
        else:
            q_heads = [q_ref[:, hs] + q_mask for hs in head_lanes]
        scores = [_dot_nt(k_blk[:, hs], qh) for hs, qh in zip(head_lanes, q_heads)]
        for h in range(MLA_HEADS):
            s = scores[h]
            if bias is not None:
                s = s + bias
            m_old = m_scr[h][0:1]
            m_new = jnp.maximum(m_old, jnp.max(s, 0, keepdims=True))
            alpha = jnp.exp2(m_old - m_new)
            p = jnp.exp2(s - m_new)
            l_new = alpha * l_scr[h][0:1] + jnp.sum(p, 0, keepdims=True)
            m_scr[h] = jnp.broadcast_to(m_new, (HALO, tq))
            l_scr[h] = jnp.broadcast_to(l_new, (HALO, tq))
            acc_scr[h] = acc_scr[h] * alpha + _dot(vt_blk[h * MLA_V:(h + 1) * MLA_V, :], p.astype(BF16))

    @pl.when(j == 0)
    def _():
        m_scr[...] = jnp.full(m_scr.shape, NEG_INF, F32)
        l_scr[...] = jnp.zeros(l_scr.shape, F32)
        acc_scr[...] = jnp.zeros(acc_scr.shape, F32)
        if has_ext:
            update(ke_ref, vte_ref)

    if causal:
        @pl.when(j < last_j)
        def _():
            update(k_ref, vt_ref)

        @pl.when(j == last_j)
        def _():
            lane = lax.broadcasted_iota(jnp.int32, (1, LANES), 1)
            q_chunk = (i * tq + lax.broadcasted_iota(jnp.int32, (tq, 1), 0)) >> CHUNK_SHIFT
            key_chunk = j * (tk // CHUNK) + lane - MASK_LANE0
            hidden = jnp.logical_and(jnp.logical_and(lane >= MASK_LANE0, lane < MASK_LANE0 + tk // CHUNK),
                                     key_chunk > q_chunk)
            update(k_ref, vt_ref, q_mask=jnp.where(hidden, -MASK_BIG, 0.0).astype(BF16))
    elif nk_valid is not None:
        kpos = j * tk + lax.broadcasted_iota(jnp.int32, (tk, 1), 0)
        update(k_ref, vt_ref, bias=jnp.where(kpos < nk_valid, 0.0, NEG_INF))
    else:
        update(k_ref, vt_ref)

    @pl.when(j == last_j)
    def _():
        for h in range(MLA_HEADS):
            o_ref[h * MLA_V:(h + 1) * MLA_V, :] = (acc_scr[h] / l_scr[h][0:1]).astype(BF16)


def _attention(q, k, vt, ext, *, tq, tk, causal=False, nk_valid=None):
    b, nq, _ = q.shape
    nk = k.shape[1]
    n_key_tiles = nk // tk
    assert not causal or tk % tq == 0
    pairs = [(i, j) for i in range(nq // tq) for j in range(n_key_tiles) if not causal or j * tk < (i + 1) * tq]
    qi = jnp.asarray([p[0] for p in pairs], jnp.int32)
    kj = jnp.asarray([p[1] for p in pairs], jnp.int32)
    hv = MLA_HEADS * MLA_V
    in_specs = [
        pl.BlockSpec((None, tq, MLA_PAD), lambda bi, s, qi, kj: (bi, qi[s], 0)),
        pl.BlockSpec((None, tk, MLA_PAD), lambda bi, s, qi, kj: (bi, kj[s], 0)),
        pl.BlockSpec((None, hv, tk), lambda bi, s, qi, kj: (bi, 0, kj[s])),
    ]
    args = [q, k, vt]
    if ext is not None:
        ne = ext[0].shape[1]
        shared = ext[0].shape[0] == 1 and b > 1
        emap = (lambda bi, s, qi, kj: (0, 0, 0)) if shared else (lambda bi, s, qi, kj: (bi, 0, 0))
        in_specs += [pl.BlockSpec((None, ne, MLA_PAD), emap), pl.BlockSpec((None, hv, ne), emap)]
        args += list(ext)
    return pl.pallas_call(
        functools.partial(_attn_body, tq=tq, tk=tk, n_key_tiles=n_key_tiles, causal=causal, nk_valid=nk_valid,
                          has_ext=ext is not None),
        grid_spec=pltpu.PrefetchScalarGridSpec(
            num_scalar_prefetch=2,
            grid=(b, len(pairs)),
            in_specs=in_specs,
            out_specs=pl.BlockSpec((None, hv, tq), lambda bi, s, qi, kj: (bi, 0, qi[s])),
            scratch_shapes=[pltpu.VMEM((MLA_HEADS, HALO, tq), F32), pltpu.VMEM((MLA_HEADS, HALO, tq), F32),
                            pltpu.VMEM((MLA_HEADS, MLA_V, tq), F32)]),
        out_shape=jax.ShapeDtypeStruct((b, hv, nq), BF16),
        compiler_params=_params(2),
        name="mla_attention",
    )(qi, kj, *args)


def _sconv_body(x_ref, g_ref, win_ref, cw_ref, wout_ref, prev_ref, o_ref, pout_ref, halo_scr, *, rows):
    t = pl.program_id(1)

    @pl.when(t == 0)
    def _():
        halo_scr[...] = prev_ref[...]

    x = x_ref[...]
    hn = _rms(x, g_ref[...]).astype(BF16)
    bg = _dot(hn, win_ref[:, :D_MODEL])
    u = _dot(hn, win_ref[:, D_MODEL:2 * D_MODEL]) * _dot(hn, win_ref[:, 2 * D_MODEL:])
    halo = halo_scr[...]
    row = lax.broadcasted_iota(jnp.int32, (rows, 1), 0)
    u1 = jnp.where(row == 0, halo[HALO - 1:HALO, :], pltpu.roll(u, 1, 0))
    u2 = jnp.where(row == 0, halo[HALO - 2:HALO - 1, :],
                   jnp.where(row == 1, halo[HALO - 1:HALO, :], pltpu.roll(u, 2, 0)))
    cw = cw_ref[...]
    y = u2 * cw[0:1] + u1 * cw[1:2] + u * cw[2:3]
    last_rows = u[rows - HALO:rows, :]
    halo_scr[...] = last_rows
    o_ref[...] = x + _dot((bg * y).astype(BF16), wout_ref[...])

    @pl.when(t == pl.num_programs(1) - 1)
    def _():
        pout_ref[...] = last_rows


def _sconv(x, prev, w, *, rows):
    b, t_len, _ = x.shape
    g, w_in, conv_w, w_out = w
    shared = prev.shape[0] == 1 and b > 1
    bsel = (lambda bi: 0) if shared else (lambda bi: bi)
    return pl.pallas_call(
        functools.partial(_sconv_body, rows=rows),
        grid=(b, t_len // rows),
        in_specs=[pl.BlockSpec((None, rows, D_MODEL), lambda bi, ti: (bi, ti, 0)), _full_spec(g.shape),
                  _full_spec(w_in.shape), _full_spec(conv_w.shape), _full_spec(w_out.shape),
                  pl.BlockSpec((None, HALO, D_MODEL), lambda bi, ti: (bsel(bi), 0, 0))],
        out_specs=[pl.BlockSpec((None, rows, D_MODEL), lambda bi, ti: (bi, ti, 0)),
                   pl.BlockSpec((None, HALO, D_MODEL), lambda bi, ti: (bi, 0, 0))],
        out_shape=[jax.ShapeDtypeStruct((b, t_len, D_MODEL), F32), jax.ShapeDtypeStruct((b, HALO, D_MODEL), F32)],
        scratch_shapes=[pltpu.VMEM((HALO, D_MODEL), F32)],
        compiler_params=_params(2),
        name="sconv_mixer",
    )(x, g, w_in, conv_w, w_out, prev)


def _pad_lanes(a, before, total):
    return jnp.pad(a, [(0, 0)] * (a.ndim - 1) + [(before, total - before - a.shape[-1])])


def _rope_tables(pos):
    half = MLA_ROPE // 2
    inv = ROPE_THETA ** (-jnp.arange(half, dtype=F32) / half)
    ang = pos.astype(F32)[:, None] * inv
    cos, sin = jnp.cos(ang), jnp.sin(ang)
    n = pos.shape[0]
    tail = LANES - MLA_NOPE - MLA_ROPE
    cos_l = jnp.concatenate([jnp.ones((n, MLA_NOPE), F32), cos, cos, jnp.ones((n, tail), F32)], -1)
    sin_l = jnp.concatenate([jnp.zeros((n, MLA_NOPE), F32), sin, sin, jnp.zeros((n, tail), F32)], -1)
    return cos_l, sin_l


def _head_pair_matrices():
    i = jnp.arange(PAIR)[:, None]
    j = jnp.arange(PAIR)[None, :]
    same = (i // LANES) == (j // LANES)
    li, lj = i % LANES, j % LANES
    half = MLA_ROPE // 2
    rope_end = MLA_NOPE + MLA_ROPE
    in_rope = lambda l: (l >= MLA_NOPE) & (l < rope_end)
    gmean = jnp.where(same & (li < MLA_NOPE) & (lj < MLA_NOPE), 1.0 / MLA_NOPE,
                      jnp.where(same & in_rope(li) & in_rope(lj), 1.0 / MLA_ROPE, 0.0))
    first, second = in_rope(lj) & (lj < MLA_NOPE + half), in_rope(lj) & (lj >= MLA_NOPE + half)
    rot = jnp.where(same & first & (li == lj + half), -1.0, jnp.where(same & second & (li == lj - half), 1.0, 0.0))
    return gmean.astype(BF16), rot.astype(BF16)


def _tail_rows(a, n):
    return jnp.pad(a, ((0, 0), (HALO - n, 0), (0, 0)))


def kernel(x_prompt, x_sample, cache_mla_ckv, cache_mla_krope, state_gdn_S, state_gdn_conv, state_sconv, meta_tokens, ffn1_norm, ffn1_w_gate, ffn1_w_up, ffn1_w_down, ffn2_norm, ffn2_w_gate, ffn2_w_up, ffn2_w_down, mix_norm, ab_w_in, ab_w_out, gdn_conv_w, gdn_A_log, gdn_dt_bias, gdn_o_norm, mla_q_norm, mla_w_uq, mla_kv_norm, mla_w_ukv, mla_qn_norm, mla_qr_norm, mla_kn_norm, mla_kr_norm, sc_w_in, sc_conv_w, sc_w_out):
    bp, tp, _ = x_prompt.shape
    bs, ts, _ = x_sample.shape
    n_frames = bp * tp
    n_sample = bs * ts

    ffn1 = [_ffn_weights(ffn1_norm[l], ffn1_w_gate[l], ffn1_w_up[l], ffn1_w_down[l]) for l in range(2)]
    ffn2 = [_ffn_weights(ffn2_norm[l], ffn2_w_gate[l], ffn2_w_up[l], ffn2_w_down[l]) for l in range(2)]
    w_in = ab_w_in[0]
    s1 = GDN_QKV + GDN_Z
    c0 = s1 + 2 * GDN_HEADS
    w_in_packed = jnp.concatenate([
        w_in[:, :s1],
        _pad_lanes(w_in[:, s1:c0], 0, LANES),
        w_in[:, c0:c0 + MLA_Q_RANK + MLA_KV_RANK],
        _pad_lanes(w_in[:, c0 + MLA_Q_RANK + MLA_KV_RANK:], MLA_NOPE, LANES),
    ], -1).astype(BF16)
    mix_g = mix_norm.reshape(2, 1, D_MODEL)
    gdn_w = (gdn_conv_w[0], _pad_lanes(gdn_A_log[0][None], 0, LANES), _pad_lanes(gdn_dt_bias[0][None], 0, LANES),
             gdn_o_norm[0][None])
    qdim = MLA_NOPE + MLA_ROPE
    w_uq = _pad_lanes(mla_w_uq[0].reshape(MLA_Q_RANK, MLA_HEADS, qdim), 0, LANES).reshape(MLA_Q_RANK, MLA_PAD)
    w_ukv = mla_w_ukv[0].reshape(MLA_KV_RANK, MLA_HEADS, MLA_NOPE + MLA_V)
    w_k = _pad_lanes(w_ukv[..., :MLA_NOPE], 0, LANES).reshape(MLA_KV_RANK, MLA_PAD)
    hv = MLA_HEADS * MLA_V
    w_vt = w_ukv[..., MLA_NOPE:].reshape(MLA_KV_RANK, hv).T
    gq = _pad_lanes(jnp.concatenate([mla_qn_norm[0], mla_qr_norm[0]])[None], 0, LANES)
    gk = _pad_lanes(jnp.concatenate([mla_kn_norm[0], mla_kr_norm[0]])[None], 0, LANES)
    gmean, rot_mat = _head_pair_matrices()
    mla_w = (mla_q_norm[0][None], w_uq.astype(BF16), mla_kv_norm[0][None], w_k.astype(BF16), w_vt.astype(BF16), gq, gk,
             gmean, rot_mat)
    w_out = ab_w_out[0].astype(BF16)
    sc_w = (mix_g[1], sc_w_in[0].astype(BF16), sc_conv_w[0], sc_w_out[0].astype(BF16))

    xf = x_prompt.reshape(n_frames, D_MODEL)
    xs = jnp.concatenate([x_sample.reshape(n_sample, D_MODEL), meta_tokens.astype(F32)], 0)
    cos_f, sin_f = _rope_tables(N_META + jnp.arange(tp))
    pos_small = jnp.concatenate([jnp.tile(PAST_LEN + jnp.arange(ts), bs), jnp.arange(N_META)])
    cos_s, sin_s = _rope_tables(pos_small)

    xf = _ffn(xf, ffn1[0])
    xs = _ffn(xs, ffn1[0])
    qkv_f, z_f, ab_f, cq_f, ckv_f, kr_f = _inproj(xf, mix_g[0], w_in_packed)
    qkv_s, z_s, ab_s, cq_s, ckv_s, kr_s = _inproj(xs, mix_g[0], w_in_packed)

    pad_meta = lambda a: jnp.pad(a[n_sample:], ((0, CHUNK - N_META), (0, 0)))[None]
    og_m, s_m, conv_m = _gdn(pad_meta(qkv_s), pad_meta(z_s), pad_meta(ab_s),
                             jnp.zeros((1, GDN_HEADS, GDN_DK, GDN_DV), F32), jnp.zeros((1, HALO, GDN_QKV), F32),
                             gdn_w, rows=CHUNK, n_valid=N_META)
    og_f, p_s, p_conv = _gdn(qkv_f.reshape(bp, tp, GDN_QKV), z_f.reshape(bp, tp, GDN_Z), ab_f.reshape(bp, tp, LANES),
                             s_m, conv_m, gdn_w, rows=4 * CHUNK)
    og_s, s_s, s_conv = _gdn(qkv_s[:n_sample].reshape(bs, ts, GDN_QKV), z_s[:n_sample].reshape(bs, ts, GDN_Z),
                             ab_s[:n_sample].reshape(bs, ts, LANES), state_gdn_S[0],
                             _tail_rows(state_gdn_conv[0], state_gdn_conv.shape[2]), gdn_w, rows=ts)

    q_f, k_f, vt_f, ckvn_f, krn_f = _mla_prep(cq_f, ckv_f, kr_f, cos_f, sin_f, mla_w, batch=bp, key_tile=ATTN_TK)
    q_s, k_s, vt_s, ckvn_s, krn_s = _mla_prep(cq_s, ckv_s, kr_s, cos_s, sin_s, mla_w, batch=1)
    k_c, vt_c = _kv_up(cache_mla_ckv[0].reshape(bs * PAST_LEN, MLA_KV_RANK),
                       _pad_lanes(cache_mla_krope[0].reshape(bs * PAST_LEN, MLA_ROPE), MLA_NOPE, LANES),
                       mla_w[3], mla_w[4], gk, gmean, batch=bs)
    k_meta, vt_meta = k_s[n_sample:][None], vt_s[:, :, n_sample:]
    vt_new = vt_s[0, :, :n_sample].reshape(hv, bs, ts).transpose(1, 0, 2)
    omt_f = _attention(q_f.reshape(bp, tp, MLA_PAD), k_f.reshape(bp, tp, MLA_PAD), vt_f, (k_meta, vt_meta),
                       tq=ATTN_TQ, tk=ATTN_TK, causal=True)
    omt_s = _attention(q_s[:n_sample].reshape(bs, ts, MLA_PAD), k_c.reshape(bs, PAST_LEN, MLA_PAD), vt_c,
                       (k_s[:n_sample].reshape(bs, ts, MLA_PAD), vt_new), tq=ts, tk=512)
    omt_m = _attention(jnp.pad(q_s[n_sample:], ((0, LANES - N_META), (0, 0)))[None],
                       jnp.pad(k_s[n_sample:], ((0, CHUNK - N_META), (0, 0)))[None],
                       jnp.pad(vt_meta, ((0, 0), (0, 0), (0, CHUNK - N_META))),
                       None, tq=LANES, tk=CHUNK, nk_valid=N_META)

    og_small = jnp.concatenate([og_s.reshape(n_sample, GDN_Z), og_m[0, :N_META]], 0)
    omt_small = jnp.concatenate([omt_s.transpose(1, 0, 2).reshape(hv, n_sample), omt_m[0, :, :N_META]], -1)[None]
    xf = _ffn(xf, ffn2[0], mix=(og_f.reshape(n_frames, GDN_Z), omt_f, w_out))
    xs = _ffn(xs, ffn2[0], mix=(og_small, omt_small, w_out))

    xf = _ffn(xf, ffn1[1])
    xs = _ffn(xs, ffn1[1])
    n_sc = state_sconv.shape[2]
    xm, sc_m = _sconv(xs[n_sample:][None], jnp.zeros((1, HALO, D_MODEL), F32), sc_w, rows=N_META)
    xf3, p_sc = _sconv(xf.reshape(bp, tp, D_MODEL), sc_m, sc_w, rows=512)
    xs3, s_sc = _sconv(xs[:n_sample].reshape(bs, ts, D_MODEL), _tail_rows(state_sconv[0], n_sc), sc_w, rows=ts)
    xf = _ffn(xf3.reshape(n_frames, D_MODEL), ffn2[1])
    xs = _ffn(jnp.concatenate([xs3.reshape(n_sample, D_MODEL), xm[0]], 0), ffn2[1])

    y_prompt = xf.reshape(bp, tp, D_MODEL)
    y_sample = xs[:n_sample].reshape(bs, ts, D_MODEL)
    rope_lo, rope_hi = MLA_NOPE, MLA_NOPE + MLA_ROPE
    meta_ckv = jnp.broadcast_to(ckvn_s[n_sample:][None], (bp, N_META, MLA_KV_RANK))
    meta_kr = jnp.broadcast_to(krn_s[n_sample:, rope_lo:rope_hi][None], (bp, N_META, MLA_ROPE))
    p_ckv = jnp.concatenate([meta_ckv, ckvn_f.reshape(bp, tp, MLA_KV_RANK)], 1)
    p_kr = jnp.concatenate([meta_kr, krn_f[:, rope_lo:rope_hi].reshape(bp, tp, MLA_ROPE)], 1)
    n_gc = state_gdn_conv.shape[2]
    return (y_prompt, y_sample,
            p_ckv[None], p_kr[None], p_s[None], p_conv[:, HALO - n_gc:][None], p_sc[:, HALO - n_sc:][None],
            ckvn_s[:n_sample].reshape(bs, ts, MLA_KV_RANK)[None],
            krn_s[:n_sample, rope_lo:rope_hi].reshape(bs, ts, MLA_ROPE)[None],
            s_s[None], s_conv[:, HALO - n_gc:][None], s_sc[:, HALO - n_sc:][None])
```

```python
import functools

import jax
import jax.numpy as jnp
from jax import lax
from jax.experimental import pallas as pl
from jax.experimental.pallas import tpu as pltpu

F32 = jnp.float32
BF16 = jnp.bfloat16

D_MODEL = 1024
D_FF = 2816
CHUNK = 64
CHUNK_SHIFT = 6
N_META = 16
EPS = 1e-6
GDN_HEADS = 4
GDN_DK = 128
GDN_DV = 128
GDN_QKV = GDN_HEADS * (2 * GDN_DK + GDN_DV)
GDN_Z = GDN_HEADS * GDN_DV
MLA_HEADS = 8
MLA_NOPE = 64
MLA_ROPE = 32
MLA_V = 64
MLA_Q_RANK = 384
MLA_KV_RANK = 256
ROPE_THETA = 10000.0
PAST_LEN = 1024
LANES = 128
HALO = 8
MLA_PAD = MLA_HEADS * LANES
FF_BLOCK = 256
FF_CHUNKS = D_FF // FF_BLOCK
VMEM_LIMIT = 60000 * 1024
NEG_INF = float("-inf")
LOG2_E = 1.4426950408889634
MASK_LANE0 = MLA_NOPE + MLA_ROPE
MASK_BIG = 2.0 ** 100
ATTN_TQ = 512
ATTN_TK = 512
FFN_ROWS = 1024
VT_ROWS = MLA_V + 16
VT_TOTAL = MLA_HEADS * VT_ROWS
ONE_PASS_JUMP = 64.0

IN_QKV = 0
IN_Z = IN_QKV + GDN_QKV
IN_AB = IN_Z + GDN_Z
IN_CQ = IN_AB + LANES
IN_CKV = IN_CQ + MLA_Q_RANK
IN_KR = IN_CKV + MLA_KV_RANK
IN_TOTAL = IN_KR + LANES


def _params(n_axes):
    return pltpu.CompilerParams(dimension_semantics=("arbitrary",) * n_axes, vmem_limit_bytes=VMEM_LIMIT)


def _rms(x, g):
    return x * lax.rsqrt(jnp.mean(x * x, -1, keepdims=True) + EPS) * g


def _sigmoid(x):
    return 1.0 / (1.0 + jnp.exp(-x))


def _silu(x):
    return x * _sigmoid(x)


def _softplus(x):
    return jnp.maximum(x, 0.0) + jnp.log1p(jnp.exp(-jnp.abs(x)))


def _dot(a, b):
    return jnp.dot(a, b, preferred_element_type=F32)


def _dot_nt(a, b):
    return lax.dot_general(a, b, (((1,), (1,)), ((), ())), preferred_element_type=F32)


def _dot_tn(a, b):
    return lax.dot_general(a, b, (((0,), (0,)), ((), ())), preferred_element_type=F32)


def _shift_rows(x, history, k):
    rolled = pltpu.roll(x, k, 0)
    row = lax.broadcasted_iota(jnp.int32, (HALO, 1), 0)
    head = rolled[:HALO]
    for j in range(k):
        head = jnp.where(row == j, history[HALO - k + j:HALO - k + j + 1, :], head)
    return jnp.concatenate([head, rolled[HALO:]], 0)


def _full_spec(shape):
    zeros = (0,) * len(shape)
    return pl.BlockSpec(shape, lambda *_: zeros)


def _row_tile(rows, want):
    return want if rows % want == 0 else rows


def _ffn_body(x_ref, g_ref, wg_ref, wu_ref, wd_ref, *rest, mixed):
    if mixed:
        og_ref, omt_ref, wmix_ref, o_ref, acc_ref = rest
        x = x_ref[...] + _dot(og_ref[...], wmix_ref[:GDN_Z, :]) + _dot_tn(omt_ref[...], wmix_ref[GDN_Z:, :])
    else:
        o_ref, acc_ref = rest
        x = x_ref[...]
    h = _rms(x, g_ref[...]).astype(BF16)
    for c in range(FF_CHUNKS):
        cols = slice(c * FF_BLOCK, (c + 1) * FF_BLOCK)
        gate = _dot(h, wg_ref[:, cols])
        up = _dot(h, wu_ref[:, cols])
        act = (_silu(gate) * up).astype(BF16)
        down = _dot(act, wd_ref[cols, :])
        if c == 0:
            acc_ref[...] = down
        else:
            acc_ref[...] += down
    o_ref[...] = x + 0.5 * acc_ref[...]


def _ffn(x, w, mix=None):
    rows = x.shape[0]
    tm = _row_tile(rows, FFN_ROWS)
    g, wg, wu, wd = w
    resident = lambda a: pl.BlockSpec(a.shape, lambda i: (0,) * a.ndim, pipeline_mode=pl.Buffered(1))
    in_specs = [pl.BlockSpec((tm, D_MODEL), lambda i: (i, 0)), _full_spec(g.shape), resident(wg), resident(wu),
                resident(wd)]
    args = [x, g, wg, wu, wd]
    if mix is not None:
        og, omt, w_mix = mix
        tiles_per_batch = rows // omt.shape[0] // tm
        in_specs += [pl.BlockSpec((tm, GDN_Z), lambda i: (i, 0)),
                     pl.BlockSpec((None, GDN_Z, tm), lambda i: (i // tiles_per_batch, 0, i % tiles_per_batch)),
                     resident(w_mix)]
        args += [og, omt, w_mix]
    return pl.pallas_call(
        functools.partial(_ffn_body, mixed=mix is not None),
        grid=(rows // tm,),
        in_specs=in_specs,
        out_specs=pl.BlockSpec((tm, D_MODEL), lambda i: (i, 0)),
        out_shape=jax.ShapeDtypeStruct((rows, D_MODEL), F32),
        scratch_shapes=[pltpu.VMEM((tm, D_MODEL), F32)],
        compiler_params=_params(1),
        name="half_ffn",
    )(*args)


def _ffn_weights(norm, w_gate, w_up, w_down):
    return norm.reshape(1, D_MODEL), w_gate.astype(BF16), w_up.astype(BF16), w_down.astype(BF16)


def _gdn_conv(x, history, cw):
    y = (_shift_rows(x, history, 3) * cw[0:1] + _shift_rows(x, history, 2) * cw[1:2]
         + _shift_rows(x, history, 1) * cw[2:3] + x * cw[3:4])
    return _silu(y)


def _inproj_body(x_ref, g_ref, w_ref, qkv_ref, z_ref, ab_ref, cq_ref, ckv_ref, kr_ref):
    hn = _rms(x_ref[...], g_ref[...]).astype(BF16)
    qkv_ref[...] = _dot(hn, w_ref[:, IN_QKV:IN_Z])
    z_ref[...] = _dot(hn, w_ref[:, IN_Z:IN_AB])
    ab_ref[...] = _dot(hn, w_ref[:, IN_AB:IN_CQ])
    cq_ref[...] = _dot(hn, w_ref[:, IN_CQ:IN_CKV])
    ckv_ref[...] = _dot(hn, w_ref[:, IN_CKV:IN_KR])
    kr_ref[...] = _dot(hn, w_ref[:, IN_KR:IN_TOTAL])


def _inproj(x, g, w):
    rows = x.shape[0]
    tm = _row_tile(rows, 512)
    widths = (GDN_QKV, GDN_Z, LANES, MLA_Q_RANK, MLA_KV_RANK, LANES)
    return pl.pallas_call(
        _inproj_body,
        grid=(rows // tm,),
        in_specs=[pl.BlockSpec((tm, D_MODEL), lambda i: (i, 0)), _full_spec(g.shape), _full_spec(w.shape)],
        out_specs=[pl.BlockSpec((tm, n), lambda i: (i, 0)) for n in widths],
        out_shape=[jax.ShapeDtypeStruct((rows, n), F32) for n in widths],
        compiler_params=_params(1),
        name="mixer_inproj",
    )(x, g, w)


class _Split:
    def __init__(self, x):
        self.x = x
        self._parts = None
        self._diag = None

    def parts(self):
        if self._parts is None:
            hi = self.x.astype(BF16)
            self._parts = (hi, (self.x - hi.astype(F32)).astype(BF16))
        return self._parts

    def diag_parts(self):
        if self._diag is None:
            first = lax.broadcasted_iota(jnp.int32, self.x.shape, 1) < CHUNK
            zero = jnp.zeros(self.x.shape, BF16)
            self._diag = tuple(jnp.concatenate([jnp.where(first, p, zero), jnp.where(first, zero, p)], 0)
                               for p in self.parts())
        return self._diag


def _split_products(pairs):
    parts = [(a.parts(), b.diag_parts()) for a, b in pairs]
    outs = []
    for (a_hi, a_lo), (b_hi, b_lo) in parts:
        a2 = jnp.concatenate([a_hi, a_lo], 0)
        r = _dot(a2, b_hi) + _dot(a2, b_lo)
        half = a_hi.shape[0]
        outs.append(_Split(r[:half] + r[half:]))
    return outs


def _unit_lower_inverses(ns, eye_f, m16, m32):
    k = len(ns)
    nd = [_Split(jnp.where(m16, n, 0.0)) for n in ns]
    n2 = _split_products([(x, x) for x in nd])
    lvl = _split_products([(x, x) for x in n2] +
                          [(_Split(eye_f - a.x), _Split(eye_f + b.x)) for a, b in zip(nd, n2)])
    n4, t = lvl[:k], lvl[k:]
    lvl = _split_products([(x, x) for x in n4] + [(a, _Split(eye_f + b.x)) for a, b in zip(t, n4)])
    n8, t = lvl[:k], lvl[k:]
    t = _split_products([(a, _Split(eye_f + b.x)) for a, b in zip(t, n8)])
    for keep in (jnp.logical_and(m32, jnp.logical_not(m16)), jnp.logical_not(m32)):
        off = [_Split(jnp.where(keep, n, 0.0)) for n in ns]
        x = _split_products(list(zip(off, t)))
        y = _split_products(list(zip(t, x)))
        t = [_Split(a.x - b.x) for a, b in zip(t, y)]
    return [a.x for a in t]


def _gdn_body(qkv_ref, z_ref, ab_ref, s0_ref, cprev_ref, cw_ref, alog_ref, dtb_ref, on_ref,
              og_ref, sout_ref, cout_ref, s_scr, halo_scr, *, rows, n_valid):
    t = pl.program_id(1)
    C = CHUNK
    row = lax.broadcasted_iota(jnp.int32, (rows, 1), 0)

    @pl.when(t == 0)
    def _():
        s_scr[...] = s0_ref[...]
        halo_scr[...] = cprev_ref[...]

    x = qkv_ref[...]
    y = _gdn_conv(x, halo_scr[...], cw_ref[...])
    last_rows = x[n_valid - HALO:n_valid, :]
    halo_scr[...] = last_rows

    ab = ab_ref[...]
    g_all = -jnp.exp(alog_ref[...]) * _softplus(ab + dtb_ref[...])
    b_all = _sigmoid(ab)
    if n_valid < rows:
        valid = row < n_valid
        y = jnp.where(valid, y, 0.0)
        g_all = jnp.where(valid, g_all, 0.0)
        b_all = jnp.where(valid, b_all, 0.0)

    ii = lax.broadcasted_iota(jnp.int32, (C, LANES), 0)
    pair_lane = lax.broadcasted_iota(jnp.int32, (C, LANES), 1)
    jj = pair_lane & (C - 1)
    first_head = pair_lane < C
    causal = ii >= jj
    strict = ii > jj
    eye_f = (ii == jj).astype(F32)
    m16 = (ii >> 4) == (jj >> 4)
    m32 = (ii >> 5) == (jj >> 5)
    on = on_ref[...]
    n_chunks = rows // C
    heads = range(GDN_HEADS)

    ri = lax.broadcasted_iota(jnp.int32, (rows, rows), 0)
    rj = lax.broadcasted_iota(jnp.int32, (rows, rows), 1)
    tril_blk = jnp.logical_and(ri >= rj, (ri >> CHUNK_SHIFT) == (rj >> CHUNK_SHIFT)).astype(BF16)
    g_hi = g_all.astype(BF16)
    g_r1 = g_all - g_hi.astype(F32)
    g_mid = g_r1.astype(BF16)
    g_lo = (g_r1 - g_mid.astype(F32)).astype(BF16)
    gcum = _dot(tril_blk, g_hi) + _dot(tril_blk, g_mid) + _dot(tril_blk, g_lo)
    egcum = jnp.exp(gcum)

    qn, kn, vn = [], [], []
    for h in heads:
        qh = y[:, h * GDN_DK:(h + 1) * GDN_DK]
        kh = y[:, GDN_HEADS * GDN_DK + h * GDN_DK:GDN_HEADS * GDN_DK + (h + 1) * GDN_DK]
        qn.append(qh * lax.rsqrt(jnp.sum(qh * qh, -1, keepdims=True) + EPS) * (GDN_DK ** -0.5))
        kn.append(kh * lax.rsqrt(jnp.sum(kh * kh, -1, keepdims=True) + EPS))
        vn.append(y[:, 2 * GDN_HEADS * GDN_DK + h * GDN_DV:2 * GDN_HEADS * GDN_DK + (h + 1) * GDN_DV])

    head_pairs = [(a, a + 1) for a in range(0, GDN_HEADS, 2)]
    chains = [(c, pair) for c in range(n_chunks) for pair in head_pairs]
    pre = {}
    n_mats = []
    for c in range(n_chunks):
        sl = slice(c * C, (c + 1) * C)
        gcum_t = gcum[sl].T
        for a, b in head_pairs:
            col = lambda arr, h: arr[sl, h:h + 1]
            gcol = jnp.where(first_head, col(gcum, a), col(gcum, b))
            beta2 = jnp.where(first_head, col(b_all, GDN_HEADS + a), col(b_all, GDN_HEADS + b))
            grow = jnp.concatenate([gcum_t[a:a + 1, :], gcum_t[b:b + 1, :]], -1)
            dec_causal = jnp.exp(jnp.where(causal, gcol - grow, NEG_INF))
            kb = {h: kn[h][sl].astype(BF16) for h in (a, b)}
            kk = jnp.concatenate([_dot_nt(kb[a], kb[a]), _dot_nt(kb[b], kb[b])], -1)
            n_mats.append(beta2 * kk * jnp.where(strict, dec_causal, 0.0))
            qk = jnp.concatenate([_dot_nt(qn[h][sl].astype(BF16), kb[h]) for h in (a, b)], -1)
            qk = (qk * dec_causal).astype(BF16)
            for h, lanes in ((a, slice(0, C)), (b, slice(C, 2 * C))):
                beta = col(b_all, GDN_HEADS + h)
                eg = col(egcum, h)
                g_last = gcum[(c + 1) * C - 1:(c + 1) * C, h:h + 1]
                qh, kh, vh = qn[h][sl], kn[h][sl], vn[h][sl]
                pre[c, h] = dict(
                    qk=qk[:, lanes],
                    rhs=jnp.concatenate([(beta * vh).astype(BF16), ((beta * eg) * kh).astype(BF16)], -1),
                    qd=(qh * eg).astype(BF16),
                    kt=(kh * jnp.exp(g_last - col(gcum, h))).astype(BF16),
                    decay=jnp.exp(g_last))
    for (c, (a, b)), tinv in zip(chains, _unit_lower_inverses(n_mats, eye_f, m16, m32)):
        tinv = tinv.astype(BF16)
        for h, lanes in ((a, slice(0, C)), (b, slice(C, 2 * C))):
            uw = _dot(tinv[:, lanes], pre[c, h]["rhs"])
            pre[c, h]["u"] = uw[:, :GDN_DV]
            pre[c, h]["wk"] = uw[:, GDN_DV:].astype(BF16)

    state = [s_scr[h] for h in heads]
    for c in range(n_chunks):
        sl = slice(c * C, (c + 1) * C)
        sb = [s.astype(BF16) for s in state]
        w = [(pre[c, h]["u"] - _dot(pre[c, h]["wk"], sb[h])).astype(BF16) for h in heads]
        o = [_dot(pre[c, h]["qd"], sb[h]) + _dot(pre[c, h]["qk"], w[h]) for h in heads]
        state = [state[h] * pre[c, h]["decay"] + _dot_tn(pre[c, h]["kt"], w[h]) for h in heads]
        for h in heads:
            gated = _rms(o[h], on) * _silu(z_ref[sl, h * GDN_DV:(h + 1) * GDN_DV])
            og_ref[sl, h * GDN_DV:(h + 1) * GDN_DV] = gated.astype(BF16)
    for h in heads:
        s_scr[h] = state[h]

    @pl.when(t == pl.num_programs(1) - 1)
    def _():
        sout_ref[...] = s_scr[...]
        cout_ref[...] = last_rows


def _gdn(qkv, z, ab, s0, cprev, w, *, rows, n_valid=None):
    b, t_len, _ = qkv.shape
    n_valid = rows if n_valid is None else n_valid
    conv_w, a_log, dt_bias, o_norm = w
    shared = s0.shape[0] == 1 and b > 1
    bsel = (lambda bi: 0) if shared else (lambda bi: bi)
    return pl.pallas_call(
        functools.partial(_gdn_body, rows=rows, n_valid=n_valid),
        grid=(b, t_len // rows),
        in_specs=[
            pl.BlockSpec((None, rows, GDN_QKV), lambda bi, ti: (bi, ti, 0)),
            pl.BlockSpec((None, rows, GDN_Z), lambda bi, ti: (bi, ti, 0)),
            pl.BlockSpec((None, rows, LANES), lambda bi, ti: (bi, ti, 0)),
            pl.BlockSpec((None, GDN_HEADS, GDN_DK, GDN_DV), lambda bi, ti: (bsel(bi), 0, 0, 0)),
            pl.BlockSpec((None, HALO, GDN_QKV), lambda bi, ti: (bsel(bi), 0, 0)),
            _full_spec(conv_w.shape), _full_spec(a_log.shape), _full_spec(dt_bias.shape), _full_spec(o_norm.shape),
        ],
        out_specs=[
            pl.BlockSpec((None, rows, GDN_Z), lambda bi, ti: (bi, ti, 0)),
            pl.BlockSpec((None, GDN_HEADS, GDN_DK, GDN_DV), lambda bi, ti: (bi, 0, 0, 0)),
            pl.BlockSpec((None, HALO, GDN_QKV), lambda bi, ti: (bi, 0, 0)),
        ],
        out_shape=[
            jax.ShapeDtypeStruct((b, t_len, GDN_Z), BF16),
            jax.ShapeDtypeStruct((b, GDN_HEADS, GDN_DK, GDN_DV), F32),
            jax.ShapeDtypeStruct((b, HALO, GDN_QKV), F32),
        ],
        scratch_shapes=[pltpu.VMEM((GDN_HEADS, GDN_DK, GDN_DV), F32), pltpu.VMEM((HALO, GDN_QKV), F32)],
        compiler_params=_params(2),
        name="gdn_mixer",
    )(qkv, z, ab, s0, cprev, conv_w, a_log, dt_bias, o_norm)


def _lane_masks():
    lane = lax.broadcasted_iota(jnp.int32, (1, LANES), 1)
    nope = lane < MLA_NOPE
    rope_a = jnp.logical_and(lane >= MLA_NOPE, lane < MLA_NOPE + MLA_ROPE // 2)
    rope_b = jnp.logical_and(lane >= MLA_NOPE + MLA_ROPE // 2, lane < MLA_NOPE + MLA_ROPE)
    return nope, rope_a, rope_b


def _rope_block(x, cos, sin, rope_a, rope_b):
    half = MLA_ROPE // 2
    rot = jnp.where(rope_a, -pltpu.roll(x, LANES - half, 1), jnp.where(rope_b, pltpu.roll(x, half, 1), 0.0))
    return x * cos + rot * sin


PAIR = 2 * LANES


def _twice(a):
    return jnp.concatenate([a, a], -1)


def _k_heads(k, shared, gk, gmean, k_out):
    shared2, gk2 = _twice(shared), _twice(gk)
    for p in range(MLA_HEADS // 2):
        x = k[:, p * PAIR:(p + 1) * PAIR]
        ms = _dot((x * x).astype(BF16), gmean)
        k_out[:, p * PAIR:(p + 1) * PAIR] = (x * lax.rsqrt(ms + EPS) * gk2 + shared2).astype(BF16)


def _mla_prep_body(cq_ref, ckv_ref, kr_ref, cos_ref, sin_ref, qn_ref, wuq_ref, kvn_ref, wk_ref, wvt_ref, vones_ref,
                   gq_ref, gk_ref, gmean_ref, rot_ref, q_out, k_out, vt_out, ckv_out, kr_out,
                   *, tiles_per_batch, key_tile_chunks):
    nope, rope_a, rope_b = _lane_masks()
    cos = cos_ref[...]
    sin = sin_ref[...]
    gk = gk_ref[...]
    gmean = gmean_ref[...]
    rot_mat = rot_ref[...]
    gq2 = _twice(gq_ref[...] * ((MLA_NOPE + MLA_ROPE) ** -0.5 * LOG2_E))
    cos2, sin2 = _twice(cos), _twice(sin)

    cqn = _rms(cq_ref[...], qn_ref[...]).astype(BF16)
    q = _dot(cqn, wuq_ref[...])
    for p in range(MLA_HEADS // 2):
        x = q[:, p * PAIR:(p + 1) * PAIR]
        ms = _dot((x * x).astype(BF16), gmean)
        x = x * lax.rsqrt(ms + EPS) * gq2
        q_out[:, p * PAIR:(p + 1) * PAIR] = (x * cos2 + _dot(x.astype(BF16), rot_mat) * sin2).astype(BF16)

    ckvn = _rms(ckv_ref[...], kvn_ref[...])
    ckv_out[...] = ckvn
    cb = ckvn.astype(BF16)
    krp = kr_ref[...]
    krn = krp * lax.rsqrt(jnp.sum(krp * krp, -1, keepdims=True) / MLA_ROPE + EPS) * jnp.where(nope, 0.0, gk)
    kr_new = _rope_block(krn, cos, sin, rope_a, rope_b)
    kr_out[...] = kr_new
    vt_out[...] = (_dot_nt(wvt_ref[...], cb) + vones_ref[...]).astype(BF16)
    shared_lanes = kr_new
    if key_tile_chunks is not None:
        tm = kr_new.shape[0]
        pos = (pl.program_id(0) % tiles_per_batch) * tm + lax.broadcasted_iota(jnp.int32, (tm, 1), 0)
        lane = lax.broadcasted_iota(jnp.int32, (1, LANES), 1)
        chunk = (pos >> CHUNK_SHIFT) & (key_tile_chunks - 1)
        shared_lanes = kr_new + (lane - MASK_LANE0 == chunk).astype(F32)
    _k_heads(_dot(cb, wk_ref[...]), shared_lanes, jnp.where(nope, gk, 0.0), gmean, k_out)


def _vt_spec(tm, tiles_per_batch):
    return pl.BlockSpec((None, VT_TOTAL, tm), lambda i: (i // tiles_per_batch, 0, i % tiles_per_batch))


def _mla_prep(cq, ckv, krp, cos, sin, w, *, batch, key_tile=None):
    rows = cq.shape[0]
    tm = _row_tile(rows, 512)
    tiles_per_batch = rows // batch // tm
    key_tile_chunks = None if key_tile is None else key_tile // CHUNK
    assert key_tile_chunks is None or (key_tile_chunks & (key_tile_chunks - 1) == 0
                                       and key_tile_chunks <= LANES - MASK_LANE0)
    q_norm, w_uq, kv_norm, w_k, w_vt, v_ones, gq, gk, gmean, rot_mat = w
    row_spec = lambda n: pl.BlockSpec((tm, n), lambda i: (i, 0))
    tab_spec = pl.BlockSpec((tm, LANES), lambda i: (i % tiles_per_batch, 0))
    return pl.pallas_call(
        functools.partial(_mla_prep_body, tiles_per_batch=tiles_per_batch, key_tile_chunks=key_tile_chunks),
        grid=(rows // tm,),
        in_specs=[row_spec(MLA_Q_RANK), row_spec(MLA_KV_RANK), row_spec(LANES), tab_spec, tab_spec,
                  _full_spec(q_norm.shape), _full_spec(w_uq.shape), _full_spec(kv_norm.shape),
                  _full_spec(w_k.shape), _full_spec(w_vt.shape), _full_spec(v_ones.shape), _full_spec(gq.shape),
                  _full_spec(gk.shape),
                  _full_spec(gmean.shape), _full_spec(rot_mat.shape)],
        out_specs=[row_spec(MLA_PAD), row_spec(MLA_PAD), _vt_spec(tm, tiles_per_batch), row_spec(MLA_KV_RANK),
                   row_spec(LANES)],
        out_shape=[jax.ShapeDtypeStruct((rows, MLA_PAD), BF16), jax.ShapeDtypeStruct((rows, MLA_PAD), BF16),
                   jax.ShapeDtypeStruct((batch, VT_TOTAL, rows // batch), BF16),
                   jax.ShapeDtypeStruct((rows, MLA_KV_RANK), F32), jax.ShapeDtypeStruct((rows, LANES), F32)],
        compiler_params=_params(1),
        name="mla_prep",
    )(cq, ckv, krp, cos, sin, q_norm, w_uq, kv_norm, w_k, w_vt, v_ones, gq, gk, gmean, rot_mat)


def _kv_up_body(ckv_ref, kr_ref, wk_ref, wvt_ref, vones_ref, gk_ref, gmean_ref, k_out, vt_out):
    nope, _, _ = _lane_masks()
    cb = ckv_ref[...].astype(BF16)
    vt_out[...] = (_dot_nt(wvt_ref[...], cb) + vones_ref[...]).astype(BF16)
    _k_heads(_dot(cb, wk_ref[...]), kr_ref[...], jnp.where(nope, gk_ref[...], 0.0), gmean_ref[...], k_out)


def _kv_up(ckv, krp, w_k, w_vt, v_ones, gk, gmean, *, batch):
    rows = ckv.shape[0]
    tm = _row_tile(rows, 512)
    tiles_per_batch = rows // batch // tm
    row_spec = lambda n: pl.BlockSpec((tm, n), lambda i: (i, 0))
    return pl.pallas_call(
        _kv_up_body,
        grid=(rows // tm,),
        in_specs=[row_spec(MLA_KV_RANK), row_spec(LANES), _full_spec(w_k.shape), _full_spec(w_vt.shape),
                  _full_spec(v_ones.shape), _full_spec(gk.shape), _full_spec(gmean.shape)],
        out_specs=[row_spec(MLA_PAD), _vt_spec(tm, tiles_per_batch)],
        out_shape=[jax.ShapeDtypeStruct((rows, MLA_PAD), BF16),
                   jax.ShapeDtypeStruct((batch, VT_TOTAL, rows // batch), BF16)],
        compiler_params=_params(1),
        name="mla_kv_up",
    )(ckv, krp, w_k, w_vt, v_ones, gk, gmean)


def _attn_body(qi_ref, kj_ref, *refs, tq, tk, n_key_tiles, causal, nk_valid, has_ext):
    if has_ext:
        q_ref, k_ref, vt_ref, ke_ref, vte_ref, o_ref, m_scr, acc_scr, slot_ref = refs
    else:
        q_ref, k_ref, vt_ref, o_ref, m_scr, acc_scr, slot_ref = refs
    step = pl.program_id(1)
    i = qi_ref[step]
    j = kj_ref[step]
    last_j = ((i + 1) * tq + tk - 1) // tk - 1 if causal else n_key_tiles - 1
    head_lanes = [slice(h * LANES, (h + 1) * LANES) for h in range(MLA_HEADS)]

    def head_scores(k_blk, q_mask):
        if q_mask is None:
            q_heads = [q_ref[:, hs] for hs in head_lanes]
        else:
            q_heads = [q_ref[:, hs] + q_mask for hs in head_lanes]
        return [_dot_nt(k_blk[:, hs], qh) for hs, qh in zip(head_lanes, q_heads)]

    def values(vt_blk, h):
        return vt_blk[h * VT_ROWS:(h + 1) * VT_ROWS, :]

    def two_pass(k_blk, vt_blk, cur, bias=None, q_mask=None):
        scores = head_scores(k_blk, q_mask)
        for h in range(MLA_HEADS):
            s = scores[h] if bias is None else scores[h] + bias
            m_old = m_scr[cur, h][0:1]
            m_new = jnp.maximum(m_old, jnp.max(s, 0, keepdims=True))
            alpha = jnp.exp2(m_old - m_new)
            p = jnp.exp2(s - m_new)
            m_scr[cur, h] = jnp.broadcast_to(m_new, (HALO, tq))
            acc_scr[cur, h] = acc_scr[cur, h] * alpha + _dot(values(vt_blk, h), p.astype(BF16))

    def one_pass(k_blk, vt_blk, cur, q_mask=None):
        scores = head_scores(k_blk, q_mask)
        rise = None
        for h in range(MLA_HEADS):
            s = scores[h]
            m_old = m_scr[cur, h][0:1]
            p = jnp.exp2(s - m_old)
            m_tile = jnp.max(s, 0, keepdims=True)
            m_new = jnp.maximum(m_old, m_tile)
            rise = m_tile - m_old if rise is None else jnp.maximum(rise, m_tile - m_old)
            m_scr[1 - cur, h] = jnp.broadcast_to(m_new, (HALO, tq))
            acc_scr[1 - cur, h] = (acc_scr[cur, h] + _dot(values(vt_blk, h), p.astype(BF16))) * jnp.exp2(m_old - m_new)
        return jnp.max(rise)

    def guarded(k_blk, vt_blk, q_mask=None):
        cur = slot_ref[0]
        rise = one_pass(k_blk, vt_blk, cur, q_mask)

        @pl.when(rise <= ONE_PASS_JUMP)
        def _():
            slot_ref[0] = 1 - cur

        @pl.when(jnp.logical_not(rise <= ONE_PASS_JUMP))
        def _():
            two_pass(k_blk, vt_blk, cur, q_mask=q_mask)

    @pl.when(j == 0)
    def _():
        slot_ref[0] = 0
        m_scr[0] = jnp.full(m_scr.shape[1:], NEG_INF, F32)
        acc_scr[0] = jnp.zeros(acc_scr.shape[1:], F32)
        if has_ext:
            two_pass(ke_ref, vte_ref, 0)

    if nk_valid is not None:
        kpos = j * tk + lax.broadcasted_iota(jnp.int32, (tk, 1), 0)
        two_pass(k_ref, vt_ref, slot_ref[0], bias=jnp.where(kpos < nk_valid, 0.0, NEG_INF))
    elif not has_ext:
        two_pass(k_ref, vt_ref, slot_ref[0])
    elif causal:
        @pl.when(j < last_j)
        def _():
            guarded(k_ref, vt_ref)

        @pl.when(j == last_j)
        def _():
            lane = lax.broadcasted_iota(jnp.int32, (1, LANES), 1)
            q_chunk = (i * tq + lax.broadcasted_iota(jnp.int32, (tq, 1), 0)) >> CHUNK_SHIFT
            key_chunk = j * (tk // CHUNK) + lane - MASK_LANE0
            hidden = jnp.logical_and(jnp.logical_and(lane >= MASK_LANE0, lane < MASK_LANE0 + tk // CHUNK),
                                     key_chunk > q_chunk)
            guarded(k_ref, vt_ref, q_mask=jnp.where(hidden, -MASK_BIG, 0.0).astype(BF16))
    else:
        guarded(k_ref, vt_ref)

    @pl.when(j == last_j)
    def _():
        cur = slot_ref[0]
        for h in range(MLA_HEADS):
            acc = acc_scr[cur, h]
            o_ref[h * MLA_V:(h + 1) * MLA_V, :] = (acc[:MLA_V] / acc[MLA_V:MLA_V + 1]).astype(BF16)


def _attention(q, k, vt, ext, *, tq, tk, causal=False, nk_valid=None):
    b, nq, _ = q.shape
    nk = k.shape[1]
    n_key_tiles = nk // tk
    assert not causal or tk % tq == 0
    pairs = [(i, j) for i in range(nq // tq) for j in range(n_key_tiles) if not causal or j * tk < (i + 1) * tq]
    qi = jnp.asarray([p[0] for p in pairs], jnp.int32)
    kj = jnp.asarray([p[1] for p in pairs], jnp.int32)
    hv = MLA_HEADS * MLA_V
    in_specs = [
        pl.BlockSpec((None, tq, MLA_PAD), lambda bi, s, qi, kj: (bi, qi[s], 0)),
        pl.BlockSpec((None, tk, MLA_PAD), lambda bi, s, qi, kj: (bi, kj[s], 0)),
        pl.BlockSpec((None, VT_TOTAL, tk), lambda bi, s, qi, kj: (bi, 0, kj[s])),
    ]
    args = [q, k, vt]
    if ext is not None:
        ne = ext[0].shape[1]
        shared = ext[0].shape[0] == 1 and b > 1
        emap = (lambda bi, s, qi, kj: (0, 0, 0)) if shared else (lambda bi, s, qi, kj: (bi, 0, 0))
        in_specs += [pl.BlockSpec((None, ne, MLA_PAD), emap), pl.BlockSpec((None, VT_TOTAL, ne), emap)]
        args += list(ext)
    return pl.pallas_call(
        functools.partial(_attn_body, tq=tq, tk=tk, n_key_tiles=n_key_tiles, causal=causal, nk_valid=nk_valid,
                          has_ext=ext is not None),
        grid_spec=pltpu.PrefetchScalarGridSpec(
            num_scalar_prefetch=2,
            grid=(b, len(pairs)),
            in_specs=in_specs,
            out_specs=pl.BlockSpec((None, hv, tq), lambda bi, s, qi, kj: (bi, 0, qi[s])),
            scratch_shapes=[pltpu.VMEM((2, MLA_HEADS, HALO, tq), F32), pltpu.VMEM((2, MLA_HEADS, VT_ROWS, tq), F32),
                            pltpu.SMEM((1,), jnp.int32)]),
        out_shape=jax.ShapeDtypeStruct((b, hv, nq), BF16),
        compiler_params=_params(2),
        name="mla_attention",
    )(qi, kj, *args)


def _sconv_body(x_ref, g_ref, win_ref, cw_ref, wout_ref, prev_ref, o_ref, pout_ref, halo_scr, *, rows):
    t = pl.program_id(1)

    @pl.when(t == 0)
    def _():
        halo_scr[...] = prev_ref[...]

    x = x_ref[...]
    hn = _rms(x, g_ref[...]).astype(BF16)
    bg = _dot(hn, win_ref[:, :D_MODEL])
    u = _dot(hn, win_ref[:, D_MODEL:2 * D_MODEL]) * _dot(hn, win_ref[:, 2 * D_MODEL:])
    halo = halo_scr[...]
    u1 = _shift_rows(u, halo, 1)
    u2 = _shift_rows(u, halo, 2)
    cw = cw_ref[...]
    y = u2 * cw[0:1] + u1 * cw[1:2] + u * cw[2:3]
    last_rows = u[rows - HALO:rows, :]
    halo_scr[...] = last_rows
    o_ref[...] = x + _dot((bg * y).astype(BF16), wout_ref[...])

    @pl.when(t == pl.num_programs(1) - 1)
    def _():
        pout_ref[...] = last_rows


def _sconv(x, prev, w, *, rows):
    b, t_len, _ = x.shape
    g, w_in, conv_w, w_out = w
    shared = prev.shape[0] == 1 and b > 1
    bsel = (lambda bi: 0) if shared else (lambda bi: bi)
    return pl.pallas_call(
        functools.partial(_sconv_body, rows=rows),
        grid=(b, t_len // rows),
        in_specs=[pl.BlockSpec((None, rows, D_MODEL), lambda bi, ti: (bi, ti, 0)), _full_spec(g.shape),
                  _full_spec(w_in.shape), _full_spec(conv_w.shape), _full_spec(w_out.shape),
                  pl.BlockSpec((None, HALO, D_MODEL), lambda bi, ti: (bsel(bi), 0, 0))],
        out_specs=[pl.BlockSpec((None, rows, D_MODEL), lambda bi, ti: (bi, ti, 0)),
                   pl.BlockSpec((None, HALO, D_MODEL), lambda bi, ti: (bi, 0, 0))],
        out_shape=[jax.ShapeDtypeStruct((b, t_len, D_MODEL), F32), jax.ShapeDtypeStruct((b, HALO, D_MODEL), F32)],
        scratch_shapes=[pltpu.VMEM((HALO, D_MODEL), F32)],
        compiler_params=_params(2),
        name="sconv_mixer",
    )(x, g, w_in, conv_w, w_out, prev)


def _pad_lanes(a, before, total):
    return jnp.pad(a, [(0, 0)] * (a.ndim - 1) + [(before, total - before - a.shape[-1])])


def _rope_tables(pos):
    half = MLA_ROPE // 2
    inv = ROPE_THETA ** (-jnp.arange(half, dtype=F32) / half)
    ang = pos.astype(F32)[:, None] * inv
    cos, sin = jnp.cos(ang), jnp.sin(ang)
    n = pos.shape[0]
    tail = LANES - MLA_NOPE - MLA_ROPE
    cos_l = jnp.concatenate([jnp.ones((n, MLA_NOPE), F32), cos, cos, jnp.ones((n, tail), F32)], -1)
    sin_l = jnp.concatenate([jnp.zeros((n, MLA_NOPE), F32), sin, sin, jnp.zeros((n, tail), F32)], -1)
    return cos_l, sin_l


def _head_pair_matrices():
    i = jnp.arange(PAIR)[:, None]
    j = jnp.arange(PAIR)[None, :]
    same = (i // LANES) == (j // LANES)
    li, lj = i % LANES, j % LANES
    half = MLA_ROPE // 2
    rope_end = MLA_NOPE + MLA_ROPE
    in_rope = lambda l: (l >= MLA_NOPE) & (l < rope_end)
    gmean = jnp.where(same & (li < MLA_NOPE) & (lj < MLA_NOPE), 1.0 / MLA_NOPE,
                      jnp.where(same & in_rope(li) & in_rope(lj), 1.0 / MLA_ROPE, 0.0))
    first, second = in_rope(lj) & (lj < MLA_NOPE + half), in_rope(lj) & (lj >= MLA_NOPE + half)
    rot = jnp.where(same & first & (li == lj + half), -1.0, jnp.where(same & second & (li == lj - half), 1.0, 0.0))
    return gmean.astype(BF16), rot.astype(BF16)


def _tail_rows(a, n):
    return jnp.pad(a, ((0, 0), (HALO - n, 0), (0, 0)))


def kernel(x_prompt, x_sample, cache_mla_ckv, cache_mla_krope, state_gdn_S, state_gdn_conv, state_sconv, meta_tokens, ffn1_norm, ffn1_w_gate, ffn1_w_up, ffn1_w_down, ffn2_norm, ffn2_w_gate, ffn2_w_up, ffn2_w_down, mix_norm, ab_w_in, ab_w_out, gdn_conv_w, gdn_A_log, gdn_dt_bias, gdn_o_norm, mla_q_norm, mla_w_uq, mla_kv_norm, mla_w_ukv, mla_qn_norm, mla_qr_norm, mla_kn_norm, mla_kr_norm, sc_w_in, sc_conv_w, sc_w_out):
    bp, tp, _ = x_prompt.shape
    bs, ts, _ = x_sample.shape
    n_frames = bp * tp
    n_sample = bs * ts

    ffn1 = [_ffn_weights(ffn1_norm[l], ffn1_w_gate[l], ffn1_w_up[l], ffn1_w_down[l]) for l in range(2)]
    ffn2 = [_ffn_weights(ffn2_norm[l], ffn2_w_gate[l], ffn2_w_up[l], ffn2_w_down[l]) for l in range(2)]
    w_in = ab_w_in[0]
    s1 = GDN_QKV + GDN_Z
    c0 = s1 + 2 * GDN_HEADS
    w_in_packed = jnp.concatenate([
        w_in[:, :s1],
        _pad_lanes(w_in[:, s1:c0], 0, LANES),
        w_in[:, c0:c0 + MLA_Q_RANK + MLA_KV_RANK],
        _pad_lanes(w_in[:, c0 + MLA_Q_RANK + MLA_KV_RANK:], MLA_NOPE, LANES),
    ], -1).astype(BF16)
    mix_g = mix_norm.reshape(2, 1, D_MODEL)
    gdn_w = (gdn_conv_w[0], _pad_lanes(gdn_A_log[0][None], 0, LANES), _pad_lanes(gdn_dt_bias[0][None], 0, LANES),
             gdn_o_norm[0][None])
    qdim = MLA_NOPE + MLA_ROPE
    w_uq = _pad_lanes(mla_w_uq[0].reshape(MLA_Q_RANK, MLA_HEADS, qdim), 0, LANES).reshape(MLA_Q_RANK, MLA_PAD)
    w_ukv = mla_w_ukv[0].reshape(MLA_KV_RANK, MLA_HEADS, MLA_NOPE + MLA_V)
    w_k = _pad_lanes(w_ukv[..., :MLA_NOPE], 0, LANES).reshape(MLA_KV_RANK, MLA_PAD)
    hv = MLA_HEADS * MLA_V
    w_vt = _pad_lanes(w_ukv[..., MLA_NOPE:], 0, VT_ROWS).reshape(MLA_KV_RANK, VT_TOTAL).T
    v_ones = (jnp.arange(VT_TOTAL) % VT_ROWS == MLA_V).astype(F32)[:, None]
    gq = _pad_lanes(jnp.concatenate([mla_qn_norm[0], mla_qr_norm[0]])[None], 0, LANES)
    gk = _pad_lanes(jnp.concatenate([mla_kn_norm[0], mla_kr_norm[0]])[None], 0, LANES)
    gmean, rot_mat = _head_pair_matrices()
    mla_w = (mla_q_norm[0][None], w_uq.astype(BF16), mla_kv_norm[0][None], w_k.astype(BF16), w_vt.astype(BF16), v_ones,
             gq, gk, gmean, rot_mat)
    w_out = ab_w_out[0].astype(BF16)
    sc_w = (mix_g[1], sc_w_in[0].astype(BF16), sc_conv_w[0], sc_w_out[0].astype(BF16))

    xf = x_prompt.reshape(n_frames, D_MODEL)
    xs = jnp.concatenate([x_sample.reshape(n_sample, D_MODEL), meta_tokens.astype(F32)], 0)
    cos_f, sin_f = _rope_tables(N_META + jnp.arange(tp))
    pos_small = jnp.concatenate([jnp.tile(PAST_LEN + jnp.arange(ts), bs), jnp.arange(N_META)])
    cos_s, sin_s = _rope_tables(pos_small)

    xf = _ffn(xf, ffn1[0])
    xs = _ffn(xs, ffn1[0])
    qkv_f, z_f, ab_f, cq_f, ckv_f, kr_f = _inproj(xf, mix_g[0], w_in_packed)
    qkv_s, z_s, ab_s, cq_s, ckv_s, kr_s = _inproj(xs, mix_g[0], w_in_packed)

    pad_meta = lambda a: jnp.pad(a[n_sample:], ((0, CHUNK - N_META), (0, 0)))[None]
    og_m, s_m, conv_m = _gdn(pad_meta(qkv_s), pad_meta(z_s), pad_meta(ab_s),
                             jnp.zeros((1, GDN_HEADS, GDN_DK, GDN_DV), F32), jnp.zeros((1, HALO, GDN_QKV), F32),
                             gdn_w, rows=CHUNK, n_valid=N_META)
    og_f, p_s, p_conv = _gdn(qkv_f.reshape(bp, tp, GDN_QKV), z_f.reshape(bp, tp, GDN_Z), ab_f.reshape(bp, tp, LANES),
                             s_m, conv_m, gdn_w, rows=4 * CHUNK)
    og_s, s_s, s_conv = _gdn(qkv_s[:n_sample].reshape(bs, ts, GDN_QKV), z_s[:n_sample].reshape(bs, ts, GDN_Z),
                             ab_s[:n_sample].reshape(bs, ts, LANES), state_gdn_S[0],
                             _tail_rows(state_gdn_conv[0], state_gdn_conv.shape[2]), gdn_w, rows=ts)

    q_f, k_f, vt_f, ckvn_f, krn_f = _mla_prep(cq_f, ckv_f, kr_f, cos_f, sin_f, mla_w, batch=bp, key_tile=ATTN_TK)
    q_s, k_s, vt_s, ckvn_s, krn_s = _mla_prep(cq_s, ckv_s, kr_s, cos_s, sin_s, mla_w, batch=1)
    k_c, vt_c = _kv_up(cache_mla_ckv[0].reshape(bs * PAST_LEN, MLA_KV_RANK),
                       _pad_lanes(cache_mla_krope[0].reshape(bs * PAST_LEN, MLA_ROPE), MLA_NOPE, LANES),
                       mla_w[3], mla_w[4], v_ones, gk, gmean, batch=bs)
    k_meta, vt_meta = k_s[n_sample:][None], vt_s[:, :, n_sample:]
    vt_new = vt_s[0, :, :n_sample].reshape(VT_TOTAL, bs, ts).transpose(1, 0, 2)
    omt_f = _attention(q_f.reshape(bp, tp, MLA_PAD), k_f.reshape(bp, tp, MLA_PAD), vt_f, (k_meta, vt_meta),
                       tq=ATTN_TQ, tk=ATTN_TK, causal=True)
    omt_s = _attention(q_s[:n_sample].reshape(bs, ts, MLA_PAD), k_c.reshape(bs, PAST_LEN, MLA_PAD), vt_c,
                       (k_s[:n_sample].reshape(bs, ts, MLA_PAD), vt_new), tq=ts, tk=512)
    omt_m = _attention(jnp.pad(q_s[n_sample:], ((0, LANES - N_META), (0, 0)))[None],
                       jnp.pad(k_s[n_sample:], ((0, CHUNK - N_META), (0, 0)))[None],
                       jnp.pad(vt_meta, ((0, 0), (0, 0), (0, CHUNK - N_META))),
                       None, tq=LANES, tk=CHUNK, nk_valid=N_META)

    og_small = jnp.concatenate([og_s.reshape(n_sample, GDN_Z), og_m[0, :N_META]], 0)
    omt_small = jnp.concatenate([omt_s.transpose(1, 0, 2).reshape(hv, n_sample), omt_m[0, :, :N_META]], -1)[None]
    xf = _ffn(xf, ffn2[0], mix=(og_f.reshape(n_frames, GDN_Z), omt_f, w_out))
    xs = _ffn(xs, ffn2[0], mix=(og_small, omt_small, w_out))

    xf = _ffn(xf, ffn1[1])
    xs = _ffn(xs, ffn1[1])
    n_sc = state_sconv.shape[2]
    xm, sc_m = _sconv(xs[n_sample:][None], jnp.zeros((1, HALO, D_MODEL), F32), sc_w, rows=N_META)
    xf3, p_sc = _sconv(xf.reshape(bp, tp, D_MODEL), sc_m, sc_w, rows=512)
    xs3, s_sc = _sconv(xs[:n_sample].reshape(bs, ts, D_MODEL), _tail_rows(state_sconv[0], n_sc), sc_w, rows=ts)
    xf = _ffn(xf3.reshape(n_frames, D_MODEL), ffn2[1])
    xs = _ffn(jnp.concatenate([xs3.reshape(n_sample, D_MODEL), xm[0]], 0), ffn2[1])

    y_prompt = xf.reshape(bp, tp, D_MODEL)
    y_sample = xs[:n_sample].reshape(bs, ts, D_MODEL)
    rope_lo, rope_hi = MLA_NOPE, MLA_NOPE + MLA_ROPE
    meta_ckv = jnp.broadcast_to(ckvn_s[n_sample:][None], (bp, N_META, MLA_KV_RANK))
    meta_kr = jnp.broadcast_to(krn_s[n_sample:, rope_lo:rope_hi][None], (bp, N_META, MLA_ROPE))
    p_ckv = jnp.concatenate([meta_ckv, ckvn_f.reshape(bp, tp, MLA_KV_RANK)], 1)
    p_kr = jnp.concatenate([meta_kr, krn_f[:, rope_lo:rope_hi].reshape(bp, tp, MLA_ROPE)], 1)
    n_gc = state_gdn_conv.shape[2]
    return (y_prompt, y_sample,
            p_ckv[None], p_kr[None], p_s[None], p_conv[:, HALO - n_gc:][None], p_sc[:, HALO - n_sc:][None],
            ckvn_s[:n_sample].reshape(bs, ts, MLA_KV_RANK)[None],
            krn_s[:n_sample, rope_lo:rope_hi].reshape(bs, ts, MLA_ROPE)[None],
            s_s[None], s_conv[:, HALO - n_gc:][None], s_sc[:, HALO - n_sc:][None])
```

```python
import functools

import jax
import jax.numpy as jnp
from jax import lax
from jax.experimental import pallas as pl
from jax.experimental.pallas import tpu as pltpu

F32 = jnp.float32
BF16 = jnp.bfloat16

D_MODEL = 1024
D_FF = 2816
CHUNK = 64
CHUNK_SHIFT = 6
N_META = 16
EPS = 1e-6
GDN_HEADS = 4
GDN_DK = 128
GDN_DV = 128
GDN_QKV = GDN_HEADS * (2 * GDN_DK + GDN_DV)
GDN_Z = GDN_HEADS * GDN_DV
MLA_HEADS = 8
MLA_NOPE = 64
MLA_ROPE = 32
MLA_V = 64
MLA_Q_RANK = 384
MLA_KV_RANK = 256
ROPE_THETA = 10000.0
PAST_LEN = 1024
LANES = 128
HALO = 8
MLA_PAD = MLA_HEADS * LANES
FF_BLOCK = 256
FF_CHUNKS = D_FF // FF_BLOCK
VMEM_LIMIT = 60000 * 1024
NEG_INF = float("-inf")
LOG2_E = 1.4426950408889634
MASK_LANE0 = MLA_NOPE + MLA_ROPE
MASK_BIG = 2.0 ** 100
ATTN_TQ = 512
ATTN_TK = 512
FFN_ROWS = 1024
INPROJ_ROWS = 512
SCONV_ROWS = 1024
GDN_ROWS = 8 * CHUNK
VT_ROWS = MLA_V + 16
VT_TOTAL = MLA_HEADS * VT_ROWS
ONE_PASS_JUMP = 64.0

IN_QKV = 0
IN_Z = IN_QKV + GDN_QKV
IN_AB = IN_Z + GDN_Z
IN_CQ = IN_AB + LANES
IN_CKV = IN_CQ + MLA_Q_RANK
IN_KR = IN_CKV + MLA_KV_RANK
IN_TOTAL = IN_KR + LANES


def _params(n_axes):
    return pltpu.CompilerParams(dimension_semantics=("arbitrary",) * n_axes, vmem_limit_bytes=VMEM_LIMIT)


def _rms(x, g):
    return x * lax.rsqrt(jnp.mean(x * x, -1, keepdims=True) + EPS) * g


def _sigmoid(x):
    return 1.0 / (1.0 + jnp.exp(-x))


def _silu(x):
    return x * _sigmoid(x)


def _softplus(x):
    return jnp.maximum(x, 0.0) + jnp.log1p(jnp.exp(-jnp.abs(x)))


def _dot(a, b):
    return jnp.dot(a, b, preferred_element_type=F32)


def _dot_nt(a, b):
    return lax.dot_general(a, b, (((1,), (1,)), ((), ())), preferred_element_type=F32)


def _dot_tn(a, b):
    return lax.dot_general(a, b, (((0,), (0,)), ((), ())), preferred_element_type=F32)


def _shift_rows(x, history, k):
    rolled = pltpu.roll(x, k, 0)
    row = lax.broadcasted_iota(jnp.int32, (HALO, 1), 0)
    head = rolled[:HALO]
    for j in range(k):
        head = jnp.where(row == j, history[HALO - k + j:HALO - k + j + 1, :], head)
    return jnp.concatenate([head, rolled[HALO:]], 0)


def _full_spec(shape):
    zeros = (0,) * len(shape)
    return pl.BlockSpec(shape, lambda *_: zeros)


def _row_tile(rows, want):
    return want if rows % want == 0 else rows


def _ffn_body(x_ref, g_ref, wg_ref, wu_ref, wd_ref, *rest, mixed):
    if mixed:
        og_ref, omt_ref, wmix_ref, o_ref, acc_ref = rest
        x = x_ref[...] + _dot(og_ref[...], wmix_ref[:GDN_Z, :]) + _dot_tn(omt_ref[...], wmix_ref[GDN_Z:, :])
    else:
        o_ref, acc_ref = rest
        x = x_ref[...]
    h = _rms(x, g_ref[...]).astype(BF16)
    for c in range(FF_CHUNKS):
        cols = slice(c * FF_BLOCK, (c + 1) * FF_BLOCK)
        gate = _dot(h, wg_ref[:, cols])
        up = _dot(h, wu_ref[:, cols])
        act = (_silu(gate) * up).astype(BF16)
        down = _dot(act, wd_ref[cols, :])
        if c == 0:
            acc_ref[...] = down
        else:
            acc_ref[...] += down
    o_ref[...] = x + 0.5 * acc_ref[...]


def _ffn(x, w, mix=None):
    rows = x.shape[0]
    tm = _row_tile(rows, FFN_ROWS)
    g, wg, wu, wd, layer = w
    resident = lambda a: pl.BlockSpec(a.shape, lambda i: (0,) * a.ndim, pipeline_mode=pl.Buffered(1))
    of_layer = lambda a: pl.BlockSpec((None,) + a.shape[1:], lambda i: (layer, 0, 0), pipeline_mode=pl.Buffered(1))
    in_specs = [pl.BlockSpec((tm, D_MODEL), lambda i: (i, 0)), _full_spec(g.shape), of_layer(wg), of_layer(wu),
                of_layer(wd)]
    args = [x, g, wg, wu, wd]
    if mix is not None:
        og, omt, w_mix = mix
        tiles_per_batch = rows // omt.shape[0] // tm
        in_specs += [pl.BlockSpec((tm, GDN_Z), lambda i: (i, 0)),
                     pl.BlockSpec((None, GDN_Z, tm), lambda i: (i // tiles_per_batch, 0, i % tiles_per_batch)),
                     resident(w_mix)]
        args += [og, omt, w_mix]
    return pl.pallas_call(
        functools.partial(_ffn_body, mixed=mix is not None),
        grid=(rows // tm,),
        in_specs=in_specs,
        out_specs=pl.BlockSpec((tm, D_MODEL), lambda i: (i, 0)),
        out_shape=jax.ShapeDtypeStruct((rows, D_MODEL), F32),
        scratch_shapes=[pltpu.VMEM((tm, D_MODEL), F32)],
        compiler_params=_params(1),
        name="half_ffn",
    )(*args)


def _to_bf16_body(x_ref, o_ref):
    o_ref[...] = x_ref[...].astype(BF16)


def _to_bf16(a):
    a2 = a.reshape(-1, a.shape[-1])
    rows, cols = a2.shape
    tm = _row_tile(rows, 512)
    out = pl.pallas_call(
        _to_bf16_body,
        grid=(rows // tm,),
        in_specs=[pl.BlockSpec((tm, cols), lambda i: (i, 0))],
        out_specs=pl.BlockSpec((tm, cols), lambda i: (i, 0)),
        out_shape=jax.ShapeDtypeStruct((rows, cols), BF16),
        compiler_params=_params(1),
        name="weights_to_bf16",
    )(a2)
    return out.reshape(a.shape)


def _ffn_weights(norm, w_gate, w_up, w_down):
    stacks = (_to_bf16(w_gate), _to_bf16(w_up), _to_bf16(w_down))
    return [(norm[l].reshape(1, D_MODEL),) + stacks + (l,) for l in range(norm.shape[0])]


def _gdn_conv(x, history, cw):
    y = (_shift_rows(x, history, 3) * cw[0:1] + _shift_rows(x, history, 2) * cw[1:2]
         + _shift_rows(x, history, 1) * cw[2:3] + x * cw[3:4])
    return _silu(y)


class _Split:
    def __init__(self, x):
        self.x = x
        self._parts = None
        self._diag = None

    def parts(self):
        if self._parts is None:
            hi = self.x.astype(BF16)
            self._parts = (hi, (self.x - hi.astype(F32)).astype(BF16))
        return self._parts

    def diag_parts(self):
        if self._diag is None:
            first = lax.broadcasted_iota(jnp.int32, self.x.shape, 1) < CHUNK
            zero = jnp.zeros(self.x.shape, BF16)
            self._diag = tuple(jnp.concatenate([jnp.where(first, p, zero), jnp.where(first, zero, p)], 0)
                               for p in self.parts())
        return self._diag


def _split_products(pairs):
    parts = [(a.parts(), b.diag_parts()) for a, b in pairs]
    outs = []
    for (a_hi, a_lo), (b_hi, b_lo) in parts:
        a2 = jnp.concatenate([a_hi, a_lo], 0)
        r = _dot(a2, b_hi) + _dot(a2, b_lo)
        half = a_hi.shape[0]
        outs.append(_Split(r[:half] + r[half:]))
    return outs


def _unit_lower_inverses(ns, eye_f, m16, m32):
    k = len(ns)
    nd = [_Split(jnp.where(m16, n, 0.0)) for n in ns]
    n2 = _split_products([(x, x) for x in nd])
    lvl = _split_products([(x, x) for x in n2] +
                          [(_Split(eye_f - a.x), _Split(eye_f + b.x)) for a, b in zip(nd, n2)])
    n4, t = lvl[:k], lvl[k:]
    lvl = _split_products([(x, x) for x in n4] + [(a, _Split(eye_f + b.x)) for a, b in zip(t, n4)])
    n8, t = lvl[:k], lvl[k:]
    t = _split_products([(a, _Split(eye_f + b.x)) for a, b in zip(t, n8)])
    for keep in (jnp.logical_and(m32, jnp.logical_not(m16)), jnp.logical_not(m32)):
        off = [_Split(jnp.where(keep, n, 0.0)) for n in ns]
        x = _split_products(list(zip(off, t)))
        y = _split_products(list(zip(t, x)))
        t = [_Split(a.x - b.x) for a, b in zip(t, y)]
    return [a.x for a in t]


def _gdn_body(qkv_ref, z_ref, ab_ref, s0_ref, cprev_ref, cw_ref, alog_ref, dtb_ref, on_ref,
              og_ref, sout_ref, cout_ref, s_scr, halo_scr, *, rows, n_valid):
    t = pl.program_id(1)
    C = CHUNK
    row = lax.broadcasted_iota(jnp.int32, (rows, 1), 0)

    @pl.when(t == 0)
    def _():
        s_scr[...] = s0_ref[...]
        halo_scr[...] = cprev_ref[...]

    x = qkv_ref[...]
    y = _gdn_conv(x, halo_scr[...], cw_ref[...])
    last_rows = x[n_valid - HALO:n_valid, :]
    halo_scr[...] = last_rows

    ab = ab_ref[...]
    g_all = -jnp.exp(alog_ref[...]) * _softplus(ab + dtb_ref[...])
    b_all = _sigmoid(ab)
    if n_valid < rows:
        valid = row < n_valid
        y = jnp.where(valid, y, 0.0)
        g_all = jnp.where(valid, g_all, 0.0)
        b_all = jnp.where(valid, b_all, 0.0)

    ii = lax.broadcasted_iota(jnp.int32, (C, LANES), 0)
    pair_lane = lax.broadcasted_iota(jnp.int32, (C, LANES), 1)
    jj = pair_lane & (C - 1)
    first_head = pair_lane < C
    causal = ii >= jj
    strict = ii > jj
    eye_f = (ii == jj).astype(F32)
    m16 = (ii >> 4) == (jj >> 4)
    m32 = (ii >> 5) == (jj >> 5)
    on = on_ref[...]
    n_chunks = rows // C
    heads = range(GDN_HEADS)

    ri = lax.broadcasted_iota(jnp.int32, (rows, rows), 0)
    rj = lax.broadcasted_iota(jnp.int32, (rows, rows), 1)
    tril_blk = jnp.logical_and(ri >= rj, (ri >> CHUNK_SHIFT) == (rj >> CHUNK_SHIFT)).astype(BF16)
    g_hi = g_all.astype(BF16)
    g_r1 = g_all - g_hi.astype(F32)
    g_mid = g_r1.astype(BF16)
    g_lo = (g_r1 - g_mid.astype(F32)).astype(BF16)
    gcum = _dot(tril_blk, g_hi) + _dot(tril_blk, g_mid) + _dot(tril_blk, g_lo)
    egcum = jnp.exp(gcum)

    qn, kn, vn = [], [], []
    for h in heads:
        qh = y[:, h * GDN_DK:(h + 1) * GDN_DK]
        kh = y[:, GDN_HEADS * GDN_DK + h * GDN_DK:GDN_HEADS * GDN_DK + (h + 1) * GDN_DK]
        qn.append(qh * lax.rsqrt(jnp.sum(qh * qh, -1, keepdims=True) + EPS) * (GDN_DK ** -0.5))
        kn.append(kh * lax.rsqrt(jnp.sum(kh * kh, -1, keepdims=True) + EPS))
        vn.append(y[:, 2 * GDN_HEADS * GDN_DK + h * GDN_DV:2 * GDN_HEADS * GDN_DK + (h + 1) * GDN_DV])

    head_pairs = [(a, a + 1) for a in range(0, GDN_HEADS, 2)]
    chains = [(c, pair) for c in range(n_chunks) for pair in head_pairs]
    pre = {}
    n_mats = []
    for c in range(n_chunks):
        sl = slice(c * C, (c + 1) * C)
        gcum_t = gcum[sl].T
        for a, b in head_pairs:
            col = lambda arr, h: arr[sl, h:h + 1]
            gcol = jnp.where(first_head, col(gcum, a), col(gcum, b))
            beta2 = jnp.where(first_head, col(b_all, GDN_HEADS + a), col(b_all, GDN_HEADS + b))
            grow = jnp.concatenate([gcum_t[a:a + 1, :], gcum_t[b:b + 1, :]], -1)
            dec_causal = jnp.exp(jnp.where(causal, gcol - grow, NEG_INF))
            kb = {h: kn[h][sl].astype(BF16) for h in (a, b)}
            kk = jnp.concatenate([_dot_nt(kb[a], kb[a]), _dot_nt(kb[b], kb[b])], -1)
            n_mats.append(beta2 * kk * jnp.where(strict, dec_causal, 0.0))
            qk = jnp.concatenate([_dot_nt(qn[h][sl].astype(BF16), kb[h]) for h in (a, b)], -1)
            qk = (qk * dec_causal).astype(BF16)
            for h, lanes in ((a, slice(0, C)), (b, slice(C, 2 * C))):
                beta = col(b_all, GDN_HEADS + h)
                eg = col(egcum, h)
                g_last = gcum[(c + 1) * C - 1:(c + 1) * C, h:h + 1]
                qh, kh, vh = qn[h][sl], kn[h][sl], vn[h][sl]
                pre[c, h] = dict(
                    qk=qk[:, lanes],
                    rhs=jnp.concatenate([(beta * vh).astype(BF16), ((beta * eg) * kh).astype(BF16)], -1),
                    qd=(qh * eg).astype(BF16),
                    kt=(kh * jnp.exp(g_last - col(gcum, h))).astype(BF16),
                    decay=jnp.exp(g_last))
    for (c, (a, b)), tinv in zip(chains, _unit_lower_inverses(n_mats, eye_f, m16, m32)):
        tinv = tinv.astype(BF16)
        for h, lanes in ((a, slice(0, C)), (b, slice(C, 2 * C))):
            uw = _dot(tinv[:, lanes], pre[c, h]["rhs"])
            pre[c, h]["u"] = uw[:, :GDN_DV]
            pre[c, h]["wk"] = uw[:, GDN_DV:].astype(BF16)

    state = [s_scr[h] for h in heads]
    for c in range(n_chunks):
        sl = slice(c * C, (c + 1) * C)
        sb = [s.astype(BF16) for s in state]
        w = [(pre[c, h]["u"] - _dot(pre[c, h]["wk"], sb[h])).astype(BF16) for h in heads]
        o = [_dot(pre[c, h]["qd"], sb[h]) + _dot(pre[c, h]["qk"], w[h]) for h in heads]
        state = [state[h] * pre[c, h]["decay"] + _dot_tn(pre[c, h]["kt"], w[h]) for h in heads]
        for h in heads:
            gated = _rms(o[h], on) * _silu(z_ref[sl, h * GDN_DV:(h + 1) * GDN_DV])
            og_ref[sl, h * GDN_DV:(h + 1) * GDN_DV] = gated.astype(BF16)
    for h in heads:
        s_scr[h] = state[h]

    @pl.when(t == pl.num_programs(1) - 1)
    def _():
        sout_ref[...] = s_scr[...]
        cout_ref[...] = last_rows


def _gdn(qkv, z, ab, s0, cprev, w, *, rows, n_valid=None):
    b, t_len, _ = qkv.shape
    n_valid = rows if n_valid is None else n_valid
    conv_w, a_log, dt_bias, o_norm = w
    shared = s0.shape[0] == 1 and b > 1
    bsel = (lambda bi: 0) if shared else (lambda bi: bi)
    return pl.pallas_call(
        functools.partial(_gdn_body, rows=rows, n_valid=n_valid),
        grid=(b, t_len // rows),
        in_specs=[
            pl.BlockSpec((None, rows, GDN_QKV), lambda bi, ti: (bi, ti, 0)),
            pl.BlockSpec((None, rows, GDN_Z), lambda bi, ti: (bi, ti, 0)),
            pl.BlockSpec((None, rows, LANES), lambda bi, ti: (bi, ti, 0)),
            pl.BlockSpec((None, GDN_HEADS, GDN_DK, GDN_DV), lambda bi, ti: (bsel(bi), 0, 0, 0)),
            pl.BlockSpec((None, HALO, GDN_QKV), lambda bi, ti: (bsel(bi), 0, 0)),
            _full_spec(conv_w.shape), _full_spec(a_log.shape), _full_spec(dt_bias.shape), _full_spec(o_norm.shape),
        ],
        out_specs=[
            pl.BlockSpec((None, rows, GDN_Z), lambda bi, ti: (bi, ti, 0)),
            pl.BlockSpec((None, GDN_HEADS, GDN_DK, GDN_DV), lambda bi, ti: (bi, 0, 0, 0)),
            pl.BlockSpec((None, HALO, GDN_QKV), lambda bi, ti: (bi, 0, 0)),
        ],
        out_shape=[
            jax.ShapeDtypeStruct((b, t_len, GDN_Z), BF16),
            jax.ShapeDtypeStruct((b, GDN_HEADS, GDN_DK, GDN_DV), F32),
            jax.ShapeDtypeStruct((b, HALO, GDN_QKV), F32),
        ],
        scratch_shapes=[pltpu.VMEM((GDN_HEADS, GDN_DK, GDN_DV), F32), pltpu.VMEM((HALO, GDN_QKV), F32)],
        compiler_params=_params(2),
        name="gdn_mixer",
    )(qkv, z, ab, s0, cprev, conv_w, a_log, dt_bias, o_norm)


def _lane_masks():
    lane = lax.broadcasted_iota(jnp.int32, (1, LANES), 1)
    nope = lane < MLA_NOPE
    rope_a = jnp.logical_and(lane >= MLA_NOPE, lane < MLA_NOPE + MLA_ROPE // 2)
    rope_b = jnp.logical_and(lane >= MLA_NOPE + MLA_ROPE // 2, lane < MLA_NOPE + MLA_ROPE)
    return nope, rope_a, rope_b


def _rope_block(x, cos, sin, rope_a, rope_b):
    half = MLA_ROPE // 2
    rot = jnp.where(rope_a, -pltpu.roll(x, LANES - half, 1), jnp.where(rope_b, pltpu.roll(x, half, 1), 0.0))
    return x * cos + rot * sin


PAIR = 2 * LANES


def _twice(a):
    return jnp.concatenate([a, a], -1)


def _k_heads(k, shared, gk, gmean, k_out):
    shared2, gk2 = _twice(shared), _twice(gk)
    for p in range(MLA_HEADS // 2):
        x = k[:, p * PAIR:(p + 1) * PAIR]
        ms = _dot((x * x).astype(BF16), gmean)
        k_out[:, p * PAIR:(p + 1) * PAIR] = (x * lax.rsqrt(ms + EPS) * gk2 + shared2).astype(BF16)


def _inproj_body(x_ref, g_ref, w_ref, cos_ref, sin_ref, qn_ref, wuq_ref, kvn_ref, wk_ref, wvt_ref, vones_ref,
                 gq_ref, gk_ref, gmean_ref, rot_ref, qkv_ref, z_ref, ab_ref, q_out, k_out, vt_out, ckv_out, kr_out,
                 *, tiles_per_batch, key_tile_chunks):
    hn = _rms(x_ref[...], g_ref[...]).astype(BF16)
    qkv_ref[...] = _dot(hn, w_ref[:, IN_QKV:IN_Z])
    z_ref[...] = _dot(hn, w_ref[:, IN_Z:IN_AB])
    ab_ref[...] = _dot(hn, w_ref[:, IN_AB:IN_CQ])
    cq = _dot(hn, w_ref[:, IN_CQ:IN_CKV])
    ckv = _dot(hn, w_ref[:, IN_CKV:IN_KR])
    krp = _dot(hn, w_ref[:, IN_KR:IN_TOTAL])

    nope, rope_a, rope_b = _lane_masks()
    cos = cos_ref[...]
    sin = sin_ref[...]
    gk = gk_ref[...]
    gmean = gmean_ref[...]
    rot_mat = rot_ref[...]
    gq2 = _twice(gq_ref[...] * ((MLA_NOPE + MLA_ROPE) ** -0.5 * LOG2_E))
    cos2, sin2 = _twice(cos), _twice(sin)

    cqn = _rms(cq, qn_ref[...]).astype(BF16)
    q = _dot(cqn, wuq_ref[...])
    for p in range(MLA_HEADS // 2):
        x = q[:, p * PAIR:(p + 1) * PAIR]
        ms = _dot((x * x).astype(BF16), gmean)
        x = x * lax.rsqrt(ms + EPS) * gq2
        q_out[:, p * PAIR:(p + 1) * PAIR] = (x * cos2 + _dot(x.astype(BF16), rot_mat) * sin2).astype(BF16)

    ckvn = _rms(ckv, kvn_ref[...])
    ckv_out[...] = ckvn
    cb = ckvn.astype(BF16)
    krn = krp * lax.rsqrt(jnp.sum(krp * krp, -1, keepdims=True) / MLA_ROPE + EPS) * jnp.where(nope, 0.0, gk)
    kr_new = _rope_block(krn, cos, sin, rope_a, rope_b)
    kr_out[...] = kr_new
    vt_out[...] = (_dot_nt(wvt_ref[...], cb) + vones_ref[...]).astype(BF16)
    shared_lanes = kr_new
    if key_tile_chunks is not None:
        tm = kr_new.shape[0]
        pos = (pl.program_id(0) % tiles_per_batch) * tm + lax.broadcasted_iota(jnp.int32, (tm, 1), 0)
        lane = lax.broadcasted_iota(jnp.int32, (1, LANES), 1)
        chunk = (pos >> CHUNK_SHIFT) & (key_tile_chunks - 1)
        shared_lanes = kr_new + (lane - MASK_LANE0 == chunk).astype(F32)
    _k_heads(_dot(cb, wk_ref[...]), shared_lanes, jnp.where(nope, gk, 0.0), gmean, k_out)


def _vt_spec(tm, tiles_per_batch):
    return pl.BlockSpec((None, VT_TOTAL, tm), lambda i: (i // tiles_per_batch, 0, i % tiles_per_batch))


def _inproj(x, g, w_in, cos, sin, w, *, batch, key_tile=None):
    rows = x.shape[0]
    tm = _row_tile(rows, INPROJ_ROWS)
    tiles_per_batch = rows // batch // tm
    key_tile_chunks = None if key_tile is None else key_tile // CHUNK
    assert key_tile_chunks is None or (key_tile_chunks & (key_tile_chunks - 1) == 0
                                       and key_tile_chunks <= LANES - MASK_LANE0)
    q_norm, w_uq, kv_norm, w_k, w_vt, v_ones, gq, gk, gmean, rot_mat = w
    row_spec = lambda n: pl.BlockSpec((tm, n), lambda i: (i, 0))
    tab_spec = pl.BlockSpec((tm, LANES), lambda i: (i % tiles_per_batch, 0))
    consts = [g, w_in, q_norm, w_uq, kv_norm, w_k, w_vt, v_ones, gq, gk, gmean, rot_mat]
    return pl.pallas_call(
        functools.partial(_inproj_body, tiles_per_batch=tiles_per_batch, key_tile_chunks=key_tile_chunks),
        grid=(rows // tm,),
        in_specs=[row_spec(D_MODEL), _full_spec(g.shape), _full_spec(w_in.shape), tab_spec, tab_spec]
        + [_full_spec(a.shape) for a in consts[2:]],
        out_specs=[row_spec(GDN_QKV), row_spec(GDN_Z), row_spec(LANES), row_spec(MLA_PAD), row_spec(MLA_PAD),
                   _vt_spec(tm, tiles_per_batch), row_spec(MLA_KV_RANK), row_spec(LANES)],
        out_shape=[jax.ShapeDtypeStruct((rows, GDN_QKV), F32), jax.ShapeDtypeStruct((rows, GDN_Z), F32),
                   jax.ShapeDtypeStruct((rows, LANES), F32),
                   jax.ShapeDtypeStruct((rows, MLA_PAD), BF16), jax.ShapeDtypeStruct((rows, MLA_PAD), BF16),
                   jax.ShapeDtypeStruct((batch, VT_TOTAL, rows // batch), BF16),
                   jax.ShapeDtypeStruct((rows, MLA_KV_RANK), F32), jax.ShapeDtypeStruct((rows, LANES), F32)],
        compiler_params=_params(1),
        name="mixer_inproj",
    )(x, g, w_in, cos, sin, *consts[2:])


def _kv_up_body(ckv_ref, kr_ref, wk_ref, wvt_ref, vones_ref, gk_ref, gmean_ref, k_out, vt_out):
    nope, _, _ = _lane_masks()
    cb = ckv_ref[...].astype(BF16)
    vt_out[...] = (_dot_nt(wvt_ref[...], cb) + vones_ref[...]).astype(BF16)
    _k_heads(_dot(cb, wk_ref[...]), kr_ref[...], jnp.where(nope, gk_ref[...], 0.0), gmean_ref[...], k_out)


def _kv_up(ckv, krp, w_k, w_vt, v_ones, gk, gmean, *, batch):
    rows = ckv.shape[0]
    tm = _row_tile(rows, 512)
    tiles_per_batch = rows // batch // tm
    row_spec = lambda n: pl.BlockSpec((tm, n), lambda i: (i, 0))
    return pl.pallas_call(
        _kv_up_body,
        grid=(rows // tm,),
        in_specs=[row_spec(MLA_KV_RANK), row_spec(LANES), _full_spec(w_k.shape), _full_spec(w_vt.shape),
                  _full_spec(v_ones.shape), _full_spec(gk.shape), _full_spec(gmean.shape)],
        out_specs=[row_spec(MLA_PAD), _vt_spec(tm, tiles_per_batch)],
        out_shape=[jax.ShapeDtypeStruct((rows, MLA_PAD), BF16),
                   jax.ShapeDtypeStruct((batch, VT_TOTAL, rows // batch), BF16)],
        compiler_params=_params(1),
        name="mla_kv_up",
    )(ckv, krp, w_k, w_vt, v_ones, gk, gmean)


def _attn_body(qi_ref, kj_ref, *refs, tq, tk, n_key_tiles, causal, nk_valid, has_ext):
    if has_ext:
        q_ref, k_ref, vt_ref, ke_ref, vte_ref, o_ref, m_scr, acc_scr, slot_ref = refs
    else:
        q_ref, k_ref, vt_ref, o_ref, m_scr, acc_scr, slot_ref = refs
    step = pl.program_id(1)
    i = qi_ref[step]
    j = kj_ref[step]
    last_j = ((i + 1) * tq + tk - 1) // tk - 1 if causal else n_key_tiles - 1
    head_lanes = [slice(h * LANES, (h + 1) * LANES) for h in range(MLA_HEADS)]

    def head_scores(k_blk, q_mask):
        if q_mask is None:
            q_heads = [q_ref[:, hs] for hs in head_lanes]
        else:
            q_heads = [q_ref[:, hs] + q_mask for hs in head_lanes]
        return [_dot_nt(k_blk[:, hs], qh) for hs, qh in zip(head_lanes, q_heads)]

    def values(vt_blk, h):
        return vt_blk[h * VT_ROWS:(h + 1) * VT_ROWS, :]

    def two_pass(k_blk, vt_blk, cur, bias=None, q_mask=None):
        scores = head_scores(k_blk, q_mask)
        for h in range(MLA_HEADS):
            s = scores[h] if bias is None else scores[h] + bias
            m_old = m_scr[cur, h][0:1]
            m_new = jnp.maximum(m_old, jnp.max(s, 0, keepdims=True))
            alpha = jnp.exp2(m_old - m_new)
            p = jnp.exp2(s - m_new)
            m_scr[cur, h] = jnp.broadcast_to(m_new, (HALO, tq))
            acc_scr[cur, h] = acc_scr[cur, h] * alpha + _dot(values(vt_blk, h), p.astype(BF16))

    def one_pass(k_blk, vt_blk, cur, q_mask=None):
        scores = head_scores(k_blk, q_mask)
        rise = None
        for h in range(MLA_HEADS):
            s = scores[h]
            m_old = m_scr[cur, h][0:1]
            p = jnp.exp2(s - m_old)
            m_tile = jnp.max(s, 0, keepdims=True)
            m_new = jnp.maximum(m_old, m_tile)
            rise = m_tile - m_old if rise is None else jnp.maximum(rise, m_tile - m_old)
            m_scr[1 - cur, h] = jnp.broadcast_to(m_new, (HALO, tq))
            acc_scr[1 - cur, h] = (acc_scr[cur, h] + _dot(values(vt_blk, h), p.astype(BF16))) * jnp.exp2(m_old - m_new)
        return jnp.max(rise)

    def guarded(k_blk, vt_blk, q_mask=None):
        cur = slot_ref[0]
        rise = one_pass(k_blk, vt_blk, cur, q_mask)

        @pl.when(rise <= ONE_PASS_JUMP)
        def _():
            slot_ref[0] = 1 - cur

        @pl.when(jnp.logical_not(rise <= ONE_PASS_JUMP))
        def _():
            two_pass(k_blk, vt_blk, cur, q_mask=q_mask)

    @pl.when(j == 0)
    def _():
        slot_ref[0] = 0
        m_scr[0] = jnp.full(m_scr.shape[1:], NEG_INF, F32)
        acc_scr[0] = jnp.zeros(acc_scr.shape[1:], F32)
        if has_ext:
            two_pass(ke_ref, vte_ref, 0)

    if nk_valid is not None:
        kpos = j * tk + lax.broadcasted_iota(jnp.int32, (tk, 1), 0)
        two_pass(k_ref, vt_ref, slot_ref[0], bias=jnp.where(kpos < nk_valid, 0.0, NEG_INF))
    elif not has_ext:
        two_pass(k_ref, vt_ref, slot_ref[0])
    elif causal:
        @pl.when(j < last_j)
        def _():
            guarded(k_ref, vt_ref)

        @pl.when(j == last_j)
        def _():
            lane = lax.broadcasted_iota(jnp.int32, (1, LANES), 1)
            q_chunk = (i * tq + lax.broadcasted_iota(jnp.int32, (tq, 1), 0)) >> CHUNK_SHIFT
            key_chunk = j * (tk // CHUNK) + lane - MASK_LANE0
            hidden = jnp.logical_and(jnp.logical_and(lane >= MASK_LANE0, lane < MASK_LANE0 + tk // CHUNK),
                                     key_chunk > q_chunk)
            guarded(k_ref, vt_ref, q_mask=jnp.where(hidden, -MASK_BIG, 0.0).astype(BF16))
    else:
        guarded(k_ref, vt_ref)

    @pl.when(j == last_j)
    def _():
        cur = slot_ref[0]
        for h in range(MLA_HEADS):
            acc = acc_scr[cur, h]
            o_ref[h * MLA_V:(h + 1) * MLA_V, :] = (acc[:MLA_V] / acc[MLA_V:MLA_V + 1]).astype(BF16)


def _attention(q, k, vt, ext, *, tq, tk, causal=False, nk_valid=None):
    b, nq, _ = q.shape
    nk = k.shape[1]
    n_key_tiles = nk // tk
    assert not causal or tk % tq == 0
    pairs = [(i, j) for i in range(nq // tq) for j in range(n_key_tiles) if not causal or j * tk < (i + 1) * tq]
    qi = jnp.asarray([p[0] for p in pairs], jnp.int32)
    kj = jnp.asarray([p[1] for p in pairs], jnp.int32)
    hv = MLA_HEADS * MLA_V
    in_specs = [
        pl.BlockSpec((None, tq, MLA_PAD), lambda bi, s, qi, kj: (bi, qi[s], 0)),
        pl.BlockSpec((None, tk, MLA_PAD), lambda bi, s, qi, kj: (bi, kj[s], 0)),
        pl.BlockSpec((None, VT_TOTAL, tk), lambda bi, s, qi, kj: (bi, 0, kj[s])),
    ]
    args = [q, k, vt]
    if ext is not None:
        ne = ext[0].shape[1]
        shared = ext[0].shape[0] == 1 and b > 1
        emap = (lambda bi, s, qi, kj: (0, 0, 0)) if shared else (lambda bi, s, qi, kj: (bi, 0, 0))
        in_specs += [pl.BlockSpec((None, ne, MLA_PAD), emap), pl.BlockSpec((None, VT_TOTAL, ne), emap)]
        args += list(ext)
    return pl.pallas_call(
        functools.partial(_attn_body, tq=tq, tk=tk, n_key_tiles=n_key_tiles, causal=causal, nk_valid=nk_valid,
                          has_ext=ext is not None),
        grid_spec=pltpu.PrefetchScalarGridSpec(
            num_scalar_prefetch=2,
            grid=(b, len(pairs)),
            in_specs=in_specs,
            out_specs=pl.BlockSpec((None, hv, tq), lambda bi, s, qi, kj: (bi, 0, qi[s])),
            scratch_shapes=[pltpu.VMEM((2, MLA_HEADS, HALO, tq), F32), pltpu.VMEM((2, MLA_HEADS, VT_ROWS, tq), F32),
                            pltpu.SMEM((1,), jnp.int32)]),
        out_shape=jax.ShapeDtypeStruct((b, hv, nq), BF16),
        compiler_params=_params(2),
        name="mla_attention",
    )(qi, kj, *args)


def _sconv_body(x_ref, g_ref, win_ref, cw_ref, wout_ref, prev_ref, o_ref, pout_ref, halo_scr, *, rows):
    t = pl.program_id(1)

    @pl.when(t == 0)
    def _():
        halo_scr[...] = prev_ref[...]

    x = x_ref[...]
    hn = _rms(x, g_ref[...]).astype(BF16)
    bg = _dot(hn, win_ref[:, :D_MODEL])
    u = _dot(hn, win_ref[:, D_MODEL:2 * D_MODEL]) * _dot(hn, win_ref[:, 2 * D_MODEL:])
    halo = halo_scr[...]
    u1 = _shift_rows(u, halo, 1)
    u2 = _shift_rows(u, halo, 2)
    cw = cw_ref[...]
    y = u2 * cw[0:1] + u1 * cw[1:2] + u * cw[2:3]
    last_rows = u[rows - HALO:rows, :]
    halo_scr[...] = last_rows
    o_ref[...] = x + _dot((bg * y).astype(BF16), wout_ref[...])

    @pl.when(t == pl.num_programs(1) - 1)
    def _():
        pout_ref[...] = last_rows


def _sconv(x, prev, w, *, rows):
    b, t_len, _ = x.shape
    g, w_in, conv_w, w_out = w
    shared = prev.shape[0] == 1 and b > 1
    bsel = (lambda bi: 0) if shared else (lambda bi: bi)
    return pl.pallas_call(
        functools.partial(_sconv_body, rows=rows),
        grid=(b, t_len // rows),
        in_specs=[pl.BlockSpec((None, rows, D_MODEL), lambda bi, ti: (bi, ti, 0)), _full_spec(g.shape),
                  _full_spec(w_in.shape), _full_spec(conv_w.shape), _full_spec(w_out.shape),
                  pl.BlockSpec((None, HALO, D_MODEL), lambda bi, ti: (bsel(bi), 0, 0))],
        out_specs=[pl.BlockSpec((None, rows, D_MODEL), lambda bi, ti: (bi, ti, 0)),
                   pl.BlockSpec((None, HALO, D_MODEL), lambda bi, ti: (bi, 0, 0))],
        out_shape=[jax.ShapeDtypeStruct((b, t_len, D_MODEL), F32), jax.ShapeDtypeStruct((b, HALO, D_MODEL), F32)],
        scratch_shapes=[pltpu.VMEM((HALO, D_MODEL), F32)],
        compiler_params=_params(2),
        name="sconv_mixer",
    )(x, g, w_in, conv_w, w_out, prev)


def _pad_lanes(a, before, total):
    return jnp.pad(a, [(0, 0)] * (a.ndim - 1) + [(before, total - before - a.shape[-1])])


def _rope_tables(pos):
    half = MLA_ROPE // 2
    inv = ROPE_THETA ** (-jnp.arange(half, dtype=F32) / half)
    ang = pos.astype(F32)[:, None] * inv
    cos, sin = jnp.cos(ang), jnp.sin(ang)
    n = pos.shape[0]
    tail = LANES - MLA_NOPE - MLA_ROPE
    cos_l = jnp.concatenate([jnp.ones((n, MLA_NOPE), F32), cos, cos, jnp.ones((n, tail), F32)], -1)
    sin_l = jnp.concatenate([jnp.zeros((n, MLA_NOPE), F32), sin, sin, jnp.zeros((n, tail), F32)], -1)
    return cos_l, sin_l


def _head_pair_matrices():
    i = jnp.arange(PAIR)[:, None]
    j = jnp.arange(PAIR)[None, :]
    same = (i // LANES) == (j // LANES)
    li, lj = i % LANES, j % LANES
    half = MLA_ROPE // 2
    rope_end = MLA_NOPE + MLA_ROPE
    in_rope = lambda l: (l >= MLA_NOPE) & (l < rope_end)
    gmean = jnp.where(same & (li < MLA_NOPE) & (lj < MLA_NOPE), 1.0 / MLA_NOPE,
                      jnp.where(same & in_rope(li) & in_rope(lj), 1.0 / MLA_ROPE, 0.0))
    first, second = in_rope(lj) & (lj < MLA_NOPE + half), in_rope(lj) & (lj >= MLA_NOPE + half)
    rot = jnp.where(same & first & (li == lj + half), -1.0, jnp.where(same & second & (li == lj - half), 1.0, 0.0))
    return gmean.astype(BF16), rot.astype(BF16)


def _tail_rows(a, n):
    return jnp.pad(a, ((0, 0), (HALO - n, 0), (0, 0)))


def kernel(x_prompt, x_sample, cache_mla_ckv, cache_mla_krope, state_gdn_S, state_gdn_conv, state_sconv, meta_tokens, ffn1_norm, ffn1_w_gate, ffn1_w_up, ffn1_w_down, ffn2_norm, ffn2_w_gate, ffn2_w_up, ffn2_w_down, mix_norm, ab_w_in, ab_w_out, gdn_conv_w, gdn_A_log, gdn_dt_bias, gdn_o_norm, mla_q_norm, mla_w_uq, mla_kv_norm, mla_w_ukv, mla_qn_norm, mla_qr_norm, mla_kn_norm, mla_kr_norm, sc_w_in, sc_conv_w, sc_w_out):
    bp, tp, _ = x_prompt.shape
    bs, ts, _ = x_sample.shape
    n_frames = bp * tp
    n_sample = bs * ts

    ffn1 = _ffn_weights(ffn1_norm, ffn1_w_gate, ffn1_w_up, ffn1_w_down)
    ffn2 = _ffn_weights(ffn2_norm, ffn2_w_gate, ffn2_w_up, ffn2_w_down)
    w_in = ab_w_in[0]
    s1 = GDN_QKV + GDN_Z
    c0 = s1 + 2 * GDN_HEADS
    w_in_packed = jnp.concatenate([
        w_in[:, :s1],
        _pad_lanes(w_in[:, s1:c0], 0, LANES),
        w_in[:, c0:c0 + MLA_Q_RANK + MLA_KV_RANK],
        _pad_lanes(w_in[:, c0 + MLA_Q_RANK + MLA_KV_RANK:], MLA_NOPE, LANES),
    ], -1).astype(BF16)
    mix_g = mix_norm.reshape(2, 1, D_MODEL)
    gdn_w = (gdn_conv_w[0], _pad_lanes(gdn_A_log[0][None], 0, LANES), _pad_lanes(gdn_dt_bias[0][None], 0, LANES),
             gdn_o_norm[0][None])
    qdim = MLA_NOPE + MLA_ROPE
    w_uq = _pad_lanes(mla_w_uq[0].reshape(MLA_Q_RANK, MLA_HEADS, qdim), 0, LANES).reshape(MLA_Q_RANK, MLA_PAD)
    w_ukv = mla_w_ukv[0].reshape(MLA_KV_RANK, MLA_HEADS, MLA_NOPE + MLA_V)
    w_k = _pad_lanes(w_ukv[..., :MLA_NOPE], 0, LANES).reshape(MLA_KV_RANK, MLA_PAD)
    hv = MLA_HEADS * MLA_V
    w_vt = _pad_lanes(w_ukv[..., MLA_NOPE:], 0, VT_ROWS).reshape(MLA_KV_RANK, VT_TOTAL).T
    v_ones = (jnp.arange(VT_TOTAL) % VT_ROWS == MLA_V).astype(F32)[:, None]
    gq = _pad_lanes(jnp.concatenate([mla_qn_norm[0], mla_qr_norm[0]])[None], 0, LANES)
    gk = _pad_lanes(jnp.concatenate([mla_kn_norm[0], mla_kr_norm[0]])[None], 0, LANES)
    gmean, rot_mat = _head_pair_matrices()
    mla_w = (mla_q_norm[0][None], w_uq.astype(BF16), mla_kv_norm[0][None], w_k.astype(BF16), w_vt.astype(BF16), v_ones,
             gq, gk, gmean, rot_mat)
    w_out = ab_w_out[0].astype(BF16)
    sc_w = (mix_g[1], sc_w_in[0].astype(BF16), sc_conv_w[0], sc_w_out[0].astype(BF16))

    xf = x_prompt.reshape(n_frames, D_MODEL)
    xs = jnp.concatenate([x_sample.reshape(n_sample, D_MODEL), meta_tokens.astype(F32)], 0)
    cos_f, sin_f = _rope_tables(N_META + jnp.arange(tp))
    pos_small = jnp.concatenate([jnp.tile(PAST_LEN + jnp.arange(ts), bs), jnp.arange(N_META)])
    cos_s, sin_s = _rope_tables(pos_small)

    xf = _ffn(xf, ffn1[0])
    xs = _ffn(xs, ffn1[0])
    qkv_f, z_f, ab_f, q_f, k_f, vt_f, ckvn_f, krn_f = _inproj(xf, mix_g[0], w_in_packed, cos_f, sin_f, mla_w,
                                                               batch=bp, key_tile=ATTN_TK)
    qkv_s, z_s, ab_s, q_s, k_s, vt_s, ckvn_s, krn_s = _inproj(xs, mix_g[0], w_in_packed, cos_s, sin_s, mla_w, batch=1)

    pad_meta = lambda a: jnp.pad(a[n_sample:], ((0, CHUNK - N_META), (0, 0)))[None]
    og_m, s_m, conv_m = _gdn(pad_meta(qkv_s), pad_meta(z_s), pad_meta(ab_s),
                             jnp.zeros((1, GDN_HEADS, GDN_DK, GDN_DV), F32), jnp.zeros((1, HALO, GDN_QKV), F32),
                             gdn_w, rows=CHUNK, n_valid=N_META)
    og_f, p_s, p_conv = _gdn(qkv_f.reshape(bp, tp, GDN_QKV), z_f.reshape(bp, tp, GDN_Z), ab_f.reshape(bp, tp, LANES),
                             s_m, conv_m, gdn_w, rows=GDN_ROWS)
    og_s, s_s, s_conv = _gdn(qkv_s[:n_sample].reshape(bs, ts, GDN_QKV), z_s[:n_sample].reshape(bs, ts, GDN_Z),
                             ab_s[:n_sample].reshape(bs, ts, LANES), state_gdn_S[0],
                             _tail_rows(state_gdn_conv[0], state_gdn_conv.shape[2]), gdn_w, rows=ts)

    k_c, vt_c = _kv_up(cache_mla_ckv[0].reshape(bs * PAST_LEN, MLA_KV_RANK),
                       _pad_lanes(cache_mla_krope[0].reshape(bs * PAST_LEN, MLA_ROPE), MLA_NOPE, LANES),
                       mla_w[3], mla_w[4], v_ones, gk, gmean, batch=bs)
    k_meta, vt_meta = k_s[n_sample:][None], vt_s[:, :, n_sample:]
    vt_new = vt_s[0, :, :n_sample].reshape(VT_TOTAL, bs, ts).transpose(1, 0, 2)
    omt_f = _attention(q_f.reshape(bp, tp, MLA_PAD), k_f.reshape(bp, tp, MLA_PAD), vt_f, (k_meta, vt_meta),
                       tq=ATTN_TQ, tk=ATTN_TK, causal=True)
    omt_s = _attention(q_s[:n_sample].reshape(bs, ts, MLA_PAD), k_c.reshape(bs, PAST_LEN, MLA_PAD), vt_c,
                       (k_s[:n_sample].reshape(bs, ts, MLA_PAD), vt_new), tq=ts, tk=512)
    omt_m = _attention(jnp.pad(q_s[n_sample:], ((0, LANES - N_META), (0, 0)))[None],
                       jnp.pad(k_s[n_sample:], ((0, CHUNK - N_META), (0, 0)))[None],
                       jnp.pad(vt_meta, ((0, 0), (0, 0), (0, CHUNK - N_META))),
                       None, tq=LANES, tk=CHUNK, nk_valid=N_META)

    og_small = jnp.concatenate([og_s.reshape(n_sample, GDN_Z), og_m[0, :N_META]], 0)
    omt_small = jnp.concatenate([omt_s.transpose(1, 0, 2).reshape(hv, n_sample), omt_m[0, :, :N_META]], -1)[None]
    xf = _ffn(xf, ffn2[0], mix=(og_f.reshape(n_frames, GDN_Z), omt_f, w_out))
    xs = _ffn(xs, ffn2[0], mix=(og_small, omt_small, w_out))

    xf = _ffn(xf, ffn1[1])
    xs = _ffn(xs, ffn1[1])
    n_sc = state_sconv.shape[2]
    xm, sc_m = _sconv(xs[n_sample:][None], jnp.zeros((1, HALO, D_MODEL), F32), sc_w, rows=N_META)
    xf3, p_sc = _sconv(xf.reshape(bp, tp, D_MODEL), sc_m, sc_w, rows=SCONV_ROWS)
    xs3, s_sc = _sconv(xs[:n_sample].reshape(bs, ts, D_MODEL), _tail_rows(state_sconv[0], n_sc), sc_w, rows=ts)
    xf = _ffn(xf3.reshape(n_frames, D_MODEL), ffn2[1])
    xs = _ffn(jnp.concatenate([xs3.reshape(n_sample, D_MODEL), xm[0]], 0), ffn2[1])

    y_prompt = xf.reshape(bp, tp, D_MODEL)
    y_sample = xs[:n_sample].reshape(bs, ts, D_MODEL)
    rope_lo, rope_hi = MLA_NOPE, MLA_NOPE + MLA_ROPE
    meta_ckv = jnp.broadcast_to(ckvn_s[n_sample:][None], (bp, N_META, MLA_KV_RANK))
    meta_kr = jnp.broadcast_to(krn_s[n_sample:, rope_lo:rope_hi][None], (bp, N_META, MLA_ROPE))
    p_ckv = jnp.concatenate([meta_ckv, ckvn_f.reshape(bp, tp, MLA_KV_RANK)], 1)
    p_kr = jnp.concatenate([meta_kr, krn_f[:, rope_lo:rope_hi].reshape(bp, tp, MLA_ROPE)], 1)
    n_gc = state_gdn_conv.shape[2]
    return (y_prompt, y_sample,
            p_ckv[None], p_kr[None], p_s[None], p_conv[:, HALO - n_gc:][None], p_sc[:, HALO - n_sc:][None],
            ckvn_s[:n_sample].reshape(bs, ts, MLA_KV_RANK)[None],
            krn_s[:n_sample, rope_lo:rope_hi].reshape(bs, ts, MLA_ROPE)[None],
            s_s[None], s_conv[:, HALO - n_gc:][None], s_sc[:, HALO - n_sc:][None])
```

```python
import functools

import jax
import jax.numpy as jnp
from jax import lax
from jax.experimental import pallas as pl
from jax.experimental.pallas import tpu as pltpu

F32 = jnp.float32
BF16 = jnp.bfloat16

D_MODEL = 1024
D_FF = 2816
CHUNK = 64
CHUNK_SHIFT = 6
N_META = 16
EPS = 1e-6
GDN_HEADS = 4
GDN_DK = 128
GDN_DV = 128
GDN_QKV = GDN_HEADS * (2 * GDN_DK + GDN_DV)
GDN_Z = GDN_HEADS * GDN_DV
MLA_HEADS = 8
MLA_NOPE = 64
MLA_ROPE = 32
MLA_V = 64
MLA_Q_RANK = 384
MLA_KV_RANK = 256
ROPE_THETA = 10000.0
PAST_LEN = 1024
LANES = 128
HALO = 8
MLA_PAD = MLA_HEADS * LANES
FF_BLOCK = 256
FF_CHUNKS = D_FF // FF_BLOCK
VMEM_LIMIT = 60000 * 1024
NEG_INF = float("-inf")
LOG2_E = 1.4426950408889634
MASK_LANE0 = MLA_NOPE + MLA_ROPE
MASK_BIG = 2.0 ** 100
ATTN_TQ = 512
ATTN_TK = 512
FFN_ROWS = 1024
INPROJ_ROWS = 512
SCONV_ROWS = 1024
GDN_ROWS = 8 * CHUNK
VT_ROWS = MLA_V + 16
VT_TOTAL = MLA_HEADS * VT_ROWS
ONE_PASS_JUMP = 64.0

IN_QKV = 0
IN_Z = IN_QKV + GDN_QKV
IN_CQ = IN_Z + GDN_Z
IN_GATES_ROPE = IN_CQ + MLA_Q_RANK
IN_CKV = IN_GATES_ROPE + LANES
IN_TOTAL = IN_CKV + MLA_KV_RANK


def _params(n_axes):
    return pltpu.CompilerParams(dimension_semantics=("arbitrary",) * n_axes, vmem_limit_bytes=VMEM_LIMIT)


def _rms(x, g):
    return x * lax.rsqrt(jnp.mean(x * x, -1, keepdims=True) + EPS) * g


def _sigmoid(x):
    return 1.0 / (1.0 + jnp.exp(-x))


def _silu(x):
    return x * _sigmoid(x)


def _softplus(x):
    return jnp.maximum(x, 0.0) + jnp.log1p(jnp.exp(-jnp.abs(x)))


def _dot(a, b):
    return jnp.dot(a, b, preferred_element_type=F32)


def _dot_nt(a, b):
    return lax.dot_general(a, b, (((1,), (1,)), ((), ())), preferred_element_type=F32)


def _dot_tn(a, b):
    return lax.dot_general(a, b, (((0,), (0,)), ((), ())), preferred_element_type=F32)


def _shift_rows(x, history, k):
    rolled = pltpu.roll(x, k, 0)
    row = lax.broadcasted_iota(jnp.int32, (HALO, 1), 0)
    head = rolled[:HALO]
    for j in range(k):
        head = jnp.where(row == j, history[HALO - k + j:HALO - k + j + 1, :], head)
    return jnp.concatenate([head, rolled[HALO:]], 0)


def _full_spec(shape):
    zeros = (0,) * len(shape)
    return pl.BlockSpec(shape, lambda *_: zeros)


def _row_tile(rows, want):
    return want if rows % want == 0 else rows


def _ffn_body(x_ref, g_ref, wg_ref, wu_ref, wd_ref, *rest, mixed):
    if mixed:
        og_ref, omt_ref, wmix_ref, o_ref, acc_ref = rest
        x = x_ref[...] + _dot(og_ref[...], wmix_ref[:GDN_Z, :]) + _dot_tn(omt_ref[...], wmix_ref[GDN_Z:, :])
    else:
        o_ref, acc_ref = rest
        x = x_ref[...]
    h = _rms(x, g_ref[...]).astype(BF16)
    for c in range(FF_CHUNKS):
        cols = slice(c * FF_BLOCK, (c + 1) * FF_BLOCK)
        gate = _dot(h, wg_ref[:, cols])
        up = _dot(h, wu_ref[:, cols])
        act = (_silu(gate) * up).astype(BF16)
        down = _dot(act, wd_ref[cols, :])
        if c == 0:
            acc_ref[...] = down
        else:
            acc_ref[...] += down
    o_ref[...] = x + 0.5 * acc_ref[...]


def _ffn(x, w, mix=None):
    rows = x.shape[0]
    tm = _row_tile(rows, FFN_ROWS)
    g, wg, wu, wd, layer = w
    resident = lambda a: pl.BlockSpec(a.shape, lambda i: (0,) * a.ndim, pipeline_mode=pl.Buffered(1))
    of_layer = lambda a: pl.BlockSpec((None,) + a.shape[1:], lambda i: (layer, 0, 0), pipeline_mode=pl.Buffered(1))
    in_specs = [pl.BlockSpec((tm, D_MODEL), lambda i: (i, 0)), _full_spec(g.shape), of_layer(wg), of_layer(wu),
                of_layer(wd)]
    args = [x, g, wg, wu, wd]
    if mix is not None:
        og, omt, w_mix = mix
        tiles_per_batch = rows // omt.shape[0] // tm
        in_specs += [pl.BlockSpec((tm, GDN_Z), lambda i: (i, 0)),
                     pl.BlockSpec((None, GDN_Z, tm), lambda i: (i // tiles_per_batch, 0, i % tiles_per_batch)),
                     resident(w_mix)]
        args += [og, omt, w_mix]
    return pl.pallas_call(
        functools.partial(_ffn_body, mixed=mix is not None),
        grid=(rows // tm,),
        in_specs=in_specs,
        out_specs=pl.BlockSpec((tm, D_MODEL), lambda i: (i, 0)),
        out_shape=jax.ShapeDtypeStruct((rows, D_MODEL), F32),
        scratch_shapes=[pltpu.VMEM((tm, D_MODEL), F32)],
        compiler_params=_params(1),
        name="half_ffn",
    )(*args)


def _to_bf16_body(x_ref, o_ref):
    o_ref[...] = x_ref[...].astype(BF16)


def _to_bf16(a):
    a2 = a.reshape(-1, a.shape[-1])
    rows, cols = a2.shape
    tm = _row_tile(rows, 512)
    out = pl.pallas_call(
        _to_bf16_body,
        grid=(rows // tm,),
        in_specs=[pl.BlockSpec((tm, cols), lambda i: (i, 0))],
        out_specs=pl.BlockSpec((tm, cols), lambda i: (i, 0)),
        out_shape=jax.ShapeDtypeStruct((rows, cols), BF16),
        compiler_params=_params(1),
        name="weights_to_bf16",
    )(a2)
    return out.reshape(a.shape)


def _ffn_weights(norm, w_gate, w_up, w_down):
    stacks = (_to_bf16(w_gate), _to_bf16(w_up), _to_bf16(w_down))
    return [(norm[l].reshape(1, D_MODEL),) + stacks + (l,) for l in range(norm.shape[0])]


def _gdn_conv(x, history, cw):
    y = (_shift_rows(x, history, 3) * cw[0:1] + _shift_rows(x, history, 2) * cw[1:2]
         + _shift_rows(x, history, 1) * cw[2:3] + x * cw[3:4])
    return _silu(y)


class _Split:
    def __init__(self, x):
        self.x = x
        self._parts = None
        self._diag = None

    def parts(self):
        if self._parts is None:
            hi = self.x.astype(BF16)
            self._parts = (hi, (self.x - hi.astype(F32)).astype(BF16))
        return self._parts

    def diag_parts(self):
        if self._diag is None:
            first = lax.broadcasted_iota(jnp.int32, self.x.shape, 1) < CHUNK
            zero = jnp.zeros(self.x.shape, BF16)
            self._diag = tuple(jnp.concatenate([jnp.where(first, p, zero), jnp.where(first, zero, p)], 0)
                               for p in self.parts())
        return self._diag


def _split_products(pairs):
    parts = [(a.parts(), b.diag_parts()) for a, b in pairs]
    outs = []
    for (a_hi, a_lo), (b_hi, b_lo) in parts:
        a2 = jnp.concatenate([a_hi, a_lo], 0)
        r = _dot(a2, b_hi) + _dot(a2, b_lo)
        half = a_hi.shape[0]
        outs.append(_Split(r[:half] + r[half:]))
    return outs


def _unit_lower_inverses(ns, eye_f, m16, m32):
    k = len(ns)
    nd = [_Split(jnp.where(m16, n, 0.0)) for n in ns]
    n2 = _split_products([(x, x) for x in nd])
    lvl = _split_products([(x, x) for x in n2] +
                          [(_Split(eye_f - a.x), _Split(eye_f + b.x)) for a, b in zip(nd, n2)])
    n4, t = lvl[:k], lvl[k:]
    lvl = _split_products([(x, x) for x in n4] + [(a, _Split(eye_f + b.x)) for a, b in zip(t, n4)])
    n8, t = lvl[:k], lvl[k:]
    t = _split_products([(a, _Split(eye_f + b.x)) for a, b in zip(t, n8)])
    for keep in (jnp.logical_and(m32, jnp.logical_not(m16)), jnp.logical_not(m32)):
        off = [_Split(jnp.where(keep, n, 0.0)) for n in ns]
        x = _split_products(list(zip(off, t)))
        y = _split_products(list(zip(t, x)))
        t = [_Split(a.x - b.x) for a, b in zip(t, y)]
    return [a.x for a in t]


def _gdn_body(qkv_ref, z_ref, ab_ref, s0_ref, cprev_ref, cw_ref, alog_ref, dtb_ref, on_ref,
              og_ref, sout_ref, cout_ref, s_scr, halo_scr, *, rows, n_valid):
    t = pl.program_id(1)
    C = CHUNK
    row = lax.broadcasted_iota(jnp.int32, (rows, 1), 0)

    @pl.when(t == 0)
    def _():
        s_scr[...] = s0_ref[...]
        halo_scr[...] = cprev_ref[...]

    x = qkv_ref[...]
    y = _gdn_conv(x, halo_scr[...], cw_ref[...])
    last_rows = x[n_valid - HALO:n_valid, :]
    halo_scr[...] = last_rows

    ab = ab_ref[...]
    g_all = -jnp.exp(alog_ref[...]) * _softplus(ab + dtb_ref[...])
    b_all = _sigmoid(ab)
    if n_valid < rows:
        valid = row < n_valid
        y = jnp.where(valid, y, 0.0)
        g_all = jnp.where(valid, g_all, 0.0)
        b_all = jnp.where(valid, b_all, 0.0)

    ii = lax.broadcasted_iota(jnp.int32, (C, LANES), 0)
    pair_lane = lax.broadcasted_iota(jnp.int32, (C, LANES), 1)
    jj = pair_lane & (C - 1)
    first_head = pair_lane < C
    causal = ii >= jj
    strict = ii > jj
    eye_f = (ii == jj).astype(F32)
    m16 = (ii >> 4) == (jj >> 4)
    m32 = (ii >> 5) == (jj >> 5)
    on = on_ref[...]
    n_chunks = rows // C
    heads = range(GDN_HEADS)

    ri = lax.broadcasted_iota(jnp.int32, (rows, rows), 0)
    rj = lax.broadcasted_iota(jnp.int32, (rows, rows), 1)
    tril_blk = jnp.logical_and(ri >= rj, (ri >> CHUNK_SHIFT) == (rj >> CHUNK_SHIFT)).astype(BF16)
    g_hi = g_all.astype(BF16)
    g_r1 = g_all - g_hi.astype(F32)
    g_mid = g_r1.astype(BF16)
    g_lo = (g_r1 - g_mid.astype(F32)).astype(BF16)
    gcum = _dot(tril_blk, g_hi) + _dot(tril_blk, g_mid) + _dot(tril_blk, g_lo)
    egcum = jnp.exp(gcum)

    qn, kn, vn = [], [], []
    for h in heads:
        qh = y[:, h * GDN_DK:(h + 1) * GDN_DK]
        kh = y[:, GDN_HEADS * GDN_DK + h * GDN_DK:GDN_HEADS * GDN_DK + (h + 1) * GDN_DK]
        qn.append(qh * lax.rsqrt(jnp.sum(qh * qh, -1, keepdims=True) + EPS) * (GDN_DK ** -0.5))
        kn.append(kh * lax.rsqrt(jnp.sum(kh * kh, -1, keepdims=True) + EPS))
        vn.append(y[:, 2 * GDN_HEADS * GDN_DK + h * GDN_DV:2 * GDN_HEADS * GDN_DK + (h + 1) * GDN_DV])

    head_pairs = [(a, a + 1) for a in range(0, GDN_HEADS, 2)]
    chains = [(c, pair) for c in range(n_chunks) for pair in head_pairs]
    pre = {}
    n_mats = []
    for c in range(n_chunks):
        sl = slice(c * C, (c + 1) * C)
        gcum_t = gcum[sl].T
        for a, b in head_pairs:
            col = lambda arr, h: arr[sl, h:h + 1]
            gcol = jnp.where(first_head, col(gcum, a), col(gcum, b))
            beta2 = jnp.where(first_head, col(b_all, GDN_HEADS + a), col(b_all, GDN_HEADS + b))
            grow = jnp.concatenate([gcum_t[a:a + 1, :], gcum_t[b:b + 1, :]], -1)
            dec_causal = jnp.exp(jnp.where(causal, gcol - grow, NEG_INF))
            kb = {h: kn[h][sl].astype(BF16) for h in (a, b)}
            kk = jnp.concatenate([_dot_nt(kb[a], kb[a]), _dot_nt(kb[b], kb[b])], -1)
            n_mats.append(beta2 * kk * jnp.where(strict, dec_causal, 0.0))
            qk = jnp.concatenate([_dot_nt(qn[h][sl].astype(BF16), kb[h]) for h in (a, b)], -1)
            qk = (qk * dec_causal).astype(BF16)
            for h, lanes in ((a, slice(0, C)), (b, slice(C, 2 * C))):
                beta = col(b_all, GDN_HEADS + h)
                eg = col(egcum, h)
                g_last = gcum[(c + 1) * C - 1:(c + 1) * C, h:h + 1]
                qh, kh, vh = qn[h][sl], kn[h][sl], vn[h][sl]
                pre[c, h] = dict(
                    qk=qk[:, lanes],
                    rhs=jnp.concatenate([(beta * vh).astype(BF16), ((beta * eg) * kh).astype(BF16)], -1),
                    qd=(qh * eg).astype(BF16),
                    kt=(kh * jnp.exp(g_last - col(gcum, h))).astype(BF16),
                    decay=jnp.exp(g_last))
    for (c, (a, b)), tinv in zip(chains, _unit_lower_inverses(n_mats, eye_f, m16, m32)):
        tinv = tinv.astype(BF16)
        for h, lanes in ((a, slice(0, C)), (b, slice(C, 2 * C))):
            uw = _dot(tinv[:, lanes], pre[c, h]["rhs"])
            pre[c, h]["u"] = uw[:, :GDN_DV]
            pre[c, h]["wk"] = uw[:, GDN_DV:].astype(BF16)

    state = [s_scr[h] for h in heads]
    for c in range(n_chunks):
        sl = slice(c * C, (c + 1) * C)
        sb = [s.astype(BF16) for s in state]
        w = [(pre[c, h]["u"] - _dot(pre[c, h]["wk"], sb[h])).astype(BF16) for h in heads]
        o = [_dot(pre[c, h]["qd"], sb[h]) + _dot(pre[c, h]["qk"], w[h]) for h in heads]
        state = [state[h] * pre[c, h]["decay"] + _dot_tn(pre[c, h]["kt"], w[h]) for h in heads]
        for h in heads:
            gated = _rms(o[h], on) * _silu(z_ref[sl, h * GDN_DV:(h + 1) * GDN_DV])
            og_ref[sl, h * GDN_DV:(h + 1) * GDN_DV] = gated.astype(BF16)
    for h in heads:
        s_scr[h] = state[h]

    @pl.when(t == pl.num_programs(1) - 1)
    def _():
        sout_ref[...] = s_scr[...]
        cout_ref[...] = last_rows


def _gdn(qkv, z, ab, s0, cprev, w, *, rows, n_valid=None):
    b, t_len, _ = qkv.shape
    n_valid = rows if n_valid is None else n_valid
    conv_w, a_log, dt_bias, o_norm = w
    shared = s0.shape[0] == 1 and b > 1
    bsel = (lambda bi: 0) if shared else (lambda bi: bi)
    return pl.pallas_call(
        functools.partial(_gdn_body, rows=rows, n_valid=n_valid),
        grid=(b, t_len // rows),
        in_specs=[
            pl.BlockSpec((None, rows, GDN_QKV), lambda bi, ti: (bi, ti, 0)),
            pl.BlockSpec((None, rows, GDN_Z), lambda bi, ti: (bi, ti, 0)),
            pl.BlockSpec((None, rows, LANES), lambda bi, ti: (bi, ti, 0)),
            pl.BlockSpec((None, GDN_HEADS, GDN_DK, GDN_DV), lambda bi, ti: (bsel(bi), 0, 0, 0)),
            pl.BlockSpec((None, HALO, GDN_QKV), lambda bi, ti: (bsel(bi), 0, 0)),
            _full_spec(conv_w.shape), _full_spec(a_log.shape), _full_spec(dt_bias.shape), _full_spec(o_norm.shape),
        ],
        out_specs=[
            pl.BlockSpec((None, rows, GDN_Z), lambda bi, ti: (bi, ti, 0)),
            pl.BlockSpec((None, GDN_HEADS, GDN_DK, GDN_DV), lambda bi, ti: (bi, 0, 0, 0)),
            pl.BlockSpec((None, HALO, GDN_QKV), lambda bi, ti: (bi, 0, 0)),
        ],
        out_shape=[
            jax.ShapeDtypeStruct((b, t_len, GDN_Z), BF16),
            jax.ShapeDtypeStruct((b, GDN_HEADS, GDN_DK, GDN_DV), F32),
            jax.ShapeDtypeStruct((b, HALO, GDN_QKV), F32),
        ],
        scratch_shapes=[pltpu.VMEM((GDN_HEADS, GDN_DK, GDN_DV), F32), pltpu.VMEM((HALO, GDN_QKV), F32)],
        compiler_params=_params(2),
        name="gdn_mixer",
    )(qkv, z, ab, s0, cprev, conv_w, a_log, dt_bias, o_norm)


def _lane_masks():
    lane = lax.broadcasted_iota(jnp.int32, (1, LANES), 1)
    nope = lane < MLA_NOPE
    rope_a = jnp.logical_and(lane >= MLA_NOPE, lane < MLA_NOPE + MLA_ROPE // 2)
    rope_b = jnp.logical_and(lane >= MLA_NOPE + MLA_ROPE // 2, lane < MLA_NOPE + MLA_ROPE)
    return nope, rope_a, rope_b


def _rope_block(x, cos, sin, rope_a, rope_b):
    half = MLA_ROPE // 2
    rot = jnp.where(rope_a, -pltpu.roll(x, LANES - half, 1), jnp.where(rope_b, pltpu.roll(x, half, 1), 0.0))
    return x * cos + rot * sin


PAIR = 2 * LANES


def _twice(a):
    return jnp.concatenate([a, a], -1)


def _k_heads(k, shared, gk, k_out):
    for h in range(MLA_HEADS):
        x = k[:, h * LANES:(h + 1) * LANES]
        inv = lax.rsqrt(jnp.sum(x * x, -1, keepdims=True) / MLA_NOPE + EPS)
        k_out[:, h * LANES:(h + 1) * LANES] = (x * inv * gk + shared).astype(BF16)


def _inproj_body(x_ref, g_ref, w_ref, cos_ref, sin_ref, qn_ref, wuq_ref, kvn_ref, wk_ref, wvt_ref, vones_ref,
                 gq_ref, gk_ref, gmean_ref, rot_ref, qkv_ref, z_ref, ab_ref, q_out, k_out, vt_out, ckv_out, kr_out,
                 *, tiles_per_batch, key_tile_chunks):
    hn = _rms(x_ref[...], g_ref[...]).astype(BF16)
    qkv_ref[...] = _dot(hn, w_ref[:, IN_QKV:IN_Z])
    z_ref[...] = _dot(hn, w_ref[:, IN_Z:IN_CQ])
    cq_gates_rope = _dot(hn, w_ref[:, IN_CQ:IN_CKV])
    cq = cq_gates_rope[:, :MLA_Q_RANK]
    gates_rope = cq_gates_rope[:, MLA_Q_RANK:]
    ab_ref[...] = gates_rope
    ckv = _dot(hn, w_ref[:, IN_CKV:IN_TOTAL])

    nope, rope_a, rope_b = _lane_masks()
    krp = jnp.where(jnp.logical_or(rope_a, rope_b), gates_rope, 0.0)
    cos = cos_ref[...]
    sin = sin_ref[...]
    gk = gk_ref[...]
    gmean = gmean_ref[...]
    rot_mat = rot_ref[...]
    gq2 = _twice(gq_ref[...] * ((MLA_NOPE + MLA_ROPE) ** -0.5 * LOG2_E))
    cos2, sin2 = _twice(cos), _twice(sin)

    cqn = _rms(cq, qn_ref[...]).astype(BF16)
    q = _dot(cqn, wuq_ref[...])
    for p in range(MLA_HEADS // 2):
        x = q[:, p * PAIR:(p + 1) * PAIR]
        ms = _dot((x * x).astype(BF16), gmean)
        x = x * lax.rsqrt(ms + EPS) * gq2
        q_out[:, p * PAIR:(p + 1) * PAIR] = (x * cos2 + _dot(x.astype(BF16), rot_mat) * sin2).astype(BF16)

    ckvn = _rms(ckv, kvn_ref[...])
    ckv_out[...] = ckvn
    cb = ckvn.astype(BF16)
    krn = krp * lax.rsqrt(jnp.sum(krp * krp, -1, keepdims=True) / MLA_ROPE + EPS) * jnp.where(nope, 0.0, gk)
    kr_new = _rope_block(krn, cos, sin, rope_a, rope_b)
    kr_out[...] = kr_new
    vt_out[...] = (_dot_nt(wvt_ref[...], cb) + vones_ref[...]).astype(BF16)
    shared_lanes = kr_new
    if key_tile_chunks is not None:
        tm = kr_new.shape[0]
        pos = (pl.program_id(0) % tiles_per_batch) * tm + lax.broadcasted_iota(jnp.int32, (tm, 1), 0)
        lane = lax.broadcasted_iota(jnp.int32, (1, LANES), 1)
        chunk = (pos >> CHUNK_SHIFT) & (key_tile_chunks - 1)
        shared_lanes = kr_new + (lane - MASK_LANE0 == chunk).astype(F32)
    _k_heads(_dot(cb, wk_ref[...]), shared_lanes, jnp.where(nope, gk, 0.0), k_out)


def _vt_spec(tm, tiles_per_batch):
    return pl.BlockSpec((None, VT_TOTAL, tm), lambda i: (i // tiles_per_batch, 0, i % tiles_per_batch))


def _inproj(x, g, w_in, cos, sin, w, *, batch, key_tile=None):
    rows = x.shape[0]
    tm = _row_tile(rows, INPROJ_ROWS)
    tiles_per_batch = rows // batch // tm
    key_tile_chunks = None if key_tile is None else key_tile // CHUNK
    assert key_tile_chunks is None or (key_tile_chunks & (key_tile_chunks - 1) == 0
                                       and key_tile_chunks <= LANES - MASK_LANE0)
    q_norm, w_uq, kv_norm, w_k, w_vt, v_ones, gq, gk, gmean, rot_mat = w
    row_spec = lambda n: pl.BlockSpec((tm, n), lambda i: (i, 0))
    tab_spec = pl.BlockSpec((tm, LANES), lambda i: (i % tiles_per_batch, 0))
    consts = [g, w_in, q_norm, w_uq, kv_norm, w_k, w_vt, v_ones, gq, gk, gmean, rot_mat]
    return pl.pallas_call(
        functools.partial(_inproj_body, tiles_per_batch=tiles_per_batch, key_tile_chunks=key_tile_chunks),
        grid=(rows // tm,),
        in_specs=[row_spec(D_MODEL), _full_spec(g.shape), _full_spec(w_in.shape), tab_spec, tab_spec]
        + [_full_spec(a.shape) for a in consts[2:]],
        out_specs=[row_spec(GDN_QKV), row_spec(GDN_Z), row_spec(LANES), row_spec(MLA_PAD), row_spec(MLA_PAD),
                   _vt_spec(tm, tiles_per_batch), row_spec(MLA_KV_RANK), row_spec(LANES)],
        out_shape=[jax.ShapeDtypeStruct((rows, GDN_QKV), F32), jax.ShapeDtypeStruct((rows, GDN_Z), F32),
                   jax.ShapeDtypeStruct((rows, LANES), F32),
                   jax.ShapeDtypeStruct((rows, MLA_PAD), BF16), jax.ShapeDtypeStruct((rows, MLA_PAD), BF16),
                   jax.ShapeDtypeStruct((batch, VT_TOTAL, rows // batch), BF16),
                   jax.ShapeDtypeStruct((rows, MLA_KV_RANK), F32), jax.ShapeDtypeStruct((rows, LANES), F32)],
        compiler_params=_params(1),
        name="mixer_inproj",
    )(x, g, w_in, cos, sin, *consts[2:])


def _kv_up_body(ckv_ref, kr_ref, wk_ref, wvt_ref, vones_ref, gk_ref, k_out, vt_out):
    nope, _, _ = _lane_masks()
    cb = ckv_ref[...].astype(BF16)
    vt_out[...] = (_dot_nt(wvt_ref[...], cb) + vones_ref[...]).astype(BF16)
    _k_heads(_dot(cb, wk_ref[...]), kr_ref[...], jnp.where(nope, gk_ref[...], 0.0), k_out)


def _kv_up(ckv, krp, w_k, w_vt, v_ones, gk, *, batch):
    rows = ckv.shape[0]
    tm = _row_tile(rows, 512)
    tiles_per_batch = rows // batch // tm
    row_spec = lambda n: pl.BlockSpec((tm, n), lambda i: (i, 0))
    return pl.pallas_call(
        _kv_up_body,
        grid=(rows // tm,),
        in_specs=[row_spec(MLA_KV_RANK), row_spec(LANES), _full_spec(w_k.shape), _full_spec(w_vt.shape),
                  _full_spec(v_ones.shape), _full_spec(gk.shape)],
        out_specs=[row_spec(MLA_PAD), _vt_spec(tm, tiles_per_batch)],
        out_shape=[jax.ShapeDtypeStruct((rows, MLA_PAD), BF16),
                   jax.ShapeDtypeStruct((batch, VT_TOTAL, rows // batch), BF16)],
        compiler_params=_params(1),
        name="mla_kv_up",
    )(ckv, krp, w_k, w_vt, v_ones, gk)


def _attn_body(qi_ref, kj_ref, *refs, tq, tk, n_key_tiles, causal, nk_valid, has_ext):
    if has_ext:
        q_ref, k_ref, vt_ref, ke_ref, vte_ref, o_ref, m_scr, acc_scr, slot_ref = refs
    else:
        q_ref, k_ref, vt_ref, o_ref, m_scr, acc_scr, slot_ref = refs
    step = pl.program_id(1)
    i = qi_ref[step]
    j = kj_ref[step]
    last_j = ((i + 1) * tq + tk - 1) // tk - 1 if causal else n_key_tiles - 1
    head_lanes = [slice(h * LANES, (h + 1) * LANES) for h in range(MLA_HEADS)]

    def head_scores(k_blk, q_mask):
        if q_mask is None:
            q_heads = [q_ref[:, hs] for hs in head_lanes]
        else:
            q_heads = [q_ref[:, hs] + q_mask for hs in head_lanes]
        return [_dot_nt(k_blk[:, hs], qh) for hs, qh in zip(head_lanes, q_heads)]

    def values(vt_blk, h):
        return vt_blk[h * VT_ROWS:(h + 1) * VT_ROWS, :]

    def two_pass(k_blk, vt_blk, cur, bias=None, q_mask=None):
        scores = head_scores(k_blk, q_mask)
        for h in range(MLA_HEADS):
            s = scores[h] if bias is None else scores[h] + bias
            m_old = m_scr[cur, h][0:1]
            m_new = jnp.maximum(m_old, jnp.max(s, 0, keepdims=True))
            alpha = jnp.exp2(m_old - m_new)
            p = jnp.exp2(s - m_new)
            m_scr[cur, h] = jnp.broadcast_to(m_new, (HALO, tq))
            acc_scr[cur, h] = acc_scr[cur, h] * alpha + _dot(values(vt_blk, h), p.astype(BF16))

    def one_pass(k_blk, vt_blk, cur, q_mask=None):
        def score(h):
            qh = q_ref[:, head_lanes[h]]
            return _dot_nt(k_blk[:, head_lanes[h]], qh if q_mask is None else qh + q_mask)

        rise = None
        ahead = score(0)
        for h in range(MLA_HEADS):
            s = ahead
            if h + 1 < MLA_HEADS:
                ahead = score(h + 1)
            m_old = m_scr[cur, h][0:1]
            p = jnp.exp2(s - m_old)
            m_tile = jnp.max(s, 0, keepdims=True)
            m_new = jnp.maximum(m_old, m_tile)
            rise = m_tile - m_old if rise is None else jnp.maximum(rise, m_tile - m_old)
            m_scr[1 - cur, h] = jnp.broadcast_to(m_new, (HALO, tq))
            acc_scr[1 - cur, h] = (acc_scr[cur, h] + _dot(values(vt_blk, h), p.astype(BF16))) * jnp.exp2(m_old - m_new)
        return jnp.max(rise)

    def guarded(k_blk, vt_blk, q_mask=None):
        cur = slot_ref[0]
        rise = one_pass(k_blk, vt_blk, cur, q_mask)

        @pl.when(rise <= ONE_PASS_JUMP)
        def _():
            slot_ref[0] = 1 - cur

        @pl.when(jnp.logical_not(rise <= ONE_PASS_JUMP))
        def _():
            two_pass(k_blk, vt_blk, cur, q_mask=q_mask)

    @pl.when(j == 0)
    def _():
        slot_ref[0] = 0
        m_scr[0] = jnp.full(m_scr.shape[1:], NEG_INF, F32)
        acc_scr[0] = jnp.zeros(acc_scr.shape[1:], F32)
        if has_ext:
            two_pass(ke_ref, vte_ref, 0)

    if nk_valid is not None:
        kpos = j * tk + lax.broadcasted_iota(jnp.int32, (tk, 1), 0)
        two_pass(k_ref, vt_ref, slot_ref[0], bias=jnp.where(kpos < nk_valid, 0.0, NEG_INF))
    elif not has_ext:
        two_pass(k_ref, vt_ref, slot_ref[0])
    elif causal:
        @pl.when(j < last_j)
        def _():
            guarded(k_ref, vt_ref)

        @pl.when(j == last_j)
        def _():
            lane = lax.broadcasted_iota(jnp.int32, (1, LANES), 1)
            q_chunk = (i * tq + lax.broadcasted_iota(jnp.int32, (tq, 1), 0)) >> CHUNK_SHIFT
            key_chunk = j * (tk // CHUNK) + lane - MASK_LANE0
            hidden = jnp.logical_and(jnp.logical_and(lane >= MASK_LANE0, lane < MASK_LANE0 + tk // CHUNK),
                                     key_chunk > q_chunk)
            guarded(k_ref, vt_ref, q_mask=jnp.where(hidden, -MASK_BIG, 0.0).astype(BF16))
    else:
        guarded(k_ref, vt_ref)

    @pl.when(j == last_j)
    def _():
        cur = slot_ref[0]
        for h in range(MLA_HEADS):
            acc = acc_scr[cur, h]
            o_ref[h * MLA_V:(h + 1) * MLA_V, :] = (acc[:MLA_V] / acc[MLA_V:MLA_V + 1]).astype(BF16)


def _attention(q, k, vt, ext, *, tq, tk, causal=False, nk_valid=None):
    b, nq, _ = q.shape
    nk = k.shape[1]
    n_key_tiles = nk // tk
    assert not causal or tk % tq == 0
    pairs = [(i, j) for i in range(nq // tq) for j in range(n_key_tiles) if not causal or j * tk < (i + 1) * tq]
    qi = jnp.asarray([p[0] for p in pairs], jnp.int32)
    kj = jnp.asarray([p[1] for p in pairs], jnp.int32)
    hv = MLA_HEADS * MLA_V
    in_specs = [
        pl.BlockSpec((None, tq, MLA_PAD), lambda bi, s, qi, kj: (bi, qi[s], 0)),
        pl.BlockSpec((None, tk, MLA_PAD), lambda bi, s, qi, kj: (bi, kj[s], 0)),
        pl.BlockSpec((None, VT_TOTAL, tk), lambda bi, s, qi, kj: (bi, 0, kj[s])),
    ]
    args = [q, k, vt]
    if ext is not None:
        ne = ext[0].shape[1]
        shared = ext[0].shape[0] == 1 and b > 1
        emap = (lambda bi, s, qi, kj: (0, 0, 0)) if shared else (lambda bi, s, qi, kj: (bi, 0, 0))
        in_specs += [pl.BlockSpec((None, ne, MLA_PAD), emap), pl.BlockSpec((None, VT_TOTAL, ne), emap)]
        args += list(ext)
    return pl.pallas_call(
        functools.partial(_attn_body, tq=tq, tk=tk, n_key_tiles=n_key_tiles, causal=causal, nk_valid=nk_valid,
                          has_ext=ext is not None),
        grid_spec=pltpu.PrefetchScalarGridSpec(
            num_scalar_prefetch=2,
            grid=(b, len(pairs)),
            in_specs=in_specs,
            out_specs=pl.BlockSpec((None, hv, tq), lambda bi, s, qi, kj: (bi, 0, qi[s])),
            scratch_shapes=[pltpu.VMEM((2, MLA_HEADS, HALO, tq), F32), pltpu.VMEM((2, MLA_HEADS, VT_ROWS, tq), F32),
                            pltpu.SMEM((1,), jnp.int32)]),
        out_shape=jax.ShapeDtypeStruct((b, hv, nq), BF16),
        compiler_params=_params(2),
        name="mla_attention",
    )(qi, kj, *args)


def _sconv_body(x_ref, g_ref, win_ref, cw_ref, wout_ref, prev_ref, o_ref, pout_ref, halo_scr, *, rows):
    t = pl.program_id(1)

    @pl.when(t == 0)
    def _():
        halo_scr[...] = prev_ref[...]

    x = x_ref[...]
    hn = _rms(x, g_ref[...]).astype(BF16)
    bg = _dot(hn, win_ref[:, :D_MODEL])
    u = _dot(hn, win_ref[:, D_MODEL:2 * D_MODEL]) * _dot(hn, win_ref[:, 2 * D_MODEL:])
    halo = halo_scr[...]
    u1 = _shift_rows(u, halo, 1)
    u2 = _shift_rows(u, halo, 2)
    cw = cw_ref[...]
    y = u2 * cw[0:1] + u1 * cw[1:2] + u * cw[2:3]
    last_rows = u[rows - HALO:rows, :]
    halo_scr[...] = last_rows
    o_ref[...] = x + _dot((bg * y).astype(BF16), wout_ref[...])

    @pl.when(t == pl.num_programs(1) - 1)
    def _():
        pout_ref[...] = last_rows


def _sconv(x, prev, w, *, rows):
    b, t_len, _ = x.shape
    g, w_in, conv_w, w_out = w
    shared = prev.shape[0] == 1 and b > 1
    bsel = (lambda bi: 0) if shared else (lambda bi: bi)
    return pl.pallas_call(
        functools.partial(_sconv_body, rows=rows),
        grid=(b, t_len // rows),
        in_specs=[pl.BlockSpec((None, rows, D_MODEL), lambda bi, ti: (bi, ti, 0)), _full_spec(g.shape),
                  _full_spec(w_in.shape), _full_spec(conv_w.shape), _full_spec(w_out.shape),
                  pl.BlockSpec((None, HALO, D_MODEL), lambda bi, ti: (bsel(bi), 0, 0))],
        out_specs=[pl.BlockSpec((None, rows, D_MODEL), lambda bi, ti: (bi, ti, 0)),
                   pl.BlockSpec((None, HALO, D_MODEL), lambda bi, ti: (bi, 0, 0))],
        out_shape=[jax.ShapeDtypeStruct((b, t_len, D_MODEL), F32), jax.ShapeDtypeStruct((b, HALO, D_MODEL), F32)],
        scratch_shapes=[pltpu.VMEM((HALO, D_MODEL), F32)],
        compiler_params=_params(2),
        name="sconv_mixer",
    )(x, g, w_in, conv_w, w_out, prev)


def _pad_lanes(a, before, total):
    return jnp.pad(a, [(0, 0)] * (a.ndim - 1) + [(before, total - before - a.shape[-1])])


def _rope_tables(pos):
    half = MLA_ROPE // 2
    inv = ROPE_THETA ** (-jnp.arange(half, dtype=F32) / half)
    ang = pos.astype(F32)[:, None] * inv
    cos, sin = jnp.cos(ang), jnp.sin(ang)
    n = pos.shape[0]
    tail = LANES - MLA_NOPE - MLA_ROPE
    cos_l = jnp.concatenate([jnp.ones((n, MLA_NOPE), F32), cos, cos, jnp.ones((n, tail), F32)], -1)
    sin_l = jnp.concatenate([jnp.zeros((n, MLA_NOPE), F32), sin, sin, jnp.zeros((n, tail), F32)], -1)
    return cos_l, sin_l


def _head_pair_matrices():
    i = jnp.arange(PAIR)[:, None]
    j = jnp.arange(PAIR)[None, :]
    same = (i // LANES) == (j // LANES)
    li, lj = i % LANES, j % LANES
    half = MLA_ROPE // 2
    rope_end = MLA_NOPE + MLA_ROPE
    in_rope = lambda l: (l >= MLA_NOPE) & (l < rope_end)
    gmean = jnp.where(same & (li < MLA_NOPE) & (lj < MLA_NOPE), 1.0 / MLA_NOPE,
                      jnp.where(same & in_rope(li) & in_rope(lj), 1.0 / MLA_ROPE, 0.0))
    first, second = in_rope(lj) & (lj < MLA_NOPE + half), in_rope(lj) & (lj >= MLA_NOPE + half)
    rot = jnp.where(same & first & (li == lj + half), -1.0, jnp.where(same & second & (li == lj - half), 1.0, 0.0))
    return gmean.astype(BF16), rot.astype(BF16)


def _tail_rows(a, n):
    return jnp.pad(a, ((0, 0), (HALO - n, 0), (0, 0)))


def kernel(x_prompt, x_sample, cache_mla_ckv, cache_mla_krope, state_gdn_S, state_gdn_conv, state_sconv, meta_tokens, ffn1_norm, ffn1_w_gate, ffn1_w_up, ffn1_w_down, ffn2_norm, ffn2_w_gate, ffn2_w_up, ffn2_w_down, mix_norm, ab_w_in, ab_w_out, gdn_conv_w, gdn_A_log, gdn_dt_bias, gdn_o_norm, mla_q_norm, mla_w_uq, mla_kv_norm, mla_w_ukv, mla_qn_norm, mla_qr_norm, mla_kn_norm, mla_kr_norm, sc_w_in, sc_conv_w, sc_w_out):
    bp, tp, _ = x_prompt.shape
    bs, ts, _ = x_sample.shape
    n_frames = bp * tp
    n_sample = bs * ts

    ffn1 = _ffn_weights(ffn1_norm, ffn1_w_gate, ffn1_w_up, ffn1_w_down)
    ffn2 = _ffn_weights(ffn2_norm, ffn2_w_gate, ffn2_w_up, ffn2_w_down)
    w_in = ab_w_in[0]
    s1 = GDN_QKV + GDN_Z
    c0 = s1 + 2 * GDN_HEADS
    c1 = c0 + MLA_Q_RANK
    c2 = c1 + MLA_KV_RANK
    gates_rope = jnp.concatenate([
        w_in[:, s1:c0], jnp.zeros((D_MODEL, MLA_NOPE - 2 * GDN_HEADS), F32), w_in[:, c2:],
        jnp.zeros((D_MODEL, LANES - MLA_NOPE - MLA_ROPE), F32)], -1)
    w_in_packed = jnp.concatenate([w_in[:, :s1], w_in[:, c0:c1], gates_rope, w_in[:, c1:c2]], -1).astype(BF16)
    mix_g = mix_norm.reshape(2, 1, D_MODEL)
    gdn_w = (gdn_conv_w[0], _pad_lanes(gdn_A_log[0][None], 0, LANES), _pad_lanes(gdn_dt_bias[0][None], 0, LANES),
             gdn_o_norm[0][None])
    qdim = MLA_NOPE + MLA_ROPE
    w_uq = _pad_lanes(mla_w_uq[0].reshape(MLA_Q_RANK, MLA_HEADS, qdim), 0, LANES).reshape(MLA_Q_RANK, MLA_PAD)
    w_ukv = mla_w_ukv[0].reshape(MLA_KV_RANK, MLA_HEADS, MLA_NOPE + MLA_V)
    w_k = _pad_lanes(w_ukv[..., :MLA_NOPE], 0, LANES).reshape(MLA_KV_RANK, MLA_PAD)
    hv = MLA_HEADS * MLA_V
    w_vt = _pad_lanes(w_ukv[..., MLA_NOPE:], 0, VT_ROWS).reshape(MLA_KV_RANK, VT_TOTAL).T
    v_ones = (jnp.arange(VT_TOTAL) % VT_ROWS == MLA_V).astype(F32)[:, None]
    gq = _pad_lanes(jnp.concatenate([mla_qn_norm[0], mla_qr_norm[0]])[None], 0, LANES)
    gk = _pad_lanes(jnp.concatenate([mla_kn_norm[0], mla_kr_norm[0]])[None], 0, LANES)
    gmean, rot_mat = _head_pair_matrices()
    mla_w = (mla_q_norm[0][None], w_uq.astype(BF16), mla_kv_norm[0][None], w_k.astype(BF16), w_vt.astype(BF16), v_ones,
             gq, gk, gmean, rot_mat)
    w_out = ab_w_out[0].astype(BF16)
    sc_w = (mix_g[1], sc_w_in[0].astype(BF16), sc_conv_w[0], sc_w_out[0].astype(BF16))

    xf = x_prompt.reshape(n_frames, D_MODEL)
    xs = jnp.concatenate([x_sample.reshape(n_sample, D_MODEL), meta_tokens.astype(F32)], 0)
    cos_f, sin_f = _rope_tables(N_META + jnp.arange(tp))
    pos_small = jnp.concatenate([jnp.tile(PAST_LEN + jnp.arange(ts), bs), jnp.arange(N_META)])
    cos_s, sin_s = _rope_tables(pos_small)

    xf = _ffn(xf, ffn1[0])
    xs = _ffn(xs, ffn1[0])
    qkv_f, z_f, ab_f, q_f, k_f, vt_f, ckvn_f, krn_f = _inproj(xf, mix_g[0], w_in_packed, cos_f, sin_f, mla_w,
                                                               batch=bp, key_tile=ATTN_TK)
    qkv_s, z_s, ab_s, q_s, k_s, vt_s, ckvn_s, krn_s = _inproj(xs, mix_g[0], w_in_packed, cos_s, sin_s, mla_w, batch=1)

    pad_meta = lambda a: jnp.pad(a[n_sample:], ((0, CHUNK - N_META), (0, 0)))[None]
    og_m, s_m, conv_m = _gdn(pad_meta(qkv_s), pad_meta(z_s), pad_meta(ab_s),
                             jnp.zeros((1, GDN_HEADS, GDN_DK, GDN_DV), F32), jnp.zeros((1, HALO, GDN_QKV), F32),
                             gdn_w, rows=CHUNK, n_valid=N_META)
    og_f, p_s, p_conv = _gdn(qkv_f.reshape(bp, tp, GDN_QKV), z_f.reshape(bp, tp, GDN_Z), ab_f.reshape(bp, tp, LANES),
                             s_m, conv_m, gdn_w, rows=GDN_ROWS)
    og_s, s_s, s_conv = _gdn(qkv_s[:n_sample].reshape(bs, ts, GDN_QKV), z_s[:n_sample].reshape(bs, ts, GDN_Z),
                             ab_s[:n_sample].reshape(bs, ts, LANES), state_gdn_S[0],
                             _tail_rows(state_gdn_conv[0], state_gdn_conv.shape[2]), gdn_w, rows=ts)

    k_c, vt_c = _kv_up(cache_mla_ckv[0].reshape(bs * PAST_LEN, MLA_KV_RANK),
                       _pad_lanes(cache_mla_krope[0].reshape(bs * PAST_LEN, MLA_ROPE), MLA_NOPE, LANES),
                       mla_w[3], mla_w[4], v_ones, gk, batch=bs)
    k_meta, vt_meta = k_s[n_sample:][None], vt_s[:, :, n_sample:]
    vt_new = vt_s[0, :, :n_sample].reshape(VT_TOTAL, bs, ts).transpose(1, 0, 2)
    omt_f = _attention(q_f.reshape(bp, tp, MLA_PAD), k_f.reshape(bp, tp, MLA_PAD), vt_f, (k_meta, vt_meta),
                       tq=ATTN_TQ, tk=ATTN_TK, causal=True)
    omt_s = _attention(q_s[:n_sample].reshape(bs, ts, MLA_PAD), k_c.reshape(bs, PAST_LEN, MLA_PAD), vt_c,
                       (k_s[:n_sample].reshape(bs, ts, MLA_PAD), vt_new), tq=ts, tk=512)
    omt_m = _attention(jnp.pad(q_s[n_sample:], ((0, LANES - N_META), (0, 0)))[None],
                       jnp.pad(k_s[n_sample:], ((0, CHUNK - N_META), (0, 0)))[None],
                       jnp.pad(vt_meta, ((0, 0), (0, 0), (0, CHUNK - N_META))),
                       None, tq=LANES, tk=CHUNK, nk_valid=N_META)

    og_small = jnp.concatenate([og_s.reshape(n_sample, GDN_Z), og_m[0, :N_META]], 0)
    omt_small = jnp.concatenate([omt_s.transpose(1, 0, 2).reshape(hv, n_sample), omt_m[0, :, :N_META]], -1)[None]
    xf = _ffn(xf, ffn2[0], mix=(og_f.reshape(n_frames, GDN_Z), omt_f, w_out))
    xs = _ffn(xs, ffn2[0], mix=(og_small, omt_small, w_out))

    xf = _ffn(xf, ffn1[1])
    xs = _ffn(xs, ffn1[1])
    n_sc = state_sconv.shape[2]
    xm, sc_m = _sconv(xs[n_sample:][None], jnp.zeros((1, HALO, D_MODEL), F32), sc_w, rows=N_META)
    xf3, p_sc = _sconv(xf.reshape(bp, tp, D_MODEL), sc_m, sc_w, rows=SCONV_ROWS)
    xs3, s_sc = _sconv(xs[:n_sample].reshape(bs, ts, D_MODEL), _tail_rows(state_sconv[0], n_sc), sc_w, rows=ts)
    xf = _ffn(xf3.reshape(n_frames, D_MODEL), ffn2[1])
    xs = _ffn(jnp.concatenate([xs3.reshape(n_sample, D_MODEL), xm[0]], 0), ffn2[1])

    y_prompt = xf.reshape(bp, tp, D_MODEL)
    y_sample = xs[:n_sample].reshape(bs, ts, D_MODEL)
    rope_lo, rope_hi = MLA_NOPE, MLA_NOPE + MLA_ROPE
    meta_ckv = jnp.broadcast_to(ckvn_s[n_sample:][None], (bp, N_META, MLA_KV_RANK))
    meta_kr = jnp.broadcast_to(krn_s[n_sample:, rope_lo:rope_hi][None], (bp, N_META, MLA_ROPE))
    p_ckv = jnp.concatenate([meta_ckv, ckvn_f.reshape(bp, tp, MLA_KV_RANK)], 1)
    p_kr = jnp.concatenate([meta_kr, krn_f[:, rope_lo:rope_hi].reshape(bp, tp, MLA_ROPE)], 1)
    n_gc = state_gdn_conv.shape[2]
    return (y_prompt, y_sample,
            p_ckv[None], p_kr[None], p_s[None], p_conv[:, HALO - n_gc:][None], p_sc[:, HALO - n_sc:][None],
            ckvn_s[:n_sample].reshape(bs, ts, MLA_KV_RANK)[None],
            krn_s[:n_sample, rope_lo:rope_hi].reshape(bs, ts, MLA_ROPE)[None],
            s_s[None], s_conv[:, HALO - n_gc:][None], s_sc[:, HALO - n_sc:][None])
```

```python
import functools

import jax
import jax.numpy as jnp
from jax import lax
from jax.experimental import pallas as pl
from jax.experimental.pallas import tpu as pltpu

F32 = jnp.float32
BF16 = jnp.bfloat16

D_MODEL = 1024
D_FF = 2816
CHUNK = 64
CHUNK_SHIFT = 6
N_META = 16
EPS = 1e-6
GDN_HEADS = 4
GDN_DK = 128
GDN_DV = 128
GDN_QKV = GDN_HEADS * (2 * GDN_DK + GDN_DV)
GDN_Z = GDN_HEADS * GDN_DV
MLA_HEADS = 8
MLA_NOPE = 64
MLA_ROPE = 32
MLA_V = 64
MLA_Q_RANK = 384
MLA_KV_RANK = 256
ROPE_THETA = 10000.0
PAST_LEN = 1024
LANES = 128
HALO = 8
MLA_PAD = MLA_HEADS * LANES
FF_BLOCK = 256
FF_CHUNKS = D_FF // FF_BLOCK
VMEM_LIMIT = 60000 * 1024
NEG_INF = float("-inf")
LOG2_E = 1.4426950408889634
MASK_LANE0 = MLA_NOPE + MLA_ROPE
MASK_BIG = 2.0 ** 100
ATTN_TQ = 512
ATTN_TK = 1024
FFN_ROWS = 1024
INPROJ_ROWS = 512
SCONV_ROWS = 1024
GDN_ROWS = 8 * CHUNK
VT_ROWS = MLA_V + 16
VT_TOTAL = MLA_HEADS * VT_ROWS
ONE_PASS_JUMP = 64.0

IN_QKV = 0
IN_Z = IN_QKV + GDN_QKV
IN_CQ = IN_Z + GDN_Z
IN_GATES_ROPE = IN_CQ + MLA_Q_RANK
IN_CKV = IN_GATES_ROPE + LANES
IN_TOTAL = IN_CKV + MLA_KV_RANK


def _params(n_axes):
    return pltpu.CompilerParams(dimension_semantics=("arbitrary",) * n_axes, vmem_limit_bytes=VMEM_LIMIT)


def _rms(x, g):
    return x * lax.rsqrt(jnp.mean(x * x, -1, keepdims=True) + EPS) * g


def _sigmoid(x):
    return 1.0 / (1.0 + jnp.exp(-x))


def _silu(x):
    return x * _sigmoid(x)


def _softplus(x):
    return jnp.maximum(x, 0.0) + jnp.log1p(jnp.exp(-jnp.abs(x)))


def _dot(a, b):
    return jnp.dot(a, b, preferred_element_type=F32)


def _dot_nt(a, b):
    return lax.dot_general(a, b, (((1,), (1,)), ((), ())), preferred_element_type=F32)


def _dot_tn(a, b):
    return lax.dot_general(a, b, (((0,), (0,)), ((), ())), preferred_element_type=F32)


def _shift_rows(x, history, k):
    rolled = pltpu.roll(x, k, 0)
    row = lax.broadcasted_iota(jnp.int32, (HALO, 1), 0)
    head = rolled[:HALO]
    for j in range(k):
        head = jnp.where(row == j, history[HALO - k + j:HALO - k + j + 1, :], head)
    return jnp.concatenate([head, rolled[HALO:]], 0)


def _full_spec(shape):
    zeros = (0,) * len(shape)
    return pl.BlockSpec(shape, lambda *_: zeros)


def _row_tile(rows, want):
    return want if rows % want == 0 else rows


def _ffn_body(x_ref, g_ref, wg_ref, wu_ref, wd_ref, *rest, mixed):
    if mixed:
        og_ref, omt_ref, wmix_ref, o_ref, acc_ref = rest
        x = x_ref[...] + _dot(og_ref[...], wmix_ref[:GDN_Z, :]) + _dot_tn(omt_ref[...], wmix_ref[GDN_Z:, :])
    else:
        o_ref, acc_ref = rest
        x = x_ref[...]
    h = _rms(x, g_ref[...]).astype(BF16)
    for c in range(FF_CHUNKS):
        cols = slice(c * FF_BLOCK, (c + 1) * FF_BLOCK)
        gate = _dot(h, wg_ref[:, cols])
        up = _dot(h, wu_ref[:, cols])
        act = (_silu(gate) * up).astype(BF16)
        down = _dot(act, wd_ref[cols, :])
        if c == 0:
            acc_ref[...] = down
        else:
            acc_ref[...] += down
    o_ref[...] = x + 0.5 * acc_ref[...]


def _ffn(x, w, mix=None):
    rows = x.shape[0]
    tm = _row_tile(rows, FFN_ROWS)
    g, wg, wu, wd, layer = w
    resident = lambda a: pl.BlockSpec(a.shape, lambda i: (0,) * a.ndim, pipeline_mode=pl.Buffered(1))
    of_layer = lambda a: pl.BlockSpec((None,) + a.shape[1:], lambda i: (layer, 0, 0), pipeline_mode=pl.Buffered(1))
    in_specs = [pl.BlockSpec((tm, D_MODEL), lambda i: (i, 0)), _full_spec(g.shape), of_layer(wg), of_layer(wu),
                of_layer(wd)]
    args = [x, g, wg, wu, wd]
    if mix is not None:
        og, omt, w_mix = mix
        tiles_per_batch = rows // omt.shape[0] // tm
        in_specs += [pl.BlockSpec((tm, GDN_Z), lambda i: (i, 0)),
                     pl.BlockSpec((None, GDN_Z, tm), lambda i: (i // tiles_per_batch, 0, i % tiles_per_batch)),
                     resident(w_mix)]
        args += [og, omt, w_mix]
    return pl.pallas_call(
        functools.partial(_ffn_body, mixed=mix is not None),
        grid=(rows // tm,),
        in_specs=in_specs,
        out_specs=pl.BlockSpec((tm, D_MODEL), lambda i: (i, 0)),
        out_shape=jax.ShapeDtypeStruct((rows, D_MODEL), F32),
        scratch_shapes=[pltpu.VMEM((tm, D_MODEL), F32)],
        compiler_params=_params(1),
        name="half_ffn",
    )(*args)


def _to_bf16_body(x_ref, o_ref):
    o_ref[...] = x_ref[...].astype(BF16)


def _to_bf16(a):
    a2 = a.reshape(-1, a.shape[-1])
    rows, cols = a2.shape
    tm = _row_tile(rows, 512)
    out = pl.pallas_call(
        _to_bf16_body,
        grid=(rows // tm,),
        in_specs=[pl.BlockSpec((tm, cols), lambda i: (i, 0))],
        out_specs=pl.BlockSpec((tm, cols), lambda i: (i, 0)),
        out_shape=jax.ShapeDtypeStruct((rows, cols), BF16),
        compiler_params=_params(1),
        name="weights_to_bf16",
    )(a2)
    return out.reshape(a.shape)


def _ffn_weights(norm, w_gate, w_up, w_down):
    stacks = (_to_bf16(w_gate), _to_bf16(w_up), _to_bf16(w_down))
    return [(norm[l].reshape(1, D_MODEL),) + stacks + (l,) for l in range(norm.shape[0])]


def _gdn_conv(x, history, cw):
    y = (_shift_rows(x, history, 3) * cw[0:1] + _shift_rows(x, history, 2) * cw[1:2]
         + _shift_rows(x, history, 1) * cw[2:3] + x * cw[3:4])
    return _silu(y)


class _Split:
    def __init__(self, x):
        self.x = x
        self._parts = None
        self._diag = None

    def parts(self):
        if self._parts is None:
            hi = self.x.astype(BF16)
            self._parts = (hi, (self.x - hi.astype(F32)).astype(BF16))
        return self._parts

    def diag_parts(self):
        if self._diag is None:
            first = lax.broadcasted_iota(jnp.int32, self.x.shape, 1) < CHUNK
            zero = jnp.zeros(self.x.shape, BF16)
            self._diag = tuple(jnp.concatenate([jnp.where(first, p, zero), jnp.where(first, zero, p)], 0)
                               for p in self.parts())
        return self._diag


def _split_products(pairs):
    parts = [(a.parts(), b.diag_parts()) for a, b in pairs]
    outs = []
    for (a_hi, a_lo), (b_hi, b_lo) in parts:
        a2 = jnp.concatenate([a_hi, a_lo], 0)
        r = _dot(a2, b_hi) + _dot(a2, b_lo)
        half = a_hi.shape[0]
        outs.append(_Split(r[:half] + r[half:]))
    return outs


def _unit_lower_inverses(ns, eye_f, m16, m32):
    k = len(ns)
    nd = [_Split(jnp.where(m16, n, 0.0)) for n in ns]
    n2 = _split_products([(x, x) for x in nd])
    lvl = _split_products([(x, x) for x in n2] +
                          [(_Split(eye_f - a.x), _Split(eye_f + b.x)) for a, b in zip(nd, n2)])
    n4, t = lvl[:k], lvl[k:]
    lvl = _split_products([(x, x) for x in n4] + [(a, _Split(eye_f + b.x)) for a, b in zip(t, n4)])
    n8, t = lvl[:k], lvl[k:]
    t = _split_products([(a, _Split(eye_f + b.x)) for a, b in zip(t, n8)])
    for keep in (jnp.logical_and(m32, jnp.logical_not(m16)), jnp.logical_not(m32)):
        off = [_Split(jnp.where(keep, n, 0.0)) for n in ns]
        x = _split_products(list(zip(off, t)))
        y = _split_products(list(zip(t, x)))
        t = [_Split(a.x - b.x) for a, b in zip(t, y)]
    return [a.x for a in t]


def _gdn_body(qkv_ref, z_ref, ab_ref, s0_ref, cprev_ref, cw_ref, alog_ref, dtb_ref, on_ref,
              og_ref, sout_ref, cout_ref, s_scr, halo_scr, *, rows, n_valid):
    t = pl.program_id(1)
    C = CHUNK
    row = lax.broadcasted_iota(jnp.int32, (rows, 1), 0)

    @pl.when(t == 0)
    def _():
        s_scr[...] = s0_ref[...]
        halo_scr[...] = cprev_ref[...]

    x = qkv_ref[...]
    y = _gdn_conv(x, halo_scr[...], cw_ref[...])
    last_rows = x[n_valid - HALO:n_valid, :]
    halo_scr[...] = last_rows

    ab = ab_ref[...]
    g_all = -jnp.exp(alog_ref[...]) * _softplus(ab + dtb_ref[...])
    b_all = _sigmoid(ab)
    if n_valid < rows:
        valid = row < n_valid
        y = jnp.where(valid, y, 0.0)
        g_all = jnp.where(valid, g_all, 0.0)
        b_all = jnp.where(valid, b_all, 0.0)

    ii = lax.broadcasted_iota(jnp.int32, (C, LANES), 0)
    pair_lane = lax.broadcasted_iota(jnp.int32, (C, LANES), 1)
    jj = pair_lane & (C - 1)
    first_head = pair_lane < C
    causal = ii >= jj
    strict = ii > jj
    eye_f = (ii == jj).astype(F32)
    m16 = (ii >> 4) == (jj >> 4)
    m32 = (ii >> 5) == (jj >> 5)
    on = on_ref[...]
    n_chunks = rows // C
    heads = range(GDN_HEADS)

    ri = lax.broadcasted_iota(jnp.int32, (rows, rows), 0)
    rj = lax.broadcasted_iota(jnp.int32, (rows, rows), 1)
    tril_blk = jnp.logical_and(ri >= rj, (ri >> CHUNK_SHIFT) == (rj >> CHUNK_SHIFT)).astype(BF16)
    g_hi = g_all.astype(BF16)
    g_r1 = g_all - g_hi.astype(F32)
    g_mid = g_r1.astype(BF16)
    g_lo = (g_r1 - g_mid.astype(F32)).astype(BF16)
    gcum = _dot(tril_blk, g_hi) + _dot(tril_blk, g_mid) + _dot(tril_blk, g_lo)
    egcum = jnp.exp(gcum)

    qn, kn, vn = [], [], []
    for h in heads:
        qh = y[:, h * GDN_DK:(h + 1) * GDN_DK]
        kh = y[:, GDN_HEADS * GDN_DK + h * GDN_DK:GDN_HEADS * GDN_DK + (h + 1) * GDN_DK]
        qn.append(qh * lax.rsqrt(jnp.sum(qh * qh, -1, keepdims=True) + EPS) * (GDN_DK ** -0.5))
        kn.append(kh * lax.rsqrt(jnp.sum(kh * kh, -1, keepdims=True) + EPS))
        vn.append(y[:, 2 * GDN_HEADS * GDN_DK + h * GDN_DV:2 * GDN_HEADS * GDN_DK + (h + 1) * GDN_DV])

    head_pairs = [(a, a + 1) for a in range(0, GDN_HEADS, 2)]
    chains = [(c, pair) for c in range(n_chunks) for pair in head_pairs]
    pre = {}
    n_mats = []
    for c in range(n_chunks):
        sl = slice(c * C, (c + 1) * C)
        gcum_t = gcum[sl].T
        for a, b in head_pairs:
            col = lambda arr, h: arr[sl, h:h + 1]
            gcol = jnp.where(first_head, col(gcum, a), col(gcum, b))
            beta2 = jnp.where(first_head, col(b_all, GDN_HEADS + a), col(b_all, GDN_HEADS + b))
            grow = jnp.concatenate([gcum_t[a:a + 1, :], gcum_t[b:b + 1, :]], -1)
            dec_causal = jnp.exp(jnp.where(causal, gcol - grow, NEG_INF))
            kb = {h: kn[h][sl].astype(BF16) for h in (a, b)}
            kk = jnp.concatenate([_dot_nt(kb[a], kb[a]), _dot_nt(kb[b], kb[b])], -1)
            n_mats.append(beta2 * kk * jnp.where(strict, dec_causal, 0.0))
            qk = jnp.concatenate([_dot_nt(qn[h][sl].astype(BF16), kb[h]) for h in (a, b)], -1)
            qk = (qk * dec_causal).astype(BF16)
            for h, lanes in ((a, slice(0, C)), (b, slice(C, 2 * C))):
                beta = col(b_all, GDN_HEADS + h)
                eg = col(egcum, h)
                g_last = gcum[(c + 1) * C - 1:(c + 1) * C, h:h + 1]
                qh, kh, vh = qn[h][sl], kn[h][sl], vn[h][sl]
                pre[c, h] = dict(
                    qk=qk[:, lanes],
                    rhs=jnp.concatenate([(beta * vh).astype(BF16), ((beta * eg) * kh).astype(BF16)], -1),
                    qd=(qh * eg).astype(BF16),
                    kt=(kh * jnp.exp(g_last - col(gcum, h))).astype(BF16),
                    decay=jnp.exp(g_last))
    for (c, (a, b)), tinv in zip(chains, _unit_lower_inverses(n_mats, eye_f, m16, m32)):
        tinv = tinv.astype(BF16)
        for h, lanes in ((a, slice(0, C)), (b, slice(C, 2 * C))):
            uw = _dot(tinv[:, lanes], pre[c, h]["rhs"])
            pre[c, h]["u"] = uw[:, :GDN_DV]
            pre[c, h]["wk"] = uw[:, GDN_DV:].astype(BF16)

    state = [s_scr[h] for h in heads]
    for c in range(n_chunks):
        sl = slice(c * C, (c + 1) * C)
        sb = [s.astype(BF16) for s in state]
        w = [(pre[c, h]["u"] - _dot(pre[c, h]["wk"], sb[h])).astype(BF16) for h in heads]
        o = [_dot(pre[c, h]["qd"], sb[h]) + _dot(pre[c, h]["qk"], w[h]) for h in heads]
        state = [state[h] * pre[c, h]["decay"] + _dot_tn(pre[c, h]["kt"], w[h]) for h in heads]
        for h in heads:
            gated = _rms(o[h], on) * _silu(z_ref[sl, h * GDN_DV:(h + 1) * GDN_DV])
            og_ref[sl, h * GDN_DV:(h + 1) * GDN_DV] = gated.astype(BF16)
    for h in heads:
        s_scr[h] = state[h]

    @pl.when(t == pl.num_programs(1) - 1)
    def _():
        sout_ref[...] = s_scr[...]
        cout_ref[...] = last_rows


def _gdn(qkv, z, ab, s0, cprev, w, *, rows, n_valid=None):
    b, t_len, _ = qkv.shape
    n_valid = rows if n_valid is None else n_valid
    conv_w, a_log, dt_bias, o_norm = w
    shared = s0.shape[0] == 1 and b > 1
    bsel = (lambda bi: 0) if shared else (lambda bi: bi)
    return pl.pallas_call(
        functools.partial(_gdn_body, rows=rows, n_valid=n_valid),
        grid=(b, t_len // rows),
        in_specs=[
            pl.BlockSpec((None, rows, GDN_QKV), lambda bi, ti: (bi, ti, 0)),
            pl.BlockSpec((None, rows, GDN_Z), lambda bi, ti: (bi, ti, 0)),
            pl.BlockSpec((None, rows, LANES), lambda bi, ti: (bi, ti, 0)),
            pl.BlockSpec((None, GDN_HEADS, GDN_DK, GDN_DV), lambda bi, ti: (bsel(bi), 0, 0, 0)),
            pl.BlockSpec((None, HALO, GDN_QKV), lambda bi, ti: (bsel(bi), 0, 0)),
            _full_spec(conv_w.shape), _full_spec(a_log.shape), _full_spec(dt_bias.shape), _full_spec(o_norm.shape),
        ],
        out_specs=[
            pl.BlockSpec((None, rows, GDN_Z), lambda bi, ti: (bi, ti, 0)),
            pl.BlockSpec((None, GDN_HEADS, GDN_DK, GDN_DV), lambda bi, ti: (bi, 0, 0, 0)),
            pl.BlockSpec((None, HALO, GDN_QKV), lambda bi, ti: (bi, 0, 0)),
        ],
        out_shape=[
            jax.ShapeDtypeStruct((b, t_len, GDN_Z), BF16),
            jax.ShapeDtypeStruct((b, GDN_HEADS, GDN_DK, GDN_DV), F32),
            jax.ShapeDtypeStruct((b, HALO, GDN_QKV), F32),
        ],
        scratch_shapes=[pltpu.VMEM((GDN_HEADS, GDN_DK, GDN_DV), F32), pltpu.VMEM((HALO, GDN_QKV), F32)],
        compiler_params=_params(2),
        name="gdn_mixer",
    )(qkv, z, ab, s0, cprev, conv_w, a_log, dt_bias, o_norm)


def _lane_masks():
    lane = lax.broadcasted_iota(jnp.int32, (1, LANES), 1)
    nope = lane < MLA_NOPE
    rope_a = jnp.logical_and(lane >= MLA_NOPE, lane < MLA_NOPE + MLA_ROPE // 2)
    rope_b = jnp.logical_and(lane >= MLA_NOPE + MLA_ROPE // 2, lane < MLA_NOPE + MLA_ROPE)
    return nope, rope_a, rope_b


def _rope_block(x, cos, sin, rope_a, rope_b):
    half = MLA_ROPE // 2
    rot = jnp.where(rope_a, -pltpu.roll(x, LANES - half, 1), jnp.where(rope_b, pltpu.roll(x, half, 1), 0.0))
    return x * cos + rot * sin


PAIR = 2 * LANES


def _twice(a):
    return jnp.concatenate([a, a], -1)


def _k_heads(k, shared, gk, k_out):
    for h in range(MLA_HEADS):
        x = k[:, h * LANES:(h + 1) * LANES]
        inv = lax.rsqrt(jnp.sum(x * x, -1, keepdims=True) / MLA_NOPE + EPS)
        k_out[:, h * LANES:(h + 1) * LANES] = (x * inv * gk + shared).astype(BF16)


def _inproj_body(x_ref, g_ref, w_ref, cos_ref, sin_ref, qn_ref, wuq_ref, kvn_ref, wk_ref, wvt_ref, vones_ref,
                 gq_ref, gk_ref, gmean_ref, rot_ref, qkv_ref, z_ref, ab_ref, q_out, k_out, vt_out, ckv_out, kr_out,
                 *, tiles_per_batch, key_tile_chunks):
    hn = _rms(x_ref[...], g_ref[...]).astype(BF16)
    qkv_ref[...] = _dot(hn, w_ref[:, IN_QKV:IN_Z])
    z_ref[...] = _dot(hn, w_ref[:, IN_Z:IN_CQ])
    cq_gates_rope = _dot(hn, w_ref[:, IN_CQ:IN_CKV])
    cq = cq_gates_rope[:, :MLA_Q_RANK]
    gates_rope = cq_gates_rope[:, MLA_Q_RANK:]
    ab_ref[...] = gates_rope
    ckv = _dot(hn, w_ref[:, IN_CKV:IN_TOTAL])

    nope, rope_a, rope_b = _lane_masks()
    krp = jnp.where(jnp.logical_or(rope_a, rope_b), gates_rope, 0.0)
    cos = cos_ref[...]
    sin = sin_ref[...]
    gk = gk_ref[...]
    gmean = gmean_ref[...]
    rot_mat = rot_ref[...]
    gq2 = _twice(gq_ref[...] * ((MLA_NOPE + MLA_ROPE) ** -0.5 * LOG2_E))
    cos2, sin2 = _twice(cos), _twice(sin)

    cqn = _rms(cq, qn_ref[...]).astype(BF16)
    q = _dot(cqn, wuq_ref[...])
    for p in range(MLA_HEADS // 2):
        x = q[:, p * PAIR:(p + 1) * PAIR]
        ms = _dot((x * x).astype(BF16), gmean)
        x = x * lax.rsqrt(ms + EPS) * gq2
        q_out[:, p * PAIR:(p + 1) * PAIR] = (x * cos2 + _dot(x.astype(BF16), rot_mat) * sin2).astype(BF16)

    ckvn = _rms(ckv, kvn_ref[...])
    ckv_out[...] = ckvn
    cb = ckvn.astype(BF16)
    krn = krp * lax.rsqrt(jnp.sum(krp * krp, -1, keepdims=True) / MLA_ROPE + EPS) * jnp.where(nope, 0.0, gk)
    kr_new = _rope_block(krn, cos, sin, rope_a, rope_b)
    kr_out[...] = kr_new
    vt_out[...] = (_dot_nt(wvt_ref[...], cb) + vones_ref[...]).astype(BF16)
    shared_lanes = kr_new
    if key_tile_chunks is not None:
        tm = kr_new.shape[0]
        pos = (pl.program_id(0) % tiles_per_batch) * tm + lax.broadcasted_iota(jnp.int32, (tm, 1), 0)
        lane = lax.broadcasted_iota(jnp.int32, (1, LANES), 1)
        chunk = (pos >> CHUNK_SHIFT) & (key_tile_chunks - 1)
        shared_lanes = kr_new + (lane - MASK_LANE0 == chunk).astype(F32)
    _k_heads(_dot(cb, wk_ref[...]), shared_lanes, jnp.where(nope, gk, 0.0), k_out)


def _vt_spec(tm, tiles_per_batch):
    return pl.BlockSpec((None, VT_TOTAL, tm), lambda i: (i // tiles_per_batch, 0, i % tiles_per_batch))


def _inproj(x, g, w_in, cos, sin, w, *, batch, key_tile=None):
    rows = x.shape[0]
    tm = _row_tile(rows, INPROJ_ROWS)
    tiles_per_batch = rows // batch // tm
    key_tile_chunks = None if key_tile is None else key_tile // CHUNK
    assert key_tile_chunks is None or (key_tile_chunks & (key_tile_chunks - 1) == 0
                                       and key_tile_chunks <= LANES - MASK_LANE0)
    q_norm, w_uq, kv_norm, w_k, w_vt, v_ones, gq, gk, gmean, rot_mat = w
    row_spec = lambda n: pl.BlockSpec((tm, n), lambda i: (i, 0))
    tab_spec = pl.BlockSpec((tm, LANES), lambda i: (i % tiles_per_batch, 0))
    consts = [g, w_in, q_norm, w_uq, kv_norm, w_k, w_vt, v_ones, gq, gk, gmean, rot_mat]
    return pl.pallas_call(
        functools.partial(_inproj_body, tiles_per_batch=tiles_per_batch, key_tile_chunks=key_tile_chunks),
        grid=(rows // tm,),
        in_specs=[row_spec(D_MODEL), _full_spec(g.shape), _full_spec(w_in.shape), tab_spec, tab_spec]
        + [_full_spec(a.shape) for a in consts[2:]],
        out_specs=[row_spec(GDN_QKV), row_spec(GDN_Z), row_spec(LANES), row_spec(MLA_PAD), row_spec(MLA_PAD),
                   _vt_spec(tm, tiles_per_batch), row_spec(MLA_KV_RANK), row_spec(LANES)],
        out_shape=[jax.ShapeDtypeStruct((rows, GDN_QKV), F32), jax.ShapeDtypeStruct((rows, GDN_Z), F32),
                   jax.ShapeDtypeStruct((rows, LANES), F32),
                   jax.ShapeDtypeStruct((rows, MLA_PAD), BF16), jax.ShapeDtypeStruct((rows, MLA_PAD), BF16),
                   jax.ShapeDtypeStruct((batch, VT_TOTAL, rows // batch), BF16),
                   jax.ShapeDtypeStruct((rows, MLA_KV_RANK), F32), jax.ShapeDtypeStruct((rows, LANES), F32)],
        compiler_params=_params(1),
        name="mixer_inproj",
    )(x, g, w_in, cos, sin, *consts[2:])


def _kv_up_body(ckv_ref, kr_ref, wk_ref, wvt_ref, vones_ref, gk_ref, k_out, vt_out):
    nope, _, _ = _lane_masks()
    cb = ckv_ref[...].astype(BF16)
    vt_out[...] = (_dot_nt(wvt_ref[...], cb) + vones_ref[...]).astype(BF16)
    _k_heads(_dot(cb, wk_ref[...]), kr_ref[...], jnp.where(nope, gk_ref[...], 0.0), k_out)


def _kv_up(ckv, krp, w_k, w_vt, v_ones, gk, *, batch):
    rows = ckv.shape[0]
    tm = _row_tile(rows, 512)
    tiles_per_batch = rows // batch // tm
    row_spec = lambda n: pl.BlockSpec((tm, n), lambda i: (i, 0))
    return pl.pallas_call(
        _kv_up_body,
        grid=(rows // tm,),
        in_specs=[row_spec(MLA_KV_RANK), row_spec(LANES), _full_spec(w_k.shape), _full_spec(w_vt.shape),
                  _full_spec(v_ones.shape), _full_spec(gk.shape)],
        out_specs=[row_spec(MLA_PAD), _vt_spec(tm, tiles_per_batch)],
        out_shape=[jax.ShapeDtypeStruct((rows, MLA_PAD), BF16),
                   jax.ShapeDtypeStruct((batch, VT_TOTAL, rows // batch), BF16)],
        compiler_params=_params(1),
        name="mla_kv_up",
    )(ckv, krp, w_k, w_vt, v_ones, gk)


def _attn_body(qi_ref, kj_ref, *refs, tq, tk, n_key_tiles, causal, nk_valid, has_ext):
    if has_ext:
        q_ref, k_ref, vt_ref, ke_ref, vte_ref, o_ref, m_scr, acc_scr, slot_ref = refs
    else:
        q_ref, k_ref, vt_ref, o_ref, m_scr, acc_scr, slot_ref = refs
    step = pl.program_id(1)
    i = qi_ref[step]
    j = kj_ref[step]
    last_j = ((i + 1) * tq + tk - 1) // tk - 1 if causal else n_key_tiles - 1
    head_lanes = [slice(h * LANES, (h + 1) * LANES) for h in range(MLA_HEADS)]

    def head_scores(k_blk, q_mask):
        if q_mask is None:
            q_heads = [q_ref[:, hs] for hs in head_lanes]
        else:
            q_heads = [q_ref[:, hs] + q_mask for hs in head_lanes]
        return [_dot_nt(k_blk[:, hs], qh) for hs, qh in zip(head_lanes, q_heads)]

    def values(vt_blk, h):
        return vt_blk[h * VT_ROWS:(h + 1) * VT_ROWS, :]

    def two_pass(k_blk, vt_blk, cur, bias=None, q_mask=None):
        scores = head_scores(k_blk, q_mask)
        for h in range(MLA_HEADS):
            s = scores[h] if bias is None else scores[h] + bias
            m_old = m_scr[cur, h][0:1]
            m_new = jnp.maximum(m_old, jnp.max(s, 0, keepdims=True))
            alpha = jnp.exp2(m_old - m_new)
            p = jnp.exp2(s - m_new)
            m_scr[cur, h] = jnp.broadcast_to(m_new, (HALO, tq))
            acc_scr[cur, h] = acc_scr[cur, h] * alpha + _dot(values(vt_blk, h), p.astype(BF16))

    def one_pass(k_blk, vt_blk, cur, q_mask=None):
        scores = head_scores(k_blk, q_mask)
        rise = None
        for h in range(MLA_HEADS):
            s = scores[h]
            m_old = m_scr[cur, h][0:1]
            p = jnp.exp2(s - m_old)
            m_tile = jnp.max(s, 0, keepdims=True)
            m_new = jnp.maximum(m_old, m_tile)
            rise = m_tile - m_old if rise is None else jnp.maximum(rise, m_tile - m_old)
            m_scr[1 - cur, h] = jnp.broadcast_to(m_new, (HALO, tq))
            acc_scr[1 - cur, h] = (acc_scr[cur, h] + _dot(values(vt_blk, h), p.astype(BF16))) * jnp.exp2(m_old - m_new)
        return jnp.max(rise)

    def guarded(k_blk, vt_blk, q_mask=None):
        cur = slot_ref[0]
        rise = one_pass(k_blk, vt_blk, cur, q_mask)

        @pl.when(rise <= ONE_PASS_JUMP)
        def _():
            slot_ref[0] = 1 - cur

        @pl.when(jnp.logical_not(rise <= ONE_PASS_JUMP))
        def _():
            two_pass(k_blk, vt_blk, cur, q_mask=q_mask)

    @pl.when(j == 0)
    def _():
        slot_ref[0] = 0
        m_scr[0] = jnp.full(m_scr.shape[1:], NEG_INF, F32)
        acc_scr[0] = jnp.zeros(acc_scr.shape[1:], F32)
        if has_ext:
            two_pass(ke_ref, vte_ref, 0)

    if nk_valid is not None:
        kpos = j * tk + lax.broadcasted_iota(jnp.int32, (tk, 1), 0)
        two_pass(k_ref, vt_ref, slot_ref[0], bias=jnp.where(kpos < nk_valid, 0.0, NEG_INF))
    elif not has_ext:
        two_pass(k_ref, vt_ref, slot_ref[0])
    elif causal:
        @pl.when(j < last_j)
        def _():
            guarded(k_ref, vt_ref)

        @pl.when(j == last_j)
        def _():
            lane = lax.broadcasted_iota(jnp.int32, (1, LANES), 1)
            q_chunk = (i * tq + lax.broadcasted_iota(jnp.int32, (tq, 1), 0)) >> CHUNK_SHIFT
            key_chunk = j * (tk // CHUNK) + lane - MASK_LANE0
            hidden = jnp.logical_and(jnp.logical_and(lane >= MASK_LANE0, lane < MASK_LANE0 + tk // CHUNK),
                                     key_chunk > q_chunk)
            guarded(k_ref, vt_ref, q_mask=jnp.where(hidden, -MASK_BIG, 0.0).astype(BF16))
    else:
        guarded(k_ref, vt_ref)

    @pl.when(j == last_j)
    def _():
        cur = slot_ref[0]
        for h in range(MLA_HEADS):
            acc = acc_scr[cur, h]
            o_ref[h * MLA_V:(h + 1) * MLA_V, :] = (acc[:MLA_V] / acc[MLA_V:MLA_V + 1]).astype(BF16)


def _attention(q, k, vt, ext, *, tq, tk, causal=False, nk_valid=None):
    b, nq, _ = q.shape
    nk = k.shape[1]
    n_key_tiles = nk // tk
    assert not causal or tk % tq == 0
    pairs = [(i, j) for i in range(nq // tq) for j in range(n_key_tiles) if not causal or j * tk < (i + 1) * tq]
    qi = jnp.asarray([p[0] for p in pairs], jnp.int32)
    kj = jnp.asarray([p[1] for p in pairs], jnp.int32)
    hv = MLA_HEADS * MLA_V
    in_specs = [
        pl.BlockSpec((None, tq, MLA_PAD), lambda bi, s, qi, kj: (bi, qi[s], 0)),
        pl.BlockSpec((None, tk, MLA_PAD), lambda bi, s, qi, kj: (bi, kj[s], 0)),
        pl.BlockSpec((None, VT_TOTAL, tk), lambda bi, s, qi, kj: (bi, 0, kj[s])),
    ]
    args = [q, k, vt]
    if ext is not None:
        ne = ext[0].shape[1]
        shared = ext[0].shape[0] == 1 and b > 1
        emap = (lambda bi, s, qi, kj: (0, 0, 0)) if shared else (lambda bi, s, qi, kj: (bi, 0, 0))
        in_specs += [pl.BlockSpec((None, ne, MLA_PAD), emap), pl.BlockSpec((None, VT_TOTAL, ne), emap)]
        args += list(ext)
    return pl.pallas_call(
        functools.partial(_attn_body, tq=tq, tk=tk, n_key_tiles=n_key_tiles, causal=causal, nk_valid=nk_valid,
                          has_ext=ext is not None),
        grid_spec=pltpu.PrefetchScalarGridSpec(
            num_scalar_prefetch=2,
            grid=(b, len(pairs)),
            in_specs=in_specs,
            out_specs=pl.BlockSpec((None, hv, tq), lambda bi, s, qi, kj: (bi, 0, qi[s])),
            scratch_shapes=[pltpu.VMEM((2, MLA_HEADS, HALO, tq), F32), pltpu.VMEM((2, MLA_HEADS, VT_ROWS, tq), F32),
                            pltpu.SMEM((1,), jnp.int32)]),
        out_shape=jax.ShapeDtypeStruct((b, hv, nq), BF16),
        compiler_params=_params(2),
        name="mla_attention",
    )(qi, kj, *args)


def _sconv_body(x_ref, g_ref, win_ref, cw_ref, wout_ref, prev_ref, o_ref, pout_ref, halo_scr, *, rows):
    t = pl.program_id(1)

    @pl.when(t == 0)
    def _():
        halo_scr[...] = prev_ref[...]

    x = x_ref[...]
    hn = _rms(x, g_ref[...]).astype(BF16)
    bg = _dot(hn, win_ref[:, :D_MODEL])
    u = _dot(hn, win_ref[:, D_MODEL:2 * D_MODEL]) * _dot(hn, win_ref[:, 2 * D_MODEL:])
    halo = halo_scr[...]
    u1 = _shift_rows(u, halo, 1)
    u2 = _shift_rows(u, halo, 2)
    cw = cw_ref[...]
    y = u2 * cw[0:1] + u1 * cw[1:2] + u * cw[2:3]
    last_rows = u[rows - HALO:rows, :]
    halo_scr[...] = last_rows
    o_ref[...] = x + _dot((bg * y).astype(BF16), wout_ref[...])

    @pl.when(t == pl.num_programs(1) - 1)
    def _():
        pout_ref[...] = last_rows


def _sconv(x, prev, w, *, rows):
    b, t_len, _ = x.shape
    g, w_in, conv_w, w_out = w
    shared = prev.shape[0] == 1 and b > 1
    bsel = (lambda bi: 0) if shared else (lambda bi: bi)
    return pl.pallas_call(
        functools.partial(_sconv_body, rows=rows),
        grid=(b, t_len // rows),
        in_specs=[pl.BlockSpec((None, rows, D_MODEL), lambda bi, ti: (bi, ti, 0)), _full_spec(g.shape),
                  _full_spec(w_in.shape), _full_spec(conv_w.shape), _full_spec(w_out.shape),
                  pl.BlockSpec((None, HALO, D_MODEL), lambda bi, ti: (bsel(bi), 0, 0))],
        out_specs=[pl.BlockSpec((None, rows, D_MODEL), lambda bi, ti: (bi, ti, 0)),
                   pl.BlockSpec((None, HALO, D_MODEL), lambda bi, ti: (bi, 0, 0))],
        out_shape=[jax.ShapeDtypeStruct((b, t_len, D_MODEL), F32), jax.ShapeDtypeStruct((b, HALO, D_MODEL), F32)],
        scratch_shapes=[pltpu.VMEM((HALO, D_MODEL), F32)],
        compiler_params=_params(2),
        name="sconv_mixer",
    )(x, g, w_in, conv_w, w_out, prev)


def _pad_lanes(a, before, total):
    return jnp.pad(a, [(0, 0)] * (a.ndim - 1) + [(before, total - before - a.shape[-1])])


def _rope_tables(pos):
    half = MLA_ROPE // 2
    inv = ROPE_THETA ** (-jnp.arange(half, dtype=F32) / half)
    ang = pos.astype(F32)[:, None] * inv
    cos, sin = jnp.cos(ang), jnp.sin(ang)
    n = pos.shape[0]
    tail = LANES - MLA_NOPE - MLA_ROPE
    cos_l = jnp.concatenate([jnp.ones((n, MLA_NOPE), F32), cos, cos, jnp.ones((n, tail), F32)], -1)
    sin_l = jnp.concatenate([jnp.zeros((n, MLA_NOPE), F32), sin, sin, jnp.zeros((n, tail), F32)], -1)
    return cos_l, sin_l


def _head_pair_matrices():
    i = jnp.arange(PAIR)[:, None]
    j = jnp.arange(PAIR)[None, :]
    same = (i // LANES) == (j // LANES)
    li, lj = i % LANES, j % LANES
    half = MLA_ROPE // 2
    rope_end = MLA_NOPE + MLA_ROPE
    in_rope = lambda l: (l >= MLA_NOPE) & (l < rope_end)
    gmean = jnp.where(same & (li < MLA_NOPE) & (lj < MLA_NOPE), 1.0 / MLA_NOPE,
                      jnp.where(same & in_rope(li) & in_rope(lj), 1.0 / MLA_ROPE, 0.0))
    first, second = in_rope(lj) & (lj < MLA_NOPE + half), in_rope(lj) & (lj >= MLA_NOPE + half)
    rot = jnp.where(same & first & (li == lj + half), -1.0, jnp.where(same & second & (li == lj - half), 1.0, 0.0))
    return gmean.astype(BF16), rot.astype(BF16)


def _tail_rows(a, n):
    return jnp.pad(a, ((0, 0), (HALO - n, 0), (0, 0)))


def kernel(x_prompt, x_sample, cache_mla_ckv, cache_mla_krope, state_gdn_S, state_gdn_conv, state_sconv, meta_tokens, ffn1_norm, ffn1_w_gate, ffn1_w_up, ffn1_w_down, ffn2_norm, ffn2_w_gate, ffn2_w_up, ffn2_w_down, mix_norm, ab_w_in, ab_w_out, gdn_conv_w, gdn_A_log, gdn_dt_bias, gdn_o_norm, mla_q_norm, mla_w_uq, mla_kv_norm, mla_w_ukv, mla_qn_norm, mla_qr_norm, mla_kn_norm, mla_kr_norm, sc_w_in, sc_conv_w, sc_w_out):
    bp, tp, _ = x_prompt.shape
    bs, ts, _ = x_sample.shape
    n_frames = bp * tp
    n_sample = bs * ts

    ffn1 = _ffn_weights(ffn1_norm, ffn1_w_gate, ffn1_w_up, ffn1_w_down)
    ffn2 = _ffn_weights(ffn2_norm, ffn2_w_gate, ffn2_w_up, ffn2_w_down)
    w_in = ab_w_in[0]
    s1 = GDN_QKV + GDN_Z
    c0 = s1 + 2 * GDN_HEADS
    c1 = c0 + MLA_Q_RANK
    c2 = c1 + MLA_KV_RANK
    gates_rope = jnp.concatenate([
        w_in[:, s1:c0], jnp.zeros((D_MODEL, MLA_NOPE - 2 * GDN_HEADS), F32), w_in[:, c2:],
        jnp.zeros((D_MODEL, LANES - MLA_NOPE - MLA_ROPE), F32)], -1)
    w_in_packed = jnp.concatenate([w_in[:, :s1], w_in[:, c0:c1], gates_rope, w_in[:, c1:c2]], -1).astype(BF16)
    mix_g = mix_norm.reshape(2, 1, D_MODEL)
    gdn_w = (gdn_conv_w[0], _pad_lanes(gdn_A_log[0][None], 0, LANES), _pad_lanes(gdn_dt_bias[0][None], 0, LANES),
             gdn_o_norm[0][None])
    qdim = MLA_NOPE + MLA_ROPE
    w_uq = _pad_lanes(mla_w_uq[0].reshape(MLA_Q_RANK, MLA_HEADS, qdim), 0, LANES).reshape(MLA_Q_RANK, MLA_PAD)
    w_ukv = mla_w_ukv[0].reshape(MLA_KV_RANK, MLA_HEADS, MLA_NOPE + MLA_V)
    w_k = _pad_lanes(w_ukv[..., :MLA_NOPE], 0, LANES).reshape(MLA_KV_RANK, MLA_PAD)
    hv = MLA_HEADS * MLA_V
    w_vt = _pad_lanes(w_ukv[..., MLA_NOPE:], 0, VT_ROWS).reshape(MLA_KV_RANK, VT_TOTAL).T
    v_ones = (jnp.arange(VT_TOTAL) % VT_ROWS == MLA_V).astype(F32)[:, None]
    gq = _pad_lanes(jnp.concatenate([mla_qn_norm[0], mla_qr_norm[0]])[None], 0, LANES)
    gk = _pad_lanes(jnp.concatenate([mla_kn_norm[0], mla_kr_norm[0]])[None], 0, LANES)
    gmean, rot_mat = _head_pair_matrices()
    mla_w = (mla_q_norm[0][None], w_uq.astype(BF16), mla_kv_norm[0][None], w_k.astype(BF16), w_vt.astype(BF16), v_ones,
             gq, gk, gmean, rot_mat)
    w_out = ab_w_out[0].astype(BF16)
    sc_w = (mix_g[1], sc_w_in[0].astype(BF16), sc_conv_w[0], sc_w_out[0].astype(BF16))

    xf = x_prompt.reshape(n_frames, D_MODEL)
    xs = jnp.concatenate([x_sample.reshape(n_sample, D_MODEL), meta_tokens.astype(F32)], 0)
    cos_f, sin_f = _rope_tables(N_META + jnp.arange(tp))
    pos_small = jnp.concatenate([jnp.tile(PAST_LEN + jnp.arange(ts), bs), jnp.arange(N_META)])
    cos_s, sin_s = _rope_tables(pos_small)

    xf = _ffn(xf, ffn1[0])
    xs = _ffn(xs, ffn1[0])
    qkv_f, z_f, ab_f, q_f, k_f, vt_f, ckvn_f, krn_f = _inproj(xf, mix_g[0], w_in_packed, cos_f, sin_f, mla_w,
                                                               batch=bp, key_tile=ATTN_TK)
    qkv_s, z_s, ab_s, q_s, k_s, vt_s, ckvn_s, krn_s = _inproj(xs, mix_g[0], w_in_packed, cos_s, sin_s, mla_w, batch=1)

    pad_meta = lambda a: jnp.pad(a[n_sample:], ((0, CHUNK - N_META), (0, 0)))[None]
    og_m, s_m, conv_m = _gdn(pad_meta(qkv_s), pad_meta(z_s), pad_meta(ab_s),
                             jnp.zeros((1, GDN_HEADS, GDN_DK, GDN_DV), F32), jnp.zeros((1, HALO, GDN_QKV), F32),
                             gdn_w, rows=CHUNK, n_valid=N_META)
    og_f, p_s, p_conv = _gdn(qkv_f.reshape(bp, tp, GDN_QKV), z_f.reshape(bp, tp, GDN_Z), ab_f.reshape(bp, tp, LANES),
                             s_m, conv_m, gdn_w, rows=GDN_ROWS)
    og_s, s_s, s_conv = _gdn(qkv_s[:n_sample].reshape(bs, ts, GDN_QKV), z_s[:n_sample].reshape(bs, ts, GDN_Z),
                             ab_s[:n_sample].reshape(bs, ts, LANES), state_gdn_S[0],
                             _tail_rows(state_gdn_conv[0], state_gdn_conv.shape[2]), gdn_w, rows=ts)

    k_c, vt_c = _kv_up(cache_mla_ckv[0].reshape(bs * PAST_LEN, MLA_KV_RANK),
                       _pad_lanes(cache_mla_krope[0].reshape(bs * PAST_LEN, MLA_ROPE), MLA_NOPE, LANES),
                       mla_w[3], mla_w[4], v_ones, gk, batch=bs)
    k_meta, vt_meta = k_s[n_sample:][None], vt_s[:, :, n_sample:]
    vt_new = vt_s[0, :, :n_sample].reshape(VT_TOTAL, bs, ts).transpose(1, 0, 2)
    omt_f = _attention(q_f.reshape(bp, tp, MLA_PAD), k_f.reshape(bp, tp, MLA_PAD), vt_f, (k_meta, vt_meta),
                       tq=ATTN_TQ, tk=ATTN_TK, causal=True)
    omt_s = _attention(q_s[:n_sample].reshape(bs, ts, MLA_PAD), k_c.reshape(bs, PAST_LEN, MLA_PAD), vt_c,
                       (k_s[:n_sample].reshape(bs, ts, MLA_PAD), vt_new), tq=ts, tk=512)
    omt_m = _attention(jnp.pad(q_s[n_sample:], ((0, LANES - N_META), (0, 0)))[None],
                       jnp.pad(k_s[n_sample:], ((0, CHUNK - N_META), (0, 0)))[None],
                       jnp.pad(vt_meta, ((0, 0), (0, 0), (0, CHUNK - N_META))),
                       None, tq=LANES, tk=CHUNK, nk_valid=N_META)

    og_small = jnp.concatenate([og_s.reshape(n_sample, GDN_Z), og_m[0, :N_META]], 0)
    omt_small = jnp.concatenate([omt_s.transpose(1, 0, 2).reshape(hv, n_sample), omt_m[0, :, :N_META]], -1)[None]
    xf = _ffn(xf, ffn2[0], mix=(og_f.reshape(n_frames, GDN_Z), omt_f, w_out))
    xs = _ffn(xs, ffn2[0], mix=(og_small, omt_small, w_out))

    xf = _ffn(xf, ffn1[1])
    xs = _ffn(xs, ffn1[1])
    n_sc = state_sconv.shape[2]
    xm, sc_m = _sconv(xs[n_sample:][None], jnp.zeros((1, HALO, D_MODEL), F32), sc_w, rows=N_META)
    xf3, p_sc = _sconv(xf.reshape(bp, tp, D_MODEL), sc_m, sc_w, rows=SCONV_ROWS)
    xs3, s_sc = _sconv(xs[:n_sample].reshape(bs, ts, D_MODEL), _tail_rows(state_sconv[0], n_sc), sc_w, rows=ts)
    xf = _ffn(xf3.reshape(n_frames, D_MODEL), ffn2[1])
    xs = _ffn(jnp.concatenate([xs3.reshape(n_sample, D_MODEL), xm[0]], 0), ffn2[1])

    y_prompt = xf.reshape(bp, tp, D_MODEL)
    y_sample = xs[:n_sample].reshape(bs, ts, D_MODEL)
    rope_lo, rope_hi = MLA_NOPE, MLA_NOPE + MLA_ROPE
    meta_ckv = jnp.broadcast_to(ckvn_s[n_sample:][None], (bp, N_META, MLA_KV_RANK))
    meta_kr = jnp.broadcast_to(krn_s[n_sample:, rope_lo:rope_hi][None], (bp, N_META, MLA_ROPE))
    p_ckv = jnp.concatenate([meta_ckv, ckvn_f.reshape(bp, tp, MLA_KV_RANK)], 1)
    p_kr = jnp.concatenate([meta_kr, krn_f[:, rope_lo:rope_hi].reshape(bp, tp, MLA_ROPE)], 1)
    n_gc = state_gdn_conv.shape[2]
    return (y_prompt, y_sample,
            p_ckv[None], p_kr[None], p_s[None], p_conv[:, HALO - n_gc:][None], p_sc[:, HALO - n_sc:][None],
            ckvn_s[:n_sample].reshape(bs, ts, MLA_KV_RANK)[None],
            krn_s[:n_sample, rope_lo:rope_hi].reshape(bs, ts, MLA_ROPE)[None],
            s_s[None], s_conv[:, HALO - n_gc:][None], s_sc[:, HALO - n_sc:][None])
```

```python
import functools

import jax
import jax.numpy as jnp
from jax import lax
from jax.experimental import pallas as pl
from jax.experimental.pallas import tpu as pltpu

F32 = jnp.float32
BF16 = jnp.bfloat16

D_MODEL = 1024
D_FF = 2816
CHUNK = 64
CHUNK_SHIFT = 6
N_META = 16
EPS = 1e-6
GDN_HEADS = 4
GDN_DK = 128
GDN_DV = 128
GDN_QKV = GDN_HEADS * (2 * GDN_DK + GDN_DV)
GDN_Z = GDN_HEADS * GDN_DV
MLA_HEADS = 8
MLA_NOPE = 64
MLA_ROPE = 32
MLA_V = 64
MLA_Q_RANK = 384
MLA_KV_RANK = 256
ROPE_THETA = 10000.0
PAST_LEN = 1024
LANES = 128
HALO = 8
MLA_PAD = MLA_HEADS * LANES
FF_BLOCK = 256
FF_CHUNKS = D_FF // FF_BLOCK
VMEM_LIMIT = 60000 * 1024
NEG_INF = float("-inf")
LOG2_E = 1.4426950408889634
MASK_LANE0 = MLA_NOPE + MLA_ROPE
MASK_BIG = 2.0 ** 100
ATTN_TQ = 512
ATTN_TK = 1024
FFN_ROWS = 1024
INPROJ_ROWS = 512
SCONV_ROWS = 1024
GDN_ROWS = 8 * CHUNK
VT_ROWS = MLA_V + 16
VT_TOTAL = MLA_HEADS * VT_ROWS
ONE_PASS_JUMP = 64.0

IN_QKV = 0
IN_Z = IN_QKV + GDN_QKV
IN_CQ = IN_Z + GDN_Z
IN_GATES_ROPE = IN_CQ + MLA_Q_RANK
IN_CKV = IN_GATES_ROPE + LANES
IN_TOTAL = IN_CKV + MLA_KV_RANK


def _params(n_axes):
    return pltpu.CompilerParams(dimension_semantics=("arbitrary",) * n_axes, vmem_limit_bytes=VMEM_LIMIT)


def _rms(x, g):
    return x * lax.rsqrt(jnp.mean(x * x, -1, keepdims=True) + EPS) * g


def _sigmoid(x):
    return 1.0 / (1.0 + jnp.exp(-x))


def _silu(x):
    return x * _sigmoid(x)


def _softplus(x):
    return jnp.maximum(x, 0.0) + jnp.log1p(jnp.exp(-jnp.abs(x)))


def _dot(a, b):
    return jnp.dot(a, b, preferred_element_type=F32)


def _dot_nt(a, b):
    return lax.dot_general(a, b, (((1,), (1,)), ((), ())), preferred_element_type=F32)


def _dot_tn(a, b):
    return lax.dot_general(a, b, (((0,), (0,)), ((), ())), preferred_element_type=F32)


def _shift_rows(x, history, k):
    rolled = pltpu.roll(x, k, 0)
    row = lax.broadcasted_iota(jnp.int32, (HALO, 1), 0)
    head = rolled[:HALO]
    for j in range(k):
        head = jnp.where(row == j, history[HALO - k + j:HALO - k + j + 1, :], head)
    return jnp.concatenate([head, rolled[HALO:]], 0)


def _full_spec(shape):
    zeros = (0,) * len(shape)
    return pl.BlockSpec(shape, lambda *_: zeros)


def _row_tile(rows, want):
    return want if rows % want == 0 else rows


def _ffn_body(x_ref, g_ref, wg_ref, wu_ref, wd_ref, *rest, mixed):
    if mixed:
        og_ref, omt_ref, wmix_ref, o_ref, acc_ref = rest
        x = x_ref[...] + _dot(og_ref[...], wmix_ref[:GDN_Z, :]) + _dot_tn(omt_ref[...], wmix_ref[GDN_Z:, :])
    else:
        o_ref, acc_ref = rest
        x = x_ref[...]
    h = _rms(x, g_ref[...]).astype(BF16)
    for c in range(FF_CHUNKS):
        cols = slice(c * FF_BLOCK, (c + 1) * FF_BLOCK)
        gate = _dot(h, wg_ref[:, cols])
        up = _dot(h, wu_ref[:, cols])
        act = (_silu(gate) * up).astype(BF16)
        down = _dot(act, wd_ref[cols, :])
        if c == 0:
            acc_ref[...] = down
        else:
            acc_ref[...] += down
    o_ref[...] = x + 0.5 * acc_ref[...]


def _ffn(x, w, mix=None):
    rows = x.shape[0]
    tm = _row_tile(rows, FFN_ROWS)
    g, wg, wu, wd, layer = w
    resident = lambda a: pl.BlockSpec(a.shape, lambda i: (0,) * a.ndim, pipeline_mode=pl.Buffered(1))
    of_layer = lambda a: pl.BlockSpec((None,) + a.shape[1:], lambda i: (layer, 0, 0), pipeline_mode=pl.Buffered(1))
    in_specs = [pl.BlockSpec((tm, D_MODEL), lambda i: (i, 0)), _full_spec(g.shape), of_layer(wg), of_layer(wu),
                of_layer(wd)]
    args = [x, g, wg, wu, wd]
    if mix is not None:
        og, omt, w_mix = mix
        tiles_per_batch = rows // omt.shape[0] // tm
        in_specs += [pl.BlockSpec((tm, GDN_Z), lambda i: (i, 0)),
                     pl.BlockSpec((None, GDN_Z, tm), lambda i: (i // tiles_per_batch, 0, i % tiles_per_batch)),
                     resident(w_mix)]
        args += [og, omt, w_mix]
    return pl.pallas_call(
        functools.partial(_ffn_body, mixed=mix is not None),
        grid=(rows // tm,),
        in_specs=in_specs,
        out_specs=pl.BlockSpec((tm, D_MODEL), lambda i: (i, 0)),
        out_shape=jax.ShapeDtypeStruct((rows, D_MODEL), F32),
        scratch_shapes=[pltpu.VMEM((tm, D_MODEL), F32)],
        compiler_params=_params(1),
        name="half_ffn",
    )(*args)


def _to_bf16_body(x_ref, o_ref):
    o_ref[...] = x_ref[...].astype(BF16)


def _to_bf16(a):
    a2 = a.reshape(-1, a.shape[-1])
    rows, cols = a2.shape
    tm = _row_tile(rows, 512)
    out = pl.pallas_call(
        _to_bf16_body,
        grid=(rows // tm,),
        in_specs=[pl.BlockSpec((tm, cols), lambda i: (i, 0))],
        out_specs=pl.BlockSpec((tm, cols), lambda i: (i, 0)),
        out_shape=jax.ShapeDtypeStruct((rows, cols), BF16),
        compiler_params=_params(1),
        name="weights_to_bf16",
    )(a2)
    return out.reshape(a.shape)


def _ffn_weights(norm, w_gate, w_up, w_down):
    stacks = (_to_bf16(w_gate), _to_bf16(w_up), _to_bf16(w_down))
    return [(norm[l].reshape(1, D_MODEL),) + stacks + (l,) for l in range(norm.shape[0])]


def _gdn_conv(x, history, cw):
    y = (_shift_rows(x, history, 3) * cw[0:1] + _shift_rows(x, history, 2) * cw[1:2]
         + _shift_rows(x, history, 1) * cw[2:3] + x * cw[3:4])
    return _silu(y)


class _Split:
    def __init__(self, x):
        self.x = x
        self._parts = None
        self._diag = None

    def parts(self):
        if self._parts is None:
            hi = self.x.astype(BF16)
            self._parts = (hi, (self.x - hi.astype(F32)).astype(BF16))
        return self._parts

    def diag_parts(self):
        if self._diag is None:
            first = lax.broadcasted_iota(jnp.int32, self.x.shape, 1) < CHUNK
            zero = jnp.zeros(self.x.shape, BF16)
            self._diag = tuple(jnp.concatenate([jnp.where(first, p, zero), jnp.where(first, zero, p)], 0)
                               for p in self.parts())
        return self._diag


def _split_products(pairs):
    parts = [(a.parts(), b.diag_parts()) for a, b in pairs]
    outs = []
    for (a_hi, a_lo), (b_hi, b_lo) in parts:
        a2 = jnp.concatenate([a_hi, a_lo], 0)
        r = _dot(a2, b_hi) + _dot(a2, b_lo)
        half = a_hi.shape[0]
        outs.append(_Split(r[:half] + r[half:]))
    return outs


def _unit_lower_inverses(ns, eye_f, m16, m32):
    k = len(ns)
    nd = [_Split(jnp.where(m16, n, 0.0)) for n in ns]
    n2 = _split_products([(x, x) for x in nd])
    lvl = _split_products([(x, x) for x in n2] +
                          [(_Split(eye_f - a.x), _Split(eye_f + b.x)) for a, b in zip(nd, n2)])
    n4, t = lvl[:k], lvl[k:]
    lvl = _split_products([(x, x) for x in n4] + [(a, _Split(eye_f + b.x)) for a, b in zip(t, n4)])
    n8, t = lvl[:k], lvl[k:]
    t = _split_products([(a, _Split(eye_f + b.x)) for a, b in zip(t, n8)])
    for keep in (jnp.logical_and(m32, jnp.logical_not(m16)), jnp.logical_not(m32)):
        off = [_Split(jnp.where(keep, n, 0.0)) for n in ns]
        x = _split_products(list(zip(off, t)))
        y = _split_products(list(zip(t, x)))
        t = [_Split(a.x - b.x) for a, b in zip(t, y)]
    return [a.x for a in t]


def _gdn_body(qkv_ref, z_ref, ab_ref, s0_ref, cprev_ref, cw_ref, alog_ref, dtb_ref, on_ref,
              og_ref, sout_ref, cout_ref, s_scr, halo_scr, *, rows, n_valid):
    t = pl.program_id(1)
    C = CHUNK
    row = lax.broadcasted_iota(jnp.int32, (rows, 1), 0)

    @pl.when(t == 0)
    def _():
        s_scr[...] = s0_ref[...]
        halo_scr[...] = cprev_ref[...]

    x = qkv_ref[...]
    y = _gdn_conv(x, halo_scr[...], cw_ref[...])
    last_rows = x[n_valid - HALO:n_valid, :]
    halo_scr[...] = last_rows

    ab = ab_ref[...]
    g_all = -jnp.exp(alog_ref[...]) * _softplus(ab + dtb_ref[...])
    b_all = _sigmoid(ab)
    if n_valid < rows:
        valid = row < n_valid
        y = jnp.where(valid, y, 0.0)
        g_all = jnp.where(valid, g_all, 0.0)
        b_all = jnp.where(valid, b_all, 0.0)

    ii = lax.broadcasted_iota(jnp.int32, (C, LANES), 0)
    pair_lane = lax.broadcasted_iota(jnp.int32, (C, LANES), 1)
    jj = pair_lane & (C - 1)
    first_head = pair_lane < C
    causal = ii >= jj
    strict = ii > jj
    eye_f = (ii == jj).astype(F32)
    m16 = (ii >> 4) == (jj >> 4)
    m32 = (ii >> 5) == (jj >> 5)
    on = on_ref[...]
    n_chunks = rows // C
    heads = range(GDN_HEADS)

    ri = lax.broadcasted_iota(jnp.int32, (rows, rows), 0)
    rj = lax.broadcasted_iota(jnp.int32, (rows, rows), 1)
    tril_blk = jnp.logical_and(ri >= rj, (ri >> CHUNK_SHIFT) == (rj >> CHUNK_SHIFT)).astype(BF16)
    g_hi = g_all.astype(BF16)
    g_r1 = g_all - g_hi.astype(F32)
    g_mid = g_r1.astype(BF16)
    g_lo = (g_r1 - g_mid.astype(F32)).astype(BF16)
    gcum = _dot(tril_blk, g_hi) + _dot(tril_blk, g_mid) + _dot(tril_blk, g_lo)
    egcum = jnp.exp(gcum)

    qn, kn, vn = [], [], []
    for h in heads:
        qh = y[:, h * GDN_DK:(h + 1) * GDN_DK]
        kh = y[:, GDN_HEADS * GDN_DK + h * GDN_DK:GDN_HEADS * GDN_DK + (h + 1) * GDN_DK]
        qn.append(qh * lax.rsqrt(jnp.sum(qh * qh, -1, keepdims=True) + EPS) * (GDN_DK ** -0.5))
        kn.append(kh * lax.rsqrt(jnp.sum(kh * kh, -1, keepdims=True) + EPS))
        vn.append(y[:, 2 * GDN_HEADS * GDN_DK + h * GDN_DV:2 * GDN_HEADS * GDN_DK + (h + 1) * GDN_DV])

    head_pairs = [(a, a + 1) for a in range(0, GDN_HEADS, 2)]
    chains = [(c, pair) for c in range(n_chunks) for pair in head_pairs]
    pre = {}
    n_mats = []
    for c in range(n_chunks):
        sl = slice(c * C, (c + 1) * C)
        gcum_t = gcum[sl].T
        for a, b in head_pairs:
            col = lambda arr, h: arr[sl, h:h + 1]
            gcol = jnp.where(first_head, col(gcum, a), col(gcum, b))
            beta2 = jnp.where(first_head, col(b_all, GDN_HEADS + a), col(b_all, GDN_HEADS + b))
            grow = jnp.concatenate([gcum_t[a:a + 1, :], gcum_t[b:b + 1, :]], -1)
            dec_causal = jnp.exp(jnp.where(causal, gcol - grow, NEG_INF))
            kb = {h: kn[h][sl].astype(BF16) for h in (a, b)}
            kk = jnp.concatenate([_dot_nt(kb[a], kb[a]), _dot_nt(kb[b], kb[b])], -1)
            n_mats.append(beta2 * kk * jnp.where(strict, dec_causal, 0.0))
            qk = jnp.concatenate([_dot_nt(qn[h][sl].astype(BF16), kb[h]) for h in (a, b)], -1)
            qk = (qk * dec_causal).astype(BF16)
            for h, lanes in ((a, slice(0, C)), (b, slice(C, 2 * C))):
                beta = col(b_all, GDN_HEADS + h)
                eg = col(egcum, h)
                g_last = gcum[(c + 1) * C - 1:(c + 1) * C, h:h + 1]
                qh, kh, vh = qn[h][sl], kn[h][sl], vn[h][sl]
                pre[c, h] = dict(
                    qk=qk[:, lanes],
                    rhs=jnp.concatenate([(beta * vh).astype(BF16), ((beta * eg) * kh).astype(BF16)], -1),
                    qd=(qh * eg).astype(BF16),
                    kt=(kh * jnp.exp(g_last - col(gcum, h))).astype(BF16),
                    decay=jnp.exp(g_last))
    for (c, (a, b)), tinv in zip(chains, _unit_lower_inverses(n_mats, eye_f, m16, m32)):
        tinv = tinv.astype(BF16)
        for h, lanes in ((a, slice(0, C)), (b, slice(C, 2 * C))):
            uw = _dot(tinv[:, lanes], pre[c, h]["rhs"])
            pre[c, h]["u"] = uw[:, :GDN_DV]
            pre[c, h]["wk"] = uw[:, GDN_DV:].astype(BF16)

    state = [s_scr[h] for h in heads]
    for c in range(n_chunks):
        sl = slice(c * C, (c + 1) * C)
        sb = [s.astype(BF16) for s in state]
        w = [(pre[c, h]["u"] - _dot(pre[c, h]["wk"], sb[h])).astype(BF16) for h in heads]
        o = [_dot(pre[c, h]["qd"], sb[h]) + _dot(pre[c, h]["qk"], w[h]) for h in heads]
        state = [state[h] * pre[c, h]["decay"] + _dot_tn(pre[c, h]["kt"], w[h]) for h in heads]
        for h in heads:
            gated = _rms(o[h], on) * _silu(z_ref[sl, h * GDN_DV:(h + 1) * GDN_DV])
            og_ref[sl, h * GDN_DV:(h + 1) * GDN_DV] = gated.astype(BF16)
    for h in heads:
        s_scr[h] = state[h]

    @pl.when(t == pl.num_programs(1) - 1)
    def _():
        sout_ref[...] = s_scr[...]
        cout_ref[...] = last_rows


def _gdn(qkv, z, ab, s0, cprev, w, *, rows, n_valid=None):
    b, t_len, _ = qkv.shape
    n_valid = rows if n_valid is None else n_valid
    conv_w, a_log, dt_bias, o_norm = w
    shared = s0.shape[0] == 1 and b > 1
    bsel = (lambda bi: 0) if shared else (lambda bi: bi)
    return pl.pallas_call(
        functools.partial(_gdn_body, rows=rows, n_valid=n_valid),
        grid=(b, t_len // rows),
        in_specs=[
            pl.BlockSpec((None, rows, GDN_QKV), lambda bi, ti: (bi, ti, 0)),
            pl.BlockSpec((None, rows, GDN_Z), lambda bi, ti: (bi, ti, 0)),
            pl.BlockSpec((None, rows, LANES), lambda bi, ti: (bi, ti, 0)),
            pl.BlockSpec((None, GDN_HEADS, GDN_DK, GDN_DV), lambda bi, ti: (bsel(bi), 0, 0, 0)),
            pl.BlockSpec((None, HALO, GDN_QKV), lambda bi, ti: (bsel(bi), 0, 0)),
            _full_spec(conv_w.shape), _full_spec(a_log.shape), _full_spec(dt_bias.shape), _full_spec(o_norm.shape),
        ],
        out_specs=[
            pl.BlockSpec((None, rows, GDN_Z), lambda bi, ti: (bi, ti, 0)),
            pl.BlockSpec((None, GDN_HEADS, GDN_DK, GDN_DV), lambda bi, ti: (bi, 0, 0, 0)),
            pl.BlockSpec((None, HALO, GDN_QKV), lambda bi, ti: (bi, 0, 0)),
        ],
        out_shape=[
            jax.ShapeDtypeStruct((b, t_len, GDN_Z), BF16),
            jax.ShapeDtypeStruct((b, GDN_HEADS, GDN_DK, GDN_DV), F32),
            jax.ShapeDtypeStruct((b, HALO, GDN_QKV), F32),
        ],
        scratch_shapes=[pltpu.VMEM((GDN_HEADS, GDN_DK, GDN_DV), F32), pltpu.VMEM((HALO, GDN_QKV), F32)],
        compiler_params=_params(2),
        name="gdn_mixer",
    )(qkv, z, ab, s0, cprev, conv_w, a_log, dt_bias, o_norm)


def _lane_masks():
    lane = lax.broadcasted_iota(jnp.int32, (1, LANES), 1)
    nope = lane < MLA_NOPE
    rope_a = jnp.logical_and(lane >= MLA_NOPE, lane < MLA_NOPE + MLA_ROPE // 2)
    rope_b = jnp.logical_and(lane >= MLA_NOPE + MLA_ROPE // 2, lane < MLA_NOPE + MLA_ROPE)
    return nope, rope_a, rope_b


def _rope_block(x, cos, sin, rope_a, rope_b):
    half = MLA_ROPE // 2
    rot = jnp.where(rope_a, -pltpu.roll(x, LANES - half, 1), jnp.where(rope_b, pltpu.roll(x, half, 1), 0.0))
    return x * cos + rot * sin


PAIR = 2 * LANES


def _twice(a):
    return jnp.concatenate([a, a], -1)


def _k_heads(k, shared, gk, k_out):
    for h in range(MLA_HEADS):
        x = k[:, h * LANES:(h + 1) * LANES]
        inv = lax.rsqrt(jnp.sum(x * x, -1, keepdims=True) / MLA_NOPE + EPS)
        k_out[:, h * LANES:(h + 1) * LANES] = (x * inv * gk + shared).astype(BF16)


def _inproj_body(x_ref, g_ref, w_ref, cos_ref, sin_ref, qn_ref, wuq_ref, kvn_ref, wk_ref, wvt_ref, vones_ref,
                 gq_ref, gk_ref, gmean_ref, rot_ref, qkv_ref, z_ref, ab_ref, q_out, k_out, vt_out, ckv_out, kr_out,
                 *, tiles_per_batch, key_tile_chunks):
    hn = _rms(x_ref[...], g_ref[...]).astype(BF16)
    qkv_ref[...] = _dot(hn, w_ref[:, IN_QKV:IN_Z])
    z_ref[...] = _dot(hn, w_ref[:, IN_Z:IN_CQ])
    cq_gates_rope = _dot(hn, w_ref[:, IN_CQ:IN_CKV])
    cq = cq_gates_rope[:, :MLA_Q_RANK]
    gates_rope = cq_gates_rope[:, MLA_Q_RANK:]
    ab_ref[...] = gates_rope
    ckv = _dot(hn, w_ref[:, IN_CKV:IN_TOTAL])

    nope, rope_a, rope_b = _lane_masks()
    krp = jnp.where(jnp.logical_or(rope_a, rope_b), gates_rope, 0.0)
    cos = cos_ref[...]
    sin = sin_ref[...]
    gk = gk_ref[...]
    gmean = gmean_ref[...]
    rot_mat = rot_ref[...]
    gq2 = _twice(gq_ref[...] * ((MLA_NOPE + MLA_ROPE) ** -0.5 * LOG2_E))
    cos2, sin2 = _twice(cos), _twice(sin)

    cqn = _rms(cq, qn_ref[...]).astype(BF16)
    q = _dot(cqn, wuq_ref[...])
    for p in range(MLA_HEADS // 2):
        x = q[:, p * PAIR:(p + 1) * PAIR]
        ms = _dot((x * x).astype(BF16), gmean)
        x = x * lax.rsqrt(ms + EPS) * gq2
        q_out[:, p * PAIR:(p + 1) * PAIR] = (x * cos2 + _dot(x.astype(BF16), rot_mat) * sin2).astype(BF16)

    ckvn = _rms(ckv, kvn_ref[...])
    ckv_out[...] = ckvn
    cb = ckvn.astype(BF16)
    krn = krp * lax.rsqrt(jnp.sum(krp * krp, -1, keepdims=True) / MLA_ROPE + EPS) * jnp.where(nope, 0.0, gk)
    kr_new = _rope_block(krn, cos, sin, rope_a, rope_b)
    kr_out[...] = kr_new
    vt_out[...] = (_dot_nt(wvt_ref[...], cb) + vones_ref[...]).astype(BF16)
    shared_lanes = kr_new
    if key_tile_chunks is not None:
        tm = kr_new.shape[0]
        pos = (pl.program_id(0) % tiles_per_batch) * tm + lax.broadcasted_iota(jnp.int32, (tm, 1), 0)
        lane = lax.broadcasted_iota(jnp.int32, (1, LANES), 1)
        chunk = (pos >> CHUNK_SHIFT) & (key_tile_chunks - 1)
        shared_lanes = kr_new + (lane - MASK_LANE0 == chunk).astype(F32)
    _k_heads(_dot(cb, wk_ref[...]), shared_lanes, jnp.where(nope, gk, 0.0), k_out)


def _vt_spec(tm, tiles_per_batch):
    return pl.BlockSpec((None, VT_TOTAL, tm), lambda i: (i // tiles_per_batch, 0, i % tiles_per_batch))


def _inproj(x, g, w_in, cos, sin, w, *, batch, key_tile=None):
    rows = x.shape[0]
    tm = _row_tile(rows, INPROJ_ROWS)
    tiles_per_batch = rows // batch // tm
    key_tile_chunks = None if key_tile is None else key_tile // CHUNK
    assert key_tile_chunks is None or (key_tile_chunks & (key_tile_chunks - 1) == 0
                                       and key_tile_chunks <= LANES - MASK_LANE0)
    q_norm, w_uq, kv_norm, w_k, w_vt, v_ones, gq, gk, gmean, rot_mat = w
    row_spec = lambda n: pl.BlockSpec((tm, n), lambda i: (i, 0))
    tab_spec = pl.BlockSpec((tm, LANES), lambda i: (i % tiles_per_batch, 0))
    consts = [g, w_in, q_norm, w_uq, kv_norm, w_k, w_vt, v_ones, gq, gk, gmean, rot_mat]
    return pl.pallas_call(
        functools.partial(_inproj_body, tiles_per_batch=tiles_per_batch, key_tile_chunks=key_tile_chunks),
        grid=(rows // tm,),
        in_specs=[row_spec(D_MODEL), _full_spec(g.shape), _full_spec(w_in.shape), tab_spec, tab_spec]
        + [_full_spec(a.shape) for a in consts[2:]],
        out_specs=[row_spec(GDN_QKV), row_spec(GDN_Z), row_spec(LANES), row_spec(MLA_PAD), row_spec(MLA_PAD),
                   _vt_spec(tm, tiles_per_batch), row_spec(MLA_KV_RANK), row_spec(LANES)],
        out_shape=[jax.ShapeDtypeStruct((rows, GDN_QKV), F32), jax.ShapeDtypeStruct((rows, GDN_Z), F32),
                   jax.ShapeDtypeStruct((rows, LANES), F32),
                   jax.ShapeDtypeStruct((rows, MLA_PAD), BF16), jax.ShapeDtypeStruct((rows, MLA_PAD), BF16),
                   jax.ShapeDtypeStruct((batch, VT_TOTAL, rows // batch), BF16),
                   jax.ShapeDtypeStruct((rows, MLA_KV_RANK), F32), jax.ShapeDtypeStruct((rows, LANES), F32)],
        compiler_params=_params(1),
        name="mixer_inproj",
    )(x, g, w_in, cos, sin, *consts[2:])


def _kv_up_body(ckv_ref, kr_ref, wk_ref, wvt_ref, vones_ref, gk_ref, k_out, vt_out):
    nope, _, _ = _lane_masks()
    cb = ckv_ref[...].astype(BF16)
    vt_out[...] = (_dot_nt(wvt_ref[...], cb) + vones_ref[...]).astype(BF16)
    _k_heads(_dot(cb, wk_ref[...]), kr_ref[...], jnp.where(nope, gk_ref[...], 0.0), k_out)


def _kv_up(ckv, krp, w_k, w_vt, v_ones, gk, *, batch):
    rows = ckv.shape[0]
    tm = _row_tile(rows, 512)
    tiles_per_batch = rows // batch // tm
    row_spec = lambda n: pl.BlockSpec((tm, n), lambda i: (i, 0))
    return pl.pallas_call(
        _kv_up_body,
        grid=(rows // tm,),
        in_specs=[row_spec(MLA_KV_RANK), row_spec(LANES), _full_spec(w_k.shape), _full_spec(w_vt.shape),
                  _full_spec(v_ones.shape), _full_spec(gk.shape)],
        out_specs=[row_spec(MLA_PAD), _vt_spec(tm, tiles_per_batch)],
        out_shape=[jax.ShapeDtypeStruct((rows, MLA_PAD), BF16),
                   jax.ShapeDtypeStruct((batch, VT_TOTAL, rows // batch), BF16)],
        compiler_params=_params(1),
        name="mla_kv_up",
    )(ckv, krp, w_k, w_vt, v_ones, gk)


def _attn_body(qi_ref, kj_ref, *refs, tq, tk, n_key_tiles, causal, nk_valid, has_ext):
    if has_ext:
        q_ref, k_ref, vt_ref, ke_ref, vte_ref, o_ref, m_scr, acc_scr, slot_ref = refs
    else:
        q_ref, k_ref, vt_ref, o_ref, m_scr, acc_scr, slot_ref = refs
    step = pl.program_id(1)
    i = qi_ref[step]
    j = kj_ref[step]
    last_j = ((i + 1) * tq + tk - 1) // tk - 1 if causal else n_key_tiles - 1
    head_lanes = [slice(h * LANES, (h + 1) * LANES) for h in range(MLA_HEADS)]

    def head_scores(k_blk, q_mask):
        if q_mask is None:
            q_heads = [q_ref[:, hs] for hs in head_lanes]
        else:
            q_heads = [q_ref[:, hs] + q_mask for hs in head_lanes]
        return [_dot_nt(k_blk[:, hs], qh) for hs, qh in zip(head_lanes, q_heads)]

    def values(vt_blk, h):
        return vt_blk[h * VT_ROWS:(h + 1) * VT_ROWS, :]

    def two_pass(k_blk, vt_blk, cur, bias=None, q_mask=None):
        scores = head_scores(k_blk, q_mask)
        for h in range(MLA_HEADS):
            s = scores[h] if bias is None else scores[h] + bias
            m_old = m_scr[cur, h][0:1]
            m_new = jnp.maximum(m_old, jnp.max(s, 0, keepdims=True))
            alpha = jnp.exp2(m_old - m_new)
            p = jnp.exp2(s - m_new)
            m_scr[cur, h] = jnp.broadcast_to(m_new, (HALO, tq))
            acc_scr[cur, h] = acc_scr[cur, h] * alpha + _dot(values(vt_blk, h), p.astype(BF16))

    def one_pass(k_blk, vt_blk, cur, q_mask=None):
        scores = head_scores(k_blk, q_mask)
        rise = None
        for h in range(MLA_HEADS):
            s = scores[h]
            m_old = m_scr[cur, h][0:1]
            p = jnp.exp2(s - m_old)
            m_tile = jnp.max(s, 0, keepdims=True)
            m_new = jnp.maximum(m_old, m_tile)
            rise = m_tile - m_old if rise is None else jnp.maximum(rise, m_tile - m_old)
            m_scr[1 - cur, h] = jnp.broadcast_to(m_new, (HALO, tq))
            acc_scr[1 - cur, h] = (acc_scr[cur, h] + _dot(values(vt_blk, h), p.astype(BF16))) * jnp.exp2(m_old - m_new)
        return jnp.max(rise)

    def guarded(k_blk, vt_blk, q_mask=None):
        cur = slot_ref[0]
        rise = one_pass(k_blk, vt_blk, cur, q_mask)

        @pl.when(rise <= ONE_PASS_JUMP)
        def _():
            slot_ref[0] = 1 - cur

        @pl.when(jnp.logical_not(rise <= ONE_PASS_JUMP))
        def _():
            two_pass(k_blk, vt_blk, cur, q_mask=q_mask)

    @pl.when(j == 0)
    def _():
        slot_ref[0] = 0
        m_scr[0] = jnp.full(m_scr.shape[1:], NEG_INF, F32)
        acc_scr[0] = jnp.zeros(acc_scr.shape[1:], F32)
        if has_ext:
            two_pass(ke_ref, vte_ref, 0)

    if nk_valid is not None:
        kpos = j * tk + lax.broadcasted_iota(jnp.int32, (tk, 1), 0)
        two_pass(k_ref, vt_ref, slot_ref[0], bias=jnp.where(kpos < nk_valid, 0.0, NEG_INF))
    elif not has_ext:
        two_pass(k_ref, vt_ref, slot_ref[0])
    elif causal:
        @pl.when(j < last_j)
        def _():
            guarded(k_ref, vt_ref)

        def diagonal(n_keys):
            lane = lax.broadcasted_iota(jnp.int32, (1, LANES), 1)
            q_chunk = (i * tq + lax.broadcasted_iota(jnp.int32, (tq, 1), 0)) >> CHUNK_SHIFT
            key_chunk = j * (tk // CHUNK) + lane - MASK_LANE0
            hidden = jnp.logical_and(jnp.logical_and(lane >= MASK_LANE0, lane < MASK_LANE0 + tk // CHUNK),
                                     key_chunk > q_chunk)
            guarded(k_ref.at[0:n_keys, :], vt_ref.at[:, 0:n_keys],
                    q_mask=jnp.where(hidden, -MASK_BIG, 0.0).astype(BF16))

        n_visible = (i + 1) * tq - j * tk
        for n_keys in range(tq, tk + 1, tq):
            pl.when(jnp.logical_and(j == last_j, n_visible == n_keys))(functools.partial(diagonal, n_keys))
    else:
        guarded(k_ref, vt_ref)

    @pl.when(j == last_j)
    def _():
        cur = slot_ref[0]
        for h in range(MLA_HEADS):
            acc = acc_scr[cur, h]
            o_ref[h * MLA_V:(h + 1) * MLA_V, :] = (acc[:MLA_V] / acc[MLA_V:MLA_V + 1]).astype(BF16)


def _attention(q, k, vt, ext, *, tq, tk, causal=False, nk_valid=None):
    b, nq, _ = q.shape
    nk = k.shape[1]
    n_key_tiles = nk // tk
    assert not causal or tk % tq == 0
    pairs = [(i, j) for i in range(nq // tq) for j in range(n_key_tiles) if not causal or j * tk < (i + 1) * tq]
    qi = jnp.asarray([p[0] for p in pairs], jnp.int32)
    kj = jnp.asarray([p[1] for p in pairs], jnp.int32)
    hv = MLA_HEADS * MLA_V
    in_specs = [
        pl.BlockSpec((None, tq, MLA_PAD), lambda bi, s, qi, kj: (bi, qi[s], 0)),
        pl.BlockSpec((None, tk, MLA_PAD), lambda bi, s, qi, kj: (bi, kj[s], 0)),
        pl.BlockSpec((None, VT_TOTAL, tk), lambda bi, s, qi, kj: (bi, 0, kj[s])),
    ]
    args = [q, k, vt]
    if ext is not None:
        ne = ext[0].shape[1]
        shared = ext[0].shape[0] == 1 and b > 1
        emap = (lambda bi, s, qi, kj: (0, 0, 0)) if shared else (lambda bi, s, qi, kj: (bi, 0, 0))
        in_specs += [pl.BlockSpec((None, ne, MLA_PAD), emap), pl.BlockSpec((None, VT_TOTAL, ne), emap)]
        args += list(ext)
    return pl.pallas_call(
        functools.partial(_attn_body, tq=tq, tk=tk, n_key_tiles=n_key_tiles, causal=causal, nk_valid=nk_valid,
                          has_ext=ext is not None),
        grid_spec=pltpu.PrefetchScalarGridSpec(
            num_scalar_prefetch=2,
            grid=(b, len(pairs)),
            in_specs=in_specs,
            out_specs=pl.BlockSpec((None, hv, tq), lambda bi, s, qi, kj: (bi, 0, qi[s])),
            scratch_shapes=[pltpu.VMEM((2, MLA_HEADS, HALO, tq), F32), pltpu.VMEM((2, MLA_HEADS, VT_ROWS, tq), F32),
                            pltpu.SMEM((1,), jnp.int32)]),
        out_shape=jax.ShapeDtypeStruct((b, hv, nq), BF16),
        compiler_params=_params(2),
        name="mla_attention",
    )(qi, kj, *args)


def _sconv_body(x_ref, g_ref, win_ref, cw_ref, wout_ref, prev_ref, o_ref, pout_ref, halo_scr, *, rows):
    t = pl.program_id(1)

    @pl.when(t == 0)
    def _():
        halo_scr[...] = prev_ref[...]

    x = x_ref[...]
    hn = _rms(x, g_ref[...]).astype(BF16)
    bg = _dot(hn, win_ref[:, :D_MODEL])
    u = _dot(hn, win_ref[:, D_MODEL:2 * D_MODEL]) * _dot(hn, win_ref[:, 2 * D_MODEL:])
    halo = halo_scr[...]
    u1 = _shift_rows(u, halo, 1)
    u2 = _shift_rows(u, halo, 2)
    cw = cw_ref[...]
    y = u2 * cw[0:1] + u1 * cw[1:2] + u * cw[2:3]
    last_rows = u[rows - HALO:rows, :]
    halo_scr[...] = last_rows
    o_ref[...] = x + _dot((bg * y).astype(BF16), wout_ref[...])

    @pl.when(t == pl.num_programs(1) - 1)
    def _():
        pout_ref[...] = last_rows


def _sconv(x, prev, w, *, rows):
    b, t_len, _ = x.shape
    g, w_in, conv_w, w_out = w
    shared = prev.shape[0] == 1 and b > 1
    bsel = (lambda bi: 0) if shared else (lambda bi: bi)
    return pl.pallas_call(
        functools.partial(_sconv_body, rows=rows),
        grid=(b, t_len // rows),
        in_specs=[pl.BlockSpec((None, rows, D_MODEL), lambda bi, ti: (bi, ti, 0)), _full_spec(g.shape),
                  _full_spec(w_in.shape), _full_spec(conv_w.shape), _full_spec(w_out.shape),
                  pl.BlockSpec((None, HALO, D_MODEL), lambda bi, ti: (bsel(bi), 0, 0))],
        out_specs=[pl.BlockSpec((None, rows, D_MODEL), lambda bi, ti: (bi, ti, 0)),
                   pl.BlockSpec((None, HALO, D_MODEL), lambda bi, ti: (bi, 0, 0))],
        out_shape=[jax.ShapeDtypeStruct((b, t_len, D_MODEL), F32), jax.ShapeDtypeStruct((b, HALO, D_MODEL), F32)],
        scratch_shapes=[pltpu.VMEM((HALO, D_MODEL), F32)],
        compiler_params=_params(2),
        name="sconv_mixer",
    )(x, g, w_in, conv_w, w_out, prev)


def _pad_lanes(a, before, total):
    return jnp.pad(a, [(0, 0)] * (a.ndim - 1) + [(before, total - before - a.shape[-1])])


def _rope_tables(pos):
    half = MLA_ROPE // 2
    inv = ROPE_THETA ** (-jnp.arange(half, dtype=F32) / half)
    ang = pos.astype(F32)[:, None] * inv
    cos, sin = jnp.cos(ang), jnp.sin(ang)
    n = pos.shape[0]
    tail = LANES - MLA_NOPE - MLA_ROPE
    cos_l = jnp.concatenate([jnp.ones((n, MLA_NOPE), F32), cos, cos, jnp.ones((n, tail), F32)], -1)
    sin_l = jnp.concatenate([jnp.zeros((n, MLA_NOPE), F32), sin, sin, jnp.zeros((n, tail), F32)], -1)
    return cos_l, sin_l


def _head_pair_matrices():
    i = jnp.arange(PAIR)[:, None]
    j = jnp.arange(PAIR)[None, :]
    same = (i // LANES) == (j // LANES)
    li, lj = i % LANES, j % LANES
    half = MLA_ROPE // 2
    rope_end = MLA_NOPE + MLA_ROPE
    in_rope = lambda l: (l >= MLA_NOPE) & (l < rope_end)
    gmean = jnp.where(same & (li < MLA_NOPE) & (lj < MLA_NOPE), 1.0 / MLA_NOPE,
                      jnp.where(same & in_rope(li) & in_rope(lj), 1.0 / MLA_ROPE, 0.0))
    first, second = in_rope(lj) & (lj < MLA_NOPE + half), in_rope(lj) & (lj >= MLA_NOPE + half)
    rot = jnp.where(same & first & (li == lj + half), -1.0, jnp.where(same & second & (li == lj - half), 1.0, 0.0))
    return gmean.astype(BF16), rot.astype(BF16)


def _tail_rows(a, n):
    return jnp.pad(a, ((0, 0), (HALO - n, 0), (0, 0)))


def kernel(x_prompt, x_sample, cache_mla_ckv, cache_mla_krope, state_gdn_S, state_gdn_conv, state_sconv, meta_tokens, ffn1_norm, ffn1_w_gate, ffn1_w_up, ffn1_w_down, ffn2_norm, ffn2_w_gate, ffn2_w_up, ffn2_w_down, mix_norm, ab_w_in, ab_w_out, gdn_conv_w, gdn_A_log, gdn_dt_bias, gdn_o_norm, mla_q_norm, mla_w_uq, mla_kv_norm, mla_w_ukv, mla_qn_norm, mla_qr_norm, mla_kn_norm, mla_kr_norm, sc_w_in, sc_conv_w, sc_w_out):
    bp, tp, _ = x_prompt.shape
    bs, ts, _ = x_sample.shape
    n_frames = bp * tp
    n_sample = bs * ts

    ffn1 = _ffn_weights(ffn1_norm, ffn1_w_gate, ffn1_w_up, ffn1_w_down)
    ffn2 = _ffn_weights(ffn2_norm, ffn2_w_gate, ffn2_w_up, ffn2_w_down)
    w_in = _to_bf16(ab_w_in)[0]
    s1 = GDN_QKV + GDN_Z
    c0 = s1 + 2 * GDN_HEADS
    c1 = c0 + MLA_Q_RANK
    c2 = c1 + MLA_KV_RANK
    gates_rope = jnp.concatenate([
        w_in[:, s1:c0], jnp.zeros((D_MODEL, MLA_NOPE - 2 * GDN_HEADS), BF16), w_in[:, c2:],
        jnp.zeros((D_MODEL, LANES - MLA_NOPE - MLA_ROPE), BF16)], -1)
    w_in_packed = jnp.concatenate([w_in[:, :s1], w_in[:, c0:c1], gates_rope, w_in[:, c1:c2]], -1)
    mix_g = mix_norm.reshape(2, 1, D_MODEL)
    gdn_w = (gdn_conv_w[0], _pad_lanes(gdn_A_log[0][None], 0, LANES), _pad_lanes(gdn_dt_bias[0][None], 0, LANES),
             gdn_o_norm[0][None])
    qdim = MLA_NOPE + MLA_ROPE
    w_uq = _pad_lanes(mla_w_uq[0].reshape(MLA_Q_RANK, MLA_HEADS, qdim), 0, LANES).reshape(MLA_Q_RANK, MLA_PAD)
    w_ukv = mla_w_ukv[0].reshape(MLA_KV_RANK, MLA_HEADS, MLA_NOPE + MLA_V)
    w_k = _pad_lanes(w_ukv[..., :MLA_NOPE], 0, LANES).reshape(MLA_KV_RANK, MLA_PAD)
    hv = MLA_HEADS * MLA_V
    w_vt = _pad_lanes(w_ukv[..., MLA_NOPE:], 0, VT_ROWS).reshape(MLA_KV_RANK, VT_TOTAL).T
    v_ones = (jnp.arange(VT_TOTAL) % VT_ROWS == MLA_V).astype(F32)[:, None]
    gq = _pad_lanes(jnp.concatenate([mla_qn_norm[0], mla_qr_norm[0]])[None], 0, LANES)
    gk = _pad_lanes(jnp.concatenate([mla_kn_norm[0], mla_kr_norm[0]])[None], 0, LANES)
    gmean, rot_mat = _head_pair_matrices()
    mla_w = (mla_q_norm[0][None], w_uq.astype(BF16), mla_kv_norm[0][None], w_k.astype(BF16), w_vt.astype(BF16), v_ones,
             gq, gk, gmean, rot_mat)
    w_out = ab_w_out[0].astype(BF16)
    sc_w = (mix_g[1], _to_bf16(sc_w_in)[0], sc_conv_w[0], sc_w_out[0].astype(BF16))

    xf = x_prompt.reshape(n_frames, D_MODEL)
    xs = jnp.concatenate([x_sample.reshape(n_sample, D_MODEL), meta_tokens.astype(F32)], 0)
    cos_f, sin_f = _rope_tables(N_META + jnp.arange(tp))
    pos_small = jnp.concatenate([jnp.tile(PAST_LEN + jnp.arange(ts), bs), jnp.arange(N_META)])
    cos_s, sin_s = _rope_tables(pos_small)

    xf = _ffn(xf, ffn1[0])
    xs = _ffn(xs, ffn1[0])
    qkv_f, z_f, ab_f, q_f, k_f, vt_f, ckvn_f, krn_f = _inproj(xf, mix_g[0], w_in_packed, cos_f, sin_f, mla_w,
                                                               batch=bp, key_tile=ATTN_TK)
    qkv_s, z_s, ab_s, q_s, k_s, vt_s, ckvn_s, krn_s = _inproj(xs, mix_g[0], w_in_packed, cos_s, sin_s, mla_w, batch=1)

    pad_meta = lambda a: jnp.pad(a[n_sample:], ((0, CHUNK - N_META), (0, 0)))[None]
    og_m, s_m, conv_m = _gdn(pad_meta(qkv_s), pad_meta(z_s), pad_meta(ab_s),
                             jnp.zeros((1, GDN_HEADS, GDN_DK, GDN_DV), F32), jnp.zeros((1, HALO, GDN_QKV), F32),
                             gdn_w, rows=CHUNK, n_valid=N_META)
    og_f, p_s, p_conv = _gdn(qkv_f.reshape(bp, tp, GDN_QKV), z_f.reshape(bp, tp, GDN_Z), ab_f.reshape(bp, tp, LANES),
                             s_m, conv_m, gdn_w, rows=GDN_ROWS)
    og_s, s_s, s_conv = _gdn(qkv_s[:n_sample].reshape(bs, ts, GDN_QKV), z_s[:n_sample].reshape(bs, ts, GDN_Z),
                             ab_s[:n_sample].reshape(bs, ts, LANES), state_gdn_S[0],
                             _tail_rows(state_gdn_conv[0], state_gdn_conv.shape[2]), gdn_w, rows=ts)

    k_c, vt_c = _kv_up(cache_mla_ckv[0].reshape(bs * PAST_LEN, MLA_KV_RANK),
                       _pad_lanes(cache_mla_krope[0].reshape(bs * PAST_LEN, MLA_ROPE), MLA_NOPE, LANES),
                       mla_w[3], mla_w[4], v_ones, gk, batch=bs)
    k_meta, vt_meta = k_s[n_sample:][None], vt_s[:, :, n_sample:]
    vt_new = vt_s[0, :, :n_sample].reshape(VT_TOTAL, bs, ts).transpose(1, 0, 2)
    omt_f = _attention(q_f.reshape(bp, tp, MLA_PAD), k_f.reshape(bp, tp, MLA_PAD), vt_f, (k_meta, vt_meta),
                       tq=ATTN_TQ, tk=ATTN_TK, causal=True)
    omt_s = _attention(q_s[:n_sample].reshape(bs, ts, MLA_PAD), k_c.reshape(bs, PAST_LEN, MLA_PAD), vt_c,
                       (k_s[:n_sample].reshape(bs, ts, MLA_PAD), vt_new), tq=ts, tk=512)
    omt_m = _attention(jnp.pad(q_s[n_sample:], ((0, LANES - N_META), (0, 0)))[None],
                       jnp.pad(k_s[n_sample:], ((0, CHUNK - N_META), (0, 0)))[None],
                       jnp.pad(vt_meta, ((0, 0), (0, 0), (0, CHUNK - N_META))),
                       None, tq=LANES, tk=CHUNK, nk_valid=N_META)

    og_small = jnp.concatenate([og_s.reshape(n_sample, GDN_Z), og_m[0, :N_META]], 0)
    omt_small = jnp.concatenate([omt_s.transpose(1, 0, 2).reshape(hv, n_sample), omt_m[0, :, :N_META]], -1)[None]
    xf = _ffn(xf, ffn2[0], mix=(og_f.reshape(n_frames, GDN_Z), omt_f, w_out))
    xs = _ffn(xs, ffn2[0], mix=(og_small, omt_small, w_out))

    xf = _ffn(xf, ffn1[1])
    xs = _ffn(xs, ffn1[1])
    n_sc = state_sconv.shape[2]
    xm, sc_m = _sconv(xs[n_sample:][None], jnp.zeros((1, HALO, D_MODEL), F32), sc_w, rows=N_META)
    xf3, p_sc = _sconv(xf.reshape(bp, tp, D_MODEL), sc_m, sc_w, rows=SCONV_ROWS)
    xs3, s_sc = _sconv(xs[:n_sample].reshape(bs, ts, D_MODEL), _tail_rows(state_sconv[0], n_sc), sc_w, rows=ts)
    xf = _ffn(xf3.reshape(n_frames, D_MODEL), ffn2[1])
    xs = _ffn(jnp.concatenate([xs3.reshape(n_sample, D_MODEL), xm[0]], 0), ffn2[1])

    y_prompt = xf.reshape(bp, tp, D_MODEL)
    y_sample = xs[:n_sample].reshape(bs, ts, D_MODEL)
    rope_lo, rope_hi = MLA_NOPE, MLA_NOPE + MLA_ROPE
    meta_ckv = jnp.broadcast_to(ckvn_s[n_sample:][None], (bp, N_META, MLA_KV_RANK))
    meta_kr = jnp.broadcast_to(krn_s[n_sample:, rope_lo:rope_hi][None], (bp, N_META, MLA_ROPE))
    p_ckv = jnp.concatenate([meta_ckv, ckvn_f.reshape(bp, tp, MLA_KV_RANK)], 1)
    p_kr = jnp.concatenate([meta_kr, krn_f[:, rope_lo:rope_hi].reshape(bp, tp, MLA_ROPE)], 1)
    n_gc = state_gdn_conv.shape[2]
    return (y_prompt, y_sample,
            p_ckv[None], p_kr[None], p_s[None], p_conv[:, HALO - n_gc:][None], p_sc[:, HALO - n_sc:][None],
            ckvn_s[:n_sample].reshape(bs, ts, MLA_KV_RANK)[None],
            krn_s[:n_sample, rope_lo:rope_hi].reshape(bs, ts, MLA_ROPE)[None],
            s_s[None], s_conv[:, HALO - n_gc:][None], s_sc[:, HALO - n_sc:][None])
```

```python
import functools

import jax
import jax.numpy as jnp
from jax import lax
from jax.experimental import pallas as pl
from jax.experimental.pallas import tpu as pltpu

F32 = jnp.float32
BF16 = jnp.bfloat16

D_MODEL = 1024
D_FF = 2816
CHUNK = 64
CHUNK_SHIFT = 6
N_META = 16
EPS = 1e-6
GDN_HEADS = 4
GDN_DK = 128
GDN_DV = 128
GDN_QKV = GDN_HEADS * (2 * GDN_DK + GDN_DV)
GDN_Z = GDN_HEADS * GDN_DV
MLA_HEADS = 8
MLA_NOPE = 64
MLA_ROPE = 32
MLA_V = 64
MLA_Q_RANK = 384
MLA_KV_RANK = 256
ROPE_THETA = 10000.0
PAST_LEN = 1024
LANES = 128
HALO = 8
MLA_PAD = MLA_HEADS * LANES
FF_BLOCK = 256
FF_CHUNKS = D_FF // FF_BLOCK
VMEM_LIMIT = 60000 * 1024
NEG_INF = float("-inf")
LOG2_E = 1.4426950408889634
MASK_LANE0 = MLA_NOPE + MLA_ROPE
MASK_BIG = 2.0 ** 100
ATTN_TQ = 512
ATTN_TK = 1024
FFN_ROWS = 1024
INPROJ_ROWS = 512
SCONV_ROWS = 1024
GDN_ROWS = 8 * CHUNK
GDN_GROUP = 4
VT_ROWS = MLA_V + 16
VT_TOTAL = MLA_HEADS * VT_ROWS
ONE_PASS_JUMP = 64.0

IN_QKV = 0
IN_Z = IN_QKV + GDN_QKV
IN_CQ = IN_Z + GDN_Z
IN_GATES_ROPE = IN_CQ + MLA_Q_RANK
IN_CKV = IN_GATES_ROPE + LANES
IN_TOTAL = IN_CKV + MLA_KV_RANK


def _params(n_axes):
    return pltpu.CompilerParams(dimension_semantics=("arbitrary",) * n_axes, vmem_limit_bytes=VMEM_LIMIT)


def _rms(x, g):
    return x * lax.rsqrt(jnp.mean(x * x, -1, keepdims=True) + EPS) * g


def _sigmoid(x):
    return 1.0 / (1.0 + jnp.exp(-x))


def _silu(x):
    return x * _sigmoid(x)


def _softplus(x):
    return jnp.maximum(x, 0.0) + jnp.log1p(jnp.exp(-jnp.abs(x)))


def _dot(a, b):
    return jnp.dot(a, b, preferred_element_type=F32)


def _dot_nt(a, b):
    return lax.dot_general(a, b, (((1,), (1,)), ((), ())), preferred_element_type=F32)


def _dot_tn(a, b):
    return lax.dot_general(a, b, (((0,), (0,)), ((), ())), preferred_element_type=F32)


def _shift_rows(x, history, k):
    rolled = pltpu.roll(x, k, 0)
    row = lax.broadcasted_iota(jnp.int32, (HALO, 1), 0)
    head = rolled[:HALO]
    for j in range(k):
        head = jnp.where(row == j, history[HALO - k + j:HALO - k + j + 1, :], head)
    return jnp.concatenate([head, rolled[HALO:]], 0)


def _full_spec(shape):
    zeros = (0,) * len(shape)
    return pl.BlockSpec(shape, lambda *_: zeros)


def _row_tile(rows, want):
    return want if rows % want == 0 else rows


def _ffn_body(x_ref, g_ref, wg_ref, wu_ref, wd_ref, *rest, mixed):
    if mixed:
        og_ref, omt_ref, wmix_ref, o_ref, acc_ref = rest
        x = x_ref[...] + _dot(og_ref[...], wmix_ref[:GDN_Z, :]) + _dot_tn(omt_ref[...], wmix_ref[GDN_Z:, :])
    else:
        o_ref, acc_ref = rest
        x = x_ref[...]
    h = _rms(x, g_ref[...]).astype(BF16)
    for c in range(FF_CHUNKS):
        cols = slice(c * FF_BLOCK, (c + 1) * FF_BLOCK)
        gate = _dot(h, wg_ref[:, cols])
        up = _dot(h, wu_ref[:, cols])
        act = (_silu(gate) * up).astype(BF16)
        down = _dot(act, wd_ref[cols, :])
        if c == 0:
            acc_ref[...] = down
        else:
            acc_ref[...] += down
    o_ref[...] = x + 0.5 * acc_ref[...]


def _ffn(x, w, mix=None):
    rows = x.shape[0]
    tm = _row_tile(rows, FFN_ROWS)
    g, wg, wu, wd, layer = w
    resident = lambda a: pl.BlockSpec(a.shape, lambda i: (0,) * a.ndim, pipeline_mode=pl.Buffered(1))
    of_layer = lambda a: pl.BlockSpec((None,) + a.shape[1:], lambda i: (layer, 0, 0), pipeline_mode=pl.Buffered(1))
    in_specs = [pl.BlockSpec((tm, D_MODEL), lambda i: (i, 0)), _full_spec(g.shape), of_layer(wg), of_layer(wu),
                of_layer(wd)]
    args = [x, g, wg, wu, wd]
    if mix is not None:
        og, omt, w_mix = mix
        tiles_per_batch = rows // omt.shape[0] // tm
        in_specs += [pl.BlockSpec((tm, GDN_Z), lambda i: (i, 0)),
                     pl.BlockSpec((None, GDN_Z, tm), lambda i: (i // tiles_per_batch, 0, i % tiles_per_batch)),
                     resident(w_mix)]
        args += [og, omt, w_mix]
    return pl.pallas_call(
        functools.partial(_ffn_body, mixed=mix is not None),
        grid=(rows // tm,),
        in_specs=in_specs,
        out_specs=pl.BlockSpec((tm, D_MODEL), lambda i: (i, 0)),
        out_shape=jax.ShapeDtypeStruct((rows, D_MODEL), F32),
        scratch_shapes=[pltpu.VMEM((tm, D_MODEL), F32)],
        compiler_params=_params(1),
        name="half_ffn",
    )(*args)


def _to_bf16_body(x_ref, o_ref):
    o_ref[...] = x_ref[...].astype(BF16)


def _to_bf16(a):
    a2 = a.reshape(-1, a.shape[-1])
    rows, cols = a2.shape
    tm = _row_tile(rows, 512)
    out = pl.pallas_call(
        _to_bf16_body,
        grid=(rows // tm,),
        in_specs=[pl.BlockSpec((tm, cols), lambda i: (i, 0))],
        out_specs=pl.BlockSpec((tm, cols), lambda i: (i, 0)),
        out_shape=jax.ShapeDtypeStruct((rows, cols), BF16),
        compiler_params=_params(1),
        name="weights_to_bf16",
    )(a2)
    return out.reshape(a.shape)


def _ffn_weights(norm, w_gate, w_up, w_down):
    stacks = (_to_bf16(w_gate), _to_bf16(w_up), _to_bf16(w_down))
    return [(norm[l].reshape(1, D_MODEL),) + stacks + (l,) for l in range(norm.shape[0])]


def _gdn_conv(x, history, cw):
    y = (_shift_rows(x, history, 3) * cw[0:1] + _shift_rows(x, history, 2) * cw[1:2]
         + _shift_rows(x, history, 1) * cw[2:3] + x * cw[3:4])
    return _silu(y)


class _Split:
    def __init__(self, x):
        self.x = x
        self._parts = None
        self._diag = None

    def parts(self):
        if self._parts is None:
            hi = self.x.astype(BF16)
            self._parts = (hi, (self.x - hi.astype(F32)).astype(BF16))
        return self._parts

    def diag_parts(self):
        if self._diag is None:
            first = lax.broadcasted_iota(jnp.int32, self.x.shape, 1) < CHUNK
            zero = jnp.zeros(self.x.shape, BF16)
            self._diag = tuple(jnp.concatenate([jnp.where(first, p, zero), jnp.where(first, zero, p)], 0)
                               for p in self.parts())
        return self._diag


def _split_products(pairs):
    parts = [(a.parts(), b.diag_parts()) for a, b in pairs]
    outs = []
    for (a_hi, a_lo), (b_hi, b_lo) in parts:
        a2 = jnp.concatenate([a_hi, a_lo], 0)
        r = _dot(a2, b_hi) + _dot(a2, b_lo)
        half = a_hi.shape[0]
        outs.append(_Split(r[:half] + r[half:]))
    return outs


def _unit_lower_inverses(ns, eye_f, block_masks):
    k = len(ns)
    nd = [_Split(jnp.where(block_masks[0], n, 0.0)) for n in ns]
    n2 = _split_products([(x, x) for x in nd])
    yield None
    lvl = _split_products([(x, x) for x in n2] +
                          [(_Split(eye_f - a.x), _Split(eye_f + b.x)) for a, b in zip(nd, n2)])
    yield None
    n4, t = lvl[:k], lvl[k:]
    t = _split_products([(a, _Split(eye_f + b.x)) for a, b in zip(t, n4)])
    yield None
    for done, merged in zip(block_masks, block_masks[1:] + [None]):
        keep = jnp.logical_not(done) if merged is None else jnp.logical_and(merged, jnp.logical_not(done))
        off = [_Split(jnp.where(keep, n, 0.0)) for n in ns]
        x = _split_products(list(zip(off, t)))
        yield None
        y = _split_products(list(zip(t, x)))
        yield None
        t = [_Split(a.x - b.x) for a, b in zip(t, y)]
    yield [a.x for a in t]


def _gdn_body(qkv_ref, z_ref, ab_ref, s0_ref, cprev_ref, cw_ref, alog_ref, dtb_ref, on_ref,
              og_ref, sout_ref, cout_ref, s_scr, halo_scr, *, rows, n_valid):
    t = pl.program_id(1)
    C = CHUNK
    row = lax.broadcasted_iota(jnp.int32, (rows, 1), 0)

    @pl.when(t == 0)
    def _():
        s_scr[...] = s0_ref[...]
        halo_scr[...] = cprev_ref[...]

    x = qkv_ref[...]
    y = _gdn_conv(x, halo_scr[...], cw_ref[...])
    last_rows = x[n_valid - HALO:n_valid, :]
    halo_scr[...] = last_rows

    ab = ab_ref[...]
    g_all = -jnp.exp(alog_ref[...]) * _softplus(ab + dtb_ref[...])
    b_all = _sigmoid(ab)
    if n_valid < rows:
        valid = row < n_valid
        y = jnp.where(valid, y, 0.0)
        g_all = jnp.where(valid, g_all, 0.0)
        b_all = jnp.where(valid, b_all, 0.0)

    ii = lax.broadcasted_iota(jnp.int32, (C, LANES), 0)
    pair_lane = lax.broadcasted_iota(jnp.int32, (C, LANES), 1)
    jj = pair_lane & (C - 1)
    first_head = pair_lane < C
    causal = ii >= jj
    strict = ii > jj
    eye_f = (ii == jj).astype(F32)
    block_masks = [(ii >> s) == (jj >> s) for s in (3, 4, 5)]
    on = on_ref[...]
    n_chunks = rows // C
    heads = range(GDN_HEADS)

    ri = lax.broadcasted_iota(jnp.int32, (rows, rows), 0)
    rj = lax.broadcasted_iota(jnp.int32, (rows, rows), 1)
    tril_blk = jnp.logical_and(ri >= rj, (ri >> CHUNK_SHIFT) == (rj >> CHUNK_SHIFT)).astype(BF16)
    g_hi = g_all.astype(BF16)
    g_r1 = g_all - g_hi.astype(F32)
    g_mid = g_r1.astype(BF16)
    g_lo = (g_r1 - g_mid.astype(F32)).astype(BF16)
    gcum = _dot(tril_blk, g_hi) + _dot(tril_blk, g_mid) + _dot(tril_blk, g_lo)
    egcum = jnp.exp(gcum)

    qn, kn, vn = [], [], []
    for h in heads:
        qh = y[:, h * GDN_DK:(h + 1) * GDN_DK]
        kh = y[:, GDN_HEADS * GDN_DK + h * GDN_DK:GDN_HEADS * GDN_DK + (h + 1) * GDN_DK]
        qn.append(qh * lax.rsqrt(jnp.sum(qh * qh, -1, keepdims=True) + EPS) * (GDN_DK ** -0.5))
        kn.append(kh * lax.rsqrt(jnp.sum(kh * kh, -1, keepdims=True) + EPS))
        vn.append(y[:, 2 * GDN_HEADS * GDN_DK + h * GDN_DV:2 * GDN_HEADS * GDN_DK + (h + 1) * GDN_DV])

    head_pairs = [(a, a + 1) for a in range(0, GDN_HEADS, 2)]
    pre = {}

    def chunk_local(chunks):
        chains = [(c, pair) for c in chunks for pair in head_pairs]
        n_mats = []
        for c in chunks:
            sl = slice(c * C, (c + 1) * C)
            gcum_t = gcum[sl].T
            for a, b in head_pairs:
                col = lambda arr, h: arr[sl, h:h + 1]
                gcol = jnp.where(first_head, col(gcum, a), col(gcum, b))
                beta2 = jnp.where(first_head, col(b_all, GDN_HEADS + a), col(b_all, GDN_HEADS + b))
                grow = jnp.concatenate([gcum_t[a:a + 1, :], gcum_t[b:b + 1, :]], -1)
                dec_causal = jnp.exp(jnp.where(causal, gcol - grow, NEG_INF))
                kb = {h: kn[h][sl].astype(BF16) for h in (a, b)}
                kk = jnp.concatenate([_dot_nt(kb[a], kb[a]), _dot_nt(kb[b], kb[b])], -1)
                n_mats.append(beta2 * kk * jnp.where(strict, dec_causal, 0.0))
                qk = jnp.concatenate([_dot_nt(qn[h][sl].astype(BF16), kb[h]) for h in (a, b)], -1)
                qk = (qk * dec_causal).astype(BF16)
                for h, lanes in ((a, slice(0, C)), (b, slice(C, 2 * C))):
                    beta = col(b_all, GDN_HEADS + h)
                    eg = col(egcum, h)
                    g_last = gcum[(c + 1) * C - 1:(c + 1) * C, h:h + 1]
                    qh, kh, vh = qn[h][sl], kn[h][sl], vn[h][sl]
                    pre[c, h] = dict(
                        qk=qk[:, lanes],
                        rhs=jnp.concatenate([(beta * vh).astype(BF16), ((beta * eg) * kh).astype(BF16)], -1),
                        qd=(qh * eg).astype(BF16),
                        kt=(kh * jnp.exp(g_last - col(gcum, h))).astype(BF16),
                        decay=jnp.exp(g_last))
            yield
        inverses = None
        for inverses in _unit_lower_inverses(n_mats, eye_f, block_masks):
            yield
        for (c, (a, b)), tinv in zip(chains, inverses):
            tinv = tinv.astype(BF16)
            for h, lanes in ((a, slice(0, C)), (b, slice(C, 2 * C))):
                uw = _dot(tinv[:, lanes], pre[c, h]["rhs"])
                pre[c, h]["u"] = uw[:, :GDN_DV]
                pre[c, h]["wk"] = uw[:, GDN_DV:].astype(BF16)

    state = [s_scr[h] for h in heads]

    def recurrence(chunks):
        for c in chunks:
            sl = slice(c * C, (c + 1) * C)
            sb = [s.astype(BF16) for s in state]
            w = [(pre[c, h]["u"] - _dot(pre[c, h]["wk"], sb[h])).astype(BF16) for h in heads]
            yield
            o = [_dot(pre[c, h]["qd"], sb[h]) + _dot(pre[c, h]["qk"], w[h]) for h in heads]
            for h in heads:
                state[h] = state[h] * pre[c, h]["decay"] + _dot_tn(pre[c, h]["kt"], w[h])
            yield
            for h in heads:
                gated = _rms(o[h], on) * _silu(z_ref[sl, h * GDN_DV:(h + 1) * GDN_DV])
                og_ref[sl, h * GDN_DV:(h + 1) * GDN_DV] = gated.astype(BF16)
            yield

    group = min(n_chunks, GDN_GROUP)
    pending = iter(())
    for g0 in range(0, n_chunks, group):
        chunks = range(g0, g0 + group)
        for _ in chunk_local(chunks):
            next(pending, None)
        for _ in pending:
            pass
        pending = recurrence(chunks)
    for _ in pending:
        pass
    for h in heads:
        s_scr[h] = state[h]

    @pl.when(t == pl.num_programs(1) - 1)
    def _():
        sout_ref[...] = s_scr[...]
        cout_ref[...] = last_rows


def _gdn(qkv, z, ab, s0, cprev, w, *, rows, n_valid=None):
    b, t_len, _ = qkv.shape
    n_valid = rows if n_valid is None else n_valid
    conv_w, a_log, dt_bias, o_norm = w
    shared = s0.shape[0] == 1 and b > 1
    bsel = (lambda bi: 0) if shared else (lambda bi: bi)
    return pl.pallas_call(
        functools.partial(_gdn_body, rows=rows, n_valid=n_valid),
        grid=(b, t_len // rows),
        in_specs=[
            pl.BlockSpec((None, rows, GDN_QKV), lambda bi, ti: (bi, ti, 0)),
            pl.BlockSpec((None, rows, GDN_Z), lambda bi, ti: (bi, ti, 0)),
            pl.BlockSpec((None, rows, LANES), lambda bi, ti: (bi, ti, 0)),
            pl.BlockSpec((None, GDN_HEADS, GDN_DK, GDN_DV), lambda bi, ti: (bsel(bi), 0, 0, 0)),
            pl.BlockSpec((None, HALO, GDN_QKV), lambda bi, ti: (bsel(bi), 0, 0)),
            _full_spec(conv_w.shape), _full_spec(a_log.shape), _full_spec(dt_bias.shape), _full_spec(o_norm.shape),
        ],
        out_specs=[
            pl.BlockSpec((None, rows, GDN_Z), lambda bi, ti: (bi, ti, 0)),
            pl.BlockSpec((None, GDN_HEADS, GDN_DK, GDN_DV), lambda bi, ti: (bi, 0, 0, 0)),
            pl.BlockSpec((None, HALO, GDN_QKV), lambda bi, ti: (bi, 0, 0)),
        ],
        out_shape=[
            jax.ShapeDtypeStruct((b, t_len, GDN_Z), BF16),
            jax.ShapeDtypeStruct((b, GDN_HEADS, GDN_DK, GDN_DV), F32),
            jax.ShapeDtypeStruct((b, HALO, GDN_QKV), F32),
        ],
        scratch_shapes=[pltpu.VMEM((GDN_HEADS, GDN_DK, GDN_DV), F32), pltpu.VMEM((HALO, GDN_QKV), F32)],
        compiler_params=_params(2),
        name="gdn_mixer",
    )(qkv, z, ab, s0, cprev, conv_w, a_log, dt_bias, o_norm)


def _lane_masks():
    lane = lax.broadcasted_iota(jnp.int32, (1, LANES), 1)
    nope = lane < MLA_NOPE
    rope_a = jnp.logical_and(lane >= MLA_NOPE, lane < MLA_NOPE + MLA_ROPE // 2)
    rope_b = jnp.logical_and(lane >= MLA_NOPE + MLA_ROPE // 2, lane < MLA_NOPE + MLA_ROPE)
    return nope, rope_a, rope_b


def _rope_block(x, cos, sin, rope_a, rope_b):
    half = MLA_ROPE // 2
    rot = jnp.where(rope_a, -pltpu.roll(x, LANES - half, 1), jnp.where(rope_b, pltpu.roll(x, half, 1), 0.0))
    return x * cos + rot * sin


PAIR = 2 * LANES


def _twice(a):
    return jnp.concatenate([a, a], -1)


def _k_heads(k, shared, gk, k_out):
    for h in range(MLA_HEADS):
        x = k[:, h * LANES:(h + 1) * LANES]
        inv = lax.rsqrt(jnp.sum(x * x, -1, keepdims=True) / MLA_NOPE + EPS)
        k_out[:, h * LANES:(h + 1) * LANES] = (x * inv * gk + shared).astype(BF16)


def _inproj_body(x_ref, g_ref, w_ref, cos_ref, sin_ref, qn_ref, wuq_ref, kvn_ref, wk_ref, wvt_ref, vones_ref,
                 gq_ref, gk_ref, gmean_ref, rot_ref, qkv_ref, z_ref, ab_ref, q_out, k_out, vt_out, ckv_out, kr_out,
                 *, tiles_per_batch, key_tile_chunks):
    hn = _rms(x_ref[...], g_ref[...]).astype(BF16)
    qkv_ref[...] = _dot(hn, w_ref[:, IN_QKV:IN_Z])
    z_ref[...] = _dot(hn, w_ref[:, IN_Z:IN_CQ])
    cq_gates_rope = _dot(hn, w_ref[:, IN_CQ:IN_CKV])
    cq = cq_gates_rope[:, :MLA_Q_RANK]
    gates_rope = cq_gates_rope[:, MLA_Q_RANK:]
    ab_ref[...] = gates_rope
    ckv = _dot(hn, w_ref[:, IN_CKV:IN_TOTAL])

    nope, rope_a, rope_b = _lane_masks()
    krp = jnp.where(jnp.logical_or(rope_a, rope_b), gates_rope, 0.0)
    cos = cos_ref[...]
    sin = sin_ref[...]
    gk = gk_ref[...]
    gmean = gmean_ref[...]
    rot_mat = rot_ref[...]
    gq2 = _twice(gq_ref[...] * ((MLA_NOPE + MLA_ROPE) ** -0.5 * LOG2_E))
    cos2, sin2 = _twice(cos), _twice(sin)

    cqn = _rms(cq, qn_ref[...]).astype(BF16)
    q = _dot(cqn, wuq_ref[...])
    for p in range(MLA_HEADS // 2):
        x = q[:, p * PAIR:(p + 1) * PAIR]
        ms = _dot((x * x).astype(BF16), gmean)
        x = x * lax.rsqrt(ms + EPS) * gq2
        q_out[:, p * PAIR:(p + 1) * PAIR] = (x * cos2 + _dot(x.astype(BF16), rot_mat) * sin2).astype(BF16)

    ckvn = _rms(ckv, kvn_ref[...])
    ckv_out[...] = ckvn
    cb = ckvn.astype(BF16)
    krn = krp * lax.rsqrt(jnp.sum(krp * krp, -1, keepdims=True) / MLA_ROPE + EPS) * jnp.where(nope, 0.0, gk)
    kr_new = _rope_block(krn, cos, sin, rope_a, rope_b)
    kr_out[...] = kr_new
    vt_out[...] = (_dot_nt(wvt_ref[...], cb) + vones_ref[...]).astype(BF16)
    shared_lanes = kr_new
    if key_tile_chunks is not None:
        tm = kr_new.shape[0]
        pos = (pl.program_id(0) % tiles_per_batch) * tm + lax.broadcasted_iota(jnp.int32, (tm, 1), 0)
        lane = lax.broadcasted_iota(jnp.int32, (1, LANES), 1)
        chunk = (pos >> CHUNK_SHIFT) & (key_tile_chunks - 1)
        shared_lanes = kr_new + (lane - MASK_LANE0 == chunk).astype(F32)
    _k_heads(_dot(cb, wk_ref[...]), shared_lanes, jnp.where(nope, gk, 0.0), k_out)


def _vt_spec(tm, tiles_per_batch):
    return pl.BlockSpec((None, VT_TOTAL, tm), lambda i: (i // tiles_per_batch, 0, i % tiles_per_batch))


def _inproj(x, g, w_in, cos, sin, w, *, batch, key_tile=None):
    rows = x.shape[0]
    tm = _row_tile(rows, INPROJ_ROWS)
    tiles_per_batch = rows // batch // tm
    key_tile_chunks = None if key_tile is None else key_tile // CHUNK
    assert key_tile_chunks is None or (key_tile_chunks & (key_tile_chunks - 1) == 0
                                       and key_tile_chunks <= LANES - MASK_LANE0)
    q_norm, w_uq, kv_norm, w_k, w_vt, v_ones, gq, gk, gmean, rot_mat = w
    row_spec = lambda n: pl.BlockSpec((tm, n), lambda i: (i, 0))
    tab_spec = pl.BlockSpec((tm, LANES), lambda i: (i % tiles_per_batch, 0))
    consts = [g, w_in, q_norm, w_uq, kv_norm, w_k, w_vt, v_ones, gq, gk, gmean, rot_mat]
    return pl.pallas_call(
        functools.partial(_inproj_body, tiles_per_batch=tiles_per_batch, key_tile_chunks=key_tile_chunks),
        grid=(rows // tm,),
        in_specs=[row_spec(D_MODEL), _full_spec(g.shape), _full_spec(w_in.shape), tab_spec, tab_spec]
        + [_full_spec(a.shape) for a in consts[2:]],
        out_specs=[row_spec(GDN_QKV), row_spec(GDN_Z), row_spec(LANES), row_spec(MLA_PAD), row_spec(MLA_PAD),
                   _vt_spec(tm, tiles_per_batch), row_spec(MLA_KV_RANK), row_spec(LANES)],
        out_shape=[jax.ShapeDtypeStruct((rows, GDN_QKV), F32), jax.ShapeDtypeStruct((rows, GDN_Z), F32),
                   jax.ShapeDtypeStruct((rows, LANES), F32),
                   jax.ShapeDtypeStruct((rows, MLA_PAD), BF16), jax.ShapeDtypeStruct((rows, MLA_PAD), BF16),
                   jax.ShapeDtypeStruct((batch, VT_TOTAL, rows // batch), BF16),
                   jax.ShapeDtypeStruct((rows, MLA_KV_RANK), F32), jax.ShapeDtypeStruct((rows, LANES), F32)],
        compiler_params=_params(1),
        name="mixer_inproj",
    )(x, g, w_in, cos, sin, *consts[2:])


def _kv_up_body(ckv_ref, kr_ref, wk_ref, wvt_ref, vones_ref, gk_ref, k_out, vt_out):
    nope, _, _ = _lane_masks()
    cb = ckv_ref[...].astype(BF16)
    vt_out[...] = (_dot_nt(wvt_ref[...], cb) + vones_ref[...]).astype(BF16)
    _k_heads(_dot(cb, wk_ref[...]), kr_ref[...], jnp.where(nope, gk_ref[...], 0.0), k_out)


def _kv_up(ckv, krp, w_k, w_vt, v_ones, gk, *, batch):
    rows = ckv.shape[0]
    tm = _row_tile(rows, 512)
    tiles_per_batch = rows // batch // tm
    row_spec = lambda n: pl.BlockSpec((tm, n), lambda i: (i, 0))
    return pl.pallas_call(
        _kv_up_body,
        grid=(rows // tm,),
        in_specs=[row_spec(MLA_KV_RANK), row_spec(LANES), _full_spec(w_k.shape), _full_spec(w_vt.shape),
                  _full_spec(v_ones.shape), _full_spec(gk.shape)],
        out_specs=[row_spec(MLA_PAD), _vt_spec(tm, tiles_per_batch)],
        out_shape=[jax.ShapeDtypeStruct((rows, MLA_PAD), BF16),
                   jax.ShapeDtypeStruct((batch, VT_TOTAL, rows // batch), BF16)],
        compiler_params=_params(1),
        name="mla_kv_up",
    )(ckv, krp, w_k, w_vt, v_ones, gk)


def _attn_body(qi_ref, kj_ref, *refs, tq, tk, n_key_tiles, causal, nk_valid, has_ext):
    if has_ext:
        q_ref, k_ref, vt_ref, ke_ref, vte_ref, o_ref, m_scr, acc_scr, slot_ref = refs
    else:
        q_ref, k_ref, vt_ref, o_ref, m_scr, acc_scr, slot_ref = refs
    step = pl.program_id(1)
    i = qi_ref[step]
    j = kj_ref[step]
    last_j = ((i + 1) * tq + tk - 1) // tk - 1 if causal else n_key_tiles - 1
    head_lanes = [slice(h * LANES, (h + 1) * LANES) for h in range(MLA_HEADS)]

    def head_scores(k_blk, q_mask):
        if q_mask is None:
            q_heads = [q_ref[:, hs] for hs in head_lanes]
        else:
            q_heads = [q_ref[:, hs] + q_mask for hs in head_lanes]
        return [_dot_nt(k_blk[:, hs], qh) for hs, qh in zip(head_lanes, q_heads)]

    def values(vt_blk, h):
        return vt_blk[h * VT_ROWS:(h + 1) * VT_ROWS, :]

    def two_pass(k_blk, vt_blk, cur, bias=None, q_mask=None):
        scores = head_scores(k_blk, q_mask)
        for h in range(MLA_HEADS):
            s = scores[h] if bias is None else scores[h] + bias
            m_old = m_scr[cur, h][0:1]
            m_new = jnp.maximum(m_old, jnp.max(s, 0, keepdims=True))
            alpha = jnp.exp2(m_old - m_new)
            p = jnp.exp2(s - m_new)
            m_scr[cur, h] = jnp.broadcast_to(m_new, (HALO, tq))
            acc_scr[cur, h] = acc_scr[cur, h] * alpha + _dot(values(vt_blk, h), p.astype(BF16))

    def one_pass(k_blk, vt_blk, cur, q_mask=None):
        scores = head_scores(k_blk, q_mask)
        rise = None
        for h in range(MLA_HEADS):
            s = scores[h]
            m_old = m_scr[cur, h][0:1]
            p = jnp.exp2(s - m_old)
            m_tile = jnp.max(s, 0, keepdims=True)
            m_new = jnp.maximum(m_old, m_tile)
            rise = m_tile - m_old if rise is None else jnp.maximum(rise, m_tile - m_old)
            m_scr[1 - cur, h] = jnp.broadcast_to(m_new, (HALO, tq))
            acc_scr[1 - cur, h] = (acc_scr[cur, h] + _dot(values(vt_blk, h), p.astype(BF16))) * jnp.exp2(m_old - m_new)
        return jnp.max(rise)

    def guarded(k_blk, vt_blk, q_mask=None):
        cur = slot_ref[0]
        rise = one_pass(k_blk, vt_blk, cur, q_mask)

        @pl.when(rise <= ONE_PASS_JUMP)
        def _():
            slot_ref[0] = 1 - cur

        @pl.when(jnp.logical_not(rise <= ONE_PASS_JUMP))
        def _():
            two_pass(k_blk, vt_blk, cur, q_mask=q_mask)

    @pl.when(j == 0)
    def _():
        slot_ref[0] = 0
        m_scr[0] = jnp.full(m_scr.shape[1:], NEG_INF, F32)
        acc_scr[0] = jnp.zeros(acc_scr.shape[1:], F32)
        if has_ext:
            two_pass(ke_ref, vte_ref, 0)

    if nk_valid is not None:
        kpos = j * tk + lax.broadcasted_iota(jnp.int32, (tk, 1), 0)
        two_pass(k_ref, vt_ref, slot_ref[0], bias=jnp.where(kpos < nk_valid, 0.0, NEG_INF))
    elif not has_ext:
        two_pass(k_ref, vt_ref, slot_ref[0])
    elif causal:
        @pl.when(j < last_j)
        def _():
            guarded(k_ref, vt_ref)

        def diagonal(n_keys):
            lane = lax.broadcasted_iota(jnp.int32, (1, LANES), 1)
            q_chunk = (i * tq + lax.broadcasted_iota(jnp.int32, (tq, 1), 0)) >> CHUNK_SHIFT
            key_chunk = j * (tk // CHUNK) + lane - MASK_LANE0
            hidden = jnp.logical_and(jnp.logical_and(lane >= MASK_LANE0, lane < MASK_LANE0 + tk // CHUNK),
                                     key_chunk > q_chunk)
            guarded(k_ref.at[0:n_keys, :], vt_ref.at[:, 0:n_keys],
                    q_mask=jnp.where(hidden, -MASK_BIG, 0.0).astype(BF16))

        n_visible = (i + 1) * tq - j * tk
        for n_keys in range(tq, tk + 1, tq):
            pl.when(jnp.logical_and(j == last_j, n_visible == n_keys))(functools.partial(diagonal, n_keys))
    else:
        guarded(k_ref, vt_ref)

    @pl.when(j == last_j)
    def _():
        cur = slot_ref[0]
        for h in range(MLA_HEADS):
            acc = acc_scr[cur, h]
            o_ref[h * MLA_V:(h + 1) * MLA_V, :] = (acc[:MLA_V] / acc[MLA_V:MLA_V + 1]).astype(BF16)


def _attention(q, k, vt, ext, *, tq, tk, causal=False, nk_valid=None):
    b, nq, _ = q.shape
    nk = k.shape[1]
    n_key_tiles = nk // tk
    assert not causal or tk % tq == 0
    pairs = [(i, j) for i in range(nq // tq) for j in range(n_key_tiles) if not causal or j * tk < (i + 1) * tq]
    qi = jnp.asarray([p[0] for p in pairs], jnp.int32)
    kj = jnp.asarray([p[1] for p in pairs], jnp.int32)
    hv = MLA_HEADS * MLA_V
    in_specs = [
        pl.BlockSpec((None, tq, MLA_PAD), lambda bi, s, qi, kj: (bi, qi[s], 0)),
        pl.BlockSpec((None, tk, MLA_PAD), lambda bi, s, qi, kj: (bi, kj[s], 0)),
        pl.BlockSpec((None, VT_TOTAL, tk), lambda bi, s, qi, kj: (bi, 0, kj[s])),
    ]
    args = [q, k, vt]
    if ext is not None:
        ne = ext[0].shape[1]
        shared = ext[0].shape[0] == 1 and b > 1
        emap = (lambda bi, s, qi, kj: (0, 0, 0)) if shared else (lambda bi, s, qi, kj: (bi, 0, 0))
        in_specs += [pl.BlockSpec((None, ne, MLA_PAD), emap), pl.BlockSpec((None, VT_TOTAL, ne), emap)]
        args += list(ext)
    return pl.pallas_call(
        functools.partial(_attn_body, tq=tq, tk=tk, n_key_tiles=n_key_tiles, causal=causal, nk_valid=nk_valid,
                          has_ext=ext is not None),
        grid_spec=pltpu.PrefetchScalarGridSpec(
            num_scalar_prefetch=2,
            grid=(b, len(pairs)),
            in_specs=in_specs,
            out_specs=pl.BlockSpec((None, hv, tq), lambda bi, s, qi, kj: (bi, 0, qi[s])),
            scratch_shapes=[pltpu.VMEM((2, MLA_HEADS, HALO, tq), F32), pltpu.VMEM((2, MLA_HEADS, VT_ROWS, tq), F32),
                            pltpu.SMEM((1,), jnp.int32)]),
        out_shape=jax.ShapeDtypeStruct((b, hv, nq), BF16),
        compiler_params=_params(2),
        name="mla_attention",
    )(qi, kj, *args)


def _sconv_body(x_ref, g_ref, win_ref, cw_ref, wout_ref, prev_ref, o_ref, pout_ref, halo_scr, *, rows):
    t = pl.program_id(1)

    @pl.when(t == 0)
    def _():
        halo_scr[...] = prev_ref[...]

    x = x_ref[...]
    hn = _rms(x, g_ref[...]).astype(BF16)
    bg = _dot(hn, win_ref[:, :D_MODEL])
    u = _dot(hn, win_ref[:, D_MODEL:2 * D_MODEL]) * _dot(hn, win_ref[:, 2 * D_MODEL:])
    halo = halo_scr[...]
    u1 = _shift_rows(u, halo, 1)
    u2 = _shift_rows(u, halo, 2)
    cw = cw_ref[...]
    y = u2 * cw[0:1] + u1 * cw[1:2] + u * cw[2:3]
    last_rows = u[rows - HALO:rows, :]
    halo_scr[...] = last_rows
    o_ref[...] = x + _dot((bg * y).astype(BF16), wout_ref[...])

    @pl.when(t == pl.num_programs(1) - 1)
    def _():
        pout_ref[...] = last_rows


def _sconv(x, prev, w, *, rows):
    b, t_len, _ = x.shape
    g, w_in, conv_w, w_out = w
    shared = prev.shape[0] == 1 and b > 1
    bsel = (lambda bi: 0) if shared else (lambda bi: bi)
    return pl.pallas_call(
        functools.partial(_sconv_body, rows=rows),
        grid=(b, t_len // rows),
        in_specs=[pl.BlockSpec((None, rows, D_MODEL), lambda bi, ti: (bi, ti, 0)), _full_spec(g.shape),
                  _full_spec(w_in.shape), _full_spec(conv_w.shape), _full_spec(w_out.shape),
                  pl.BlockSpec((None, HALO, D_MODEL), lambda bi, ti: (bsel(bi), 0, 0))],
        out_specs=[pl.BlockSpec((None, rows, D_MODEL), lambda bi, ti: (bi, ti, 0)),
                   pl.BlockSpec((None, HALO, D_MODEL), lambda bi, ti: (bi, 0, 0))],
        out_shape=[jax.ShapeDtypeStruct((b, t_len, D_MODEL), F32), jax.ShapeDtypeStruct((b, HALO, D_MODEL), F32)],
        scratch_shapes=[pltpu.VMEM((HALO, D_MODEL), F32)],
        compiler_params=_params(2),
        name="sconv_mixer",
    )(x, g, w_in, conv_w, w_out, prev)


def _pad_lanes(a, before, total):
    return jnp.pad(a, [(0, 0)] * (a.ndim - 1) + [(before, total - before - a.shape[-1])])


def _rope_tables(pos):
    half = MLA_ROPE // 2
    inv = ROPE_THETA ** (-jnp.arange(half, dtype=F32) / half)
    ang = pos.astype(F32)[:, None] * inv
    cos, sin = jnp.cos(ang), jnp.sin(ang)
    n = pos.shape[0]
    tail = LANES - MLA_NOPE - MLA_ROPE
    cos_l = jnp.concatenate([jnp.ones((n, MLA_NOPE), F32), cos, cos, jnp.ones((n, tail), F32)], -1)
    sin_l = jnp.concatenate([jnp.zeros((n, MLA_NOPE), F32), sin, sin, jnp.zeros((n, tail), F32)], -1)
    return cos_l, sin_l


def _head_pair_matrices():
    i = jnp.arange(PAIR)[:, None]
    j = jnp.arange(PAIR)[None, :]
    same = (i // LANES) == (j // LANES)
    li, lj = i % LANES, j % LANES
    half = MLA_ROPE // 2
    rope_end = MLA_NOPE + MLA_ROPE
    in_rope = lambda l: (l >= MLA_NOPE) & (l < rope_end)
    gmean = jnp.where(same & (li < MLA_NOPE) & (lj < MLA_NOPE), 1.0 / MLA_NOPE,
                      jnp.where(same & in_rope(li) & in_rope(lj), 1.0 / MLA_ROPE, 0.0))
    first, second = in_rope(lj) & (lj < MLA_NOPE + half), in_rope(lj) & (lj >= MLA_NOPE + half)
    rot = jnp.where(same & first & (li == lj + half), -1.0, jnp.where(same & second & (li == lj - half), 1.0, 0.0))
    return gmean.astype(BF16), rot.astype(BF16)


def _tail_rows(a, n):
    return jnp.pad(a, ((0, 0), (HALO - n, 0), (0, 0)))


def kernel(x_prompt, x_sample, cache_mla_ckv, cache_mla_krope, state_gdn_S, state_gdn_conv, state_sconv, meta_tokens, ffn1_norm, ffn1_w_gate, ffn1_w_up, ffn1_w_down, ffn2_norm, ffn2_w_gate, ffn2_w_up, ffn2_w_down, mix_norm, ab_w_in, ab_w_out, gdn_conv_w, gdn_A_log, gdn_dt_bias, gdn_o_norm, mla_q_norm, mla_w_uq, mla_kv_norm, mla_w_ukv, mla_qn_norm, mla_qr_norm, mla_kn_norm, mla_kr_norm, sc_w_in, sc_conv_w, sc_w_out):
    bp, tp, _ = x_prompt.shape
    bs, ts, _ = x_sample.shape
    n_frames = bp * tp
    n_sample = bs * ts

    ffn1 = _ffn_weights(ffn1_norm, ffn1_w_gate, ffn1_w_up, ffn1_w_down)
    ffn2 = _ffn_weights(ffn2_norm, ffn2_w_gate, ffn2_w_up, ffn2_w_down)
    w_in = _to_bf16(ab_w_in)[0]
    s1 = GDN_QKV + GDN_Z
    c0 = s1 + 2 * GDN_HEADS
    c1 = c0 + MLA_Q_RANK
    c2 = c1 + MLA_KV_RANK
    gates_rope = jnp.concatenate([
        w_in[:, s1:c0], jnp.zeros((D_MODEL, MLA_NOPE - 2 * GDN_HEADS), BF16), w_in[:, c2:],
        jnp.zeros((D_MODEL, LANES - MLA_NOPE - MLA_ROPE), BF16)], -1)
    w_in_packed = jnp.concatenate([w_in[:, :s1], w_in[:, c0:c1], gates_rope, w_in[:, c1:c2]], -1)
    mix_g = mix_norm.reshape(2, 1, D_MODEL)
    gdn_w = (gdn_conv_w[0], _pad_lanes(gdn_A_log[0][None], 0, LANES), _pad_lanes(gdn_dt_bias[0][None], 0, LANES),
             gdn_o_norm[0][None])
    qdim = MLA_NOPE + MLA_ROPE
    w_uq = _pad_lanes(mla_w_uq[0].reshape(MLA_Q_RANK, MLA_HEADS, qdim), 0, LANES).reshape(MLA_Q_RANK, MLA_PAD)
    w_ukv = mla_w_ukv[0].reshape(MLA_KV_RANK, MLA_HEADS, MLA_NOPE + MLA_V)
    w_k = _pad_lanes(w_ukv[..., :MLA_NOPE], 0, LANES).reshape(MLA_KV_RANK, MLA_PAD)
    hv = MLA_HEADS * MLA_V
    w_vt = _pad_lanes(w_ukv[..., MLA_NOPE:], 0, VT_ROWS).reshape(MLA_KV_RANK, VT_TOTAL).T
    v_ones = (jnp.arange(VT_TOTAL) % VT_ROWS == MLA_V).astype(F32)[:, None]
    gq = _pad_lanes(jnp.concatenate([mla_qn_norm[0], mla_qr_norm[0]])[None], 0, LANES)
    gk = _pad_lanes(jnp.concatenate([mla_kn_norm[0], mla_kr_norm[0]])[None], 0, LANES)
    gmean, rot_mat = _head_pair_matrices()
    mla_w = (mla_q_norm[0][None], w_uq.astype(BF16), mla_kv_norm[0][None], w_k.astype(BF16), w_vt.astype(BF16), v_ones,
             gq, gk, gmean, rot_mat)
    w_out = ab_w_out[0].astype(BF16)
    sc_w = (mix_g[1], _to_bf16(sc_w_in)[0], sc_conv_w[0], sc_w_out[0].astype(BF16))

    xf = x_prompt.reshape(n_frames, D_MODEL)
    xs = jnp.concatenate([x_sample.reshape(n_sample, D_MODEL), meta_tokens.astype(F32)], 0)
    cos_f, sin_f = _rope_tables(N_META + jnp.arange(tp))
    pos_small = jnp.concatenate([jnp.tile(PAST_LEN + jnp.arange(ts), bs), jnp.arange(N_META)])
    cos_s, sin_s = _rope_tables(pos_small)

    xf = _ffn(xf, ffn1[0])
    xs = _ffn(xs, ffn1[0])
    qkv_f, z_f, ab_f, q_f, k_f, vt_f, ckvn_f, krn_f = _inproj(xf, mix_g[0], w_in_packed, cos_f, sin_f, mla_w,
                                                               batch=bp, key_tile=ATTN_TK)
    qkv_s, z_s, ab_s, q_s, k_s, vt_s, ckvn_s, krn_s = _inproj(xs, mix_g[0], w_in_packed, cos_s, sin_s, mla_w, batch=1)

    pad_meta = lambda a: jnp.pad(a[n_sample:], ((0, CHUNK - N_META), (0, 0)))[None]
    og_m, s_m, conv_m = _gdn(pad_meta(qkv_s), pad_meta(z_s), pad_meta(ab_s),
                             jnp.zeros((1, GDN_HEADS, GDN_DK, GDN_DV), F32), jnp.zeros((1, HALO, GDN_QKV), F32),
                             gdn_w, rows=CHUNK, n_valid=N_META)
    og_f, p_s, p_conv = _gdn(qkv_f.reshape(bp, tp, GDN_QKV), z_f.reshape(bp, tp, GDN_Z), ab_f.reshape(bp, tp, LANES),
                             s_m, conv_m, gdn_w, rows=GDN_ROWS)
    og_s, s_s, s_conv = _gdn(qkv_s[:n_sample].reshape(bs, ts, GDN_QKV), z_s[:n_sample].reshape(bs, ts, GDN_Z),
                             ab_s[:n_sample].reshape(bs, ts, LANES), state_gdn_S[0],
                             _tail_rows(state_gdn_conv[0], state_gdn_conv.shape[2]), gdn_w, rows=ts)

    k_c, vt_c = _kv_up(cache_mla_ckv[0].reshape(bs * PAST_LEN, MLA_KV_RANK),
                       _pad_lanes(cache_mla_krope[0].reshape(bs * PAST_LEN, MLA_ROPE), MLA_NOPE, LANES),
                       mla_w[3], mla_w[4], v_ones, gk, batch=bs)
    k_meta, vt_meta = k_s[n_sample:][None], vt_s[:, :, n_sample:]
    vt_new = vt_s[0, :, :n_sample].reshape(VT_TOTAL, bs, ts).transpose(1, 0, 2)
    omt_f = _attention(q_f.reshape(bp, tp, MLA_PAD), k_f.reshape(bp, tp, MLA_PAD), vt_f, (k_meta, vt_meta),
                       tq=ATTN_TQ, tk=ATTN_TK, causal=True)
    omt_s = _attention(q_s[:n_sample].reshape(bs, ts, MLA_PAD), k_c.reshape(bs, PAST_LEN, MLA_PAD), vt_c,
                       (k_s[:n_sample].reshape(bs, ts, MLA_PAD), vt_new), tq=ts, tk=512)
    omt_m = _attention(jnp.pad(q_s[n_sample:], ((0, LANES - N_META), (0, 0)))[None],
                       jnp.pad(k_s[n_sample:], ((0, CHUNK - N_META), (0, 0)))[None],
                       jnp.pad(vt_meta, ((0, 0), (0, 0), (0, CHUNK - N_META))),
                       None, tq=LANES, tk=CHUNK, nk_valid=N_META)

    og_small = jnp.concatenate([og_s.reshape(n_sample, GDN_Z), og_m[0, :N_META]], 0)
    omt_small = jnp.concatenate([omt_s.transpose(1, 0, 2).reshape(hv, n_sample), omt_m[0, :, :N_META]], -1)[None]
    xf = _ffn(xf, ffn2[0], mix=(og_f.reshape(n_frames, GDN_Z), omt_f, w_out))
    xs = _ffn(xs, ffn2[0], mix=(og_small, omt_small, w_out))

    xf = _ffn(xf, ffn1[1])
    xs = _ffn(xs, ffn1[1])
    n_sc = state_sconv.shape[2]
    xm, sc_m = _sconv(xs[n_sample:][None], jnp.zeros((1, HALO, D_MODEL), F32), sc_w, rows=N_META)
    xf3, p_sc = _sconv(xf.reshape(bp, tp, D_MODEL), sc_m, sc_w, rows=SCONV_ROWS)
    xs3, s_sc = _sconv(xs[:n_sample].reshape(bs, ts, D_MODEL), _tail_rows(state_sconv[0], n_sc), sc_w, rows=ts)
    xf = _ffn(xf3.reshape(n_frames, D_MODEL), ffn2[1])
    xs = _ffn(jnp.concatenate([xs3.reshape(n_sample, D_MODEL), xm[0]], 0), ffn2[1])

    y_prompt = xf.reshape(bp, tp, D_MODEL)
    y_sample = xs[:n_sample].reshape(bs, ts, D_MODEL)
    rope_lo, rope_hi = MLA_NOPE, MLA_NOPE + MLA_ROPE
    meta_ckv = jnp.broadcast_to(ckvn_s[n_sample:][None], (bp, N_META, MLA_KV_RANK))
    meta_kr = jnp.broadcast_to(krn_s[n_sample:, rope_lo:rope_hi][None], (bp, N_META, MLA_ROPE))
    p_ckv = jnp.concatenate([meta_ckv, ckvn_f.reshape(bp, tp, MLA_KV_RANK)], 1)
    p_kr = jnp.concatenate([meta_kr, krn_f[:, rope_lo:rope_hi].reshape(bp, tp, MLA_ROPE)], 1)
    n_gc = state_gdn_conv.shape[2]
    return (y_prompt, y_sample,
            p_ckv[None], p_kr[None], p_s[None], p_conv[:, HALO - n_gc:][None], p_sc[:, HALO - n_sc:][None],
            ckvn_s[:n_sample].reshape(bs, ts, MLA_KV_RANK)[None],
            krn_s[:n_sample, rope_lo:rope_hi].reshape(bs, ts, MLA_ROPE)[None],
            s_s[None], s_conv[:, HALO - n_gc:][None], s_sc[:, HALO - n_sc:][None])
```

```python
import functools

import jax
import jax.numpy as jnp
from jax import lax
from jax.experimental import pallas as pl
from jax.experimental.pallas import tpu as pltpu

F32 = jnp.float32
BF16 = jnp.bfloat16

D_MODEL = 1024
D_FF = 2816
CHUNK = 64
CHUNK_SHIFT = 6
N_META = 16
EPS = 1e-6
GDN_HEADS = 4
GDN_DK = 128
GDN_DV = 128
GDN_QKV = GDN_HEADS * (2 * GDN_DK + GDN_DV)
GDN_Z = GDN_HEADS * GDN_DV
MLA_HEADS = 8
MLA_NOPE = 64
MLA_ROPE = 32
MLA_V = 64
MLA_Q_RANK = 384
MLA_KV_RANK = 256
ROPE_THETA = 10000.0
PAST_LEN = 1024
LANES = 128
HALO = 8
MLA_PAD = MLA_HEADS * LANES
FF_BLOCK = 256
FF_CHUNKS = D_FF // FF_BLOCK
BF16_SUBLANES = 16
VMEM_LIMIT = 60000 * 1024
NEG_INF = float("-inf")
LOG2_E = 1.4426950408889634
MASK_LANE0 = MLA_NOPE + MLA_ROPE
MASK_BIG = 2.0 ** 100
ATTN_TQ = 512
ATTN_TK = 1024
FFN_ROWS = 1024
INPROJ_ROWS = 512
SCONV_ROWS = 1024
GDN_ROWS = 8 * CHUNK
GDN_GROUP = 4
VT_ROWS = MLA_V + BF16_SUBLANES
VT_TOTAL = MLA_HEADS * VT_ROWS
ONE_PASS_JUMP = 64.0

IN_QKV = 0
IN_Z = IN_QKV + GDN_QKV
IN_CQ = IN_Z + GDN_Z
IN_GATES_ROPE = IN_CQ + MLA_Q_RANK
IN_CKV = IN_GATES_ROPE + LANES
IN_TOTAL = IN_CKV + MLA_KV_RANK


def _params(n_axes):
    return pltpu.CompilerParams(dimension_semantics=("arbitrary",) * n_axes, vmem_limit_bytes=VMEM_LIMIT)


def _rms(x, g):
    return x * lax.rsqrt(jnp.mean(x * x, -1, keepdims=True) + EPS) * g


def _sigmoid(x):
    return 1.0 / (1.0 + jnp.exp(-x))


def _silu(x):
    return x * _sigmoid(x)


def _softplus(x):
    return jnp.maximum(x, 0.0) + jnp.log1p(jnp.exp(-jnp.abs(x)))


def _dot(a, b):
    return jnp.dot(a, b, preferred_element_type=F32)


def _dot_nt(a, b):
    return lax.dot_general(a, b, (((1,), (1,)), ((), ())), preferred_element_type=F32)


def _dot_tn(a, b):
    return lax.dot_general(a, b, (((0,), (0,)), ((), ())), preferred_element_type=F32)


def _shift_rows(x, history, k):
    rolled = pltpu.roll(x, k, 0)
    row = lax.broadcasted_iota(jnp.int32, (HALO, 1), 0)
    head = rolled[:HALO]
    for j in range(k):
        head = jnp.where(row == j, history[HALO - k + j:HALO - k + j + 1, :], head)
    return jnp.concatenate([head, rolled[HALO:]], 0)


def _full_spec(shape):
    zeros = (0,) * len(shape)
    return pl.BlockSpec(shape, lambda *_: zeros)


def _row_tile(rows, want):
    return want if rows % want == 0 else rows


def _ffn_body(x_ref, g_ref, wg_ref, wu_ref, wd_ref, *rest, mixed):
    if mixed:
        og_ref, omt_ref, wmix_ref, o_ref, act_ref = rest
        x = x_ref[...] + _dot(og_ref[...], wmix_ref[:GDN_Z, :]) + _dot_tn(omt_ref[...], wmix_ref[GDN_Z:, :])
    else:
        o_ref, act_ref = rest
        x = x_ref[...]
    h = _rms(x, g_ref[...]).astype(BF16)
    for c in range(FF_CHUNKS):
        cols = slice(c * FF_BLOCK, (c + 1) * FF_BLOCK)
        gate = _dot(h, wg_ref[:, cols])
        up = _dot(h, wu_ref[:, cols])
        act_ref[:, cols] = (_silu(gate) * up).astype(BF16)
    o_ref[...] = x + 0.5 * _dot(act_ref[...], wd_ref[...])


def _ffn(x, w, mix=None):
    rows = x.shape[0]
    tm = _row_tile(rows, FFN_ROWS)
    g, wg, wu, wd, layer = w
    resident = lambda a: pl.BlockSpec(a.shape, lambda i: (0,) * a.ndim, pipeline_mode=pl.Buffered(1))
    of_layer = lambda a: pl.BlockSpec((None,) + a.shape[1:], lambda i: (layer, 0, 0), pipeline_mode=pl.Buffered(1))
    in_specs = [pl.BlockSpec((tm, D_MODEL), lambda i: (i, 0)), _full_spec(g.shape), of_layer(wg), of_layer(wu),
                of_layer(wd)]
    args = [x, g, wg, wu, wd]
    if mix is not None:
        og, omt, w_mix = mix
        tiles_per_batch = rows // omt.shape[0] // tm
        in_specs += [pl.BlockSpec((tm, GDN_Z), lambda i: (i, 0)),
                     pl.BlockSpec((None, GDN_Z, tm), lambda i: (i // tiles_per_batch, 0, i % tiles_per_batch)),
                     resident(w_mix)]
        args += [og, omt, w_mix]
    return pl.pallas_call(
        functools.partial(_ffn_body, mixed=mix is not None),
        grid=(rows // tm,),
        in_specs=in_specs,
        out_specs=pl.BlockSpec((tm, D_MODEL), lambda i: (i, 0)),
        out_shape=jax.ShapeDtypeStruct((rows, D_MODEL), F32),
        scratch_shapes=[pltpu.VMEM((tm, D_FF), BF16)],
        compiler_params=_params(1),
        name="half_ffn",
    )(*args)


def _to_bf16_body(x_ref, o_ref):
    o_ref[...] = x_ref[...].astype(BF16)


def _to_bf16(a):
    a2 = a.reshape(-1, a.shape[-1])
    rows, cols = a2.shape
    tm = _row_tile(rows, 512)
    out = pl.pallas_call(
        _to_bf16_body,
        grid=(rows // tm,),
        in_specs=[pl.BlockSpec((tm, cols), lambda i: (i, 0))],
        out_specs=pl.BlockSpec((tm, cols), lambda i: (i, 0)),
        out_shape=jax.ShapeDtypeStruct((rows, cols), BF16),
        compiler_params=_params(1),
        name="weights_to_bf16",
    )(a2)
    return out.reshape(a.shape)


def _ffn_weights(norm, w_gate, w_up, w_down):
    stacks = (_to_bf16(w_gate), _to_bf16(w_up), _to_bf16(w_down))
    return [(norm[l].reshape(1, D_MODEL),) + stacks + (l,) for l in range(norm.shape[0])]


def _gdn_conv(x, history, cw):
    y = (_shift_rows(x, history, 3) * cw[0:1] + _shift_rows(x, history, 2) * cw[1:2]
         + _shift_rows(x, history, 1) * cw[2:3] + x * cw[3:4])
    return _silu(y)


class _Split:
    def __init__(self, x):
        self.x = x
        self._parts = None
        self._diag = None

    def parts(self):
        if self._parts is None:
            hi = self.x.astype(BF16)
            self._parts = (hi, (self.x - hi.astype(F32)).astype(BF16))
        return self._parts

    def diag_parts(self):
        if self._diag is None:
            first = lax.broadcasted_iota(jnp.int32, self.x.shape, 1) < CHUNK
            zero = jnp.zeros(self.x.shape, BF16)
            self._diag = tuple(jnp.concatenate([jnp.where(first, p, zero), jnp.where(first, zero, p)], 0)
                               for p in self.parts())
        return self._diag


def _split_products(pairs):
    parts = [(a.parts(), b.diag_parts()) for a, b in pairs]
    outs = []
    for (a_hi, a_lo), (b_hi, b_lo) in parts:
        a2 = jnp.concatenate([a_hi, a_lo], 0)
        r = _dot(a2, b_hi) + _dot(a2, b_lo)
        half = a_hi.shape[0]
        outs.append(_Split(r[:half] + r[half:]))
    return outs


def _unit_lower_inverses(ns, eye_f, block_masks):
    k = len(ns)
    nd = [_Split(jnp.where(block_masks[0], n, 0.0)) for n in ns]
    n2 = _split_products([(x, x) for x in nd])
    yield None
    lvl = _split_products([(x, x) for x in n2] +
                          [(_Split(eye_f - a.x), _Split(eye_f + b.x)) for a, b in zip(nd, n2)])
    yield None
    n4, t = lvl[:k], lvl[k:]
    t = _split_products([(a, _Split(eye_f + b.x)) for a, b in zip(t, n4)])
    yield None
    for done, merged in zip(block_masks, block_masks[1:] + [None]):
        keep = jnp.logical_not(done) if merged is None else jnp.logical_and(merged, jnp.logical_not(done))
        off = [_Split(jnp.where(keep, n, 0.0)) for n in ns]
        x = _split_products(list(zip(off, t)))
        yield None
        y = _split_products(list(zip(t, x)))
        yield None
        t = [_Split(a.x - b.x) for a, b in zip(t, y)]
    yield [a.x for a in t]


def _gdn_body(qkv_ref, z_ref, ab_ref, s0_ref, cprev_ref, cw_ref, alog_ref, dtb_ref, on_ref,
              og_ref, sout_ref, cout_ref, s_scr, halo_scr, *, rows, n_valid):
    t = pl.program_id(1)
    C = CHUNK
    row = lax.broadcasted_iota(jnp.int32, (rows, 1), 0)

    @pl.when(t == 0)
    def _():
        s_scr[...] = s0_ref[...]
        halo_scr[...] = cprev_ref[...]

    x = qkv_ref[...]
    y = _gdn_conv(x, halo_scr[...], cw_ref[...])
    last_rows = x[n_valid - HALO:n_valid, :]
    halo_scr[...] = last_rows

    ab = ab_ref[...]
    g_all = -jnp.exp(alog_ref[...]) * _softplus(ab + dtb_ref[...])
    b_all = _sigmoid(ab)
    if n_valid < rows:
        valid = row < n_valid
        y = jnp.where(valid, y, 0.0)
        g_all = jnp.where(valid, g_all, 0.0)
        b_all = jnp.where(valid, b_all, 0.0)

    ii = lax.broadcasted_iota(jnp.int32, (C, LANES), 0)
    pair_lane = lax.broadcasted_iota(jnp.int32, (C, LANES), 1)
    jj = pair_lane & (C - 1)
    first_head = pair_lane < C
    causal = ii >= jj
    strict = ii > jj
    eye_f = (ii == jj).astype(F32)
    block_masks = [(ii >> s) == (jj >> s) for s in (3, 4, 5)]
    on = on_ref[...]
    n_chunks = rows // C
    heads = range(GDN_HEADS)

    ri = lax.broadcasted_iota(jnp.int32, (rows, rows), 0)
    rj = lax.broadcasted_iota(jnp.int32, (rows, rows), 1)
    tril_blk = jnp.logical_and(ri >= rj, (ri >> CHUNK_SHIFT) == (rj >> CHUNK_SHIFT)).astype(BF16)
    g_hi = g_all.astype(BF16)
    g_r1 = g_all - g_hi.astype(F32)
    g_mid = g_r1.astype(BF16)
    g_lo = (g_r1 - g_mid.astype(F32)).astype(BF16)
    gcum = _dot(tril_blk, g_hi) + _dot(tril_blk, g_mid) + _dot(tril_blk, g_lo)
    egcum = jnp.exp(gcum)

    qn, kn, vn = [], [], []
    for h in heads:
        qh = y[:, h * GDN_DK:(h + 1) * GDN_DK]
        kh = y[:, GDN_HEADS * GDN_DK + h * GDN_DK:GDN_HEADS * GDN_DK + (h + 1) * GDN_DK]
        qn.append(qh * lax.rsqrt(jnp.sum(qh * qh, -1, keepdims=True) + EPS) * (GDN_DK ** -0.5))
        kn.append(kh * lax.rsqrt(jnp.sum(kh * kh, -1, keepdims=True) + EPS))
        vn.append(y[:, 2 * GDN_HEADS * GDN_DK + h * GDN_DV:2 * GDN_HEADS * GDN_DK + (h + 1) * GDN_DV])

    head_pairs = [(a, a + 1) for a in range(0, GDN_HEADS, 2)]
    pre = {}

    def chunk_local(chunks):
        chains = [(c, pair) for c in chunks for pair in head_pairs]
        n_mats = []
        for c in chunks:
            sl = slice(c * C, (c + 1) * C)
            gcum_t = gcum[sl].T
            for a, b in head_pairs:
                col = lambda arr, h: arr[sl, h:h + 1]
                gcol = jnp.where(first_head, col(gcum, a), col(gcum, b))
                beta2 = jnp.where(first_head, col(b_all, GDN_HEADS + a), col(b_all, GDN_HEADS + b))
                grow = jnp.concatenate([gcum_t[a:a + 1, :], gcum_t[b:b + 1, :]], -1)
                dec_causal = jnp.exp(jnp.where(causal, gcol - grow, NEG_INF))
                kb = {h: kn[h][sl].astype(BF16) for h in (a, b)}
                kk = jnp.concatenate([_dot_nt(kb[a], kb[a]), _dot_nt(kb[b], kb[b])], -1)
                n_mats.append(beta2 * kk * jnp.where(strict, dec_causal, 0.0))
                qk = jnp.concatenate([_dot_nt(qn[h][sl].astype(BF16), kb[h]) for h in (a, b)], -1)
                qk = (qk * dec_causal).astype(BF16)
                for h, lanes in ((a, slice(0, C)), (b, slice(C, 2 * C))):
                    beta = col(b_all, GDN_HEADS + h)
                    eg = col(egcum, h)
                    g_last = gcum[(c + 1) * C - 1:(c + 1) * C, h:h + 1]
                    qh, kh, vh = qn[h][sl], kn[h][sl], vn[h][sl]
                    pre[c, h] = dict(
                        qk=qk[:, lanes],
                        rhs=jnp.concatenate([(beta * vh).astype(BF16), ((beta * eg) * kh).astype(BF16)], -1),
                        qd=(qh * eg).astype(BF16),
                        kt=(kh * jnp.exp(g_last - col(gcum, h))).astype(BF16),
                        decay=jnp.exp(g_last))
            yield
        inverses = None
        for inverses in _unit_lower_inverses(n_mats, eye_f, block_masks):
            yield
        for (c, (a, b)), tinv in zip(chains, inverses):
            tinv = tinv.astype(BF16)
            for h, lanes in ((a, slice(0, C)), (b, slice(C, 2 * C))):
                uw = _dot(tinv[:, lanes], pre[c, h]["rhs"])
                pre[c, h]["u"] = uw[:, :GDN_DV]
                pre[c, h]["wk"] = uw[:, GDN_DV:].astype(BF16)

    state = [s_scr[h] for h in heads]

    def recurrence(chunks):
        for c in chunks:
            sl = slice(c * C, (c + 1) * C)
            sb = [s.astype(BF16) for s in state]
            w = [(pre[c, h]["u"] - _dot(pre[c, h]["wk"], sb[h])).astype(BF16) for h in heads]
            yield
            o = [_dot(pre[c, h]["qd"], sb[h]) + _dot(pre[c, h]["qk"], w[h]) for h in heads]
            for h in heads:
                state[h] = state[h] * pre[c, h]["decay"] + _dot_tn(pre[c, h]["kt"], w[h])
            yield
            for h in heads:
                gated = _rms(o[h], on) * _silu(z_ref[sl, h * GDN_DV:(h + 1) * GDN_DV])
                og_ref[sl, h * GDN_DV:(h + 1) * GDN_DV] = gated.astype(BF16)
            yield

    group = min(n_chunks, GDN_GROUP)
    pending = iter(())
    for g0 in range(0, n_chunks, group):
        chunks = range(g0, g0 + group)
        for _ in chunk_local(chunks):
            next(pending, None)
        for _ in pending:
            pass
        pending = recurrence(chunks)
    for _ in pending:
        pass
    for h in heads:
        s_scr[h] = state[h]

    @pl.when(t == pl.num_programs(1) - 1)
    def _():
        sout_ref[...] = s_scr[...]
        cout_ref[...] = last_rows


def _gdn(qkv, z, ab, s0, cprev, w, *, rows, n_valid=None):
    b, t_len, _ = qkv.shape
    n_valid = rows if n_valid is None else n_valid
    conv_w, a_log, dt_bias, o_norm = w
    shared = s0.shape[0] == 1 and b > 1
    bsel = (lambda bi: 0) if shared else (lambda bi: bi)
    return pl.pallas_call(
        functools.partial(_gdn_body, rows=rows, n_valid=n_valid),
        grid=(b, t_len // rows),
        in_specs=[
            pl.BlockSpec((None, rows, GDN_QKV), lambda bi, ti: (bi, ti, 0)),
            pl.BlockSpec((None, rows, GDN_Z), lambda bi, ti: (bi, ti, 0)),
            pl.BlockSpec((None, rows, LANES), lambda bi, ti: (bi, ti, 0)),
            pl.BlockSpec((None, GDN_HEADS, GDN_DK, GDN_DV), lambda bi, ti: (bsel(bi), 0, 0, 0)),
            pl.BlockSpec((None, HALO, GDN_QKV), lambda bi, ti: (bsel(bi), 0, 0)),
            _full_spec(conv_w.shape), _full_spec(a_log.shape), _full_spec(dt_bias.shape), _full_spec(o_norm.shape),
        ],
        out_specs=[
            pl.BlockSpec((None, rows, GDN_Z), lambda bi, ti: (bi, ti, 0)),
            pl.BlockSpec((None, GDN_HEADS, GDN_DK, GDN_DV), lambda bi, ti: (bi, 0, 0, 0)),
            pl.BlockSpec((None, HALO, GDN_QKV), lambda bi, ti: (bi, 0, 0)),
        ],
        out_shape=[
            jax.ShapeDtypeStruct((b, t_len, GDN_Z), BF16),
            jax.ShapeDtypeStruct((b, GDN_HEADS, GDN_DK, GDN_DV), F32),
            jax.ShapeDtypeStruct((b, HALO, GDN_QKV), F32),
        ],
        scratch_shapes=[pltpu.VMEM((GDN_HEADS, GDN_DK, GDN_DV), F32), pltpu.VMEM((HALO, GDN_QKV), F32)],
        compiler_params=_params(2),
        name="gdn_mixer",
    )(qkv, z, ab, s0, cprev, conv_w, a_log, dt_bias, o_norm)


def _lane_masks():
    lane = lax.broadcasted_iota(jnp.int32, (1, LANES), 1)
    nope = lane < MLA_NOPE
    rope_a = jnp.logical_and(lane >= MLA_NOPE, lane < MLA_NOPE + MLA_ROPE // 2)
    rope_b = jnp.logical_and(lane >= MLA_NOPE + MLA_ROPE // 2, lane < MLA_NOPE + MLA_ROPE)
    return nope, rope_a, rope_b


def _rope_block(x, cos, sin, rope_a, rope_b):
    half = MLA_ROPE // 2
    rot = jnp.where(rope_a, -pltpu.roll(x, LANES - half, 1), jnp.where(rope_b, pltpu.roll(x, half, 1), 0.0))
    return x * cos + rot * sin


PAIR = 2 * LANES


def _twice(a):
    return jnp.concatenate([a, a], -1)


def _k_heads(k, shared, gk, k_out):
    for h in range(MLA_HEADS):
        x = k[:, h * LANES:(h + 1) * LANES]
        inv = lax.rsqrt(jnp.sum(x * x, -1, keepdims=True) / MLA_NOPE + EPS)
        k_out[:, h * LANES:(h + 1) * LANES] = (x * inv * gk + shared).astype(BF16)


def _inproj_body(x_ref, g_ref, w_ref, cos_ref, sin_ref, qn_ref, wuq_ref, kvn_ref, wk_ref, wvt_ref, vones_ref,
                 gq_ref, gk_ref, gmean_ref, rot_ref, qkv_ref, z_ref, ab_ref, q_out, k_out, vt_out, ckv_out, kr_out,
                 *, tiles_per_batch, key_tile_chunks):
    hn = _rms(x_ref[...], g_ref[...]).astype(BF16)
    qkv_ref[...] = _dot(hn, w_ref[:, IN_QKV:IN_Z])
    z_ref[...] = _dot(hn, w_ref[:, IN_Z:IN_CQ])
    cq_gates_rope = _dot(hn, w_ref[:, IN_CQ:IN_CKV])
    cq = cq_gates_rope[:, :MLA_Q_RANK]
    gates_rope = cq_gates_rope[:, MLA_Q_RANK:]
    ab_ref[...] = gates_rope
    ckv = _dot(hn, w_ref[:, IN_CKV:IN_TOTAL])

    nope, rope_a, rope_b = _lane_masks()
    krp = jnp.where(jnp.logical_or(rope_a, rope_b), gates_rope, 0.0)
    cos = cos_ref[...]
    sin = sin_ref[...]
    gk = gk_ref[...]
    gmean = gmean_ref[...]
    rot_mat = rot_ref[...]
    gq2 = _twice(gq_ref[...] * ((MLA_NOPE + MLA_ROPE) ** -0.5 * LOG2_E))
    cos2, sin2 = _twice(cos), _twice(sin)

    cqn = _rms(cq, qn_ref[...]).astype(BF16)
    q = _dot(cqn, wuq_ref[...])
    for p in range(MLA_HEADS // 2):
        x = q[:, p * PAIR:(p + 1) * PAIR]
        ms = _dot((x * x).astype(BF16), gmean)
        x = x * lax.rsqrt(ms + EPS) * gq2
        q_out[:, p * PAIR:(p + 1) * PAIR] = (x * cos2 + _dot(x.astype(BF16), rot_mat) * sin2).astype(BF16)

    ckvn = _rms(ckv, kvn_ref[...])
    ckv_out[...] = ckvn
    cb = ckvn.astype(BF16)
    krn = krp * lax.rsqrt(jnp.sum(krp * krp, -1, keepdims=True) / MLA_ROPE + EPS) * jnp.where(nope, 0.0, gk)
    kr_new = _rope_block(krn, cos, sin, rope_a, rope_b)
    kr_out[...] = kr_new
    vt_out[...] = (_dot_nt(wvt_ref[...], cb) + vones_ref[...]).astype(BF16)
    shared_lanes = kr_new
    if key_tile_chunks is not None:
        tm = kr_new.shape[0]
        pos = (pl.program_id(0) % tiles_per_batch) * tm + lax.broadcasted_iota(jnp.int32, (tm, 1), 0)
        lane = lax.broadcasted_iota(jnp.int32, (1, LANES), 1)
        chunk = (pos >> CHUNK_SHIFT) & (key_tile_chunks - 1)
        shared_lanes = kr_new + (lane - MASK_LANE0 == chunk).astype(F32)
    _k_heads(_dot(cb, wk_ref[...]), shared_lanes, jnp.where(nope, gk, 0.0), k_out)


def _vt_spec(tm, tiles_per_batch):
    return pl.BlockSpec((None, VT_TOTAL, tm), lambda i: (i // tiles_per_batch, 0, i % tiles_per_batch))


def _inproj(x, g, w_in, cos, sin, w, *, batch, key_tile=None):
    rows = x.shape[0]
    tm = _row_tile(rows, INPROJ_ROWS)
    tiles_per_batch = rows // batch // tm
    key_tile_chunks = None if key_tile is None else key_tile // CHUNK
    assert key_tile_chunks is None or (key_tile_chunks & (key_tile_chunks - 1) == 0
                                       and key_tile_chunks <= LANES - MASK_LANE0)
    q_norm, w_uq, kv_norm, w_k, w_vt, v_ones, gq, gk, gmean, rot_mat = w
    row_spec = lambda n: pl.BlockSpec((tm, n), lambda i: (i, 0))
    tab_spec = pl.BlockSpec((tm, LANES), lambda i: (i % tiles_per_batch, 0))
    consts = [g, w_in, q_norm, w_uq, kv_norm, w_k, w_vt, v_ones, gq, gk, gmean, rot_mat]
    return pl.pallas_call(
        functools.partial(_inproj_body, tiles_per_batch=tiles_per_batch, key_tile_chunks=key_tile_chunks),
        grid=(rows // tm,),
        in_specs=[row_spec(D_MODEL), _full_spec(g.shape), _full_spec(w_in.shape), tab_spec, tab_spec]
        + [_full_spec(a.shape) for a in consts[2:]],
        out_specs=[row_spec(GDN_QKV), row_spec(GDN_Z), row_spec(LANES), row_spec(MLA_PAD), row_spec(MLA_PAD),
                   _vt_spec(tm, tiles_per_batch), row_spec(MLA_KV_RANK), row_spec(LANES)],
        out_shape=[jax.ShapeDtypeStruct((rows, GDN_QKV), F32), jax.ShapeDtypeStruct((rows, GDN_Z), F32),
                   jax.ShapeDtypeStruct((rows, LANES), F32),
                   jax.ShapeDtypeStruct((rows, MLA_PAD), BF16), jax.ShapeDtypeStruct((rows, MLA_PAD), BF16),
                   jax.ShapeDtypeStruct((batch, VT_TOTAL, rows // batch), BF16),
                   jax.ShapeDtypeStruct((rows, MLA_KV_RANK), F32), jax.ShapeDtypeStruct((rows, LANES), F32)],
        compiler_params=_params(1),
        name="mixer_inproj",
    )(x, g, w_in, cos, sin, *consts[2:])


def _kv_up_body(ckv_ref, kr_ref, wk_ref, wvt_ref, vones_ref, gk_ref, k_out, vt_out):
    nope, _, _ = _lane_masks()
    cb = ckv_ref[...].astype(BF16)
    vt_out[...] = (_dot_nt(wvt_ref[...], cb) + vones_ref[...]).astype(BF16)
    _k_heads(_dot(cb, wk_ref[...]), kr_ref[...], jnp.where(nope, gk_ref[...], 0.0), k_out)


def _kv_up(ckv, krp, w_k, w_vt, v_ones, gk, *, batch):
    rows = ckv.shape[0]
    tm = _row_tile(rows, 512)
    tiles_per_batch = rows // batch // tm
    row_spec = lambda n: pl.BlockSpec((tm, n), lambda i: (i, 0))
    return pl.pallas_call(
        _kv_up_body,
        grid=(rows // tm,),
        in_specs=[row_spec(MLA_KV_RANK), row_spec(LANES), _full_spec(w_k.shape), _full_spec(w_vt.shape),
                  _full_spec(v_ones.shape), _full_spec(gk.shape)],
        out_specs=[row_spec(MLA_PAD), _vt_spec(tm, tiles_per_batch)],
        out_shape=[jax.ShapeDtypeStruct((rows, MLA_PAD), BF16),
                   jax.ShapeDtypeStruct((batch, VT_TOTAL, rows // batch), BF16)],
        compiler_params=_params(1),
        name="mla_kv_up",
    )(ckv, krp, w_k, w_vt, v_ones, gk)


def _attn_body(qi_ref, kj_ref, *refs, tq, tk, n_key_tiles, causal, nk_valid, has_ext):
    if has_ext:
        q_ref, k_ref, vt_ref, ke_ref, vte_ref, o_ref, m_scr, acc_scr, slot_ref = refs
    else:
        q_ref, k_ref, vt_ref, o_ref, m_scr, acc_scr, slot_ref = refs
    step = pl.program_id(1)
    i = qi_ref[step]
    j = kj_ref[step]
    last_j = ((i + 1) * tq + tk - 1) // tk - 1 if causal else n_key_tiles - 1
    head_lanes = [slice(h * LANES, (h + 1) * LANES) for h in range(MLA_HEADS)]

    def head_scores(k_blk, q_mask):
        if q_mask is None:
            q_heads = [q_ref[:, hs] for hs in head_lanes]
        else:
            q_heads = [q_ref[:, hs] + q_mask for hs in head_lanes]
        return [_dot_nt(k_blk[:, hs], qh) for hs, qh in zip(head_lanes, q_heads)]

    def values(vt_blk, h):
        return vt_blk[h * VT_ROWS:(h + 1) * VT_ROWS, :]

    def two_pass(k_blk, vt_blk, cur, bias=None, q_mask=None):
        scores = head_scores(k_blk, q_mask)
        for h in range(MLA_HEADS):
            s = scores[h] if bias is None else scores[h] + bias
            m_old = m_scr[cur, h][0:1]
            m_new = jnp.maximum(m_old, jnp.max(s, 0, keepdims=True))
            alpha = jnp.exp2(m_old - m_new)
            p = jnp.exp2(s - m_new)
            m_scr[cur, h] = jnp.broadcast_to(m_new, (HALO, tq))
            acc_scr[cur, h] = acc_scr[cur, h] * alpha + _dot(values(vt_blk, h), p.astype(BF16))

    def one_pass(k_blk, vt_blk, cur, q_mask=None):
        scores = head_scores(k_blk, q_mask)
        rise = None
        for h in range(MLA_HEADS):
            s = scores[h]
            m_old = m_scr[cur, h][0:1]
            p = jnp.exp2(s - m_old)
            m_tile = jnp.max(s, 0, keepdims=True)
            m_new = jnp.maximum(m_old, m_tile)
            rise = m_tile - m_old if rise is None else jnp.maximum(rise, m_tile - m_old)
            m_scr[1 - cur, h] = jnp.broadcast_to(m_new, (HALO, tq))
            acc_scr[1 - cur, h] = (acc_scr[cur, h] + _dot(values(vt_blk, h), p.astype(BF16))) * jnp.exp2(m_old - m_new)
        return jnp.max(rise)

    def guarded(k_blk, vt_blk, q_mask=None):
        cur = slot_ref[0]
        rise = one_pass(k_blk, vt_blk, cur, q_mask)

        @pl.when(rise <= ONE_PASS_JUMP)
        def _():
            slot_ref[0] = 1 - cur

        @pl.when(jnp.logical_not(rise <= ONE_PASS_JUMP))
        def _():
            two_pass(k_blk, vt_blk, cur, q_mask=q_mask)

    @pl.when(j == 0)
    def _():
        slot_ref[0] = 0
        m_scr[0] = jnp.full(m_scr.shape[1:], NEG_INF, F32)
        acc_scr[0] = jnp.zeros(acc_scr.shape[1:], F32)
        if has_ext:
            two_pass(ke_ref, vte_ref, 0)

    if nk_valid is not None:
        kpos = j * tk + lax.broadcasted_iota(jnp.int32, (tk, 1), 0)
        two_pass(k_ref, vt_ref, slot_ref[0], bias=jnp.where(kpos < nk_valid, 0.0, NEG_INF))
    elif not has_ext:
        two_pass(k_ref, vt_ref, slot_ref[0])
    elif causal:
        @pl.when(j < last_j)
        def _():
            guarded(k_ref, vt_ref)

        def diagonal(n_keys):
            lane = lax.broadcasted_iota(jnp.int32, (1, LANES), 1)
            q_chunk = (i * tq + lax.broadcasted_iota(jnp.int32, (tq, 1), 0)) >> CHUNK_SHIFT
            key_chunk = j * (tk // CHUNK) + lane - MASK_LANE0
            hidden = jnp.logical_and(jnp.logical_and(lane >= MASK_LANE0, lane < MASK_LANE0 + tk // CHUNK),
                                     key_chunk > q_chunk)
            guarded(k_ref.at[0:n_keys, :], vt_ref.at[:, 0:n_keys],
                    q_mask=jnp.where(hidden, -MASK_BIG, 0.0).astype(BF16))

        n_visible = (i + 1) * tq - j * tk
        for n_keys in range(tq, tk + 1, tq):
            pl.when(jnp.logical_and(j == last_j, n_visible == n_keys))(functools.partial(diagonal, n_keys))
    else:
        guarded(k_ref, vt_ref)

    @pl.when(j == last_j)
    def _():
        cur = slot_ref[0]
        for h in range(MLA_HEADS):
            acc = acc_scr[cur, h]
            o_ref[h * MLA_V:(h + 1) * MLA_V, :] = (acc[:MLA_V] / acc[MLA_V:MLA_V + 1]).astype(BF16)


def _attention(q, k, vt, ext, *, tq, tk, causal=False, nk_valid=None):
    b, nq, _ = q.shape
    nk = k.shape[1]
    n_key_tiles = nk // tk
    assert not causal or tk % tq == 0
    pairs = [(i, j) for i in range(nq // tq) for j in range(n_key_tiles) if not causal or j * tk < (i + 1) * tq]
    qi = jnp.asarray([p[0] for p in pairs], jnp.int32)
    kj = jnp.asarray([p[1] for p in pairs], jnp.int32)
    hv = MLA_HEADS * MLA_V
    in_specs = [
        pl.BlockSpec((None, tq, MLA_PAD), lambda bi, s, qi, kj: (bi, qi[s], 0)),
        pl.BlockSpec((None, tk, MLA_PAD), lambda bi, s, qi, kj: (bi, kj[s], 0)),
        pl.BlockSpec((None, VT_TOTAL, tk), lambda bi, s, qi, kj: (bi, 0, kj[s])),
    ]
    args = [q, k, vt]
    if ext is not None:
        ne = ext[0].shape[1]
        shared = ext[0].shape[0] == 1 and b > 1
        emap = (lambda bi, s, qi, kj: (0, 0, 0)) if shared else (lambda bi, s, qi, kj: (bi, 0, 0))
        in_specs += [pl.BlockSpec((None, ne, MLA_PAD), emap), pl.BlockSpec((None, VT_TOTAL, ne), emap)]
        args += list(ext)
    return pl.pallas_call(
        functools.partial(_attn_body, tq=tq, tk=tk, n_key_tiles=n_key_tiles, causal=causal, nk_valid=nk_valid,
                          has_ext=ext is not None),
        grid_spec=pltpu.PrefetchScalarGridSpec(
            num_scalar_prefetch=2,
            grid=(b, len(pairs)),
            in_specs=in_specs,
            out_specs=pl.BlockSpec((None, hv, tq), lambda bi, s, qi, kj: (bi, 0, qi[s])),
            scratch_shapes=[pltpu.VMEM((2, MLA_HEADS, HALO, tq), F32), pltpu.VMEM((2, MLA_HEADS, VT_ROWS, tq), F32),
                            pltpu.SMEM((1,), jnp.int32)]),
        out_shape=jax.ShapeDtypeStruct((b, hv, nq), BF16),
        compiler_params=_params(2),
        name="mla_attention",
    )(qi, kj, *args)


def _sconv_body(x_ref, g_ref, win_ref, cw_ref, wout_ref, prev_ref, o_ref, pout_ref, halo_scr, *, rows):
    t = pl.program_id(1)

    @pl.when(t == 0)
    def _():
        halo_scr[...] = prev_ref[...]

    x = x_ref[...]
    hn = _rms(x, g_ref[...]).astype(BF16)
    bg = _dot(hn, win_ref[:, :D_MODEL])
    u = _dot(hn, win_ref[:, D_MODEL:2 * D_MODEL]) * _dot(hn, win_ref[:, 2 * D_MODEL:])
    halo = halo_scr[...]
    u1 = _shift_rows(u, halo, 1)
    u2 = _shift_rows(u, halo, 2)
    cw = cw_ref[...]
    y = u2 * cw[0:1] + u1 * cw[1:2] + u * cw[2:3]
    last_rows = u[rows - HALO:rows, :]
    halo_scr[...] = last_rows
    o_ref[...] = x + _dot((bg * y).astype(BF16), wout_ref[...])

    @pl.when(t == pl.num_programs(1) - 1)
    def _():
        pout_ref[...] = last_rows


def _sconv(x, prev, w, *, rows):
    b, t_len, _ = x.shape
    g, w_in, conv_w, w_out = w
    shared = prev.shape[0] == 1 and b > 1
    bsel = (lambda bi: 0) if shared else (lambda bi: bi)
    return pl.pallas_call(
        functools.partial(_sconv_body, rows=rows),
        grid=(b, t_len // rows),
        in_specs=[pl.BlockSpec((None, rows, D_MODEL), lambda bi, ti: (bi, ti, 0)), _full_spec(g.shape),
                  _full_spec(w_in.shape), _full_spec(conv_w.shape), _full_spec(w_out.shape),
                  pl.BlockSpec((None, HALO, D_MODEL), lambda bi, ti: (bsel(bi), 0, 0))],
        out_specs=[pl.BlockSpec((None, rows, D_MODEL), lambda bi, ti: (bi, ti, 0)),
                   pl.BlockSpec((None, HALO, D_MODEL), lambda bi, ti: (bi, 0, 0))],
        out_shape=[jax.ShapeDtypeStruct((b, t_len, D_MODEL), F32), jax.ShapeDtypeStruct((b, HALO, D_MODEL), F32)],
        scratch_shapes=[pltpu.VMEM((HALO, D_MODEL), F32)],
        compiler_params=_params(2),
        name="sconv_mixer",
    )(x, g, w_in, conv_w, w_out, prev)


def _pad_lanes(a, before, total):
    return jnp.pad(a, [(0, 0)] * (a.ndim - 1) + [(before, total - before - a.shape[-1])])


def _rope_tables(pos):
    half = MLA_ROPE // 2
    inv = ROPE_THETA ** (-jnp.arange(half, dtype=F32) / half)
    ang = pos.astype(F32)[:, None] * inv
    cos, sin = jnp.cos(ang), jnp.sin(ang)
    n = pos.shape[0]
    tail = LANES - MLA_NOPE - MLA_ROPE
    cos_l = jnp.concatenate([jnp.ones((n, MLA_NOPE), F32), cos, cos, jnp.ones((n, tail), F32)], -1)
    sin_l = jnp.concatenate([jnp.zeros((n, MLA_NOPE), F32), sin, sin, jnp.zeros((n, tail), F32)], -1)
    return cos_l, sin_l


def _head_pair_matrices():
    i = jnp.arange(PAIR)[:, None]
    j = jnp.arange(PAIR)[None, :]
    same = (i // LANES) == (j // LANES)
    li, lj = i % LANES, j % LANES
    half = MLA_ROPE // 2
    rope_end = MLA_NOPE + MLA_ROPE
    in_rope = lambda l: (l >= MLA_NOPE) & (l < rope_end)
    gmean = jnp.where(same & (li < MLA_NOPE) & (lj < MLA_NOPE), 1.0 / MLA_NOPE,
                      jnp.where(same & in_rope(li) & in_rope(lj), 1.0 / MLA_ROPE, 0.0))
    first, second = in_rope(lj) & (lj < MLA_NOPE + half), in_rope(lj) & (lj >= MLA_NOPE + half)
    rot = jnp.where(same & first & (li == lj + half), -1.0, jnp.where(same & second & (li == lj - half), 1.0, 0.0))
    return gmean.astype(BF16), rot.astype(BF16)


def _tail_rows(a, n):
    return jnp.pad(a, ((0, 0), (HALO - n, 0), (0, 0)))


def kernel(x_prompt, x_sample, cache_mla_ckv, cache_mla_krope, state_gdn_S, state_gdn_conv, state_sconv, meta_tokens, ffn1_norm, ffn1_w_gate, ffn1_w_up, ffn1_w_down, ffn2_norm, ffn2_w_gate, ffn2_w_up, ffn2_w_down, mix_norm, ab_w_in, ab_w_out, gdn_conv_w, gdn_A_log, gdn_dt_bias, gdn_o_norm, mla_q_norm, mla_w_uq, mla_kv_norm, mla_w_ukv, mla_qn_norm, mla_qr_norm, mla_kn_norm, mla_kr_norm, sc_w_in, sc_conv_w, sc_w_out):
    bp, tp, _ = x_prompt.shape
    bs, ts, _ = x_sample.shape
    n_frames = bp * tp
    n_sample = bs * ts

    ffn1 = _ffn_weights(ffn1_norm, ffn1_w_gate, ffn1_w_up, ffn1_w_down)
    ffn2 = _ffn_weights(ffn2_norm, ffn2_w_gate, ffn2_w_up, ffn2_w_down)
    w_in = _to_bf16(ab_w_in)[0]
    s1 = GDN_QKV + GDN_Z
    c0 = s1 + 2 * GDN_HEADS
    c1 = c0 + MLA_Q_RANK
    c2 = c1 + MLA_KV_RANK
    gates_rope = jnp.concatenate([
        w_in[:, s1:c0], jnp.zeros((D_MODEL, MLA_NOPE - 2 * GDN_HEADS), BF16), w_in[:, c2:],
        jnp.zeros((D_MODEL, LANES - MLA_NOPE - MLA_ROPE), BF16)], -1)
    w_in_packed = jnp.concatenate([w_in[:, :s1], w_in[:, c0:c1], gates_rope, w_in[:, c1:c2]], -1)
    mix_g = mix_norm.reshape(2, 1, D_MODEL)
    gdn_w = (gdn_conv_w[0], _pad_lanes(gdn_A_log[0][None], 0, LANES), _pad_lanes(gdn_dt_bias[0][None], 0, LANES),
             gdn_o_norm[0][None])
    qdim = MLA_NOPE + MLA_ROPE
    w_uq = _pad_lanes(mla_w_uq[0].reshape(MLA_Q_RANK, MLA_HEADS, qdim), 0, LANES).reshape(MLA_Q_RANK, MLA_PAD)
    w_ukv = mla_w_ukv[0].reshape(MLA_KV_RANK, MLA_HEADS, MLA_NOPE + MLA_V)
    w_k = _pad_lanes(w_ukv[..., :MLA_NOPE], 0, LANES).reshape(MLA_KV_RANK, MLA_PAD)
    hv = MLA_HEADS * MLA_V
    w_vt = _pad_lanes(w_ukv[..., MLA_NOPE:], 0, VT_ROWS).reshape(MLA_KV_RANK, VT_TOTAL).T
    v_ones = (jnp.arange(VT_TOTAL) % VT_ROWS == MLA_V).astype(F32)[:, None]
    gq = _pad_lanes(jnp.concatenate([mla_qn_norm[0], mla_qr_norm[0]])[None], 0, LANES)
    gk = _pad_lanes(jnp.concatenate([mla_kn_norm[0], mla_kr_norm[0]])[None], 0, LANES)
    gmean, rot_mat = _head_pair_matrices()
    mla_w = (mla_q_norm[0][None], w_uq.astype(BF16), mla_kv_norm[0][None], w_k.astype(BF16), w_vt.astype(BF16), v_ones,
             gq, gk, gmean, rot_mat)
    w_out = ab_w_out[0].astype(BF16)
    sc_w = (mix_g[1], _to_bf16(sc_w_in)[0], sc_conv_w[0], sc_w_out[0].astype(BF16))

    xf = x_prompt.reshape(n_frames, D_MODEL)
    xs = jnp.concatenate([x_sample.reshape(n_sample, D_MODEL), meta_tokens.astype(F32)], 0)
    cos_f, sin_f = _rope_tables(N_META + jnp.arange(tp))
    pos_small = jnp.concatenate([jnp.tile(PAST_LEN + jnp.arange(ts), bs), jnp.arange(N_META)])
    cos_s, sin_s = _rope_tables(pos_small)

    xf = _ffn(xf, ffn1[0])
    xs = _ffn(xs, ffn1[0])
    qkv_f, z_f, ab_f, q_f, k_f, vt_f, ckvn_f, krn_f = _inproj(xf, mix_g[0], w_in_packed, cos_f, sin_f, mla_w,
                                                               batch=bp, key_tile=ATTN_TK)
    qkv_s, z_s, ab_s, q_s, k_s, vt_s, ckvn_s, krn_s = _inproj(xs, mix_g[0], w_in_packed, cos_s, sin_s, mla_w, batch=1)

    pad_meta = lambda a: jnp.pad(a[n_sample:], ((0, CHUNK - N_META), (0, 0)))[None]
    og_m, s_m, conv_m = _gdn(pad_meta(qkv_s), pad_meta(z_s), pad_meta(ab_s),
                             jnp.zeros((1, GDN_HEADS, GDN_DK, GDN_DV), F32), jnp.zeros((1, HALO, GDN_QKV), F32),
                             gdn_w, rows=CHUNK, n_valid=N_META)
    og_f, p_s, p_conv = _gdn(qkv_f.reshape(bp, tp, GDN_QKV), z_f.reshape(bp, tp, GDN_Z), ab_f.reshape(bp, tp, LANES),
                             s_m, conv_m, gdn_w, rows=GDN_ROWS)
    og_s, s_s, s_conv = _gdn(qkv_s[:n_sample].reshape(bs, ts, GDN_QKV), z_s[:n_sample].reshape(bs, ts, GDN_Z),
                             ab_s[:n_sample].reshape(bs, ts, LANES), state_gdn_S[0],
                             _tail_rows(state_gdn_conv[0], state_gdn_conv.shape[2]), gdn_w, rows=ts)

    k_c, vt_c = _kv_up(cache_mla_ckv[0].reshape(bs * PAST_LEN, MLA_KV_RANK),
                       _pad_lanes(cache_mla_krope[0].reshape(bs * PAST_LEN, MLA_ROPE), MLA_NOPE, LANES),
                       mla_w[3], mla_w[4], v_ones, gk, batch=bs)
    k_meta, vt_meta = k_s[n_sample:][None], vt_s[:, :, n_sample:]
    vt_new = vt_s[0, :, :n_sample].reshape(VT_TOTAL, bs, ts).transpose(1, 0, 2)
    omt_f = _attention(q_f.reshape(bp, tp, MLA_PAD), k_f.reshape(bp, tp, MLA_PAD), vt_f, (k_meta, vt_meta),
                       tq=ATTN_TQ, tk=ATTN_TK, causal=True)
    omt_s = _attention(q_s[:n_sample].reshape(bs, ts, MLA_PAD), k_c.reshape(bs, PAST_LEN, MLA_PAD), vt_c,
                       (k_s[:n_sample].reshape(bs, ts, MLA_PAD), vt_new), tq=ts, tk=512)
    omt_m = _attention(jnp.pad(q_s[n_sample:], ((0, LANES - N_META), (0, 0)))[None],
                       jnp.pad(k_s[n_sample:], ((0, CHUNK - N_META), (0, 0)))[None],
                       jnp.pad(vt_meta, ((0, 0), (0, 0), (0, CHUNK - N_META))),
                       None, tq=LANES, tk=CHUNK, nk_valid=N_META)

    og_small = jnp.concatenate([og_s.reshape(n_sample, GDN_Z), og_m[0, :N_META]], 0)
    omt_small = jnp.concatenate([omt_s.transpose(1, 0, 2).reshape(hv, n_sample), omt_m[0, :, :N_META]], -1)[None]
    xf = _ffn(xf, ffn2[0], mix=(og_f.reshape(n_frames, GDN_Z), omt_f, w_out))
    xs = _ffn(xs, ffn2[0], mix=(og_small, omt_small, w_out))

    xf = _ffn(xf, ffn1[1])
    xs = _ffn(xs, ffn1[1])
    n_sc = state_sconv.shape[2]
    xm, sc_m = _sconv(xs[n_sample:][None], jnp.zeros((1, HALO, D_MODEL), F32), sc_w, rows=N_META)
    xf3, p_sc = _sconv(xf.reshape(bp, tp, D_MODEL), sc_m, sc_w, rows=SCONV_ROWS)
    xs3, s_sc = _sconv(xs[:n_sample].reshape(bs, ts, D_MODEL), _tail_rows(state_sconv[0], n_sc), sc_w, rows=ts)
    xf = _ffn(xf3.reshape(n_frames, D_MODEL), ffn2[1])
    xs = _ffn(jnp.concatenate([xs3.reshape(n_sample, D_MODEL), xm[0]], 0), ffn2[1])

    y_prompt = xf.reshape(bp, tp, D_MODEL)
    y_sample = xs[:n_sample].reshape(bs, ts, D_MODEL)
    rope_lo, rope_hi = MLA_NOPE, MLA_NOPE + MLA_ROPE
    meta_ckv = jnp.broadcast_to(ckvn_s[n_sample:][None], (bp, N_META, MLA_KV_RANK))
    meta_kr = jnp.broadcast_to(krn_s[n_sample:, rope_lo:rope_hi][None], (bp, N_META, MLA_ROPE))
    p_ckv = jnp.concatenate([meta_ckv, ckvn_f.reshape(bp, tp, MLA_KV_RANK)], 1)
    p_kr = jnp.concatenate([meta_kr, krn_f[:, rope_lo:rope_hi].reshape(bp, tp, MLA_ROPE)], 1)
    n_gc = state_gdn_conv.shape[2]
    return (y_prompt, y_sample,
            p_ckv[None], p_kr[None], p_s[None], p_conv[:, HALO - n_gc:][None], p_sc[:, HALO - n_sc:][None],
            ckvn_s[:n_sample].reshape(bs, ts, MLA_KV_RANK)[None],
            krn_s[:n_sample, rope_lo:rope_hi].reshape(bs, ts, MLA_ROPE)[None],
            s_s[None], s_conv[:, HALO - n_gc:][None], s_sc[:, HALO - n_sc:][None])
```

```python
import functools

import jax
import jax.numpy as jnp
from jax import lax
from jax.experimental import pallas as pl
from jax.experimental.pallas import tpu as pltpu

F32 = jnp.float32
BF16 = jnp.bfloat16

D_MODEL = 1024
D_FF = 2816
CHUNK = 64
CHUNK_SHIFT = 6
N_META = 16
EPS = 1e-6
GDN_HEADS = 4
GDN_DK = 128
GDN_DV = 128
GDN_QKV = GDN_HEADS * (2 * GDN_DK + GDN_DV)
GDN_Z = GDN_HEADS * GDN_DV
MLA_HEADS = 8
MLA_NOPE = 64
MLA_ROPE = 32
MLA_V = 64
MLA_Q_RANK = 384
MLA_KV_RANK = 256
ROPE_THETA = 10000.0
PAST_LEN = 1024
LANES = 128
HALO = 8
MLA_PAD = MLA_HEADS * LANES
FF_BLOCK = 256
FF_CHUNKS = D_FF // FF_BLOCK
BF16_SUBLANES = 16
VMEM_LIMIT = 60000 * 1024
NEG_INF = float("-inf")
LOG2_E = 1.4426950408889634
MASK_LANE0 = MLA_NOPE + MLA_ROPE
MASK_BIG = 2.0 ** 100
ATTN_TQ = 512
ATTN_TK = 1024
FFN_ROWS = 1024
INPROJ_ROWS = 512
SCONV_ROWS = 1024
GDN_ROWS = 8 * CHUNK
GDN_GROUP = 4
VT_ROWS = MLA_V + BF16_SUBLANES
VT_TOTAL = MLA_HEADS * VT_ROWS
ONE_PASS_JUMP = 64.0

IN_QKV = 0
IN_Z = IN_QKV + GDN_QKV
IN_CQ = IN_Z + GDN_Z
IN_GATES_ROPE = IN_CQ + MLA_Q_RANK
IN_CKV = IN_GATES_ROPE + LANES
IN_TOTAL = IN_CKV + MLA_KV_RANK


def _params(n_axes):
    return pltpu.CompilerParams(dimension_semantics=("arbitrary",) * n_axes, vmem_limit_bytes=VMEM_LIMIT)


def _rms(x, g):
    return x * lax.rsqrt(jnp.mean(x * x, -1, keepdims=True) + EPS) * g


def _sigmoid(x):
    return 1.0 / (1.0 + jnp.exp(-x))


def _silu(x):
    return x * _sigmoid(x)


def _softplus(x):
    return jnp.maximum(x, 0.0) + jnp.log1p(jnp.exp(-jnp.abs(x)))


def _dot(a, b):
    return jnp.dot(a, b, preferred_element_type=F32)


def _dot_nt(a, b):
    return lax.dot_general(a, b, (((1,), (1,)), ((), ())), preferred_element_type=F32)


def _dot_tn(a, b):
    return lax.dot_general(a, b, (((0,), (0,)), ((), ())), preferred_element_type=F32)


def _shift_rows(x, history, k):
    rolled = pltpu.roll(x, k, 0)
    row = lax.broadcasted_iota(jnp.int32, (HALO, 1), 0)
    head = rolled[:HALO]
    for j in range(k):
        head = jnp.where(row == j, history[HALO - k + j:HALO - k + j + 1, :], head)
    return jnp.concatenate([head, rolled[HALO:]], 0)


def _full_spec(shape):
    zeros = (0,) * len(shape)
    return pl.BlockSpec(shape, lambda *_: zeros)


def _row_tile(rows, want):
    return want if rows % want == 0 else rows


def _ffn_body(x_ref, g_ref, wg_ref, wu_ref, wd_ref, *rest, mixed):
    if mixed:
        og_ref, omt_ref, wmix_ref, o_ref, act_ref = rest
        x = x_ref[...] + _dot(og_ref[...], wmix_ref[:GDN_Z, :]) + _dot_tn(omt_ref[...], wmix_ref[GDN_Z:, :])
    else:
        o_ref, act_ref = rest
        x = x_ref[...]
    h = _rms(x, g_ref[...]).astype(BF16)
    for c in range(FF_CHUNKS):
        cols = slice(c * FF_BLOCK, (c + 1) * FF_BLOCK)
        gate = _dot(h, wg_ref[:, cols])
        up = _dot(h, wu_ref[:, cols])
        act_ref[:, cols] = (_silu(gate) * up).astype(BF16)
    o_ref[...] = x + 0.5 * _dot(act_ref[...], wd_ref[...])


def _ffn(x, w, mix=None):
    rows = x.shape[0]
    tm = _row_tile(rows, FFN_ROWS)
    g, wg, wu, wd, layer = w
    resident = lambda a: pl.BlockSpec(a.shape, lambda i: (0,) * a.ndim, pipeline_mode=pl.Buffered(1))
    of_layer = lambda a: pl.BlockSpec((None,) + a.shape[1:], lambda i: (layer, 0, 0), pipeline_mode=pl.Buffered(1))
    in_specs = [pl.BlockSpec((tm, D_MODEL), lambda i: (i, 0)), _full_spec(g.shape), of_layer(wg), of_layer(wu),
                of_layer(wd)]
    args = [x, g, wg, wu, wd]
    if mix is not None:
        og, omt, w_mix = mix
        tiles_per_batch = rows // omt.shape[0] // tm
        in_specs += [pl.BlockSpec((tm, GDN_Z), lambda i: (i, 0)),
                     pl.BlockSpec((None, GDN_Z, tm), lambda i: (i // tiles_per_batch, 0, i % tiles_per_batch)),
                     resident(w_mix)]
        args += [og, omt, w_mix]
    return pl.pallas_call(
        functools.partial(_ffn_body, mixed=mix is not None),
        grid=(rows // tm,),
        in_specs=in_specs,
        out_specs=pl.BlockSpec((tm, D_MODEL), lambda i: (i, 0)),
        out_shape=jax.ShapeDtypeStruct((rows, D_MODEL), F32),
        scratch_shapes=[pltpu.VMEM((tm, D_FF), BF16)],
        compiler_params=_params(1),
        name="half_ffn",
    )(*args)


def _to_bf16_body(x_ref, o_ref):
    o_ref[...] = x_ref[...].astype(BF16)


def _to_bf16(a):
    a2 = a.reshape(-1, a.shape[-1])
    rows, cols = a2.shape
    tm = _row_tile(rows, 512)
    out = pl.pallas_call(
        _to_bf16_body,
        grid=(rows // tm,),
        in_specs=[pl.BlockSpec((tm, cols), lambda i: (i, 0))],
        out_specs=pl.BlockSpec((tm, cols), lambda i: (i, 0)),
        out_shape=jax.ShapeDtypeStruct((rows, cols), BF16),
        compiler_params=_params(1),
        name="weights_to_bf16",
    )(a2)
    return out.reshape(a.shape)


def _ffn_weights(norm, stacks):
    return [(norm[l].reshape(1, D_MODEL),) + tuple(stacks) + (l,) for l in range(norm.shape[0])]


def _gdn_conv(x, history, cw):
    y = (_shift_rows(x, history, 3) * cw[0:1] + _shift_rows(x, history, 2) * cw[1:2]
         + _shift_rows(x, history, 1) * cw[2:3] + x * cw[3:4])
    return _silu(y)


class _Split:
    def __init__(self, x):
        self.x = x
        self._parts = None
        self._diag = None

    def parts(self):
        if self._parts is None:
            hi = self.x.astype(BF16)
            self._parts = (hi, (self.x - hi.astype(F32)).astype(BF16))
        return self._parts

    def diag_parts(self):
        if self._diag is None:
            first = lax.broadcasted_iota(jnp.int32, self.x.shape, 1) < CHUNK
            zero = jnp.zeros(self.x.shape, BF16)
            self._diag = tuple(jnp.concatenate([jnp.where(first, p, zero), jnp.where(first, zero, p)], 0)
                               for p in self.parts())
        return self._diag


def _split_products(pairs):
    parts = [(a.parts(), b.diag_parts()) for a, b in pairs]
    outs = []
    for (a_hi, a_lo), (b_hi, b_lo) in parts:
        a2 = jnp.concatenate([a_hi, a_lo], 0)
        r = _dot(a2, b_hi) + _dot(a2, b_lo)
        half = a_hi.shape[0]
        outs.append(_Split(r[:half] + r[half:]))
    return outs


def _unit_lower_inverses(ns, eye_f, block_masks):
    k = len(ns)
    nd = [_Split(jnp.where(block_masks[0], n, 0.0)) for n in ns]
    n2 = _split_products([(x, x) for x in nd])
    yield None
    lvl = _split_products([(x, x) for x in n2] +
                          [(_Split(eye_f - a.x), _Split(eye_f + b.x)) for a, b in zip(nd, n2)])
    yield None
    n4, t = lvl[:k], lvl[k:]
    t = _split_products([(a, _Split(eye_f + b.x)) for a, b in zip(t, n4)])
    yield None
    for done, merged in zip(block_masks, block_masks[1:] + [None]):
        keep = jnp.logical_not(done) if merged is None else jnp.logical_and(merged, jnp.logical_not(done))
        off = [_Split(jnp.where(keep, n, 0.0)) for n in ns]
        x = _split_products(list(zip(off, t)))
        yield None
        y = _split_products(list(zip(t, x)))
        yield None
        t = [_Split(a.x - b.x) for a, b in zip(t, y)]
    yield [a.x for a in t]


def _gdn_body(qkv_ref, z_ref, ab_ref, s0_ref, cprev_ref, cw_ref, alog_ref, dtb_ref, on_ref,
              og_ref, sout_ref, cout_ref, s_scr, halo_scr, *, rows, n_valid):
    t = pl.program_id(1)
    C = CHUNK
    row = lax.broadcasted_iota(jnp.int32, (rows, 1), 0)

    @pl.when(t == 0)
    def _():
        s_scr[...] = s0_ref[...]
        halo_scr[...] = cprev_ref[...]

    x = qkv_ref[...]
    y = _gdn_conv(x, halo_scr[...], cw_ref[...])
    last_rows = x[n_valid - HALO:n_valid, :]
    halo_scr[...] = last_rows

    ab = ab_ref[...]
    g_all = -jnp.exp(alog_ref[...]) * _softplus(ab + dtb_ref[...])
    b_all = _sigmoid(ab)
    if n_valid < rows:
        valid = row < n_valid
        y = jnp.where(valid, y, 0.0)
        g_all = jnp.where(valid, g_all, 0.0)
        b_all = jnp.where(valid, b_all, 0.0)

    ii = lax.broadcasted_iota(jnp.int32, (C, LANES), 0)
    pair_lane = lax.broadcasted_iota(jnp.int32, (C, LANES), 1)
    jj = pair_lane & (C - 1)
    first_head = pair_lane < C
    causal = ii >= jj
    strict = ii > jj
    eye_f = (ii == jj).astype(F32)
    block_masks = [(ii >> s) == (jj >> s) for s in (3, 4, 5)]
    on = on_ref[...]
    n_chunks = rows // C
    heads = range(GDN_HEADS)

    ri = lax.broadcasted_iota(jnp.int32, (rows, rows), 0)
    rj = lax.broadcasted_iota(jnp.int32, (rows, rows), 1)
    tril_blk = jnp.logical_and(ri >= rj, (ri >> CHUNK_SHIFT) == (rj >> CHUNK_SHIFT)).astype(BF16)
    g_hi = g_all.astype(BF16)
    g_r1 = g_all - g_hi.astype(F32)
    g_mid = g_r1.astype(BF16)
    g_lo = (g_r1 - g_mid.astype(F32)).astype(BF16)
    gcum = _dot(tril_blk, g_hi) + _dot(tril_blk, g_mid) + _dot(tril_blk, g_lo)
    egcum = jnp.exp(gcum)

    qn, kn, vn = [], [], []
    for h in heads:
        qh = y[:, h * GDN_DK:(h + 1) * GDN_DK]
        kh = y[:, GDN_HEADS * GDN_DK + h * GDN_DK:GDN_HEADS * GDN_DK + (h + 1) * GDN_DK]
        qn.append(qh * lax.rsqrt(jnp.sum(qh * qh, -1, keepdims=True) + EPS) * (GDN_DK ** -0.5))
        kn.append(kh * lax.rsqrt(jnp.sum(kh * kh, -1, keepdims=True) + EPS))
        vn.append(y[:, 2 * GDN_HEADS * GDN_DK + h * GDN_DV:2 * GDN_HEADS * GDN_DK + (h + 1) * GDN_DV])

    head_pairs = [(a, a + 1) for a in range(0, GDN_HEADS, 2)]
    pre = {}

    def chunk_local(chunks):
        chains = [(c, pair) for c in chunks for pair in head_pairs]
        n_mats = []
        for c in chunks:
            sl = slice(c * C, (c + 1) * C)
            gcum_t = gcum[sl].T
            for a, b in head_pairs:
                col = lambda arr, h: arr[sl, h:h + 1]
                gcol = jnp.where(first_head, col(gcum, a), col(gcum, b))
                beta2 = jnp.where(first_head, col(b_all, GDN_HEADS + a), col(b_all, GDN_HEADS + b))
                grow = jnp.concatenate([gcum_t[a:a + 1, :], gcum_t[b:b + 1, :]], -1)
                dec_causal = jnp.exp(jnp.where(causal, gcol - grow, NEG_INF))
                kb = {h: kn[h][sl].astype(BF16) for h in (a, b)}
                kk = jnp.concatenate([_dot_nt(kb[a], kb[a]), _dot_nt(kb[b], kb[b])], -1)
                n_mats.append(beta2 * kk * jnp.where(strict, dec_causal, 0.0))
                qk = jnp.concatenate([_dot_nt(qn[h][sl].astype(BF16), kb[h]) for h in (a, b)], -1)
                qk = (qk * dec_causal).astype(BF16)
                for h, lanes in ((a, slice(0, C)), (b, slice(C, 2 * C))):
                    beta = col(b_all, GDN_HEADS + h)
                    eg = col(egcum, h)
                    g_last = gcum[(c + 1) * C - 1:(c + 1) * C, h:h + 1]
                    qh, kh, vh = qn[h][sl], kn[h][sl], vn[h][sl]
                    pre[c, h] = dict(
                        qk=qk[:, lanes],
                        rhs=jnp.concatenate([(beta * vh).astype(BF16), ((beta * eg) * kh).astype(BF16)], -1),
                        qd=(qh * eg).astype(BF16),
                        kt=(kh * jnp.exp(g_last - col(gcum, h))).astype(BF16),
                        decay=jnp.exp(g_last))
            yield
        inverses = None
        for inverses in _unit_lower_inverses(n_mats, eye_f, block_masks):
            yield
        for (c, (a, b)), tinv in zip(chains, inverses):
            tinv = tinv.astype(BF16)
            for h, lanes in ((a, slice(0, C)), (b, slice(C, 2 * C))):
                uw = _dot(tinv[:, lanes], pre[c, h]["rhs"])
                pre[c, h]["u"] = uw[:, :GDN_DV]
                pre[c, h]["wk"] = uw[:, GDN_DV:].astype(BF16)

    state = [s_scr[h] for h in heads]

    def recurrence(chunks):
        for c in chunks:
            sl = slice(c * C, (c + 1) * C)
            sb = [s.astype(BF16) for s in state]
            w = [(pre[c, h]["u"] - _dot(pre[c, h]["wk"], sb[h])).astype(BF16) for h in heads]
            yield
            o = [_dot(pre[c, h]["qd"], sb[h]) + _dot(pre[c, h]["qk"], w[h]) for h in heads]
            for h in heads:
                state[h] = state[h] * pre[c, h]["decay"] + _dot_tn(pre[c, h]["kt"], w[h])
            yield
            for h in heads:
                gated = _rms(o[h], on) * _silu(z_ref[sl, h * GDN_DV:(h + 1) * GDN_DV])
                og_ref[sl, h * GDN_DV:(h + 1) * GDN_DV] = gated.astype(BF16)
            yield

    group = min(n_chunks, GDN_GROUP)
    pending = iter(())
    for g0 in range(0, n_chunks, group):
        chunks = range(g0, g0 + group)
        for _ in chunk_local(chunks):
            next(pending, None)
        for _ in pending:
            pass
        pending = recurrence(chunks)
    for _ in pending:
        pass
    for h in heads:
        s_scr[h] = state[h]

    @pl.when(t == pl.num_programs(1) - 1)
    def _():
        sout_ref[...] = s_scr[...]
        cout_ref[...] = last_rows


def _gdn(qkv, z, ab, s0, cprev, w, *, rows, n_valid=None):
    b, t_len, _ = qkv.shape
    n_valid = rows if n_valid is None else n_valid
    conv_w, a_log, dt_bias, o_norm = w
    shared = s0.shape[0] == 1 and b > 1
    bsel = (lambda bi: 0) if shared else (lambda bi: bi)
    return pl.pallas_call(
        functools.partial(_gdn_body, rows=rows, n_valid=n_valid),
        grid=(b, t_len // rows),
        in_specs=[
            pl.BlockSpec((None, rows, GDN_QKV), lambda bi, ti: (bi, ti, 0)),
            pl.BlockSpec((None, rows, GDN_Z), lambda bi, ti: (bi, ti, 0)),
            pl.BlockSpec((None, rows, LANES), lambda bi, ti: (bi, ti, 0)),
            pl.BlockSpec((None, GDN_HEADS, GDN_DK, GDN_DV), lambda bi, ti: (bsel(bi), 0, 0, 0)),
            pl.BlockSpec((None, HALO, GDN_QKV), lambda bi, ti: (bsel(bi), 0, 0)),
            _full_spec(conv_w.shape), _full_spec(a_log.shape), _full_spec(dt_bias.shape), _full_spec(o_norm.shape),
        ],
        out_specs=[
            pl.BlockSpec((None, rows, GDN_Z), lambda bi, ti: (bi, ti, 0)),
            pl.BlockSpec((None, GDN_HEADS, GDN_DK, GDN_DV), lambda bi, ti: (bi, 0, 0, 0)),
            pl.BlockSpec((None, HALO, GDN_QKV), lambda bi, ti: (bi, 0, 0)),
        ],
        out_shape=[
            jax.ShapeDtypeStruct((b, t_len, GDN_Z), BF16),
            jax.ShapeDtypeStruct((b, GDN_HEADS, GDN_DK, GDN_DV), F32),
            jax.ShapeDtypeStruct((b, HALO, GDN_QKV), F32),
        ],
        scratch_shapes=[pltpu.VMEM((GDN_HEADS, GDN_DK, GDN_DV), F32), pltpu.VMEM((HALO, GDN_QKV), F32)],
        compiler_params=_params(2),
        name="gdn_mixer",
    )(qkv, z, ab, s0, cprev, conv_w, a_log, dt_bias, o_norm)


def _lane_masks():
    lane = lax.broadcasted_iota(jnp.int32, (1, LANES), 1)
    nope = lane < MLA_NOPE
    rope_a = jnp.logical_and(lane >= MLA_NOPE, lane < MLA_NOPE + MLA_ROPE // 2)
    rope_b = jnp.logical_and(lane >= MLA_NOPE + MLA_ROPE // 2, lane < MLA_NOPE + MLA_ROPE)
    return nope, rope_a, rope_b


def _rope_block(x, cos, sin, rope_a, rope_b):
    half = MLA_ROPE // 2
    rot = jnp.where(rope_a, -pltpu.roll(x, LANES - half, 1), jnp.where(rope_b, pltpu.roll(x, half, 1), 0.0))
    return x * cos + rot * sin


PAIR = 2 * LANES


def _twice(a):
    return jnp.concatenate([a, a], -1)


def _k_heads(k, shared, gk, k_out):
    for h in range(MLA_HEADS):
        x = k[:, h * LANES:(h + 1) * LANES]
        inv = lax.rsqrt(jnp.sum(x * x, -1, keepdims=True) / MLA_NOPE + EPS)
        k_out[:, h * LANES:(h + 1) * LANES] = (x * inv * gk + shared).astype(BF16)


def _inproj_body(x_ref, g_ref, w_ref, cos_ref, sin_ref, qn_ref, wuq_ref, kvn_ref, wk_ref, wvt_ref, vones_ref,
                 gq_ref, gk_ref, gmean_ref, rot_ref, *rest, tiles_per_batch, key_tile_chunks, n_cast):
    cast_in, rest = rest[:n_cast], rest[n_cast:]
    qkv_ref, z_ref, ab_ref, q_out, k_out, vt_out, ckv_out, kr_out = rest[:8]
    for src, dst in zip(cast_in, rest[8:]):
        dst[...] = src[...].astype(BF16)
    hn = _rms(x_ref[...], g_ref[...]).astype(BF16)
    qkv_ref[...] = _dot(hn, w_ref[:, IN_QKV:IN_Z])
    z_ref[...] = _dot(hn, w_ref[:, IN_Z:IN_CQ])
    cq_gates_rope = _dot(hn, w_ref[:, IN_CQ:IN_CKV])
    cq = cq_gates_rope[:, :MLA_Q_RANK]
    gates_rope = cq_gates_rope[:, MLA_Q_RANK:]
    ab_ref[...] = gates_rope
    ckv = _dot(hn, w_ref[:, IN_CKV:IN_TOTAL])

    nope, rope_a, rope_b = _lane_masks()
    krp = jnp.where(jnp.logical_or(rope_a, rope_b), gates_rope, 0.0)
    cos = cos_ref[...]
    sin = sin_ref[...]
    gk = gk_ref[...]
    gmean = gmean_ref[...]
    rot_mat = rot_ref[...]
    gq2 = _twice(gq_ref[...] * ((MLA_NOPE + MLA_ROPE) ** -0.5 * LOG2_E))
    cos2, sin2 = _twice(cos), _twice(sin)

    cqn = _rms(cq, qn_ref[...]).astype(BF16)
    q = _dot(cqn, wuq_ref[...])
    for p in range(MLA_HEADS // 2):
        x = q[:, p * PAIR:(p + 1) * PAIR]
        ms = _dot((x * x).astype(BF16), gmean)
        x = x * lax.rsqrt(ms + EPS) * gq2
        q_out[:, p * PAIR:(p + 1) * PAIR] = (x * cos2 + _dot(x.astype(BF16), rot_mat) * sin2).astype(BF16)

    ckvn = _rms(ckv, kvn_ref[...])
    ckv_out[...] = ckvn
    cb = ckvn.astype(BF16)
    krn = krp * lax.rsqrt(jnp.sum(krp * krp, -1, keepdims=True) / MLA_ROPE + EPS) * jnp.where(nope, 0.0, gk)
    kr_new = _rope_block(krn, cos, sin, rope_a, rope_b)
    kr_out[...] = kr_new
    vt_out[...] = (_dot_nt(wvt_ref[...], cb) + vones_ref[...]).astype(BF16)
    shared_lanes = kr_new
    if key_tile_chunks is not None:
        tm = kr_new.shape[0]
        pos = (pl.program_id(0) % tiles_per_batch) * tm + lax.broadcasted_iota(jnp.int32, (tm, 1), 0)
        lane = lax.broadcasted_iota(jnp.int32, (1, LANES), 1)
        chunk = (pos >> CHUNK_SHIFT) & (key_tile_chunks - 1)
        shared_lanes = kr_new + (lane - MASK_LANE0 == chunk).astype(F32)
    _k_heads(_dot(cb, wk_ref[...]), shared_lanes, jnp.where(nope, gk, 0.0), k_out)


def _vt_spec(tm, tiles_per_batch):
    return pl.BlockSpec((None, VT_TOTAL, tm), lambda i: (i // tiles_per_batch, 0, i % tiles_per_batch))


def _inproj(x, g, w_in, cos, sin, w, *, batch, key_tile=None, cast=()):
    rows = x.shape[0]
    tm = _row_tile(rows, INPROJ_ROWS)
    steps = rows // tm
    tiles_per_batch = rows // batch // tm
    key_tile_chunks = None if key_tile is None else key_tile // CHUNK
    assert key_tile_chunks is None or (key_tile_chunks & (key_tile_chunks - 1) == 0
                                       and key_tile_chunks <= LANES - MASK_LANE0)
    q_norm, w_uq, kv_norm, w_k, w_vt, v_ones, gq, gk, gmean, rot_mat = w
    row_spec = lambda n: pl.BlockSpec((tm, n), lambda i: (i, 0))
    tab_spec = pl.BlockSpec((tm, LANES), lambda i: (i % tiles_per_batch, 0))
    consts = [g, w_in, q_norm, w_uq, kv_norm, w_k, w_vt, v_ones, gq, gk, gmean, rot_mat]
    cast2d = [a.reshape(-1, a.shape[-1]) for a in cast]
    cast_specs = []
    for a in cast2d:
        share = 1 if (a.shape[0] // steps) % BF16_SUBLANES == 0 else 2
        slab = a.shape[0] * share // steps
        assert slab % BF16_SUBLANES == 0 and slab * steps == a.shape[0] * share
        cast_specs.append(pl.BlockSpec((slab, a.shape[1]), lambda i, share=share: (i // share, 0)))
    outs = pl.pallas_call(
        functools.partial(_inproj_body, tiles_per_batch=tiles_per_batch, key_tile_chunks=key_tile_chunks,
                          n_cast=len(cast)),
        grid=(steps,),
        in_specs=[row_spec(D_MODEL), _full_spec(g.shape), _full_spec(w_in.shape), tab_spec, tab_spec]
        + [_full_spec(a.shape) for a in consts[2:]] + cast_specs,
        out_specs=[row_spec(GDN_QKV), row_spec(GDN_Z), row_spec(LANES), row_spec(MLA_PAD), row_spec(MLA_PAD),
                   _vt_spec(tm, tiles_per_batch), row_spec(MLA_KV_RANK), row_spec(LANES)] + cast_specs,
        out_shape=[jax.ShapeDtypeStruct((rows, GDN_QKV), F32), jax.ShapeDtypeStruct((rows, GDN_Z), F32),
                   jax.ShapeDtypeStruct((rows, LANES), F32),
                   jax.ShapeDtypeStruct((rows, MLA_PAD), BF16), jax.ShapeDtypeStruct((rows, MLA_PAD), BF16),
                   jax.ShapeDtypeStruct((batch, VT_TOTAL, rows // batch), BF16),
                   jax.ShapeDtypeStruct((rows, MLA_KV_RANK), F32), jax.ShapeDtypeStruct((rows, LANES), F32)]
        + [jax.ShapeDtypeStruct(a.shape, BF16) for a in cast2d],
        compiler_params=_params(1),
        name="mixer_inproj",
    )(x, g, w_in, cos, sin, *consts[2:], *cast2d)
    outs = list(outs)
    return outs[:8] + [o.reshape(a.shape) for o, a in zip(outs[8:], cast)]


def _kv_up_body(ckv_ref, kr_ref, wk_ref, wvt_ref, vones_ref, gk_ref, k_out, vt_out):
    nope, _, _ = _lane_masks()
    cb = ckv_ref[...].astype(BF16)
    vt_out[...] = (_dot_nt(wvt_ref[...], cb) + vones_ref[...]).astype(BF16)
    _k_heads(_dot(cb, wk_ref[...]), kr_ref[...], jnp.where(nope, gk_ref[...], 0.0), k_out)


def _kv_up(ckv, krp, w_k, w_vt, v_ones, gk, *, batch):
    rows = ckv.shape[0]
    tm = _row_tile(rows, 512)
    tiles_per_batch = rows // batch // tm
    row_spec = lambda n: pl.BlockSpec((tm, n), lambda i: (i, 0))
    return pl.pallas_call(
        _kv_up_body,
        grid=(rows // tm,),
        in_specs=[row_spec(MLA_KV_RANK), row_spec(LANES), _full_spec(w_k.shape), _full_spec(w_vt.shape),
                  _full_spec(v_ones.shape), _full_spec(gk.shape)],
        out_specs=[row_spec(MLA_PAD), _vt_spec(tm, tiles_per_batch)],
        out_shape=[jax.ShapeDtypeStruct((rows, MLA_PAD), BF16),
                   jax.ShapeDtypeStruct((batch, VT_TOTAL, rows // batch), BF16)],
        compiler_params=_params(1),
        name="mla_kv_up",
    )(ckv, krp, w_k, w_vt, v_ones, gk)


def _attn_body(qi_ref, kj_ref, *refs, tq, tk, n_key_tiles, causal, nk_valid, has_ext):
    if has_ext:
        q_ref, k_ref, vt_ref, ke_ref, vte_ref, o_ref, m_scr, acc_scr, slot_ref = refs
    else:
        q_ref, k_ref, vt_ref, o_ref, m_scr, acc_scr, slot_ref = refs
    step = pl.program_id(1)
    i = qi_ref[step]
    j = kj_ref[step]
    last_j = ((i + 1) * tq + tk - 1) // tk - 1 if causal else n_key_tiles - 1
    head_lanes = [slice(h * LANES, (h + 1) * LANES) for h in range(MLA_HEADS)]

    def head_scores(k_blk, q_mask):
        if q_mask is None:
            q_heads = [q_ref[:, hs] for hs in head_lanes]
        else:
            q_heads = [q_ref[:, hs] + q_mask for hs in head_lanes]
        return [_dot_nt(k_blk[:, hs], qh) for hs, qh in zip(head_lanes, q_heads)]

    def values(vt_blk, h):
        return vt_blk[h * VT_ROWS:(h + 1) * VT_ROWS, :]

    def two_pass(k_blk, vt_blk, cur, bias=None, q_mask=None):
        scores = head_scores(k_blk, q_mask)
        for h in range(MLA_HEADS):
            s = scores[h] if bias is None else scores[h] + bias
            m_old = m_scr[cur, h][0:1]
            m_new = jnp.maximum(m_old, jnp.max(s, 0, keepdims=True))
            alpha = jnp.exp2(m_old - m_new)
            p = jnp.exp2(s - m_new)
            m_scr[cur, h] = jnp.broadcast_to(m_new, (HALO, tq))
            acc_scr[cur, h] = acc_scr[cur, h] * alpha + _dot(values(vt_blk, h), p.astype(BF16))

    def one_pass(k_blk, vt_blk, cur, q_mask=None):
        scores = head_scores(k_blk, q_mask)
        rise = None
        for h in range(MLA_HEADS):
            s = scores[h]
            m_old = m_scr[cur, h][0:1]
            p = jnp.exp2(s - m_old)
            m_tile = jnp.max(s, 0, keepdims=True)
            m_new = jnp.maximum(m_old, m_tile)
            rise = m_tile - m_old if rise is None else jnp.maximum(rise, m_tile - m_old)
            m_scr[1 - cur, h] = jnp.broadcast_to(m_new, (HALO, tq))
            acc_scr[1 - cur, h] = (acc_scr[cur, h] + _dot(values(vt_blk, h), p.astype(BF16))) * jnp.exp2(m_old - m_new)
        return jnp.max(rise)

    def guarded(k_blk, vt_blk, q_mask=None):
        cur = slot_ref[0]
        rise = one_pass(k_blk, vt_blk, cur, q_mask)

        @pl.when(rise <= ONE_PASS_JUMP)
        def _():
            slot_ref[0] = 1 - cur

        @pl.when(jnp.logical_not(rise <= ONE_PASS_JUMP))
        def _():
            two_pass(k_blk, vt_blk, cur, q_mask=q_mask)

    @pl.when(j == 0)
    def _():
        slot_ref[0] = 0
        m_scr[0] = jnp.full(m_scr.shape[1:], NEG_INF, F32)
        acc_scr[0] = jnp.zeros(acc_scr.shape[1:], F32)
        if has_ext:
            two_pass(ke_ref, vte_ref, 0)

    if nk_valid is not None:
        kpos = j * tk + lax.broadcasted_iota(jnp.int32, (tk, 1), 0)
        two_pass(k_ref, vt_ref, slot_ref[0], bias=jnp.where(kpos < nk_valid, 0.0, NEG_INF))
    elif not has_ext:
        two_pass(k_ref, vt_ref, slot_ref[0])
    elif causal:
        @pl.when(j < last_j)
        def _():
            guarded(k_ref, vt_ref)

        def diagonal(n_keys):
            lane = lax.broadcasted_iota(jnp.int32, (1, LANES), 1)
            q_chunk = (i * tq + lax.broadcasted_iota(jnp.int32, (tq, 1), 0)) >> CHUNK_SHIFT
            key_chunk = j * (tk // CHUNK) + lane - MASK_LANE0
            hidden = jnp.logical_and(jnp.logical_and(lane >= MASK_LANE0, lane < MASK_LANE0 + tk // CHUNK),
                                     key_chunk > q_chunk)
            guarded(k_ref.at[0:n_keys, :], vt_ref.at[:, 0:n_keys],
                    q_mask=jnp.where(hidden, -MASK_BIG, 0.0).astype(BF16))

        n_visible = (i + 1) * tq - j * tk
        for n_keys in range(tq, tk + 1, tq):
            pl.when(jnp.logical_and(j == last_j, n_visible == n_keys))(functools.partial(diagonal, n_keys))
    else:
        guarded(k_ref, vt_ref)

    @pl.when(j == last_j)
    def _():
        cur = slot_ref[0]
        for h in range(MLA_HEADS):
            acc = acc_scr[cur, h]
            o_ref[h * MLA_V:(h + 1) * MLA_V, :] = (acc[:MLA_V] / acc[MLA_V:MLA_V + 1]).astype(BF16)


def _attention(q, k, vt, ext, *, tq, tk, causal=False, nk_valid=None):
    b, nq, _ = q.shape
    nk = k.shape[1]
    n_key_tiles = nk // tk
    assert not causal or tk % tq == 0
    pairs = [(i, j) for i in range(nq // tq) for j in range(n_key_tiles) if not causal or j * tk < (i + 1) * tq]
    qi = jnp.asarray([p[0] for p in pairs], jnp.int32)
    kj = jnp.asarray([p[1] for p in pairs], jnp.int32)
    hv = MLA_HEADS * MLA_V
    in_specs = [
        pl.BlockSpec((None, tq, MLA_PAD), lambda bi, s, qi, kj: (bi, qi[s], 0)),
        pl.BlockSpec((None, tk, MLA_PAD), lambda bi, s, qi, kj: (bi, kj[s], 0)),
        pl.BlockSpec((None, VT_TOTAL, tk), lambda bi, s, qi, kj: (bi, 0, kj[s])),
    ]
    args = [q, k, vt]
    if ext is not None:
        ne = ext[0].shape[1]
        shared = ext[0].shape[0] == 1 and b > 1
        emap = (lambda bi, s, qi, kj: (0, 0, 0)) if shared else (lambda bi, s, qi, kj: (bi, 0, 0))
        in_specs += [pl.BlockSpec((None, ne, MLA_PAD), emap), pl.BlockSpec((None, VT_TOTAL, ne), emap)]
        args += list(ext)
    return pl.pallas_call(
        functools.partial(_attn_body, tq=tq, tk=tk, n_key_tiles=n_key_tiles, causal=causal, nk_valid=nk_valid,
                          has_ext=ext is not None),
        grid_spec=pltpu.PrefetchScalarGridSpec(
            num_scalar_prefetch=2,
            grid=(b, len(pairs)),
            in_specs=in_specs,
            out_specs=pl.BlockSpec((None, hv, tq), lambda bi, s, qi, kj: (bi, 0, qi[s])),
            scratch_shapes=[pltpu.VMEM((2, MLA_HEADS, HALO, tq), F32), pltpu.VMEM((2, MLA_HEADS, VT_ROWS, tq), F32),
                            pltpu.SMEM((1,), jnp.int32)]),
        out_shape=jax.ShapeDtypeStruct((b, hv, nq), BF16),
        compiler_params=_params(2),
        name="mla_attention",
    )(qi, kj, *args)


def _sconv_body(x_ref, g_ref, win_ref, cw_ref, wout_ref, prev_ref, o_ref, pout_ref, halo_scr, *, rows):
    t = pl.program_id(1)

    @pl.when(t == 0)
    def _():
        halo_scr[...] = prev_ref[...]

    x = x_ref[...]
    hn = _rms(x, g_ref[...]).astype(BF16)
    bg = _dot(hn, win_ref[:, :D_MODEL])
    u = _dot(hn, win_ref[:, D_MODEL:2 * D_MODEL]) * _dot(hn, win_ref[:, 2 * D_MODEL:])
    halo = halo_scr[...]
    u1 = _shift_rows(u, halo, 1)
    u2 = _shift_rows(u, halo, 2)
    cw = cw_ref[...]
    y = u2 * cw[0:1] + u1 * cw[1:2] + u * cw[2:3]
    last_rows = u[rows - HALO:rows, :]
    halo_scr[...] = last_rows
    o_ref[...] = x + _dot((bg * y).astype(BF16), wout_ref[...])

    @pl.when(t == pl.num_programs(1) - 1)
    def _():
        pout_ref[...] = last_rows


def _sconv(x, prev, w, *, rows):
    b, t_len, _ = x.shape
    g, w_in, conv_w, w_out = w
    shared = prev.shape[0] == 1 and b > 1
    bsel = (lambda bi: 0) if shared else (lambda bi: bi)
    return pl.pallas_call(
        functools.partial(_sconv_body, rows=rows),
        grid=(b, t_len // rows),
        in_specs=[pl.BlockSpec((None, rows, D_MODEL), lambda bi, ti: (bi, ti, 0)), _full_spec(g.shape),
                  _full_spec(w_in.shape), _full_spec(conv_w.shape), _full_spec(w_out.shape),
                  pl.BlockSpec((None, HALO, D_MODEL), lambda bi, ti: (bsel(bi), 0, 0))],
        out_specs=[pl.BlockSpec((None, rows, D_MODEL), lambda bi, ti: (bi, ti, 0)),
                   pl.BlockSpec((None, HALO, D_MODEL), lambda bi, ti: (bi, 0, 0))],
        out_shape=[jax.ShapeDtypeStruct((b, t_len, D_MODEL), F32), jax.ShapeDtypeStruct((b, HALO, D_MODEL), F32)],
        scratch_shapes=[pltpu.VMEM((HALO, D_MODEL), F32)],
        compiler_params=_params(2),
        name="sconv_mixer",
    )(x, g, w_in, conv_w, w_out, prev)


def _pad_lanes(a, before, total):
    return jnp.pad(a, [(0, 0)] * (a.ndim - 1) + [(before, total - before - a.shape[-1])])


def _rope_tables(pos):
    half = MLA_ROPE // 2
    inv = ROPE_THETA ** (-jnp.arange(half, dtype=F32) / half)
    ang = pos.astype(F32)[:, None] * inv
    cos, sin = jnp.cos(ang), jnp.sin(ang)
    n = pos.shape[0]
    tail = LANES - MLA_NOPE - MLA_ROPE
    cos_l = jnp.concatenate([jnp.ones((n, MLA_NOPE), F32), cos, cos, jnp.ones((n, tail), F32)], -1)
    sin_l = jnp.concatenate([jnp.zeros((n, MLA_NOPE), F32), sin, sin, jnp.zeros((n, tail), F32)], -1)
    return cos_l, sin_l


def _head_pair_matrices():
    i = jnp.arange(PAIR)[:, None]
    j = jnp.arange(PAIR)[None, :]
    same = (i // LANES) == (j // LANES)
    li, lj = i % LANES, j % LANES
    half = MLA_ROPE // 2
    rope_end = MLA_NOPE + MLA_ROPE
    in_rope = lambda l: (l >= MLA_NOPE) & (l < rope_end)
    gmean = jnp.where(same & (li < MLA_NOPE) & (lj < MLA_NOPE), 1.0 / MLA_NOPE,
                      jnp.where(same & in_rope(li) & in_rope(lj), 1.0 / MLA_ROPE, 0.0))
    first, second = in_rope(lj) & (lj < MLA_NOPE + half), in_rope(lj) & (lj >= MLA_NOPE + half)
    rot = jnp.where(same & first & (li == lj + half), -1.0, jnp.where(same & second & (li == lj - half), 1.0, 0.0))
    return gmean.astype(BF16), rot.astype(BF16)


def _tail_rows(a, n):
    return jnp.pad(a, ((0, 0), (HALO - n, 0), (0, 0)))


def kernel(x_prompt, x_sample, cache_mla_ckv, cache_mla_krope, state_gdn_S, state_gdn_conv, state_sconv, meta_tokens, ffn1_norm, ffn1_w_gate, ffn1_w_up, ffn1_w_down, ffn2_norm, ffn2_w_gate, ffn2_w_up, ffn2_w_down, mix_norm, ab_w_in, ab_w_out, gdn_conv_w, gdn_A_log, gdn_dt_bias, gdn_o_norm, mla_q_norm, mla_w_uq, mla_kv_norm, mla_w_ukv, mla_qn_norm, mla_qr_norm, mla_kn_norm, mla_kr_norm, sc_w_in, sc_conv_w, sc_w_out):
    bp, tp, _ = x_prompt.shape
    bs, ts, _ = x_sample.shape
    n_frames = bp * tp
    n_sample = bs * ts

    ffn1 = _ffn_weights(ffn1_norm, [_to_bf16(a) for a in (ffn1_w_gate, ffn1_w_up, ffn1_w_down)])
    w_in = _to_bf16(ab_w_in)[0]
    s1 = GDN_QKV + GDN_Z
    c0 = s1 + 2 * GDN_HEADS
    c1 = c0 + MLA_Q_RANK
    c2 = c1 + MLA_KV_RANK
    gates_rope = jnp.concatenate([
        w_in[:, s1:c0], jnp.zeros((D_MODEL, MLA_NOPE - 2 * GDN_HEADS), BF16), w_in[:, c2:],
        jnp.zeros((D_MODEL, LANES - MLA_NOPE - MLA_ROPE), BF16)], -1)
    w_in_packed = jnp.concatenate([w_in[:, :s1], w_in[:, c0:c1], gates_rope, w_in[:, c1:c2]], -1)
    mix_g = mix_norm.reshape(2, 1, D_MODEL)
    gdn_w = (gdn_conv_w[0], _pad_lanes(gdn_A_log[0][None], 0, LANES), _pad_lanes(gdn_dt_bias[0][None], 0, LANES),
             gdn_o_norm[0][None])
    qdim = MLA_NOPE + MLA_ROPE
    w_uq = _pad_lanes(mla_w_uq[0].reshape(MLA_Q_RANK, MLA_HEADS, qdim), 0, LANES).reshape(MLA_Q_RANK, MLA_PAD)
    w_ukv = mla_w_ukv[0].reshape(MLA_KV_RANK, MLA_HEADS, MLA_NOPE + MLA_V)
    w_k = _pad_lanes(w_ukv[..., :MLA_NOPE], 0, LANES).reshape(MLA_KV_RANK, MLA_PAD)
    hv = MLA_HEADS * MLA_V
    w_vt = _pad_lanes(w_ukv[..., MLA_NOPE:], 0, VT_ROWS).reshape(MLA_KV_RANK, VT_TOTAL).T
    v_ones = (jnp.arange(VT_TOTAL) % VT_ROWS == MLA_V).astype(F32)[:, None]
    gq = _pad_lanes(jnp.concatenate([mla_qn_norm[0], mla_qr_norm[0]])[None], 0, LANES)
    gk = _pad_lanes(jnp.concatenate([mla_kn_norm[0], mla_kr_norm[0]])[None], 0, LANES)
    gmean, rot_mat = _head_pair_matrices()
    mla_w = (mla_q_norm[0][None], w_uq.astype(BF16), mla_kv_norm[0][None], w_k.astype(BF16), w_vt.astype(BF16), v_ones,
             gq, gk, gmean, rot_mat)
    w_out = ab_w_out[0].astype(BF16)
    sc_w = (mix_g[1], _to_bf16(sc_w_in)[0], sc_conv_w[0], sc_w_out[0].astype(BF16))

    xf = x_prompt.reshape(n_frames, D_MODEL)
    xs = jnp.concatenate([x_sample.reshape(n_sample, D_MODEL), meta_tokens.astype(F32)], 0)
    cos_f, sin_f = _rope_tables(N_META + jnp.arange(tp))
    pos_small = jnp.concatenate([jnp.tile(PAST_LEN + jnp.arange(ts), bs), jnp.arange(N_META)])
    cos_s, sin_s = _rope_tables(pos_small)

    xf = _ffn(xf, ffn1[0])
    xs = _ffn(xs, ffn1[0])
    qkv_f, z_f, ab_f, q_f, k_f, vt_f, ckvn_f, krn_f, *ffn2_stacks = _inproj(
        xf, mix_g[0], w_in_packed, cos_f, sin_f, mla_w, batch=bp, key_tile=ATTN_TK,
        cast=(ffn2_w_gate, ffn2_w_up, ffn2_w_down))
    ffn2 = _ffn_weights(ffn2_norm, ffn2_stacks)
    qkv_s, z_s, ab_s, q_s, k_s, vt_s, ckvn_s, krn_s = _inproj(xs, mix_g[0], w_in_packed, cos_s, sin_s, mla_w, batch=1)

    pad_meta = lambda a: jnp.pad(a[n_sample:], ((0, CHUNK - N_META), (0, 0)))[None]
    og_m, s_m, conv_m = _gdn(pad_meta(qkv_s), pad_meta(z_s), pad_meta(ab_s),
                             jnp.zeros((1, GDN_HEADS, GDN_DK, GDN_DV), F32), jnp.zeros((1, HALO, GDN_QKV), F32),
                             gdn_w, rows=CHUNK, n_valid=N_META)
    og_f, p_s, p_conv = _gdn(qkv_f.reshape(bp, tp, GDN_QKV), z_f.reshape(bp, tp, GDN_Z), ab_f.reshape(bp, tp, LANES),
                             s_m, conv_m, gdn_w, rows=GDN_ROWS)
    og_s, s_s, s_conv = _gdn(qkv_s[:n_sample].reshape(bs, ts, GDN_QKV), z_s[:n_sample].reshape(bs, ts, GDN_Z),
                             ab_s[:n_sample].reshape(bs, ts, LANES), state_gdn_S[0],
                             _tail_rows(state_gdn_conv[0], state_gdn_conv.shape[2]), gdn_w, rows=ts)

    k_c, vt_c = _kv_up(cache_mla_ckv[0].reshape(bs * PAST_LEN, MLA_KV_RANK),
                       _pad_lanes(cache_mla_krope[0].reshape(bs * PAST_LEN, MLA_ROPE), MLA_NOPE, LANES),
                       mla_w[3], mla_w[4], v_ones, gk, batch=bs)
    k_meta, vt_meta = k_s[n_sample:][None], vt_s[:, :, n_sample:]
    vt_new = vt_s[0, :, :n_sample].reshape(VT_TOTAL, bs, ts).transpose(1, 0, 2)
    omt_f = _attention(q_f.reshape(bp, tp, MLA_PAD), k_f.reshape(bp, tp, MLA_PAD), vt_f, (k_meta, vt_meta),
                       tq=ATTN_TQ, tk=ATTN_TK, causal=True)
    omt_s = _attention(q_s[:n_sample].reshape(bs, ts, MLA_PAD), k_c.reshape(bs, PAST_LEN, MLA_PAD), vt_c,
                       (k_s[:n_sample].reshape(bs, ts, MLA_PAD), vt_new), tq=ts, tk=512)
    omt_m = _attention(jnp.pad(q_s[n_sample:], ((0, LANES - N_META), (0, 0)))[None],
                       jnp.pad(k_s[n_sample:], ((0, CHUNK - N_META), (0, 0)))[None],
                       jnp.pad(vt_meta, ((0, 0), (0, 0), (0, CHUNK - N_META))),
                       None, tq=LANES, tk=CHUNK, nk_valid=N_META)

    og_small = jnp.concatenate([og_s.reshape(n_sample, GDN_Z), og_m[0, :N_META]], 0)
    omt_small = jnp.concatenate([omt_s.transpose(1, 0, 2).reshape(hv, n_sample), omt_m[0, :, :N_META]], -1)[None]
    xf = _ffn(xf, ffn2[0], mix=(og_f.reshape(n_frames, GDN_Z), omt_f, w_out))
    xs = _ffn(xs, ffn2[0], mix=(og_small, omt_small, w_out))

    xf = _ffn(xf, ffn1[1])
    xs = _ffn(xs, ffn1[1])
    n_sc = state_sconv.shape[2]
    xm, sc_m = _sconv(xs[n_sample:][None], jnp.zeros((1, HALO, D_MODEL), F32), sc_w, rows=N_META)
    xf3, p_sc = _sconv(xf.reshape(bp, tp, D_MODEL), sc_m, sc_w, rows=SCONV_ROWS)
    xs3, s_sc = _sconv(xs[:n_sample].reshape(bs, ts, D_MODEL), _tail_rows(state_sconv[0], n_sc), sc_w, rows=ts)
    xf = _ffn(xf3.reshape(n_frames, D_MODEL), ffn2[1])
    xs = _ffn(jnp.concatenate([xs3.reshape(n_sample, D_MODEL), xm[0]], 0), ffn2[1])

    y_prompt = xf.reshape(bp, tp, D_MODEL)
    y_sample = xs[:n_sample].reshape(bs, ts, D_MODEL)
    rope_lo, rope_hi = MLA_NOPE, MLA_NOPE + MLA_ROPE
    meta_ckv = jnp.broadcast_to(ckvn_s[n_sample:][None], (bp, N_META, MLA_KV_RANK))
    meta_kr = jnp.broadcast_to(krn_s[n_sample:, rope_lo:rope_hi][None], (bp, N_META, MLA_ROPE))
    p_ckv = jnp.concatenate([meta_ckv, ckvn_f.reshape(bp, tp, MLA_KV_RANK)], 1)
    p_kr = jnp.concatenate([meta_kr, krn_f[:, rope_lo:rope_hi].reshape(bp, tp, MLA_ROPE)], 1)
    n_gc = state_gdn_conv.shape[2]
    return (y_prompt, y_sample,
            p_ckv[None], p_kr[None], p_s[None], p_conv[:, HALO - n_gc:][None], p_sc[:, HALO - n_sc:][None],
            ckvn_s[:n_sample].reshape(bs, ts, MLA_KV_RANK)[None],
            krn_s[:n_sample, rope_lo:rope_hi].reshape(bs, ts, MLA_ROPE)[None],
            s_s[None], s_conv[:, HALO - n_gc:][None], s_sc[:, HALO - n_sc:][None])
```

```python
import functools

import jax
import jax.numpy as jnp
from jax import lax
from jax.experimental import pallas as pl
from jax.experimental.pallas import tpu as pltpu

F32 = jnp.float32
BF16 = jnp.bfloat16

D_MODEL = 1024
D_FF = 2816
CHUNK = 64
CHUNK_SHIFT = 6
N_META = 16
EPS = 1e-6
GDN_HEADS = 4
GDN_DK = 128
GDN_DV = 128
GDN_QKV = GDN_HEADS * (2 * GDN_DK + GDN_DV)
GDN_Z = GDN_HEADS * GDN_DV
MLA_HEADS = 8
MLA_NOPE = 64
MLA_ROPE = 32
MLA_V = 64
MLA_Q_RANK = 384
MLA_KV_RANK = 256
ROPE_THETA = 10000.0
PAST_LEN = 1024
LANES = 128
HALO = 8
MLA_PAD = MLA_HEADS * LANES
FF_BLOCK = 256
FF_CHUNKS = D_FF // FF_BLOCK
BF16_SUBLANES = 16
VMEM_LIMIT = 60000 * 1024
NEG_INF = float("-inf")
LOG2_E = 1.4426950408889634
MASK_LANE0 = MLA_NOPE + MLA_ROPE
MASK_BIG = 2.0 ** 100
ATTN_TQ = 512
ATTN_TK = 1024
FFN_ROWS = 1024
INPROJ_ROWS = 512
SCONV_ROWS = 1024
GDN_ROWS = 8 * CHUNK
GDN_GROUP = 4
VT_ROWS = MLA_V + BF16_SUBLANES
VT_TOTAL = MLA_HEADS * VT_ROWS
ONE_PASS_JUMP = 64.0

IN_QKV = 0
IN_Z = IN_QKV + GDN_QKV
IN_CQ = IN_Z + GDN_Z
IN_GATES_ROPE = IN_CQ + MLA_Q_RANK
IN_CKV = IN_GATES_ROPE + LANES
IN_TOTAL = IN_CKV + MLA_KV_RANK


def _params(n_axes):
    return pltpu.CompilerParams(dimension_semantics=("arbitrary",) * n_axes, vmem_limit_bytes=VMEM_LIMIT)


def _rms(x, g):
    return x * lax.rsqrt(jnp.mean(x * x, -1, keepdims=True) + EPS) * g


def _sigmoid(x):
    return 1.0 / (1.0 + jnp.exp(-x))


def _silu(x):
    return x * _sigmoid(x)


def _softplus(x):
    return jnp.maximum(x, 0.0) + jnp.log1p(jnp.exp(-jnp.abs(x)))


def _dot(a, b):
    return jnp.dot(a, b, preferred_element_type=F32)


def _dot_nt(a, b):
    return lax.dot_general(a, b, (((1,), (1,)), ((), ())), preferred_element_type=F32)


def _dot_tn(a, b):
    return lax.dot_general(a, b, (((0,), (0,)), ((), ())), preferred_element_type=F32)


def _shift_rows(x, history, k):
    rolled = pltpu.roll(x, k, 0)
    row = lax.broadcasted_iota(jnp.int32, (HALO, 1), 0)
    head = rolled[:HALO]
    for j in range(k):
        head = jnp.where(row == j, history[HALO - k + j:HALO - k + j + 1, :], head)
    return jnp.concatenate([head, rolled[HALO:]], 0)


def _full_spec(shape):
    zeros = (0,) * len(shape)
    return pl.BlockSpec(shape, lambda *_: zeros)


def _row_tile(rows, want):
    return want if rows % want == 0 else rows


def _ffn_body(x_ref, g_ref, wg_ref, wu_ref, wd_ref, *rest, mixed):
    if mixed:
        og_ref, omt_ref, wmix_ref, o_ref, act_ref = rest
        x = x_ref[...] + _dot(og_ref[...], wmix_ref[:GDN_Z, :]) + _dot_tn(omt_ref[...], wmix_ref[GDN_Z:, :])
    else:
        o_ref, act_ref = rest
        x = x_ref[...]
    h = _rms(x, g_ref[...]).astype(BF16)
    for c in range(FF_CHUNKS):
        cols = slice(c * FF_BLOCK, (c + 1) * FF_BLOCK)
        gate = _dot(h, wg_ref[:, cols])
        up = _dot(h, wu_ref[:, cols])
        act_ref[:, cols] = (_silu(gate) * up).astype(BF16)
    o_ref[...] = x + 0.5 * _dot(act_ref[...], wd_ref[...])


def _ffn(x, w, mix=None):
    rows = x.shape[0]
    tm = _row_tile(rows, FFN_ROWS)
    g, wg, wu, wd, layer = w
    resident = lambda a: pl.BlockSpec(a.shape, lambda i: (0,) * a.ndim, pipeline_mode=pl.Buffered(1))
    of_layer = lambda a: pl.BlockSpec((None,) + a.shape[1:], lambda i: (layer, 0, 0), pipeline_mode=pl.Buffered(1))
    in_specs = [pl.BlockSpec((tm, D_MODEL), lambda i: (i, 0)), _full_spec(g.shape), of_layer(wg), of_layer(wu),
                of_layer(wd)]
    args = [x, g, wg, wu, wd]
    if mix is not None:
        og, omt, w_mix = mix
        tiles_per_batch = rows // omt.shape[0] // tm
        in_specs += [pl.BlockSpec((tm, GDN_Z), lambda i: (i, 0)),
                     pl.BlockSpec((None, GDN_Z, tm), lambda i: (i // tiles_per_batch, 0, i % tiles_per_batch)),
                     resident(w_mix)]
        args += [og, omt, w_mix]
    return pl.pallas_call(
        functools.partial(_ffn_body, mixed=mix is not None),
        grid=(rows // tm,),
        in_specs=in_specs,
        out_specs=pl.BlockSpec((tm, D_MODEL), lambda i: (i, 0)),
        out_shape=jax.ShapeDtypeStruct((rows, D_MODEL), F32),
        scratch_shapes=[pltpu.VMEM((tm, D_FF), BF16)],
        compiler_params=_params(1),
        name="half_ffn",
    )(*args)


def _to_bf16_body(x_ref, o_ref):
    o_ref[...] = x_ref[...].astype(BF16)


def _to_bf16(a, layer=None):
    depth, rows, cols = a.shape
    a2 = a.reshape(depth * rows, cols)
    n_rows, first = (depth * rows, 0) if layer is None else (rows, layer * rows)
    tm = _row_tile(rows, FF_BLOCK)
    out = pl.pallas_call(
        _to_bf16_body,
        grid=(n_rows // tm,),
        in_specs=[pl.BlockSpec((tm, cols), lambda i: (first // tm + i, 0))],
        out_specs=pl.BlockSpec((tm, cols), lambda i: (i, 0)),
        out_shape=jax.ShapeDtypeStruct((n_rows, cols), BF16),
        compiler_params=_params(1),
        name="weights_to_bf16",
    )(a2)
    return out.reshape((n_rows // rows, rows, cols))


def _ffn_weights(norm, stacks):
    return [(norm[l].reshape(1, D_MODEL),) + tuple(stacks) + (l,) for l in range(norm.shape[0])]


def _gdn_conv(x, history, cw):
    y = (_shift_rows(x, history, 3) * cw[0:1] + _shift_rows(x, history, 2) * cw[1:2]
         + _shift_rows(x, history, 1) * cw[2:3] + x * cw[3:4])
    return _silu(y)


class _Split:
    def __init__(self, x):
        self.x = x
        self._parts = None
        self._diag = None

    def parts(self):
        if self._parts is None:
            hi = self.x.astype(BF16)
            self._parts = (hi, (self.x - hi.astype(F32)).astype(BF16))
        return self._parts

    def diag_parts(self):
        if self._diag is None:
            first = lax.broadcasted_iota(jnp.int32, self.x.shape, 1) < CHUNK
            zero = jnp.zeros(self.x.shape, BF16)
            self._diag = tuple(jnp.concatenate([jnp.where(first, p, zero), jnp.where(first, zero, p)], 0)
                               for p in self.parts())
        return self._diag


def _split_products(pairs):
    parts = [(a.parts(), b.diag_parts()) for a, b in pairs]
    outs = []
    for (a_hi, a_lo), (b_hi, b_lo) in parts:
        a2 = jnp.concatenate([a_hi, a_lo], 0)
        r = _dot(a2, b_hi) + _dot(a2, b_lo)
        half = a_hi.shape[0]
        outs.append(_Split(r[:half] + r[half:]))
    return outs


def _unit_lower_inverses(ns, eye_f, block_masks):
    k = len(ns)
    nd = [_Split(jnp.where(block_masks[0], n, 0.0)) for n in ns]
    n2 = _split_products([(x, x) for x in nd])
    yield None
    lvl = _split_products([(x, x) for x in n2] +
                          [(_Split(eye_f - a.x), _Split(eye_f + b.x)) for a, b in zip(nd, n2)])
    yield None
    n4, t = lvl[:k], lvl[k:]
    t = _split_products([(a, _Split(eye_f + b.x)) for a, b in zip(t, n4)])
    yield None
    for done, merged in zip(block_masks, block_masks[1:] + [None]):
        keep = jnp.logical_not(done) if merged is None else jnp.logical_and(merged, jnp.logical_not(done))
        off = [_Split(jnp.where(keep, n, 0.0)) for n in ns]
        x = _split_products(list(zip(off, t)))
        yield None
        y = _split_products(list(zip(t, x)))
        yield None
        t = [_Split(a.x - b.x) for a, b in zip(t, y)]
    yield [a.x for a in t]


def _gdn_body(qkv_ref, z_ref, ab_ref, s0_ref, cprev_ref, cw_ref, alog_ref, dtb_ref, on_ref,
              og_ref, sout_ref, cout_ref, s_scr, halo_scr, *, rows, n_valid):
    t = pl.program_id(1)
    C = CHUNK
    row = lax.broadcasted_iota(jnp.int32, (rows, 1), 0)

    @pl.when(t == 0)
    def _():
        s_scr[...] = s0_ref[...]
        halo_scr[...] = cprev_ref[...]

    x = qkv_ref[...]
    y = _gdn_conv(x, halo_scr[...], cw_ref[...])
    last_rows = x[n_valid - HALO:n_valid, :]
    halo_scr[...] = last_rows

    ab = ab_ref[...]
    g_all = -jnp.exp(alog_ref[...]) * _softplus(ab + dtb_ref[...])
    b_all = _sigmoid(ab)
    if n_valid < rows:
        valid = row < n_valid
        y = jnp.where(valid, y, 0.0)
        g_all = jnp.where(valid, g_all, 0.0)
        b_all = jnp.where(valid, b_all, 0.0)

    ii = lax.broadcasted_iota(jnp.int32, (C, LANES), 0)
    pair_lane = lax.broadcasted_iota(jnp.int32, (C, LANES), 1)
    jj = pair_lane & (C - 1)
    first_head = pair_lane < C
    causal = ii >= jj
    strict = ii > jj
    eye_f = (ii == jj).astype(F32)
    block_masks = [(ii >> s) == (jj >> s) for s in (3, 4, 5)]
    on = on_ref[...]
    n_chunks = rows // C
    heads = range(GDN_HEADS)

    ri = lax.broadcasted_iota(jnp.int32, (rows, rows), 0)
    rj = lax.broadcasted_iota(jnp.int32, (rows, rows), 1)
    tril_blk = jnp.logical_and(ri >= rj, (ri >> CHUNK_SHIFT) == (rj >> CHUNK_SHIFT)).astype(BF16)
    g_hi = g_all.astype(BF16)
    g_r1 = g_all - g_hi.astype(F32)
    g_mid = g_r1.astype(BF16)
    g_lo = (g_r1 - g_mid.astype(F32)).astype(BF16)
    gcum = _dot(tril_blk, g_hi) + _dot(tril_blk, g_mid) + _dot(tril_blk, g_lo)
    egcum = jnp.exp(gcum)

    qn, kn, vn = [], [], []
    for h in heads:
        qh = y[:, h * GDN_DK:(h + 1) * GDN_DK]
        kh = y[:, GDN_HEADS * GDN_DK + h * GDN_DK:GDN_HEADS * GDN_DK + (h + 1) * GDN_DK]
        qn.append(qh * lax.rsqrt(jnp.sum(qh * qh, -1, keepdims=True) + EPS) * (GDN_DK ** -0.5))
        kn.append(kh * lax.rsqrt(jnp.sum(kh * kh, -1, keepdims=True) + EPS))
        vn.append(y[:, 2 * GDN_HEADS * GDN_DK + h * GDN_DV:2 * GDN_HEADS * GDN_DK + (h + 1) * GDN_DV])

    head_pairs = [(a, a + 1) for a in range(0, GDN_HEADS, 2)]
    pre = {}

    def chunk_local(chunks):
        chains = [(c, pair) for c in chunks for pair in head_pairs]
        n_mats = []
        for c in chunks:
            sl = slice(c * C, (c + 1) * C)
            gcum_t = gcum[sl].T
            for a, b in head_pairs:
                col = lambda arr, h: arr[sl, h:h + 1]
                gcol = jnp.where(first_head, col(gcum, a), col(gcum, b))
                beta2 = jnp.where(first_head, col(b_all, GDN_HEADS + a), col(b_all, GDN_HEADS + b))
                grow = jnp.concatenate([gcum_t[a:a + 1, :], gcum_t[b:b + 1, :]], -1)
                dec_causal = jnp.exp(jnp.where(causal, gcol - grow, NEG_INF))
                kb = {h: kn[h][sl].astype(BF16) for h in (a, b)}
                kk = jnp.concatenate([_dot_nt(kb[a], kb[a]), _dot_nt(kb[b], kb[b])], -1)
                n_mats.append(beta2 * kk * jnp.where(strict, dec_causal, 0.0))
                qk = jnp.concatenate([_dot_nt(qn[h][sl].astype(BF16), kb[h]) for h in (a, b)], -1)
                qk = (qk * dec_causal).astype(BF16)
                for h, lanes in ((a, slice(0, C)), (b, slice(C, 2 * C))):
                    beta = col(b_all, GDN_HEADS + h)
                    eg = col(egcum, h)
                    g_last = gcum[(c + 1) * C - 1:(c + 1) * C, h:h + 1]
                    qh, kh, vh = qn[h][sl], kn[h][sl], vn[h][sl]
                    pre[c, h] = dict(
                        qk=qk[:, lanes],
                        rhs=jnp.concatenate([(beta * vh).astype(BF16), ((beta * eg) * kh).astype(BF16)], -1),
                        qd=(qh * eg).astype(BF16),
                        kt=(kh * jnp.exp(g_last - col(gcum, h))).astype(BF16),
                        decay=jnp.exp(g_last))
            yield
        inverses = None
        for inverses in _unit_lower_inverses(n_mats, eye_f, block_masks):
            yield
        for (c, (a, b)), tinv in zip(chains, inverses):
            tinv = tinv.astype(BF16)
            for h, lanes in ((a, slice(0, C)), (b, slice(C, 2 * C))):
                uw = _dot(tinv[:, lanes], pre[c, h]["rhs"])
                pre[c, h]["u"] = uw[:, :GDN_DV]
                pre[c, h]["wk"] = uw[:, GDN_DV:].astype(BF16)

    state = [s_scr[h] for h in heads]

    def recurrence(chunks):
        for c in chunks:
            sl = slice(c * C, (c + 1) * C)
            sb = [s.astype(BF16) for s in state]
            w = [(pre[c, h]["u"] - _dot(pre[c, h]["wk"], sb[h])).astype(BF16) for h in heads]
            yield
            o = [_dot(pre[c, h]["qd"], sb[h]) + _dot(pre[c, h]["qk"], w[h]) for h in heads]
            for h in heads:
                state[h] = state[h] * pre[c, h]["decay"] + _dot_tn(pre[c, h]["kt"], w[h])
            yield
            for h in heads:
                gated = _rms(o[h], on) * _silu(z_ref[sl, h * GDN_DV:(h + 1) * GDN_DV])
                og_ref[sl, h * GDN_DV:(h + 1) * GDN_DV] = gated.astype(BF16)
            yield

    group = min(n_chunks, GDN_GROUP)
    pending = iter(())
    for g0 in range(0, n_chunks, group):
        chunks = range(g0, g0 + group)
        for _ in chunk_local(chunks):
            next(pending, None)
        for _ in pending:
            pass
        pending = recurrence(chunks)
    for _ in pending:
        pass
    for h in heads:
        s_scr[h] = state[h]

    @pl.when(t == pl.num_programs(1) - 1)
    def _():
        sout_ref[...] = s_scr[...]
        cout_ref[...] = last_rows


def _gdn(qkv, z, ab, s0, cprev, w, *, rows, n_valid=None):
    b, t_len, _ = qkv.shape
    n_valid = rows if n_valid is None else n_valid
    conv_w, a_log, dt_bias, o_norm = w
    shared = s0.shape[0] == 1 and b > 1
    bsel = (lambda bi: 0) if shared else (lambda bi: bi)
    return pl.pallas_call(
        functools.partial(_gdn_body, rows=rows, n_valid=n_valid),
        grid=(b, t_len // rows),
        in_specs=[
            pl.BlockSpec((None, rows, GDN_QKV), lambda bi, ti: (bi, ti, 0)),
            pl.BlockSpec((None, rows, GDN_Z), lambda bi, ti: (bi, ti, 0)),
            pl.BlockSpec((None, rows, LANES), lambda bi, ti: (bi, ti, 0)),
            pl.BlockSpec((None, GDN_HEADS, GDN_DK, GDN_DV), lambda bi, ti: (bsel(bi), 0, 0, 0)),
            pl.BlockSpec((None, HALO, GDN_QKV), lambda bi, ti: (bsel(bi), 0, 0)),
            _full_spec(conv_w.shape), _full_spec(a_log.shape), _full_spec(dt_bias.shape), _full_spec(o_norm.shape),
        ],
        out_specs=[
            pl.BlockSpec((None, rows, GDN_Z), lambda bi, ti: (bi, ti, 0)),
            pl.BlockSpec((None, GDN_HEADS, GDN_DK, GDN_DV), lambda bi, ti: (bi, 0, 0, 0)),
            pl.BlockSpec((None, HALO, GDN_QKV), lambda bi, ti: (bi, 0, 0)),
        ],
        out_shape=[
            jax.ShapeDtypeStruct((b, t_len, GDN_Z), BF16),
            jax.ShapeDtypeStruct((b, GDN_HEADS, GDN_DK, GDN_DV), F32),
            jax.ShapeDtypeStruct((b, HALO, GDN_QKV), F32),
        ],
        scratch_shapes=[pltpu.VMEM((GDN_HEADS, GDN_DK, GDN_DV), F32), pltpu.VMEM((HALO, GDN_QKV), F32)],
        compiler_params=_params(2),
        name="gdn_mixer",
    )(qkv, z, ab, s0, cprev, conv_w, a_log, dt_bias, o_norm)


def _lane_masks():
    lane = lax.broadcasted_iota(jnp.int32, (1, LANES), 1)
    nope = lane < MLA_NOPE
    rope_a = jnp.logical_and(lane >= MLA_NOPE, lane < MLA_NOPE + MLA_ROPE // 2)
    rope_b = jnp.logical_and(lane >= MLA_NOPE + MLA_ROPE // 2, lane < MLA_NOPE + MLA_ROPE)
    return nope, rope_a, rope_b


def _rope_block(x, cos, sin, rope_a, rope_b):
    half = MLA_ROPE // 2
    rot = jnp.where(rope_a, -pltpu.roll(x, LANES - half, 1), jnp.where(rope_b, pltpu.roll(x, half, 1), 0.0))
    return x * cos + rot * sin


PAIR = 2 * LANES


def _twice(a):
    return jnp.concatenate([a, a], -1)


def _k_heads(k, shared, gk, k_out):
    for h in range(MLA_HEADS):
        x = k[:, h * LANES:(h + 1) * LANES]
        inv = lax.rsqrt(jnp.sum(x * x, -1, keepdims=True) / MLA_NOPE + EPS)
        k_out[:, h * LANES:(h + 1) * LANES] = (x * inv * gk + shared).astype(BF16)


def _inproj_body(x_ref, g_ref, w_ref, cos_ref, sin_ref, qn_ref, wuq_ref, kvn_ref, wk_ref, wvt_ref, vones_ref,
                 gq_ref, gk_ref, gmean_ref, rot_ref, *rest, tiles_per_batch, key_tile_chunks, n_cast):
    cast_in, rest = rest[:n_cast], rest[n_cast:]
    qkv_ref, z_ref, ab_ref, q_out, k_out, vt_out, ckv_out, kr_out = rest[:8]
    for src, dst in zip(cast_in, rest[8:]):
        dst[...] = src[...].astype(BF16)
    hn = _rms(x_ref[...], g_ref[...]).astype(BF16)
    qkv_ref[...] = _dot(hn, w_ref[:, IN_QKV:IN_Z])
    z_ref[...] = _dot(hn, w_ref[:, IN_Z:IN_CQ])
    cq_gates_rope = _dot(hn, w_ref[:, IN_CQ:IN_CKV])
    cq = cq_gates_rope[:, :MLA_Q_RANK]
    gates_rope = cq_gates_rope[:, MLA_Q_RANK:]
    ab_ref[...] = gates_rope
    ckv = _dot(hn, w_ref[:, IN_CKV:IN_TOTAL])

    nope, rope_a, rope_b = _lane_masks()
    krp = jnp.where(jnp.logical_or(rope_a, rope_b), gates_rope, 0.0)
    cos = cos_ref[...]
    sin = sin_ref[...]
    gk = gk_ref[...]
    gmean = gmean_ref[...]
    rot_mat = rot_ref[...]
    gq2 = _twice(gq_ref[...] * ((MLA_NOPE + MLA_ROPE) ** -0.5 * LOG2_E))
    cos2, sin2 = _twice(cos), _twice(sin)

    cqn = _rms(cq, qn_ref[...]).astype(BF16)
    q = _dot(cqn, wuq_ref[...])
    for p in range(MLA_HEADS // 2):
        x = q[:, p * PAIR:(p + 1) * PAIR]
        ms = _dot((x * x).astype(BF16), gmean)
        x = x * lax.rsqrt(ms + EPS) * gq2
        q_out[:, p * PAIR:(p + 1) * PAIR] = (x * cos2 + _dot(x.astype(BF16), rot_mat) * sin2).astype(BF16)

    ckvn = _rms(ckv, kvn_ref[...])
    ckv_out[...] = ckvn
    cb = ckvn.astype(BF16)
    krn = krp * lax.rsqrt(jnp.sum(krp * krp, -1, keepdims=True) / MLA_ROPE + EPS) * jnp.where(nope, 0.0, gk)
    kr_new = _rope_block(krn, cos, sin, rope_a, rope_b)
    kr_out[...] = kr_new
    vt_out[...] = (_dot_nt(wvt_ref[...], cb) + vones_ref[...]).astype(BF16)
    shared_lanes = kr_new
    if key_tile_chunks is not None:
        tm = kr_new.shape[0]
        pos = (pl.program_id(0) % tiles_per_batch) * tm + lax.broadcasted_iota(jnp.int32, (tm, 1), 0)
        lane = lax.broadcasted_iota(jnp.int32, (1, LANES), 1)
        chunk = (pos >> CHUNK_SHIFT) & (key_tile_chunks - 1)
        shared_lanes = kr_new + (lane - MASK_LANE0 == chunk).astype(F32)
    _k_heads(_dot(cb, wk_ref[...]), shared_lanes, jnp.where(nope, gk, 0.0), k_out)


def _vt_spec(tm, tiles_per_batch):
    return pl.BlockSpec((None, VT_TOTAL, tm), lambda i: (i // tiles_per_batch, 0, i % tiles_per_batch))


def _inproj(x, g, w_in, cos, sin, w, *, batch, key_tile=None, cast=()):
    rows = x.shape[0]
    tm = _row_tile(rows, INPROJ_ROWS)
    steps = rows // tm
    tiles_per_batch = rows // batch // tm
    key_tile_chunks = None if key_tile is None else key_tile // CHUNK
    assert key_tile_chunks is None or (key_tile_chunks & (key_tile_chunks - 1) == 0
                                       and key_tile_chunks <= LANES - MASK_LANE0)
    q_norm, w_uq, kv_norm, w_k, w_vt, v_ones, gq, gk, gmean, rot_mat = w
    row_spec = lambda n: pl.BlockSpec((tm, n), lambda i: (i, 0))
    tab_spec = pl.BlockSpec((tm, LANES), lambda i: (i % tiles_per_batch, 0))
    consts = [g, w_in, q_norm, w_uq, kv_norm, w_k, w_vt, v_ones, gq, gk, gmean, rot_mat]
    cast2d = [a.reshape(-1, a.shape[-1]) for a, _ in cast]
    cast_in_specs, cast_out_specs, cast_shapes = [], [], []
    for a, layer in cast:
        depth, r, c = a.shape
        n_rows, first = (depth * r, 0) if layer is None else (r, layer * r)
        share = next(s for s in (1, 2, 4, 8) if (n_rows * s) % (steps * BF16_SUBLANES) == 0)
        slab = n_rows * share // steps
        assert slab % BF16_SUBLANES == 0 and slab * steps == n_rows * share and first % slab == 0
        cast_in_specs.append(pl.BlockSpec((slab, c), lambda i, share=share, off=first // slab: (off + i // share, 0)))
        cast_out_specs.append(pl.BlockSpec((slab, c), lambda i, share=share: (i // share, 0)))
        cast_shapes.append((n_rows // r, r, c))
    outs = pl.pallas_call(
        functools.partial(_inproj_body, tiles_per_batch=tiles_per_batch, key_tile_chunks=key_tile_chunks,
                          n_cast=len(cast)),
        grid=(steps,),
        in_specs=[row_spec(D_MODEL), _full_spec(g.shape), _full_spec(w_in.shape), tab_spec, tab_spec]
        + [_full_spec(a.shape) for a in consts[2:]] + cast_in_specs,
        out_specs=[row_spec(GDN_QKV), row_spec(GDN_Z), row_spec(LANES), row_spec(MLA_PAD), row_spec(MLA_PAD),
                   _vt_spec(tm, tiles_per_batch), row_spec(MLA_KV_RANK), row_spec(LANES)] + cast_out_specs,
        out_shape=[jax.ShapeDtypeStruct((rows, GDN_QKV), F32), jax.ShapeDtypeStruct((rows, GDN_Z), F32),
                   jax.ShapeDtypeStruct((rows, LANES), F32),
                   jax.ShapeDtypeStruct((rows, MLA_PAD), BF16), jax.ShapeDtypeStruct((rows, MLA_PAD), BF16),
                   jax.ShapeDtypeStruct((batch, VT_TOTAL, rows // batch), BF16),
                   jax.ShapeDtypeStruct((rows, MLA_KV_RANK), F32), jax.ShapeDtypeStruct((rows, LANES), F32)]
        + [jax.ShapeDtypeStruct((d * r, c), BF16) for d, r, c in cast_shapes],
        compiler_params=_params(1),
        name="mixer_inproj",
    )(x, g, w_in, cos, sin, *consts[2:], *cast2d)
    outs = list(outs)
    return outs[:8] + [o.reshape(shape) for o, shape in zip(outs[8:], cast_shapes)]


def _kv_up_body(ckv_ref, kr_ref, wk_ref, wvt_ref, vones_ref, gk_ref, k_out, vt_out):
    nope, _, _ = _lane_masks()
    cb = ckv_ref[...].astype(BF16)
    vt_out[...] = (_dot_nt(wvt_ref[...], cb) + vones_ref[...]).astype(BF16)
    _k_heads(_dot(cb, wk_ref[...]), kr_ref[...], jnp.where(nope, gk_ref[...], 0.0), k_out)


def _kv_up(ckv, krp, w_k, w_vt, v_ones, gk, *, batch):
    rows = ckv.shape[0]
    tm = _row_tile(rows, 512)
    tiles_per_batch = rows // batch // tm
    row_spec = lambda n: pl.BlockSpec((tm, n), lambda i: (i, 0))
    return pl.pallas_call(
        _kv_up_body,
        grid=(rows // tm,),
        in_specs=[row_spec(MLA_KV_RANK), row_spec(LANES), _full_spec(w_k.shape), _full_spec(w_vt.shape),
                  _full_spec(v_ones.shape), _full_spec(gk.shape)],
        out_specs=[row_spec(MLA_PAD), _vt_spec(tm, tiles_per_batch)],
        out_shape=[jax.ShapeDtypeStruct((rows, MLA_PAD), BF16),
                   jax.ShapeDtypeStruct((batch, VT_TOTAL, rows // batch), BF16)],
        compiler_params=_params(1),
        name="mla_kv_up",
    )(ckv, krp, w_k, w_vt, v_ones, gk)


def _attn_body(qi_ref, kj_ref, *refs, tq, tk, n_key_tiles, causal, nk_valid, has_ext):
    if has_ext:
        q_ref, k_ref, vt_ref, ke_ref, vte_ref, o_ref, m_scr, acc_scr, slot_ref = refs
    else:
        q_ref, k_ref, vt_ref, o_ref, m_scr, acc_scr, slot_ref = refs
    step = pl.program_id(1)
    i = qi_ref[step]
    j = kj_ref[step]
    last_j = ((i + 1) * tq + tk - 1) // tk - 1 if causal else n_key_tiles - 1
    head_lanes = [slice(h * LANES, (h + 1) * LANES) for h in range(MLA_HEADS)]

    def head_scores(k_blk, q_mask):
        if q_mask is None:
            q_heads = [q_ref[:, hs] for hs in head_lanes]
        else:
            q_heads = [q_ref[:, hs] + q_mask for hs in head_lanes]
        return [_dot_nt(k_blk[:, hs], qh) for hs, qh in zip(head_lanes, q_heads)]

    def values(vt_blk, h):
        return vt_blk[h * VT_ROWS:(h + 1) * VT_ROWS, :]

    def two_pass(k_blk, vt_blk, cur, bias=None, q_mask=None):
        scores = head_scores(k_blk, q_mask)
        for h in range(MLA_HEADS):
            s = scores[h] if bias is None else scores[h] + bias
            m_old = m_scr[cur, h][0:1]
            m_new = jnp.maximum(m_old, jnp.max(s, 0, keepdims=True))
            alpha = jnp.exp2(m_old - m_new)
            p = jnp.exp2(s - m_new)
            m_scr[cur, h] = jnp.broadcast_to(m_new, (HALO, tq))
            acc_scr[cur, h] = acc_scr[cur, h] * alpha + _dot(values(vt_blk, h), p.astype(BF16))

    def one_pass(k_blk, vt_blk, cur, q_mask=None):
        scores = head_scores(k_blk, q_mask)
        rise = None
        for h in range(MLA_HEADS):
            s = scores[h]
            m_old = m_scr[cur, h][0:1]
            p = jnp.exp2(s - m_old)
            m_tile = jnp.max(s, 0, keepdims=True)
            m_new = jnp.maximum(m_old, m_tile)
            rise = m_tile - m_old if rise is None else jnp.maximum(rise, m_tile - m_old)
            m_scr[1 - cur, h] = jnp.broadcast_to(m_new, (HALO, tq))
            acc_scr[1 - cur, h] = (acc_scr[cur, h] + _dot(values(vt_blk, h), p.astype(BF16))) * jnp.exp2(m_old - m_new)
        return jnp.max(rise)

    def guarded(k_blk, vt_blk, q_mask=None):
        cur = slot_ref[0]
        rise = one_pass(k_blk, vt_blk, cur, q_mask)

        @pl.when(rise <= ONE_PASS_JUMP)
        def _():
            slot_ref[0] = 1 - cur

        @pl.when(jnp.logical_not(rise <= ONE_PASS_JUMP))
        def _():
            two_pass(k_blk, vt_blk, cur, q_mask=q_mask)

    @pl.when(j == 0)
    def _():
        slot_ref[0] = 0
        m_scr[0] = jnp.full(m_scr.shape[1:], NEG_INF, F32)
        acc_scr[0] = jnp.zeros(acc_scr.shape[1:], F32)
        if has_ext:
            two_pass(ke_ref, vte_ref, 0)

    if nk_valid is not None:
        kpos = j * tk + lax.broadcasted_iota(jnp.int32, (tk, 1), 0)
        two_pass(k_ref, vt_ref, slot_ref[0], bias=jnp.where(kpos < nk_valid, 0.0, NEG_INF))
    elif not has_ext:
        two_pass(k_ref, vt_ref, slot_ref[0])
    elif causal:
        @pl.when(j < last_j)
        def _():
            guarded(k_ref, vt_ref)

        def diagonal(n_keys):
            lane = lax.broadcasted_iota(jnp.int32, (1, LANES), 1)
            q_chunk = (i * tq + lax.broadcasted_iota(jnp.int32, (tq, 1), 0)) >> CHUNK_SHIFT
            key_chunk = j * (tk // CHUNK) + lane - MASK_LANE0
            hidden = jnp.logical_and(jnp.logical_and(lane >= MASK_LANE0, lane < MASK_LANE0 + tk // CHUNK),
                                     key_chunk > q_chunk)
            guarded(k_ref.at[0:n_keys, :], vt_ref.at[:, 0:n_keys],
                    q_mask=jnp.where(hidden, -MASK_BIG, 0.0).astype(BF16))

        n_visible = (i + 1) * tq - j * tk
        for n_keys in range(tq, tk + 1, tq):
            pl.when(jnp.logical_and(j == last_j, n_visible == n_keys))(functools.partial(diagonal, n_keys))
    else:
        guarded(k_ref, vt_ref)

    @pl.when(j == last_j)
    def _():
        cur = slot_ref[0]
        for h in range(MLA_HEADS):
            acc = acc_scr[cur, h]
            o_ref[h * MLA_V:(h + 1) * MLA_V, :] = (acc[:MLA_V] / acc[MLA_V:MLA_V + 1]).astype(BF16)


def _attention(q, k, vt, ext, *, tq, tk, causal=False, nk_valid=None):
    b, nq, _ = q.shape
    nk = k.shape[1]
    n_key_tiles = nk // tk
    assert not causal or tk % tq == 0
    pairs = [(i, j) for i in range(nq // tq) for j in range(n_key_tiles) if not causal or j * tk < (i + 1) * tq]
    qi = jnp.asarray([p[0] for p in pairs], jnp.int32)
    kj = jnp.asarray([p[1] for p in pairs], jnp.int32)
    hv = MLA_HEADS * MLA_V
    in_specs = [
        pl.BlockSpec((None, tq, MLA_PAD), lambda bi, s, qi, kj: (bi, qi[s], 0)),
        pl.BlockSpec((None, tk, MLA_PAD), lambda bi, s, qi, kj: (bi, kj[s], 0)),
        pl.BlockSpec((None, VT_TOTAL, tk), lambda bi, s, qi, kj: (bi, 0, kj[s])),
    ]
    args = [q, k, vt]
    if ext is not None:
        ne = ext[0].shape[1]
        shared = ext[0].shape[0] == 1 and b > 1
        emap = (lambda bi, s, qi, kj: (0, 0, 0)) if shared else (lambda bi, s, qi, kj: (bi, 0, 0))
        in_specs += [pl.BlockSpec((None, ne, MLA_PAD), emap), pl.BlockSpec((None, VT_TOTAL, ne), emap)]
        args += list(ext)
    return pl.pallas_call(
        functools.partial(_attn_body, tq=tq, tk=tk, n_key_tiles=n_key_tiles, causal=causal, nk_valid=nk_valid,
                          has_ext=ext is not None),
        grid_spec=pltpu.PrefetchScalarGridSpec(
            num_scalar_prefetch=2,
            grid=(b, len(pairs)),
            in_specs=in_specs,
            out_specs=pl.BlockSpec((None, hv, tq), lambda bi, s, qi, kj: (bi, 0, qi[s])),
            scratch_shapes=[pltpu.VMEM((2, MLA_HEADS, HALO, tq), F32), pltpu.VMEM((2, MLA_HEADS, VT_ROWS, tq), F32),
                            pltpu.SMEM((1,), jnp.int32)]),
        out_shape=jax.ShapeDtypeStruct((b, hv, nq), BF16),
        compiler_params=_params(2),
        name="mla_attention",
    )(qi, kj, *args)


def _sconv_body(x_ref, g_ref, win_ref, cw_ref, wout_ref, prev_ref, o_ref, pout_ref, halo_scr, *, rows):
    t = pl.program_id(1)

    @pl.when(t == 0)
    def _():
        halo_scr[...] = prev_ref[...]

    x = x_ref[...]
    hn = _rms(x, g_ref[...]).astype(BF16)
    bg = _dot(hn, win_ref[:, :D_MODEL])
    u = _dot(hn, win_ref[:, D_MODEL:2 * D_MODEL]) * _dot(hn, win_ref[:, 2 * D_MODEL:])
    halo = halo_scr[...]
    u1 = _shift_rows(u, halo, 1)
    u2 = _shift_rows(u, halo, 2)
    cw = cw_ref[...]
    y = u2 * cw[0:1] + u1 * cw[1:2] + u * cw[2:3]
    last_rows = u[rows - HALO:rows, :]
    halo_scr[...] = last_rows
    o_ref[...] = x + _dot((bg * y).astype(BF16), wout_ref[...])

    @pl.when(t == pl.num_programs(1) - 1)
    def _():
        pout_ref[...] = last_rows


def _sconv(x, prev, w, *, rows):
    b, t_len, _ = x.shape
    g, w_in, conv_w, w_out = w
    shared = prev.shape[0] == 1 and b > 1
    bsel = (lambda bi: 0) if shared else (lambda bi: bi)
    return pl.pallas_call(
        functools.partial(_sconv_body, rows=rows),
        grid=(b, t_len // rows),
        in_specs=[pl.BlockSpec((None, rows, D_MODEL), lambda bi, ti: (bi, ti, 0)), _full_spec(g.shape),
                  _full_spec(w_in.shape), _full_spec(conv_w.shape), _full_spec(w_out.shape),
                  pl.BlockSpec((None, HALO, D_MODEL), lambda bi, ti: (bsel(bi), 0, 0))],
        out_specs=[pl.BlockSpec((None, rows, D_MODEL), lambda bi, ti: (bi, ti, 0)),
                   pl.BlockSpec((None, HALO, D_MODEL), lambda bi, ti: (bi, 0, 0))],
        out_shape=[jax.ShapeDtypeStruct((b, t_len, D_MODEL), F32), jax.ShapeDtypeStruct((b, HALO, D_MODEL), F32)],
        scratch_shapes=[pltpu.VMEM((HALO, D_MODEL), F32)],
        compiler_params=_params(2),
        name="sconv_mixer",
    )(x, g, w_in, conv_w, w_out, prev)


def _pad_lanes(a, before, total):
    return jnp.pad(a, [(0, 0)] * (a.ndim - 1) + [(before, total - before - a.shape[-1])])


def _rope_tables(pos):
    half = MLA_ROPE // 2
    inv = ROPE_THETA ** (-jnp.arange(half, dtype=F32) / half)
    ang = pos.astype(F32)[:, None] * inv
    cos, sin = jnp.cos(ang), jnp.sin(ang)
    n = pos.shape[0]
    tail = LANES - MLA_NOPE - MLA_ROPE
    cos_l = jnp.concatenate([jnp.ones((n, MLA_NOPE), F32), cos, cos, jnp.ones((n, tail), F32)], -1)
    sin_l = jnp.concatenate([jnp.zeros((n, MLA_NOPE), F32), sin, sin, jnp.zeros((n, tail), F32)], -1)
    return cos_l, sin_l


def _head_pair_matrices():
    i = jnp.arange(PAIR)[:, None]
    j = jnp.arange(PAIR)[None, :]
    same = (i // LANES) == (j // LANES)
    li, lj = i % LANES, j % LANES
    half = MLA_ROPE // 2
    rope_end = MLA_NOPE + MLA_ROPE
    in_rope = lambda l: (l >= MLA_NOPE) & (l < rope_end)
    gmean = jnp.where(same & (li < MLA_NOPE) & (lj < MLA_NOPE), 1.0 / MLA_NOPE,
                      jnp.where(same & in_rope(li) & in_rope(lj), 1.0 / MLA_ROPE, 0.0))
    first, second = in_rope(lj) & (lj < MLA_NOPE + half), in_rope(lj) & (lj >= MLA_NOPE + half)
    rot = jnp.where(same & first & (li == lj + half), -1.0, jnp.where(same & second & (li == lj - half), 1.0, 0.0))
    return gmean.astype(BF16), rot.astype(BF16)


def _tail_rows(a, n):
    return jnp.pad(a, ((0, 0), (HALO - n, 0), (0, 0)))


def kernel(x_prompt, x_sample, cache_mla_ckv, cache_mla_krope, state_gdn_S, state_gdn_conv, state_sconv, meta_tokens, ffn1_norm, ffn1_w_gate, ffn1_w_up, ffn1_w_down, ffn2_norm, ffn2_w_gate, ffn2_w_up, ffn2_w_down, mix_norm, ab_w_in, ab_w_out, gdn_conv_w, gdn_A_log, gdn_dt_bias, gdn_o_norm, mla_q_norm, mla_w_uq, mla_kv_norm, mla_w_ukv, mla_qn_norm, mla_qr_norm, mla_kn_norm, mla_kr_norm, sc_w_in, sc_conv_w, sc_w_out):
    bp, tp, _ = x_prompt.shape
    bs, ts, _ = x_sample.shape
    n_frames = bp * tp
    n_sample = bs * ts

    ffn1_l0 = _ffn_weights(ffn1_norm[0:1], [_to_bf16(a, layer=0) for a in (ffn1_w_gate, ffn1_w_up, ffn1_w_down)])[0]
    w_in = _to_bf16(ab_w_in)[0]
    s1 = GDN_QKV + GDN_Z
    c0 = s1 + 2 * GDN_HEADS
    c1 = c0 + MLA_Q_RANK
    c2 = c1 + MLA_KV_RANK
    gates_rope = jnp.concatenate([
        w_in[:, s1:c0], jnp.zeros((D_MODEL, MLA_NOPE - 2 * GDN_HEADS), BF16), w_in[:, c2:],
        jnp.zeros((D_MODEL, LANES - MLA_NOPE - MLA_ROPE), BF16)], -1)
    w_in_packed = jnp.concatenate([w_in[:, :s1], w_in[:, c0:c1], gates_rope, w_in[:, c1:c2]], -1)
    mix_g = mix_norm.reshape(2, 1, D_MODEL)
    gdn_w = (gdn_conv_w[0], _pad_lanes(gdn_A_log[0][None], 0, LANES), _pad_lanes(gdn_dt_bias[0][None], 0, LANES),
             gdn_o_norm[0][None])
    qdim = MLA_NOPE + MLA_ROPE
    w_uq = _pad_lanes(mla_w_uq[0].reshape(MLA_Q_RANK, MLA_HEADS, qdim), 0, LANES).reshape(MLA_Q_RANK, MLA_PAD)
    w_ukv = mla_w_ukv[0].reshape(MLA_KV_RANK, MLA_HEADS, MLA_NOPE + MLA_V)
    w_k = _pad_lanes(w_ukv[..., :MLA_NOPE], 0, LANES).reshape(MLA_KV_RANK, MLA_PAD)
    hv = MLA_HEADS * MLA_V
    w_vt = _pad_lanes(w_ukv[..., MLA_NOPE:], 0, VT_ROWS).reshape(MLA_KV_RANK, VT_TOTAL).T
    v_ones = (jnp.arange(VT_TOTAL) % VT_ROWS == MLA_V).astype(F32)[:, None]
    gq = _pad_lanes(jnp.concatenate([mla_qn_norm[0], mla_qr_norm[0]])[None], 0, LANES)
    gk = _pad_lanes(jnp.concatenate([mla_kn_norm[0], mla_kr_norm[0]])[None], 0, LANES)
    gmean, rot_mat = _head_pair_matrices()
    mla_w = (mla_q_norm[0][None], w_uq.astype(BF16), mla_kv_norm[0][None], w_k.astype(BF16), w_vt.astype(BF16), v_ones,
             gq, gk, gmean, rot_mat)
    w_out = ab_w_out[0].astype(BF16)

    xf = x_prompt.reshape(n_frames, D_MODEL)
    xs = jnp.concatenate([x_sample.reshape(n_sample, D_MODEL), meta_tokens.astype(F32)], 0)
    cos_f, sin_f = _rope_tables(N_META + jnp.arange(tp))
    pos_small = jnp.concatenate([jnp.tile(PAST_LEN + jnp.arange(ts), bs), jnp.arange(N_META)])
    cos_s, sin_s = _rope_tables(pos_small)

    xf = _ffn(xf, ffn1_l0)
    xs = _ffn(xs, ffn1_l0)
    later = [(a, None) for a in (ffn2_w_gate, ffn2_w_up, ffn2_w_down)]
    later += [(a, 1) for a in (ffn1_w_gate, ffn1_w_up, ffn1_w_down)] + [(sc_w_in, None)]
    qkv_f, z_f, ab_f, q_f, k_f, vt_f, ckvn_f, krn_f, *later_bf16 = _inproj(
        xf, mix_g[0], w_in_packed, cos_f, sin_f, mla_w, batch=bp, key_tile=ATTN_TK, cast=later)
    ffn2 = _ffn_weights(ffn2_norm, later_bf16[0:3])
    ffn1_l1 = _ffn_weights(ffn1_norm[1:2], later_bf16[3:6])[0]
    sc_w = (mix_g[1], later_bf16[6][0], sc_conv_w[0], sc_w_out[0].astype(BF16))
    qkv_s, z_s, ab_s, q_s, k_s, vt_s, ckvn_s, krn_s = _inproj(xs, mix_g[0], w_in_packed, cos_s, sin_s, mla_w, batch=1)

    pad_meta = lambda a: jnp.pad(a[n_sample:], ((0, CHUNK - N_META), (0, 0)))[None]
    og_m, s_m, conv_m = _gdn(pad_meta(qkv_s), pad_meta(z_s), pad_meta(ab_s),
                             jnp.zeros((1, GDN_HEADS, GDN_DK, GDN_DV), F32), jnp.zeros((1, HALO, GDN_QKV), F32),
                             gdn_w, rows=CHUNK, n_valid=N_META)
    og_f, p_s, p_conv = _gdn(qkv_f.reshape(bp, tp, GDN_QKV), z_f.reshape(bp, tp, GDN_Z), ab_f.reshape(bp, tp, LANES),
                             s_m, conv_m, gdn_w, rows=GDN_ROWS)
    og_s, s_s, s_conv = _gdn(qkv_s[:n_sample].reshape(bs, ts, GDN_QKV), z_s[:n_sample].reshape(bs, ts, GDN_Z),
                             ab_s[:n_sample].reshape(bs, ts, LANES), state_gdn_S[0],
                             _tail_rows(state_gdn_conv[0], state_gdn_conv.shape[2]), gdn_w, rows=ts)

    k_c, vt_c = _kv_up(cache_mla_ckv[0].reshape(bs * PAST_LEN, MLA_KV_RANK),
                       _pad_lanes(cache_mla_krope[0].reshape(bs * PAST_LEN, MLA_ROPE), MLA_NOPE, LANES),
                       mla_w[3], mla_w[4], v_ones, gk, batch=bs)
    k_meta, vt_meta = k_s[n_sample:][None], vt_s[:, :, n_sample:]
    vt_new = vt_s[0, :, :n_sample].reshape(VT_TOTAL, bs, ts).transpose(1, 0, 2)
    omt_f = _attention(q_f.reshape(bp, tp, MLA_PAD), k_f.reshape(bp, tp, MLA_PAD), vt_f, (k_meta, vt_meta),
                       tq=ATTN_TQ, tk=ATTN_TK, causal=True)
    omt_s = _attention(q_s[:n_sample].reshape(bs, ts, MLA_PAD), k_c.reshape(bs, PAST_LEN, MLA_PAD), vt_c,
                       (k_s[:n_sample].reshape(bs, ts, MLA_PAD), vt_new), tq=ts, tk=512)
    omt_m = _attention(jnp.pad(q_s[n_sample:], ((0, LANES - N_META), (0, 0)))[None],
                       jnp.pad(k_s[n_sample:], ((0, CHUNK - N_META), (0, 0)))[None],
                       jnp.pad(vt_meta, ((0, 0), (0, 0), (0, CHUNK - N_META))),
                       None, tq=LANES, tk=CHUNK, nk_valid=N_META)

    og_small = jnp.concatenate([og_s.reshape(n_sample, GDN_Z), og_m[0, :N_META]], 0)
    omt_small = jnp.concatenate([omt_s.transpose(1, 0, 2).reshape(hv, n_sample), omt_m[0, :, :N_META]], -1)[None]
    xf = _ffn(xf, ffn2[0], mix=(og_f.reshape(n_frames, GDN_Z), omt_f, w_out))
    xs = _ffn(xs, ffn2[0], mix=(og_small, omt_small, w_out))

    xf = _ffn(xf, ffn1_l1)
    xs = _ffn(xs, ffn1_l1)
    n_sc = state_sconv.shape[2]
    xm, sc_m = _sconv(xs[n_sample:][None], jnp.zeros((1, HALO, D_MODEL), F32), sc_w, rows=N_META)
    xf3, p_sc = _sconv(xf.reshape(bp, tp, D_MODEL), sc_m, sc_w, rows=SCONV_ROWS)
    xs3, s_sc = _sconv(xs[:n_sample].reshape(bs, ts, D_MODEL), _tail_rows(state_sconv[0], n_sc), sc_w, rows=ts)
    xf = _ffn(xf3.reshape(n_frames, D_MODEL), ffn2[1])
    xs = _ffn(jnp.concatenate([xs3.reshape(n_sample, D_MODEL), xm[0]], 0), ffn2[1])

    y_prompt = xf.reshape(bp, tp, D_MODEL)
    y_sample = xs[:n_sample].reshape(bs, ts, D_MODEL)
    rope_lo, rope_hi = MLA_NOPE, MLA_NOPE + MLA_ROPE
    meta_ckv = jnp.broadcast_to(ckvn_s[n_sample:][None], (bp, N_META, MLA_KV_RANK))
    meta_kr = jnp.broadcast_to(krn_s[n_sample:, rope_lo:rope_hi][None], (bp, N_META, MLA_ROPE))
    p_ckv = jnp.concatenate([meta_ckv, ckvn_f.reshape(bp, tp, MLA_KV_RANK)], 1)
    p_kr = jnp.concatenate([meta_kr, krn_f[:, rope_lo:rope_hi].reshape(bp, tp, MLA_ROPE)], 1)
    n_gc = state_gdn_conv.shape[2]
    return (y_prompt, y_sample,
            p_ckv[None], p_kr[None], p_s[None], p_conv[:, HALO - n_gc:][None], p_sc[:, HALO - n_sc:][None],
            ckvn_s[:n_sample].reshape(bs, ts, MLA_KV_RANK)[None],
            krn_s[:n_sample, rope_lo:rope_hi].reshape(bs, ts, MLA_ROPE)[None],
            s_s[None], s_conv[:, HALO - n_gc:][None], s_sc[:, HALO - n_sc:][None])
```

```python
import functools

import jax
import jax.numpy as jnp
from jax import lax
from jax.experimental import pallas as pl
from jax.experimental.pallas import tpu as pltpu

F32 = jnp.float32
BF16 = jnp.bfloat16

D_MODEL = 1024
D_FF = 2816
CHUNK = 64
CHUNK_SHIFT = 6
N_META = 16
EPS = 1e-6
GDN_HEADS = 4
GDN_DK = 128
GDN_DV = 128
GDN_QKV = GDN_HEADS * (2 * GDN_DK + GDN_DV)
GDN_Z = GDN_HEADS * GDN_DV
MLA_HEADS = 8
MLA_NOPE = 64
MLA_ROPE = 32
MLA_V = 64
MLA_Q_RANK = 384
MLA_KV_RANK = 256
ROPE_THETA = 10000.0
PAST_LEN = 1024
LANES = 128
HALO = 8
MLA_PAD = MLA_HEADS * LANES
FF_BLOCK = 256
FF_CHUNKS = D_FF // FF_BLOCK
BF16_SUBLANES = 16
VMEM_LIMIT = 60000 * 1024
NEG_INF = float("-inf")
LOG2_E = 1.4426950408889634
MASK_LANE0 = MLA_NOPE + MLA_ROPE
MASK_BIG = 2.0 ** 100
ATTN_TQ = 512
ATTN_TK = 1024
FFN_ROWS = 1024
INPROJ_ROWS = 512
SCONV_ROWS = 1024
GDN_ROWS = 8 * CHUNK
GDN_GROUP = 4
VT_ROWS = MLA_V + BF16_SUBLANES
VT_TOTAL = MLA_HEADS * VT_ROWS
ONE_PASS_JUMP = 64.0

IN_QKV = 0
IN_Z = IN_QKV + GDN_QKV
IN_CQ = IN_Z + GDN_Z
IN_GATES_ROPE = IN_CQ + MLA_Q_RANK
IN_CKV = IN_GATES_ROPE + LANES
IN_TOTAL = IN_CKV + MLA_KV_RANK


def _params(n_axes):
    return pltpu.CompilerParams(dimension_semantics=("arbitrary",) * n_axes, vmem_limit_bytes=VMEM_LIMIT)


def _rms(x, g):
    return x * lax.rsqrt(jnp.mean(x * x, -1, keepdims=True) + EPS) * g


def _sigmoid(x):
    return 1.0 / (1.0 + jnp.exp(-x))


def _silu(x):
    return x * _sigmoid(x)


def _softplus(x):
    return jnp.maximum(x, 0.0) + jnp.log1p(jnp.exp(-jnp.abs(x)))


def _dot(a, b):
    return jnp.dot(a, b, preferred_element_type=F32)


def _dot_nt(a, b):
    return lax.dot_general(a, b, (((1,), (1,)), ((), ())), preferred_element_type=F32)


def _dot_tn(a, b):
    return lax.dot_general(a, b, (((0,), (0,)), ((), ())), preferred_element_type=F32)


def _shift_rows(x, history, k):
    rolled = pltpu.roll(x, k, 0)
    row = lax.broadcasted_iota(jnp.int32, (HALO, 1), 0)
    head = rolled[:HALO]
    for j in range(k):
        head = jnp.where(row == j, history[HALO - k + j:HALO - k + j + 1, :], head)
    return jnp.concatenate([head, rolled[HALO:]], 0)


def _full_spec(shape):
    zeros = (0,) * len(shape)
    return pl.BlockSpec(shape, lambda *_: zeros)


def _row_tile(rows, want):
    return want if rows % want == 0 else rows


def _ffn_body(x_ref, g_ref, wg_ref, wu_ref, wd_ref, *rest, mixed):
    if mixed:
        og_ref, omt_ref, wmix_ref, o_ref, act_ref = rest
        x = x_ref[...] + _dot(og_ref[...], wmix_ref[:GDN_Z, :]) + _dot_tn(omt_ref[...], wmix_ref[GDN_Z:, :])
    else:
        o_ref, act_ref = rest
        x = x_ref[...]
    h = _rms(x, g_ref[...]).astype(BF16)
    for c in range(FF_CHUNKS):
        cols = slice(c * FF_BLOCK, (c + 1) * FF_BLOCK)
        gate = _dot(h, wg_ref[:, cols])
        up = _dot(h, wu_ref[:, cols])
        act_ref[:, cols] = (_silu(gate) * up).astype(BF16)
    o_ref[...] = x + 0.5 * _dot(act_ref[...], wd_ref[...])


def _ffn(x, w, mix=None):
    rows = x.shape[0]
    tm = _row_tile(rows, FFN_ROWS)
    g, wg, wu, wd, layer = w
    resident = lambda a: pl.BlockSpec(a.shape, lambda i: (0,) * a.ndim, pipeline_mode=pl.Buffered(1))
    of_layer = lambda a: pl.BlockSpec((None,) + a.shape[1:], lambda i: (layer, 0, 0), pipeline_mode=pl.Buffered(1))
    in_specs = [pl.BlockSpec((tm, D_MODEL), lambda i: (i, 0)), _full_spec(g.shape), of_layer(wg), of_layer(wu),
                of_layer(wd)]
    args = [x, g, wg, wu, wd]
    if mix is not None:
        og, omt, w_mix = mix
        tiles_per_batch = rows // omt.shape[0] // tm
        in_specs += [pl.BlockSpec((tm, GDN_Z), lambda i: (i, 0)),
                     pl.BlockSpec((None, GDN_Z, tm), lambda i: (i // tiles_per_batch, 0, i % tiles_per_batch)),
                     resident(w_mix)]
        args += [og, omt, w_mix]
    return pl.pallas_call(
        functools.partial(_ffn_body, mixed=mix is not None),
        grid=(rows // tm,),
        in_specs=in_specs,
        out_specs=pl.BlockSpec((tm, D_MODEL), lambda i: (i, 0)),
        out_shape=jax.ShapeDtypeStruct((rows, D_MODEL), F32),
        scratch_shapes=[pltpu.VMEM((tm, D_FF), BF16)],
        compiler_params=_params(1),
        name="half_ffn",
    )(*args)


def _to_bf16_body(x_ref, o_ref):
    o_ref[...] = x_ref[...].astype(BF16)


def _to_bf16(a):
    a2 = a.reshape(-1, a.shape[-1])
    rows, cols = a2.shape
    tm = _row_tile(rows, 512)
    out = pl.pallas_call(
        _to_bf16_body,
        grid=(rows // tm,),
        in_specs=[pl.BlockSpec((tm, cols), lambda i: (i, 0))],
        out_specs=pl.BlockSpec((tm, cols), lambda i: (i, 0)),
        out_shape=jax.ShapeDtypeStruct((rows, cols), BF16),
        compiler_params=_params(1),
        name="weights_to_bf16",
    )(a2)
    return out.reshape(a.shape)


def _ffn_weights(norm, stacks):
    return [(norm[l].reshape(1, D_MODEL),) + tuple(stacks) + (l,) for l in range(norm.shape[0])]


def _gdn_conv(x, history, cw):
    y = (_shift_rows(x, history, 3) * cw[0:1] + _shift_rows(x, history, 2) * cw[1:2]
         + _shift_rows(x, history, 1) * cw[2:3] + x * cw[3:4])
    return _silu(y)


class _Split:
    def __init__(self, x):
        self.x = x
        self._parts = None
        self._diag = None

    def parts(self):
        if self._parts is None:
            hi = self.x.astype(BF16)
            self._parts = (hi, (self.x - hi.astype(F32)).astype(BF16))
        return self._parts

    def diag_parts(self):
        if self._diag is None:
            first = lax.broadcasted_iota(jnp.int32, self.x.shape, 1) < CHUNK
            zero = jnp.zeros(self.x.shape, BF16)
            self._diag = tuple(jnp.concatenate([jnp.where(first, p, zero), jnp.where(first, zero, p)], 0)
                               for p in self.parts())
        return self._diag


def _split_products(pairs):
    parts = [(a.parts(), b.diag_parts()) for a, b in pairs]
    outs = []
    for (a_hi, a_lo), (b_hi, b_lo) in parts:
        a2 = jnp.concatenate([a_hi, a_lo], 0)
        r = _dot(a2, b_hi) + _dot(a2, b_lo)
        half = a_hi.shape[0]
        outs.append(_Split(r[:half] + r[half:]))
    return outs


def _unit_lower_inverses(ns, eye_f, block_masks):
    k = len(ns)
    nd = [_Split(jnp.where(block_masks[0], n, 0.0)) for n in ns]
    n2 = _split_products([(x, x) for x in nd])
    yield None
    lvl = _split_products([(x, x) for x in n2] +
                          [(_Split(eye_f - a.x), _Split(eye_f + b.x)) for a, b in zip(nd, n2)])
    yield None
    n4, t = lvl[:k], lvl[k:]
    t = _split_products([(a, _Split(eye_f + b.x)) for a, b in zip(t, n4)])
    yield None
    for done, merged in zip(block_masks, block_masks[1:] + [None]):
        keep = jnp.logical_not(done) if merged is None else jnp.logical_and(merged, jnp.logical_not(done))
        off = [_Split(jnp.where(keep, n, 0.0)) for n in ns]
        x = _split_products(list(zip(off, t)))
        yield None
        y = _split_products(list(zip(t, x)))
        yield None
        t = [_Split(a.x - b.x) for a, b in zip(t, y)]
    yield [a.x for a in t]


def _gdn_body(qkv_ref, z_ref, ab_ref, s0_ref, cprev_ref, cw_ref, alog_ref, dtb_ref, on_ref,
              og_ref, sout_ref, cout_ref, s_scr, halo_scr, *, rows, n_valid):
    t = pl.program_id(1)
    C = CHUNK
    row = lax.broadcasted_iota(jnp.int32, (rows, 1), 0)

    @pl.when(t == 0)
    def _():
        s_scr[...] = s0_ref[...]
        halo_scr[...] = cprev_ref[...]

    x = qkv_ref[...]
    y = _gdn_conv(x, halo_scr[...], cw_ref[...])
    last_rows = x[n_valid - HALO:n_valid, :]
    halo_scr[...] = last_rows

    ab = ab_ref[...]
    g_all = -jnp.exp(alog_ref[...]) * _softplus(ab + dtb_ref[...])
    b_all = _sigmoid(ab)
    if n_valid < rows:
        valid = row < n_valid
        y = jnp.where(valid, y, 0.0)
        g_all = jnp.where(valid, g_all, 0.0)
        b_all = jnp.where(valid, b_all, 0.0)

    ii = lax.broadcasted_iota(jnp.int32, (C, LANES), 0)
    pair_lane = lax.broadcasted_iota(jnp.int32, (C, LANES), 1)
    jj = pair_lane & (C - 1)
    first_head = pair_lane < C
    causal = ii >= jj
    strict = ii > jj
    eye_f = (ii == jj).astype(F32)
    block_masks = [(ii >> s) == (jj >> s) for s in (3, 4, 5)]
    on = on_ref[...]
    n_chunks = rows // C
    heads = range(GDN_HEADS)

    ri = lax.broadcasted_iota(jnp.int32, (rows, rows), 0)
    rj = lax.broadcasted_iota(jnp.int32, (rows, rows), 1)
    tril_blk = jnp.logical_and(ri >= rj, (ri >> CHUNK_SHIFT) == (rj >> CHUNK_SHIFT)).astype(BF16)
    g_hi = g_all.astype(BF16)
    g_r1 = g_all - g_hi.astype(F32)
    g_mid = g_r1.astype(BF16)
    g_lo = (g_r1 - g_mid.astype(F32)).astype(BF16)
    gcum = _dot(tril_blk, g_hi) + _dot(tril_blk, g_mid) + _dot(tril_blk, g_lo)
    egcum = jnp.exp(gcum)

    qn, kn, vn = [], [], []
    for h in heads:
        qh = y[:, h * GDN_DK:(h + 1) * GDN_DK]
        kh = y[:, GDN_HEADS * GDN_DK + h * GDN_DK:GDN_HEADS * GDN_DK + (h + 1) * GDN_DK]
        qn.append(qh * lax.rsqrt(jnp.sum(qh * qh, -1, keepdims=True) + EPS) * (GDN_DK ** -0.5))
        kn.append(kh * lax.rsqrt(jnp.sum(kh * kh, -1, keepdims=True) + EPS))
        vn.append(y[:, 2 * GDN_HEADS * GDN_DK + h * GDN_DV:2 * GDN_HEADS * GDN_DK + (h + 1) * GDN_DV])

    head_pairs = [(a, a + 1) for a in range(0, GDN_HEADS, 2)]
    pre = {}

    def chunk_local(chunks):
        chains = [(c, pair) for c in chunks for pair in head_pairs]
        n_mats = []
        for c in chunks:
            sl = slice(c * C, (c + 1) * C)
            gcum_t = gcum[sl].T
            for a, b in head_pairs:
                col = lambda arr, h: arr[sl, h:h + 1]
                gcol = jnp.where(first_head, col(gcum, a), col(gcum, b))
                beta2 = jnp.where(first_head, col(b_all, GDN_HEADS + a), col(b_all, GDN_HEADS + b))
                grow = jnp.concatenate([gcum_t[a:a + 1, :], gcum_t[b:b + 1, :]], -1)
                dec_causal = jnp.exp(jnp.where(causal, gcol - grow, NEG_INF))
                kb = {h: kn[h][sl].astype(BF16) for h in (a, b)}
                kk = jnp.concatenate([_dot_nt(kb[a], kb[a]), _dot_nt(kb[b], kb[b])], -1)
                n_mats.append(beta2 * kk * jnp.where(strict, dec_causal, 0.0))
                qk = jnp.concatenate([_dot_nt(qn[h][sl].astype(BF16), kb[h]) for h in (a, b)], -1)
                qk = (qk * dec_causal).astype(BF16)
                for h, lanes in ((a, slice(0, C)), (b, slice(C, 2 * C))):
                    beta = col(b_all, GDN_HEADS + h)
                    eg = col(egcum, h)
                    g_last = gcum[(c + 1) * C - 1:(c + 1) * C, h:h + 1]
                    qh, kh, vh = qn[h][sl], kn[h][sl], vn[h][sl]
                    pre[c, h] = dict(
                        qk=qk[:, lanes],
                        rhs=jnp.concatenate([(beta * vh).astype(BF16), ((beta * eg) * kh).astype(BF16)], -1),
                        qd=(qh * eg).astype(BF16),
                        kt=(kh * jnp.exp(g_last - col(gcum, h))).astype(BF16),
                        decay=jnp.exp(g_last))
            yield
        inverses = None
        for inverses in _unit_lower_inverses(n_mats, eye_f, block_masks):
            yield
        for (c, (a, b)), tinv in zip(chains, inverses):
            tinv = tinv.astype(BF16)
            for h, lanes in ((a, slice(0, C)), (b, slice(C, 2 * C))):
                uw = _dot(tinv[:, lanes], pre[c, h]["rhs"])
                pre[c, h]["u"] = uw[:, :GDN_DV]
                pre[c, h]["wk"] = uw[:, GDN_DV:].astype(BF16)

    state = [s_scr[h] for h in heads]

    def recurrence(chunks):
        for c in chunks:
            sl = slice(c * C, (c + 1) * C)
            sb = [s.astype(BF16) for s in state]
            w = [(pre[c, h]["u"] - _dot(pre[c, h]["wk"], sb[h])).astype(BF16) for h in heads]
            yield
            o = [_dot(pre[c, h]["qd"], sb[h]) + _dot(pre[c, h]["qk"], w[h]) for h in heads]
            for h in heads:
                state[h] = state[h] * pre[c, h]["decay"] + _dot_tn(pre[c, h]["kt"], w[h])
            yield
            for h in heads:
                gated = _rms(o[h], on) * _silu(z_ref[sl, h * GDN_DV:(h + 1) * GDN_DV])
                og_ref[sl, h * GDN_DV:(h + 1) * GDN_DV] = gated.astype(BF16)
            yield

    group = min(n_chunks, GDN_GROUP)
    pending = iter(())
    for g0 in range(0, n_chunks, group):
        chunks = range(g0, g0 + group)
        for _ in chunk_local(chunks):
            next(pending, None)
        for _ in pending:
            pass
        pending = recurrence(chunks)
    for _ in pending:
        pass
    for h in heads:
        s_scr[h] = state[h]

    @pl.when(t == pl.num_programs(1) - 1)
    def _():
        sout_ref[...] = s_scr[...]
        cout_ref[...] = last_rows


def _gdn(qkv, z, ab, s0, cprev, w, *, rows, n_valid=None):
    b, t_len, _ = qkv.shape
    n_valid = rows if n_valid is None else n_valid
    conv_w, a_log, dt_bias, o_norm = w
    shared = s0.shape[0] == 1 and b > 1
    bsel = (lambda bi: 0) if shared else (lambda bi: bi)
    return pl.pallas_call(
        functools.partial(_gdn_body, rows=rows, n_valid=n_valid),
        grid=(b, t_len // rows),
        in_specs=[
            pl.BlockSpec((None, rows, GDN_QKV), lambda bi, ti: (bi, ti, 0)),
            pl.BlockSpec((None, rows, GDN_Z), lambda bi, ti: (bi, ti, 0)),
            pl.BlockSpec((None, rows, LANES), lambda bi, ti: (bi, ti, 0)),
            pl.BlockSpec((None, GDN_HEADS, GDN_DK, GDN_DV), lambda bi, ti: (bsel(bi), 0, 0, 0)),
            pl.BlockSpec((None, HALO, GDN_QKV), lambda bi, ti: (bsel(bi), 0, 0)),
            _full_spec(conv_w.shape), _full_spec(a_log.shape), _full_spec(dt_bias.shape), _full_spec(o_norm.shape),
        ],
        out_specs=[
            pl.BlockSpec((None, rows, GDN_Z), lambda bi, ti: (bi, ti, 0)),
            pl.BlockSpec((None, GDN_HEADS, GDN_DK, GDN_DV), lambda bi, ti: (bi, 0, 0, 0)),
            pl.BlockSpec((None, HALO, GDN_QKV), lambda bi, ti: (bi, 0, 0)),
        ],
        out_shape=[
            jax.ShapeDtypeStruct((b, t_len, GDN_Z), BF16),
            jax.ShapeDtypeStruct((b, GDN_HEADS, GDN_DK, GDN_DV), F32),
            jax.ShapeDtypeStruct((b, HALO, GDN_QKV), F32),
        ],
        scratch_shapes=[pltpu.VMEM((GDN_HEADS, GDN_DK, GDN_DV), F32), pltpu.VMEM((HALO, GDN_QKV), F32)],
        compiler_params=_params(2),
        name="gdn_mixer",
    )(qkv, z, ab, s0, cprev, conv_w, a_log, dt_bias, o_norm)


def _lane_masks():
    lane = lax.broadcasted_iota(jnp.int32, (1, LANES), 1)
    nope = lane < MLA_NOPE
    rope_a = jnp.logical_and(lane >= MLA_NOPE, lane < MLA_NOPE + MLA_ROPE // 2)
    rope_b = jnp.logical_and(lane >= MLA_NOPE + MLA_ROPE // 2, lane < MLA_NOPE + MLA_ROPE)
    return nope, rope_a, rope_b


def _rope_block(x, cos, sin, rope_a, rope_b):
    half = MLA_ROPE // 2
    rot = jnp.where(rope_a, -pltpu.roll(x, LANES - half, 1), jnp.where(rope_b, pltpu.roll(x, half, 1), 0.0))
    return x * cos + rot * sin


PAIR = 2 * LANES


def _twice(a):
    return jnp.concatenate([a, a], -1)


def _k_heads(k, shared, gk, k_out):
    for h in range(MLA_HEADS):
        x = k[:, h * LANES:(h + 1) * LANES]
        inv = lax.rsqrt(jnp.sum(x * x, -1, keepdims=True) / MLA_NOPE + EPS)
        k_out[:, h * LANES:(h + 1) * LANES] = (x * inv * gk + shared).astype(BF16)


def _inproj_body(x_ref, g_ref, w_ref, cos_ref, sin_ref, qn_ref, wuq_ref, kvn_ref, wk_ref, wvt_ref, vones_ref,
                 gq_ref, gk_ref, gmean_ref, rot_ref, *rest, tiles_per_batch, key_tile_chunks, n_cast):
    cast_in, rest = rest[:n_cast], rest[n_cast:]
    qkv_ref, z_ref, ab_ref, q_out, k_out, vt_out, ckv_out, kr_out = rest[:8]
    for src, dst in zip(cast_in, rest[8:]):
        dst[...] = src[...].astype(BF16)
    hn = _rms(x_ref[...], g_ref[...]).astype(BF16)
    qkv_ref[...] = _dot(hn, w_ref[:, IN_QKV:IN_Z])
    z_ref[...] = _dot(hn, w_ref[:, IN_Z:IN_CQ])
    cq_gates_rope = _dot(hn, w_ref[:, IN_CQ:IN_CKV])
    cq = cq_gates_rope[:, :MLA_Q_RANK]
    gates_rope = cq_gates_rope[:, MLA_Q_RANK:]
    ab_ref[...] = gates_rope
    ckv = _dot(hn, w_ref[:, IN_CKV:IN_TOTAL])

    nope, rope_a, rope_b = _lane_masks()
    krp = jnp.where(jnp.logical_or(rope_a, rope_b), gates_rope, 0.0)
    cos = cos_ref[...]
    sin = sin_ref[...]
    gk = gk_ref[...]
    gmean = gmean_ref[...]
    rot_mat = rot_ref[...]
    gq2 = _twice(gq_ref[...] * ((MLA_NOPE + MLA_ROPE) ** -0.5 * LOG2_E))
    cos2, sin2 = _twice(cos), _twice(sin)

    cqn = _rms(cq, qn_ref[...]).astype(BF16)
    q = _dot(cqn, wuq_ref[...])
    for p in range(MLA_HEADS // 2):
        x = q[:, p * PAIR:(p + 1) * PAIR]
        ms = _dot((x * x).astype(BF16), gmean)
        x = x * lax.rsqrt(ms + EPS) * gq2
        q_out[:, p * PAIR:(p + 1) * PAIR] = (x * cos2 + _dot(x.astype(BF16), rot_mat) * sin2).astype(BF16)

    ckvn = _rms(ckv, kvn_ref[...])
    ckv_out[...] = ckvn
    cb = ckvn.astype(BF16)
    krn = krp * lax.rsqrt(jnp.sum(krp * krp, -1, keepdims=True) / MLA_ROPE + EPS) * jnp.where(nope, 0.0, gk)
    kr_new = _rope_block(krn, cos, sin, rope_a, rope_b)
    kr_out[...] = kr_new
    vt_out[...] = (_dot_nt(wvt_ref[...], cb) + vones_ref[...]).astype(BF16)
    shared_lanes = kr_new
    if key_tile_chunks is not None:
        tm = kr_new.shape[0]
        pos = (pl.program_id(0) % tiles_per_batch) * tm + lax.broadcasted_iota(jnp.int32, (tm, 1), 0)
        lane = lax.broadcasted_iota(jnp.int32, (1, LANES), 1)
        chunk = (pos >> CHUNK_SHIFT) & (key_tile_chunks - 1)
        shared_lanes = kr_new + (lane - MASK_LANE0 == chunk).astype(F32)
    _k_heads(_dot(cb, wk_ref[...]), shared_lanes, jnp.where(nope, gk, 0.0), k_out)


def _vt_spec(tm, tiles_per_batch):
    return pl.BlockSpec((None, VT_TOTAL, tm), lambda i: (i // tiles_per_batch, 0, i % tiles_per_batch))


def _inproj(x, g, w_in, cos, sin, w, *, batch, key_tile=None, cast=()):
    rows = x.shape[0]
    tm = _row_tile(rows, INPROJ_ROWS)
    steps = rows // tm
    tiles_per_batch = rows // batch // tm
    key_tile_chunks = None if key_tile is None else key_tile // CHUNK
    assert key_tile_chunks is None or (key_tile_chunks & (key_tile_chunks - 1) == 0
                                       and key_tile_chunks <= LANES - MASK_LANE0)
    q_norm, w_uq, kv_norm, w_k, w_vt, v_ones, gq, gk, gmean, rot_mat = w
    row_spec = lambda n: pl.BlockSpec((tm, n), lambda i: (i, 0))
    tab_spec = pl.BlockSpec((tm, LANES), lambda i: (i % tiles_per_batch, 0))
    consts = [g, w_in, q_norm, w_uq, kv_norm, w_k, w_vt, v_ones, gq, gk, gmean, rot_mat]
    cast2d = [a.reshape(-1, a.shape[-1]) for a in cast]
    cast_specs = []
    for a in cast2d:
        share = 1 if (a.shape[0] // steps) % BF16_SUBLANES == 0 else 2
        slab = a.shape[0] * share // steps
        assert slab % BF16_SUBLANES == 0 and slab * steps == a.shape[0] * share
        cast_specs.append(pl.BlockSpec((slab, a.shape[1]), lambda i, share=share: (i // share, 0)))
    outs = pl.pallas_call(
        functools.partial(_inproj_body, tiles_per_batch=tiles_per_batch, key_tile_chunks=key_tile_chunks,
                          n_cast=len(cast)),
        grid=(steps,),
        in_specs=[row_spec(D_MODEL), _full_spec(g.shape), _full_spec(w_in.shape), tab_spec, tab_spec]
        + [_full_spec(a.shape) for a in consts[2:]] + cast_specs,
        out_specs=[row_spec(GDN_QKV), row_spec(GDN_Z), row_spec(LANES), row_spec(MLA_PAD), row_spec(MLA_PAD),
                   _vt_spec(tm, tiles_per_batch), row_spec(MLA_KV_RANK), row_spec(LANES)] + cast_specs,
        out_shape=[jax.ShapeDtypeStruct((rows, GDN_QKV), F32), jax.ShapeDtypeStruct((rows, GDN_Z), F32),
                   jax.ShapeDtypeStruct((rows, LANES), F32),
                   jax.ShapeDtypeStruct((rows, MLA_PAD), BF16), jax.ShapeDtypeStruct((rows, MLA_PAD), BF16),
                   jax.ShapeDtypeStruct((batch, VT_TOTAL, rows // batch), BF16),
                   jax.ShapeDtypeStruct((rows, MLA_KV_RANK), F32), jax.ShapeDtypeStruct((rows, LANES), F32)]
        + [jax.ShapeDtypeStruct(a.shape, BF16) for a in cast2d],
        compiler_params=_params(1),
        name="mixer_inproj",
    )(x, g, w_in, cos, sin, *consts[2:], *cast2d)
    outs = list(outs)
    return outs[:8] + [o.reshape(a.shape) for o, a in zip(outs[8:], cast)]


def _kv_up_body(ckv_ref, kr_ref, wk_ref, wvt_ref, vones_ref, gk_ref, k_out, vt_out):
    nope, _, _ = _lane_masks()
    cb = ckv_ref[...].astype(BF16)
    vt_out[...] = (_dot_nt(wvt_ref[...], cb) + vones_ref[...]).astype(BF16)
    _k_heads(_dot(cb, wk_ref[...]), kr_ref[...], jnp.where(nope, gk_ref[...], 0.0), k_out)


def _kv_up(ckv, krp, w_k, w_vt, v_ones, gk, *, batch):
    rows = ckv.shape[0]
    tm = _row_tile(rows, 512)
    tiles_per_batch = rows // batch // tm
    row_spec = lambda n: pl.BlockSpec((tm, n), lambda i: (i, 0))
    return pl.pallas_call(
        _kv_up_body,
        grid=(rows // tm,),
        in_specs=[row_spec(MLA_KV_RANK), row_spec(LANES), _full_spec(w_k.shape), _full_spec(w_vt.shape),
                  _full_spec(v_ones.shape), _full_spec(gk.shape)],
        out_specs=[row_spec(MLA_PAD), _vt_spec(tm, tiles_per_batch)],
        out_shape=[jax.ShapeDtypeStruct((rows, MLA_PAD), BF16),
                   jax.ShapeDtypeStruct((batch, VT_TOTAL, rows // batch), BF16)],
        compiler_params=_params(1),
        name="mla_kv_up",
    )(ckv, krp, w_k, w_vt, v_ones, gk)


def _attn_body(qi_ref, kj_ref, *refs, tq, tk, n_key_tiles, causal, nk_valid, has_ext):
    if has_ext:
        q_ref, k_ref, vt_ref, ke_ref, vte_ref, o_ref, m_scr, acc_scr, slot_ref = refs
    else:
        q_ref, k_ref, vt_ref, o_ref, m_scr, acc_scr, slot_ref = refs
    step = pl.program_id(1)
    i = qi_ref[step]
    j = kj_ref[step]
    last_j = ((i + 1) * tq + tk - 1) // tk - 1 if causal else n_key_tiles - 1
    head_lanes = [slice(h * LANES, (h + 1) * LANES) for h in range(MLA_HEADS)]

    def head_scores(k_blk, q_mask):
        if q_mask is None:
            q_heads = [q_ref[:, hs] for hs in head_lanes]
        else:
            q_heads = [q_ref[:, hs] + q_mask for hs in head_lanes]
        return [_dot_nt(k_blk[:, hs], qh) for hs, qh in zip(head_lanes, q_heads)]

    def values(vt_blk, h):
        return vt_blk[h * VT_ROWS:(h + 1) * VT_ROWS, :]

    def two_pass(k_blk, vt_blk, cur, bias=None, q_mask=None):
        scores = head_scores(k_blk, q_mask)
        for h in range(MLA_HEADS):
            s = scores[h] if bias is None else scores[h] + bias
            m_old = m_scr[cur, h][0:1]
            m_new = jnp.maximum(m_old, jnp.max(s, 0, keepdims=True))
            alpha = jnp.exp2(m_old - m_new)
            p = jnp.exp2(s - m_new)
            m_scr[cur, h] = jnp.broadcast_to(m_new, (HALO, tq))
            acc_scr[cur, h] = acc_scr[cur, h] * alpha + _dot(values(vt_blk, h), p.astype(BF16))

    def one_pass(k_blk, vt_blk, cur, q_mask=None):
        scores = head_scores(k_blk, q_mask)
        rise = None
        for h in range(MLA_HEADS):
            s = scores[h]
            m_old = m_scr[cur, h][0:1]
            p = jnp.exp2(s - m_old)
            m_tile = jnp.max(s, 0, keepdims=True)
            m_new = jnp.maximum(m_old, m_tile)
            rise = m_tile - m_old if rise is None else jnp.maximum(rise, m_tile - m_old)
            m_scr[1 - cur, h] = jnp.broadcast_to(m_new, (HALO, tq))
            acc_scr[1 - cur, h] = (acc_scr[cur, h] + _dot(values(vt_blk, h), p.astype(BF16))) * jnp.exp2(m_old - m_new)
        return jnp.max(rise)

    def guarded(k_blk, vt_blk, q_mask=None):
        cur = slot_ref[0]
        rise = one_pass(k_blk, vt_blk, cur, q_mask)

        @pl.when(rise <= ONE_PASS_JUMP)
        def _():
            slot_ref[0] = 1 - cur

        @pl.when(jnp.logical_not(rise <= ONE_PASS_JUMP))
        def _():
            two_pass(k_blk, vt_blk, cur, q_mask=q_mask)

    @pl.when(j == 0)
    def _():
        slot_ref[0] = 0
        m_scr[0] = jnp.full(m_scr.shape[1:], NEG_INF, F32)
        acc_scr[0] = jnp.zeros(acc_scr.shape[1:], F32)
        if has_ext:
            two_pass(ke_ref, vte_ref, 0)

    if nk_valid is not None:
        kpos = j * tk + lax.broadcasted_iota(jnp.int32, (tk, 1), 0)
        two_pass(k_ref, vt_ref, slot_ref[0], bias=jnp.where(kpos < nk_valid, 0.0, NEG_INF))
    elif not has_ext:
        two_pass(k_ref, vt_ref, slot_ref[0])
    elif causal:
        @pl.when(j < last_j)
        def _():
            guarded(k_ref, vt_ref)

        def diagonal(n_keys):
            lane = lax.broadcasted_iota(jnp.int32, (1, LANES), 1)
            q_chunk = (i * tq + lax.broadcasted_iota(jnp.int32, (tq, 1), 0)) >> CHUNK_SHIFT
            key_chunk = j * (tk // CHUNK) + lane - MASK_LANE0
            hidden = jnp.logical_and(jnp.logical_and(lane >= MASK_LANE0, lane < MASK_LANE0 + tk // CHUNK),
                                     key_chunk > q_chunk)
            guarded(k_ref.at[0:n_keys, :], vt_ref.at[:, 0:n_keys],
                    q_mask=jnp.where(hidden, -MASK_BIG, 0.0).astype(BF16))

        n_visible = (i + 1) * tq - j * tk
        for n_keys in range(tq, tk + 1, tq):
            pl.when(jnp.logical_and(j == last_j, n_visible == n_keys))(functools.partial(diagonal, n_keys))
    else:
        guarded(k_ref, vt_ref)

    @pl.when(j == last_j)
    def _():
        cur = slot_ref[0]
        for h in range(MLA_HEADS):
            acc = acc_scr[cur, h]
            o_ref[h * MLA_V:(h + 1) * MLA_V, :] = (acc[:MLA_V] / acc[MLA_V:MLA_V + 1]).astype(BF16)


def _attention(q, k, vt, ext, *, tq, tk, causal=False, nk_valid=None):
    b, nq, _ = q.shape
    nk = k.shape[1]
    n_key_tiles = nk // tk
    assert not causal or tk % tq == 0
    pairs = [(i, j) for i in range(nq // tq) for j in range(n_key_tiles) if not causal or j * tk < (i + 1) * tq]
    qi = jnp.asarray([p[0] for p in pairs], jnp.int32)
    kj = jnp.asarray([p[1] for p in pairs], jnp.int32)
    hv = MLA_HEADS * MLA_V
    in_specs = [
        pl.BlockSpec((None, tq, MLA_PAD), lambda bi, s, qi, kj: (bi, qi[s], 0)),
        pl.BlockSpec((None, tk, MLA_PAD), lambda bi, s, qi, kj: (bi, kj[s], 0)),
        pl.BlockSpec((None, VT_TOTAL, tk), lambda bi, s, qi, kj: (bi, 0, kj[s])),
    ]
    args = [q, k, vt]
    if ext is not None:
        ne = ext[0].shape[1]
        shared = ext[0].shape[0] == 1 and b > 1
        emap = (lambda bi, s, qi, kj: (0, 0, 0)) if shared else (lambda bi, s, qi, kj: (bi, 0, 0))
        in_specs += [pl.BlockSpec((None, ne, MLA_PAD), emap), pl.BlockSpec((None, VT_TOTAL, ne), emap)]
        args += list(ext)
    return pl.pallas_call(
        functools.partial(_attn_body, tq=tq, tk=tk, n_key_tiles=n_key_tiles, causal=causal, nk_valid=nk_valid,
                          has_ext=ext is not None),
        grid_spec=pltpu.PrefetchScalarGridSpec(
            num_scalar_prefetch=2,
            grid=(b, len(pairs)),
            in_specs=in_specs,
            out_specs=pl.BlockSpec((None, hv, tq), lambda bi, s, qi, kj: (bi, 0, qi[s])),
            scratch_shapes=[pltpu.VMEM((2, MLA_HEADS, HALO, tq), F32), pltpu.VMEM((2, MLA_HEADS, VT_ROWS, tq), F32),
                            pltpu.SMEM((1,), jnp.int32)]),
        out_shape=jax.ShapeDtypeStruct((b, hv, nq), BF16),
        compiler_params=_params(2),
        name="mla_attention",
    )(qi, kj, *args)


def _sconv_body(x_ref, g_ref, win_ref, cw_ref, wout_ref, prev_ref, o_ref, pout_ref, halo_scr, *, rows):
    t = pl.program_id(1)

    @pl.when(t == 0)
    def _():
        halo_scr[...] = prev_ref[...]

    x = x_ref[...]
    hn = _rms(x, g_ref[...]).astype(BF16)
    gated = []
    for c in range(D_MODEL // FF_BLOCK):
        cols = slice(c * FF_BLOCK, (c + 1) * FF_BLOCK)
        shifted = lambda base: slice(base + c * FF_BLOCK, base + (c + 1) * FF_BLOCK)
        bg = _dot(hn, win_ref[:, shifted(0)])
        u = _dot(hn, win_ref[:, shifted(D_MODEL)]) * _dot(hn, win_ref[:, shifted(2 * D_MODEL)])
        halo = halo_scr[:, cols]
        cw = cw_ref[:, cols]
        y = _shift_rows(u, halo, 2) * cw[0:1] + _shift_rows(u, halo, 1) * cw[1:2] + u * cw[2:3]
        halo_scr[:, cols] = u[rows - HALO:rows, :]
        gated.append((bg * y).astype(BF16))
    o_ref[...] = x + _dot(jnp.concatenate(gated, -1), wout_ref[...])

    @pl.when(t == pl.num_programs(1) - 1)
    def _():
        pout_ref[...] = halo_scr[...]


def _sconv(x, prev, w, *, rows):
    b, t_len, _ = x.shape
    g, w_in, conv_w, w_out = w
    shared = prev.shape[0] == 1 and b > 1
    bsel = (lambda bi: 0) if shared else (lambda bi: bi)
    return pl.pallas_call(
        functools.partial(_sconv_body, rows=rows),
        grid=(b, t_len // rows),
        in_specs=[pl.BlockSpec((None, rows, D_MODEL), lambda bi, ti: (bi, ti, 0)), _full_spec(g.shape),
                  _full_spec(w_in.shape), _full_spec(conv_w.shape), _full_spec(w_out.shape),
                  pl.BlockSpec((None, HALO, D_MODEL), lambda bi, ti: (bsel(bi), 0, 0))],
        out_specs=[pl.BlockSpec((None, rows, D_MODEL), lambda bi, ti: (bi, ti, 0)),
                   pl.BlockSpec((None, HALO, D_MODEL), lambda bi, ti: (bi, 0, 0))],
        out_shape=[jax.ShapeDtypeStruct((b, t_len, D_MODEL), F32), jax.ShapeDtypeStruct((b, HALO, D_MODEL), F32)],
        scratch_shapes=[pltpu.VMEM((HALO, D_MODEL), F32)],
        compiler_params=_params(2),
        name="sconv_mixer",
    )(x, g, w_in, conv_w, w_out, prev)


def _pad_lanes(a, before, total):
    return jnp.pad(a, [(0, 0)] * (a.ndim - 1) + [(before, total - before - a.shape[-1])])


def _rope_tables(pos):
    half = MLA_ROPE // 2
    inv = ROPE_THETA ** (-jnp.arange(half, dtype=F32) / half)
    ang = pos.astype(F32)[:, None] * inv
    cos, sin = jnp.cos(ang), jnp.sin(ang)
    n = pos.shape[0]
    tail = LANES - MLA_NOPE - MLA_ROPE
    cos_l = jnp.concatenate([jnp.ones((n, MLA_NOPE), F32), cos, cos, jnp.ones((n, tail), F32)], -1)
    sin_l = jnp.concatenate([jnp.zeros((n, MLA_NOPE), F32), sin, sin, jnp.zeros((n, tail), F32)], -1)
    return cos_l, sin_l


def _head_pair_matrices():
    i = jnp.arange(PAIR)[:, None]
    j = jnp.arange(PAIR)[None, :]
    same = (i // LANES) == (j // LANES)
    li, lj = i % LANES, j % LANES
    half = MLA_ROPE // 2
    rope_end = MLA_NOPE + MLA_ROPE
    in_rope = lambda l: (l >= MLA_NOPE) & (l < rope_end)
    gmean = jnp.where(same & (li < MLA_NOPE) & (lj < MLA_NOPE), 1.0 / MLA_NOPE,
                      jnp.where(same & in_rope(li) & in_rope(lj), 1.0 / MLA_ROPE, 0.0))
    first, second = in_rope(lj) & (lj < MLA_NOPE + half), in_rope(lj) & (lj >= MLA_NOPE + half)
    rot = jnp.where(same & first & (li == lj + half), -1.0, jnp.where(same & second & (li == lj - half), 1.0, 0.0))
    return gmean.astype(BF16), rot.astype(BF16)


def _tail_rows(a, n):
    return jnp.pad(a, ((0, 0), (HALO - n, 0), (0, 0)))


def kernel(x_prompt, x_sample, cache_mla_ckv, cache_mla_krope, state_gdn_S, state_gdn_conv, state_sconv, meta_tokens, ffn1_norm, ffn1_w_gate, ffn1_w_up, ffn1_w_down, ffn2_norm, ffn2_w_gate, ffn2_w_up, ffn2_w_down, mix_norm, ab_w_in, ab_w_out, gdn_conv_w, gdn_A_log, gdn_dt_bias, gdn_o_norm, mla_q_norm, mla_w_uq, mla_kv_norm, mla_w_ukv, mla_qn_norm, mla_qr_norm, mla_kn_norm, mla_kr_norm, sc_w_in, sc_conv_w, sc_w_out):
    bp, tp, _ = x_prompt.shape
    bs, ts, _ = x_sample.shape
    n_frames = bp * tp
    n_sample = bs * ts

    ffn1 = _ffn_weights(ffn1_norm, [_to_bf16(a) for a in (ffn1_w_gate, ffn1_w_up, ffn1_w_down)])
    w_in = _to_bf16(ab_w_in)[0]
    s1 = GDN_QKV + GDN_Z
    c0 = s1 + 2 * GDN_HEADS
    c1 = c0 + MLA_Q_RANK
    c2 = c1 + MLA_KV_RANK
    gates_rope = jnp.concatenate([
        w_in[:, s1:c0], jnp.zeros((D_MODEL, MLA_NOPE - 2 * GDN_HEADS), BF16), w_in[:, c2:],
        jnp.zeros((D_MODEL, LANES - MLA_NOPE - MLA_ROPE), BF16)], -1)
    w_in_packed = jnp.concatenate([w_in[:, :s1], w_in[:, c0:c1], gates_rope, w_in[:, c1:c2]], -1)
    mix_g = mix_norm.reshape(2, 1, D_MODEL)
    gdn_w = (gdn_conv_w[0], _pad_lanes(gdn_A_log[0][None], 0, LANES), _pad_lanes(gdn_dt_bias[0][None], 0, LANES),
             gdn_o_norm[0][None])
    qdim = MLA_NOPE + MLA_ROPE
    w_uq = _pad_lanes(mla_w_uq[0].reshape(MLA_Q_RANK, MLA_HEADS, qdim), 0, LANES).reshape(MLA_Q_RANK, MLA_PAD)
    w_ukv = mla_w_ukv[0].reshape(MLA_KV_RANK, MLA_HEADS, MLA_NOPE + MLA_V)
    w_k = _pad_lanes(w_ukv[..., :MLA_NOPE], 0, LANES).reshape(MLA_KV_RANK, MLA_PAD)
    hv = MLA_HEADS * MLA_V
    w_vt = _pad_lanes(w_ukv[..., MLA_NOPE:], 0, VT_ROWS).reshape(MLA_KV_RANK, VT_TOTAL).T
    v_ones = (jnp.arange(VT_TOTAL) % VT_ROWS == MLA_V).astype(F32)[:, None]
    gq = _pad_lanes(jnp.concatenate([mla_qn_norm[0], mla_qr_norm[0]])[None], 0, LANES)
    gk = _pad_lanes(jnp.concatenate([mla_kn_norm[0], mla_kr_norm[0]])[None], 0, LANES)
    gmean, rot_mat = _head_pair_matrices()
    mla_w = (mla_q_norm[0][None], w_uq.astype(BF16), mla_kv_norm[0][None], w_k.astype(BF16), w_vt.astype(BF16), v_ones,
             gq, gk, gmean, rot_mat)
    w_out = ab_w_out[0].astype(BF16)
    sc_w = (mix_g[1], _to_bf16(sc_w_in)[0], sc_conv_w[0], sc_w_out[0].astype(BF16))

    xf = x_prompt.reshape(n_frames, D_MODEL)
    xs = jnp.concatenate([x_sample.reshape(n_sample, D_MODEL), meta_tokens.astype(F32)], 0)
    cos_f, sin_f = _rope_tables(N_META + jnp.arange(tp))
    pos_small = jnp.concatenate([jnp.tile(PAST_LEN + jnp.arange(ts), bs), jnp.arange(N_META)])
    cos_s, sin_s = _rope_tables(pos_small)

    xf = _ffn(xf, ffn1[0])
    xs = _ffn(xs, ffn1[0])
    qkv_f, z_f, ab_f, q_f, k_f, vt_f, ckvn_f, krn_f, *ffn2_stacks = _inproj(
        xf, mix_g[0], w_in_packed, cos_f, sin_f, mla_w, batch=bp, key_tile=ATTN_TK,
        cast=(ffn2_w_gate, ffn2_w_up, ffn2_w_down))
    ffn2 = _ffn_weights(ffn2_norm, ffn2_stacks)
    qkv_s, z_s, ab_s, q_s, k_s, vt_s, ckvn_s, krn_s = _inproj(xs, mix_g[0], w_in_packed, cos_s, sin_s, mla_w, batch=1)

    pad_meta = lambda a: jnp.pad(a[n_sample:], ((0, CHUNK - N_META), (0, 0)))[None]
    og_m, s_m, conv_m = _gdn(pad_meta(qkv_s), pad_meta(z_s), pad_meta(ab_s),
                             jnp.zeros((1, GDN_HEADS, GDN_DK, GDN_DV), F32), jnp.zeros((1, HALO, GDN_QKV), F32),
                             gdn_w, rows=CHUNK, n_valid=N_META)
    og_f, p_s, p_conv = _gdn(qkv_f.reshape(bp, tp, GDN_QKV), z_f.reshape(bp, tp, GDN_Z), ab_f.reshape(bp, tp, LANES),
                             s_m, conv_m, gdn_w, rows=GDN_ROWS)
    og_s, s_s, s_conv = _gdn(qkv_s[:n_sample].reshape(bs, ts, GDN_QKV), z_s[:n_sample].reshape(bs, ts, GDN_Z),
                             ab_s[:n_sample].reshape(bs, ts, LANES), state_gdn_S[0],
                             _tail_rows(state_gdn_conv[0], state_gdn_conv.shape[2]), gdn_w, rows=ts)

    k_c, vt_c = _kv_up(cache_mla_ckv[0].reshape(bs * PAST_LEN, MLA_KV_RANK),
                       _pad_lanes(cache_mla_krope[0].reshape(bs * PAST_LEN, MLA_ROPE), MLA_NOPE, LANES),
                       mla_w[3], mla_w[4], v_ones, gk, batch=bs)
    k_meta, vt_meta = k_s[n_sample:][None], vt_s[:, :, n_sample:]
    vt_new = vt_s[0, :, :n_sample].reshape(VT_TOTAL, bs, ts).transpose(1, 0, 2)
    omt_f = _attention(q_f.reshape(bp, tp, MLA_PAD), k_f.reshape(bp, tp, MLA_PAD), vt_f, (k_meta, vt_meta),
                       tq=ATTN_TQ, tk=ATTN_TK, causal=True)
    omt_s = _attention(q_s[:n_sample].reshape(bs, ts, MLA_PAD), k_c.reshape(bs, PAST_LEN, MLA_PAD), vt_c,
                       (k_s[:n_sample].reshape(bs, ts, MLA_PAD), vt_new), tq=ts, tk=512)
    omt_m = _attention(jnp.pad(q_s[n_sample:], ((0, LANES - N_META), (0, 0)))[None],
                       jnp.pad(k_s[n_sample:], ((0, CHUNK - N_META), (0, 0)))[None],
                       jnp.pad(vt_meta, ((0, 0), (0, 0), (0, CHUNK - N_META))),
                       None, tq=LANES, tk=CHUNK, nk_valid=N_META)

    og_small = jnp.concatenate([og_s.reshape(n_sample, GDN_Z), og_m[0, :N_META]], 0)
    omt_small = jnp.concatenate([omt_s.transpose(1, 0, 2).reshape(hv, n_sample), omt_m[0, :, :N_META]], -1)[None]
    xf = _ffn(xf, ffn2[0], mix=(og_f.reshape(n_frames, GDN_Z), omt_f, w_out))
    xs = _ffn(xs, ffn2[0], mix=(og_small, omt_small, w_out))

    xf = _ffn(xf, ffn1[1])
    xs = _ffn(xs, ffn1[1])
    n_sc = state_sconv.shape[2]
    xm, sc_m = _sconv(xs[n_sample:][None], jnp.zeros((1, HALO, D_MODEL), F32), sc_w, rows=N_META)
    xf3, p_sc = _sconv(xf.reshape(bp, tp, D_MODEL), sc_m, sc_w, rows=SCONV_ROWS)
    xs3, s_sc = _sconv(xs[:n_sample].reshape(bs, ts, D_MODEL), _tail_rows(state_sconv[0], n_sc), sc_w, rows=ts)
    xf = _ffn(xf3.reshape(n_frames, D_MODEL), ffn2[1])
    xs = _ffn(jnp.concatenate([xs3.reshape(n_sample, D_MODEL), xm[0]], 0), ffn2[1])

    y_prompt = xf.reshape(bp, tp, D_MODEL)
    y_sample = xs[:n_sample].reshape(bs, ts, D_MODEL)
    rope_lo, rope_hi = MLA_NOPE, MLA_NOPE + MLA_ROPE
    meta_ckv = jnp.broadcast_to(ckvn_s[n_sample:][None], (bp, N_META, MLA_KV_RANK))
    meta_kr = jnp.broadcast_to(krn_s[n_sample:, rope_lo:rope_hi][None], (bp, N_META, MLA_ROPE))
    p_ckv = jnp.concatenate([meta_ckv, ckvn_f.reshape(bp, tp, MLA_KV_RANK)], 1)
    p_kr = jnp.concatenate([meta_kr, krn_f[:, rope_lo:rope_hi].reshape(bp, tp, MLA_ROPE)], 1)
    n_gc = state_gdn_conv.shape[2]
    return (y_prompt, y_sample,
            p_ckv[None], p_kr[None], p_s[None], p_conv[:, HALO - n_gc:][None], p_sc[:, HALO - n_sc:][None],
            ckvn_s[:n_sample].reshape(bs, ts, MLA_KV_RANK)[None],
            krn_s[:n_sample, rope_lo:rope_hi].reshape(bs, ts, MLA_ROPE)[None],
            s_s[None], s_conv[:, HALO - n_gc:][None], s_sc[:, HALO - n_sc:][None])
```
